```python
import math
import jax, jax.numpy as jnp
from jax import lax
import numpy as np

D_MODEL = 1024
BATCH = 4
SEQ = 4096
DEPTH = 2

PLE_DIM = 256
N_EVEN = (DEPTH + 1) // 2
N_ODD = DEPTH // 2
DEEPNORM_ALPHA = (2.0 * DEPTH) ** 0.25
DEEPNORM_BETA = (8.0 * DEPTH) ** -0.25
LN_EPS = 1e-5
A_WIDTH = D_MODEL // 2
A_GROUPS = 8
A_GROUP_DIM = A_WIDTH // A_GROUPS
A_CHUNK = 128
B_HEADS = 4
B_DV = (D_MODEL // 2) // B_HEADS
B_DK = B_DV // 2
B_GATE_RANK = 16
B_TAU = 16.0
B_CHUNK = 64
C_HEADS = 8
C_DH = D_MODEL // (2 * C_HEADS)
C_QBLOCK = 128
FFN_DIM = 2816
N_EXPERTS = 8
TOP_K = 2
EXPERT_DIM = 3584
MOE_BLOCK = 256
E_SPLIT_SIZES = [A_WIDTH, A_WIDTH, B_HEADS * B_DK, B_HEADS * B_DK, B_HEADS * B_DV, B_HEADS * B_DV]
E_IN_COLS = sum(E_SPLIT_SIZES) + B_GATE_RANK
E_MIX_WIDTH = A_WIDTH + B_HEADS * B_DV

kernel_name = "hybrid_gmlp_gla_diffattn_moe_deepnorm"

F32 = jnp.float32


def layer_norm(x, g, b):
    xf = x.astype(F32)
    mu = jnp.mean(xf, axis=-1, keepdims=True)
    var = jnp.mean(jnp.square(xf - mu), axis=-1, keepdims=True)
    return ((xf - mu) * lax.rsqrt(var + LN_EPS) * g + b).astype(x.dtype)


def rms_norm(x, g):
    xf = x.astype(F32)
    return (xf * lax.rsqrt(jnp.mean(jnp.square(xf), axis=-1, keepdims=True) + LN_EPS) * g).astype(x.dtype)


def spatial_gating(u, v, ln_g, ln_b, ws, bs):
    bn, s, _ = v.shape
    nc = s // A_CHUNK
    v = layer_norm(v, ln_g, ln_b).reshape(bn, nc, A_CHUNK, A_GROUPS, A_GROUP_DIM)
    causal = jnp.tril(jnp.ones((A_CHUNK, A_CHUNK), dtype=bool))
    w = jnp.where(causal, ws, 0)
    sg = jnp.einsum('gts,bcsgd->bctgd', w, v) + bs.T[:, :, None]
    return u * sg.reshape(bn, s, A_WIDTH)


def gla_chunked(q, k, v, log_a):
    bn, s, h, dk = q.shape
    dv = v.shape[-1]
    L = B_CHUNK
    nc = s // L
    q, k, v, log_a = [t.reshape(bn, nc, L, h, t.shape[-1]) for t in (q, k, v, log_a)]
    b = jnp.cumsum(log_a, axis=2)
    ref = b[:, :, L // 2 - 1:L // 2]
    b_last = b[:, :, L - 1:L]
    causal = jnp.tril(jnp.ones((L, L), dtype=bool))
    scores = jnp.einsum('bclhk,bcmhk->bchlm', q * jnp.exp(b - ref), k * jnp.exp(ref - b))
    scores = jnp.where(causal, scores, 0.0)
    o_intra = jnp.einsum('bchlm,bcmhv->bclhv', scores, v)
    kv = jnp.einsum('bclhk,bclhv->bchkv', k * jnp.exp(b_last - b), v)
    decay = jnp.exp(b_last[:, :, 0])

    def step(state, inp):
        d, kv_c = inp
        return d[..., None] * state + kv_c, state

    init = jnp.zeros((bn, h, dk, dv), F32)
    _, s_prev = lax.scan(step, init, (jnp.moveaxis(decay, 1, 0), jnp.moveaxis(kv, 1, 0)))
    s_prev = jnp.moveaxis(s_prev, 0, 1)
    o_inter = jnp.einsum('bclhk,bchkv->bclhv', q * jnp.exp(b), s_prev)
    return (o_intra + o_inter).reshape(bn, s, h, dv)


def even_mixer(x, w_in, a_ln_g, a_ln_b, a_ws, a_bs, b_w_gate_up, b_b_gate, b_norm_g, w_out):
    bn, s, _ = x.shape
    z = x @ w_in
    splits = np.cumsum(E_SPLIT_SIZES).tolist()
    u, v, q, k, vb, r, g_low = jnp.split(z, splits, axis=-1)
    ya = spatial_gating(jax.nn.gelu(u), jax.nn.gelu(v), a_ln_g, a_ln_b, a_ws, a_bs)
    q = q.reshape(bn, s, B_HEADS, B_DK).astype(F32) * (B_DK ** -0.5)
    k = k.reshape(bn, s, B_HEADS, B_DK).astype(F32)
    vb = vb.reshape(bn, s, B_HEADS, B_DV).astype(F32)
    log_a = jax.nn.log_sigmoid((g_low @ b_w_gate_up + b_b_gate).astype(F32)) / B_TAU
    log_a = log_a.reshape(bn, s, B_HEADS, B_DK)
    o = rms_norm(gla_chunked(q, k, vb, log_a), b_norm_g)
    yb = o.reshape(bn, s, B_HEADS * B_DV).astype(x.dtype) * jax.nn.silu(r)
    return jnp.concatenate([ya, yb], axis=-1) @ w_out


def diff_attention(x, w_qkv, lam_q1, lam_k1, lam_q2, lam_k2, subln_g, w_out, lambda_init):
    bn, s, _ = x.shape
    q, k, v = jnp.split(x @ w_qkv, 3, axis=-1)
    q = q.reshape(bn, s, C_HEADS, 2, C_DH) * (C_DH ** -0.5)
    k = k.reshape(bn, s, C_HEADS, 2, C_DH)
    v = v.reshape(bn, s, C_HEADS, 2 * C_DH)
    lam = (jnp.exp(jnp.sum(lam_q1 * lam_k1).astype(F32))
           - jnp.exp(jnp.sum(lam_q2 * lam_k2).astype(F32)) + lambda_init)
    nb = s // C_QBLOCK
    qb = jnp.moveaxis(q.reshape(bn, nb, C_QBLOCK, C_HEADS, 2, C_DH), 1, 0)
    kpos = jnp.arange(s)

    def block(args):
        qi, i = args
        sc = jnp.einsum('bqhcd,bkhcd->bhcqk', qi, k).astype(F32)
        qpos = i * C_QBLOCK + jnp.arange(C_QBLOCK)
        sc = jnp.where(kpos[None, :] <= qpos[:, None], sc, -jnp.inf)
        pr = jax.nn.softmax(sc, axis=-1)
        a = pr[:, :, 0] - lam * pr[:, :, 1]
        return jnp.einsum('bhqk,bkhe->bqhe', a.astype(v.dtype), v)

    o = lax.map(block, (qb, jnp.arange(nb)))
    o = jnp.moveaxis(o, 0, 1).reshape(bn, s, C_HEADS, 2 * C_DH)
    o = rms_norm(o, subln_g) * (1.0 - lambda_init)
    return o.reshape(bn, s, C_HEADS * 2 * C_DH) @ w_out


def swiglu(x, wg, wu, wd):
    return (jax.nn.silu(x @ wg) * (x @ wu)) @ wd


def moe_swiglu(x, w_router, wg, wu, wd):
    bn, s, d = x.shape
    n = bn * s
    xf = x.reshape(n, d)
    logits = (xf @ w_router).astype(F32)
    top_val, top_idx = lax.top_k(logits, TOP_K)
    gates = jax.nn.softmax(top_val, axis=-1)
    flat_e = top_idx.reshape(-1)
    flat_tok = jnp.repeat(jnp.arange(n, dtype=jnp.int32), TOP_K)
    flat_g = gates.reshape(-1)
    order = jnp.argsort(flat_e)
    se = flat_e[order]
    counts = jnp.bincount(flat_e, length=N_EXPERTS)
    padded = (counts + MOE_BLOCK - 1) // MOE_BLOCK * MOE_BLOCK
    start = jnp.cumsum(counts) - counts
    pend = jnp.cumsum(padded)
    pstart = pend - padded
    dest = pstart[se] + jnp.arange(n * TOP_K) - start[se]
    n_rows = n * TOP_K + N_EXPERTS * MOE_BLOCK
    n_blocks = n_rows // MOE_BLOCK
    row_tok = jnp.full((n_rows,), n, jnp.int32).at[dest].set(flat_tok[order])
    row_gate = jnp.zeros((n_rows,), F32).at[dest].set(flat_g[order])
    blk_e = jnp.minimum(jnp.searchsorted(pend, jnp.arange(n_blocks) * MOE_BLOCK, side='right'), N_EXPERTS - 1)
    x_pad = jnp.concatenate([xf, jnp.zeros((1, d), xf.dtype)], axis=0)
    xin = x_pad[row_tok].reshape(n_blocks, MOE_BLOCK, d)

    def expert_block(args):
        xb, e = args
        return (jax.nn.silu(xb @ wg[e]) * (xb @ wu[e])) @ wd[e]

    y = lax.map(expert_block, (xin, blk_e)).reshape(n_rows, d)
    out = jnp.zeros((n + 1, d), F32).at[row_tok].add(y.astype(F32) * row_gate[:, None])[:n]
    return out.astype(x.dtype).reshape(bn, s, d)


def per_layer_embed(x, p_i, w_proj, w_gate, b_gate):
    return x + jax.nn.sigmoid(x @ w_gate + b_gate) * (p_i @ w_proj)


def setup_inputs(seed: int = 0) -> dict:
    key = jax.random.key(seed)
    keys = iter(jax.random.split(key, 64))

    def nrm(shape, scale):
        return jax.random.normal(next(keys), shape, F32) * scale

    def gain(shape):
        return 1.0 + nrm(shape, 0.02)

    D = D_MODEL
    beta = DEEPNORM_BETA
    return {
        "x": nrm((BATCH, SEQ, D), 1.0),
        "p": nrm((DEPTH, BATCH, SEQ, PLE_DIM), 1.0),
        "e_w_in": nrm((N_EVEN, D, E_IN_COLS), D ** -0.5),
        "e_a_ln_g": gain((N_EVEN, A_WIDTH)),
        "e_a_ln_b": nrm((N_EVEN, A_WIDTH), 0.02),
        "e_a_ws": nrm((N_EVEN, A_GROUPS, A_CHUNK, A_CHUNK), A_CHUNK ** -0.5),
        "e_a_bs": gain((N_EVEN, A_GROUPS, A_CHUNK)),
        "e_b_w_gate_up": nrm((N_EVEN, B_GATE_RANK, B_HEADS * B_DK), B_GATE_RANK ** -0.5),
        "e_b_b_gate": nrm((N_EVEN, B_HEADS * B_DK), 0.02),
        "e_b_norm_g": gain((N_EVEN, B_DV)),
        "e_w_out": nrm((N_EVEN, E_MIX_WIDTH, D), E_MIX_WIDTH ** -0.5 * beta),
        "e_ln1_g": gain((N_EVEN, D)),
        "e_ln1_b": nrm((N_EVEN, D), 0.02),
        "e_ffn_wg": nrm((N_EVEN, D, FFN_DIM), D ** -0.5),
        "e_ffn_wu": nrm((N_EVEN, D, FFN_DIM), D ** -0.5),
        "e_ffn_wd": nrm((N_EVEN, FFN_DIM, D), FFN_DIM ** -0.5 * beta),
        "e_ln2_g": gain((N_EVEN, D)),
        "e_ln2_b": nrm((N_EVEN, D), 0.02),
        "o_w_qkv": nrm((N_ODD, D, 3 * C_HEADS * 2 * C_DH), D ** -0.5),
        "o_lam_q1": nrm((N_ODD, C_DH), 0.1),
        "o_lam_k1": nrm((N_ODD, C_DH), 0.1),
        "o_lam_q2": nrm((N_ODD, C_DH), 0.1),
        "o_lam_k2": nrm((N_ODD, C_DH), 0.1),
        "o_subln_g": gain((N_ODD, 2 * C_DH)),
        "o_w_out": nrm((N_ODD, C_HEADS * 2 * C_DH, D), (C_HEADS * 2 * C_DH) ** -0.5 * beta),
        "o_ln1_g": gain((N_ODD, D)),
        "o_ln1_b": nrm((N_ODD, D), 0.02),
        "o_router": nrm((N_ODD, D, N_EXPERTS), D ** -0.5),
        "o_exp_wg": nrm((N_ODD, N_EXPERTS, D, EXPERT_DIM), D ** -0.5),
        "o_exp_wu": nrm((N_ODD, N_EXPERTS, D, EXPERT_DIM), D ** -0.5),
        "o_exp_wd": nrm((N_ODD, N_EXPERTS, EXPERT_DIM, D), EXPERT_DIM ** -0.5 * beta),
        "o_ln2_g": gain((N_ODD, D)),
        "o_ln2_b": nrm((N_ODD, D), 0.02),
        "ple_w_proj": nrm((DEPTH, PLE_DIM, D), PLE_DIM ** -0.5),
        "ple_w_gate": nrm((DEPTH, D, D), D ** -0.5),
        "ple_b_gate": nrm((DEPTH, D), 0.02),
    }


def reference(x, p,
              e_w_in, e_a_ln_g, e_a_ln_b, e_a_ws, e_a_bs, e_b_w_gate_up, e_b_b_gate, e_b_norm_g,
              e_w_out, e_ln1_g, e_ln1_b, e_ffn_wg, e_ffn_wu, e_ffn_wd, e_ln2_g, e_ln2_b,
              o_w_qkv, o_lam_q1, o_lam_k1, o_lam_q2, o_lam_k2, o_subln_g, o_w_out, o_ln1_g, o_ln1_b,
              o_router, o_exp_wg, o_exp_wu, o_exp_wd, o_ln2_g, o_ln2_b,
              ple_w_proj, ple_w_gate, ple_b_gate):
    for i in range(DEPTH):
        j = i // 2
        if i % 2 == 0:
            m = even_mixer(x, e_w_in[j], e_a_ln_g[j], e_a_ln_b[j], e_a_ws[j], e_a_bs[j],
                           e_b_w_gate_up[j], e_b_b_gate[j], e_b_norm_g[j], e_w_out[j])
            x = layer_norm(DEEPNORM_ALPHA * x + m, e_ln1_g[j], e_ln1_b[j])
            f = swiglu(x, e_ffn_wg[j], e_ffn_wu[j], e_ffn_wd[j])
            x = layer_norm(DEEPNORM_ALPHA * x + f, e_ln2_g[j], e_ln2_b[j])
        else:
            lambda_init = 0.8 - 0.6 * math.exp(-0.3 * i)
            m = diff_attention(x, o_w_qkv[j], o_lam_q1[j], o_lam_k1[j], o_lam_q2[j], o_lam_k2[j],
                               o_subln_g[j], o_w_out[j], lambda_init)
            x = layer_norm(DEEPNORM_ALPHA * x + m, o_ln1_g[j], o_ln1_b[j])
            f = moe_swiglu(x, o_router[j], o_exp_wg[j], o_exp_wu[j], o_exp_wd[j])
            x = layer_norm(DEEPNORM_ALPHA * x + f, o_ln2_g[j], o_ln2_b[j])
        x = per_layer_embed(x, p[i], ple_w_proj[i], ple_w_gate[i], ple_b_gate[i])
    return x
```

```python
import functools
import math

import jax
import jax.numpy as jnp
from jax import lax
from jax.experimental import pallas as pl
from jax.experimental.pallas import tpu as pltpu

F32 = jnp.float32
BF16 = jnp.bfloat16

DEPTH = 2
DEEPNORM_ALPHA = (2.0 * DEPTH) ** 0.25
LN_EPS = 1e-5
A_CHUNK = 128
A_GROUPS = 8
B_HEADS = 4
B_CHUNK = 64
B_TAU = 16.0
C_HEADS = 8
N_EXPERTS = 8
TOP_K = 2
LANES = 128
VMEM_LIMIT = 56 * 1024 * 1024

NT_DIMS = (((1,), (1,)), ((), ()))
TN_DIMS = (((0,), (0,)), ((), ()))


def _dot(a, b):
    return jnp.dot(a, b, preferred_element_type=F32)


def _layer_norm(x, g, b):
    mu = jnp.mean(x, axis=-1, keepdims=True)
    xc = x - mu
    var = jnp.mean(xc * xc, axis=-1, keepdims=True)
    return xc * lax.rsqrt(var + LN_EPS) * g + b


def _sigmoid(x):
    return 1.0 / (1.0 + jnp.exp(-x))


def _split3(a):
    hi = a.astype(BF16)
    r1 = a - hi.astype(F32)
    mid = r1.astype(BF16)
    lo = (r1 - mid.astype(F32)).astype(BF16)
    return hi, mid, lo


def _const_spec(shape):
    zeros = (0,) * len(shape)
    return pl.BlockSpec(shape, lambda *_: zeros, pipeline_mode=pl.Buffered(1))


def _params(n_axes):
    return pltpu.CompilerParams(dimension_semantics=("arbitrary",) * n_axes,
                                vmem_limit_bytes=VMEM_LIMIT)


def _even_mixer_kernel(x_ref, w_in_ref, a_g_ref, a_b_ref, wcat_ref, abias_ref, mstack_ref,
                       w_up_ref, b_gate_ref, ng_ref, w_out_ref, ln_g_ref, ln_b_ref,
                       o_ref, state_ref, *, tile, aw, dkh, dvh):
    @pl.when(pl.program_id(1) == 0)
    def _():
        state_ref[...] = jnp.zeros_like(state_ref)

    hk = B_HEADS * dkh
    hv = B_HEADS * dvh
    x = x_ref[...]
    z = _dot(x.astype(BF16), w_in_ref[...])

    u = jax.nn.gelu(z[:, 0:aw])
    v = _layer_norm(jax.nn.gelu(z[:, aw:2 * aw]), a_g_ref[...], a_b_ref[...])
    gd = aw // A_GROUPS
    rows = lax.broadcasted_iota(jnp.int32, (A_CHUNK, A_GROUPS * A_CHUNK), 0)
    cols = lax.broadcasted_iota(jnp.int32, (A_CHUNK, A_GROUPS * A_CHUNK), 1)
    wcat = jnp.where((cols % A_CHUNK) <= rows, wcat_ref[...], 0.0).astype(BF16)
    r_bd = lax.broadcasted_iota(jnp.int32, (A_GROUPS * A_CHUNK, aw), 0) // A_CHUNK
    c_bd = lax.broadcasted_iota(jnp.int32, (A_GROUPS * A_CHUNK, aw), 1) // gd
    mask_bd = r_bd == c_bd
    ya_parts = []
    for c in range(tile // A_CHUNK):
        sl = slice(c * A_CHUNK, (c + 1) * A_CHUNK)
        v_rep = jnp.concatenate([v[sl]] * A_GROUPS, axis=0)
        v_bd = jnp.where(mask_bd, v_rep, 0.0).astype(BF16)
        sg = _dot(wcat, v_bd) + abias_ref[...]
        ya_parts.append(u[sl] * sg)
    ya = jnp.concatenate(ya_parts, axis=0)

    o0 = 2 * aw
    q = z[:, o0:o0 + hk] * (dkh ** -0.5)
    k = z[:, o0 + hk:o0 + 2 * hk]
    vv = z[:, o0 + 2 * hk:o0 + 2 * hk + hv]
    r = z[:, o0 + 2 * hk + hv:o0 + 2 * hk + 2 * hv]
    g_low = z[:, o0 + 2 * hk + 2 * hv:]
    pre = _dot(g_low.astype(BF16), w_up_ref[...]) + b_gate_ref[...]
    log_a = (jnp.minimum(pre, 0.0) - jnp.log1p(jnp.exp(-jnp.abs(pre)))) * (1.0 / B_TAU)
    la_hi, la_mid, la_lo = _split3(log_a)
    ms = mstack_ref[...]
    cs = _dot(ms, la_hi) + _dot(ms, la_mid) + _dot(ms, la_lo)
    b_cum = cs[0:tile]
    d_mid = cs[tile:2 * tile]
    d_last = cs[2 * tile:3 * tile]
    qe = (q * jnp.exp(d_mid)).astype(BF16)
    ke = k * jnp.exp(-d_mid)
    kd = (k * jnp.exp(d_last)).astype(BF16)
    qb = (q * jnp.exp(b_cum)).astype(BF16)
    dec = jnp.exp(b_cum + d_last)
    vvb = vv.astype(BF16)

    mask_kk = (lax.broadcasted_iota(jnp.int32, (hk, hk), 0) // dkh
               == lax.broadcasted_iota(jnp.int32, (hk, hk), 1) // dkh)
    mask_vbd = (lax.broadcasted_iota(jnp.int32, (hk, hv), 0) // dkh
                == lax.broadcasted_iota(jnp.int32, (hk, hv), 1) // dvh)
    mask_st = (lax.broadcasted_iota(jnp.int32, (hv, hk), 0) // dvh
               == lax.broadcasted_iota(jnp.int32, (hv, hk), 1) // dkh)
    causal = ((lax.broadcasted_iota(jnp.int32, (B_CHUNK, hk), 1) % B_CHUNK)
              <= lax.broadcasted_iota(jnp.int32, (B_CHUNK, hk), 0))

    st = state_ref[...]
    o_parts = []
    for c in range(tile // B_CHUNK):
        sl = slice(c * B_CHUNK, (c + 1) * B_CHUNK)
        ke_bd = jnp.where(mask_kk, jnp.concatenate([ke[sl]] * B_HEADS, axis=0), 0.0).astype(BF16)
        s_cat = lax.dot_general(qe[sl], ke_bd, NT_DIMS, preferred_element_type=F32)
        s_cat = jnp.where(causal, s_cat, 0.0).astype(BF16)
        v_bd = jnp.where(mask_vbd, jnp.concatenate([vv[sl]] * B_HEADS, axis=0), 0.0).astype(BF16)
        o_c = _dot(s_cat, v_bd) + lax.dot_general(qb[sl], st.astype(BF16), NT_DIMS,
                                                   preferred_element_type=F32)
        kv_t = lax.dot_general(vvb[sl], kd[sl], TN_DIMS, preferred_element_type=F32)
        st = dec[c * B_CHUNK:c * B_CHUNK + 1] * st + jnp.where(mask_st, kv_t, 0.0)
        o_parts.append(o_c)
    state_ref[...] = st
    o = jnp.concatenate(o_parts, axis=0)

    yb_parts = []
    for h in range(B_HEADS):
        oh = o[:, h * dvh:(h + 1) * dvh]
        msq = jnp.mean(oh * oh, axis=-1, keepdims=True)
        yb_parts.append(oh * lax.rsqrt(msq + LN_EPS))
    yb = jnp.concatenate(yb_parts, axis=1) * ng_ref[...] * (r * _sigmoid(r))

    y_cat = jnp.concatenate([ya, yb], axis=1).astype(BF16)
    m = _dot(y_cat, w_out_ref[...])
    o_ref[...] = _layer_norm(DEEPNORM_ALPHA * x + m, ln_g_ref[...], ln_b_ref[...])


def _gla_cumsum_matrices(tile):
    i = jnp.arange(tile)[:, None]
    j = jnp.arange(tile)[None, :]
    same = (i // B_CHUNK) == (j // B_CHUNK)
    m_cum = same & (j <= i)
    m_mid = same & (j <= (i // B_CHUNK) * B_CHUNK + B_CHUNK // 2 - 1)
    m_last = same
    f = lambda t: t.astype(F32)
    return jnp.concatenate([f(m_cum), f(m_cum) - f(m_mid), f(m_last) - f(m_cum)], axis=0).astype(BF16)


def _even_mixer(x, w_in, a_ln_g, a_ln_b, a_ws, a_bs, w_gate_up, b_gate, norm_g, w_out, ln_g, ln_b,
                *, tile=256):
    bn, s, d = x.shape
    aw = a_ln_g.shape[0]
    hk = w_gate_up.shape[1]
    dkh = hk // B_HEADS
    dvh = norm_g.shape[0]
    hv = B_HEADS * dvh
    rank = w_gate_up.shape[0]
    main = 2 * aw + 2 * hk + 2 * hv
    gd = aw // A_GROUPS
    w_in_p = jnp.concatenate([w_in[:, :main], jnp.pad(w_in[:, main:], ((0, 0), (0, LANES - rank)))],
                             axis=1).astype(BF16)
    w_up_p = jnp.pad(w_gate_up, ((0, LANES - rank), (0, 0))).astype(BF16)
    wcat = jnp.transpose(a_ws, (1, 0, 2)).reshape(A_CHUNK, A_GROUPS * A_CHUNK)
    abias = jnp.repeat(a_bs.T, gd, axis=1)
    mstack = _gla_cumsum_matrices(tile)
    ng = jnp.tile(norm_g, B_HEADS)[None, :]
    row = lambda t: t[None, :]
    kern = functools.partial(_even_mixer_kernel, tile=tile, aw=aw, dkh=dkh, dvh=dvh)
    tile_spec = pl.BlockSpec((None, tile, d), lambda b, i: (b, i, 0))
    return pl.pallas_call(
        kern,
        grid=(bn, s // tile),
        in_specs=[tile_spec, _const_spec(w_in_p.shape), _const_spec((1, aw)), _const_spec((1, aw)),
                  _const_spec(wcat.shape), _const_spec(abias.shape), _const_spec(mstack.shape),
                  _const_spec(w_up_p.shape), _const_spec((1, hk)), _const_spec((1, hv)),
                  _const_spec(w_out.shape), _const_spec((1, d)), _const_spec((1, d))],
        out_specs=tile_spec,
        out_shape=jax.ShapeDtypeStruct(x.shape, F32),
        scratch_shapes=[pltpu.VMEM((hv, hk), F32)],
        compiler_params=_params(2),
        name="even_mixer",
    )(x, w_in_p, row(a_ln_g), row(a_ln_b), wcat, abias, mstack, w_up_p, row(b_gate), ng,
      w_out.astype(BF16), row(ln_g), row(ln_b))


def _ple(y, p, wpg_ref, bpg_ref, wpp_ref):
    gate = _sigmoid(_dot(y.astype(BF16), wpg_ref[...]) + bpg_ref[...])
    return y + gate * _dot(p.astype(BF16), wpp_ref[...])


def _ffn_ple_kernel(x_ref, p_ref, wg_ref, wu_ref, wd_ref, ln_g_ref, ln_b_ref, wpg_ref, bpg_ref,
                    wpp_ref, o_ref, *, chunks):
    x = x_ref[...]
    xb = x.astype(BF16)
    acc = None
    for lo, hi in chunks:
        g = _dot(xb, wg_ref[:, lo:hi])
        u = _dot(xb, wu_ref[:, lo:hi])
        h = (g * _sigmoid(g) * u).astype(BF16)
        part = _dot(h, wd_ref[lo:hi, :])
        acc = part if acc is None else acc + part
    y = _layer_norm(DEEPNORM_ALPHA * x + acc, ln_g_ref[...], ln_b_ref[...])
    o_ref[...] = _ple(y, p_ref[...], wpg_ref, bpg_ref, wpp_ref)


def _ffn_ple(x2d, p2d, wg, wu, wd, ln_g, ln_b, wpg, bpg, wpp, *, tile=512, fchunk=1024):
    n, d = x2d.shape
    f = wg.shape[1]
    pd = p2d.shape[1]
    chunks = tuple((lo, min(lo + fchunk, f)) for lo in range(0, f, fchunk))
    row = lambda t: t[None, :]
    return pl.pallas_call(
        functools.partial(_ffn_ple_kernel, chunks=chunks),
        grid=(n // tile,),
        in_specs=[pl.BlockSpec((tile, d), lambda i: (i, 0)), pl.BlockSpec((tile, pd), lambda i: (i, 0)),
                  _const_spec((d, f)), _const_spec((d, f)), _const_spec((f, d)),
                  _const_spec((1, d)), _const_spec((1, d)),
                  _const_spec((d, d)), _const_spec((1, d)), _const_spec((pd, d))],
        out_specs=pl.BlockSpec((tile, d), lambda i: (i, 0)),
        out_shape=jax.ShapeDtypeStruct((n, d), F32),
        compiler_params=_params(1),
        name="ffn_ple",
    )(x2d, p2d, wg.astype(BF16), wu.astype(BF16), wd.astype(BF16), row(ln_g), row(ln_b),
      wpg.astype(BF16), row(bpg), wpp.astype(BF16))


def _qkv_kernel(x_ref, w_ref, q_ref, k_ref, vt_ref, *, hd, scale):
    z = _dot(x_ref[...].astype(BF16), w_ref[...])
    q_ref[...] = (z[:, 0:hd] * scale).astype(BF16)
    k_ref[...] = z[:, hd:2 * hd].astype(BF16)
    vt_ref[...] = z[:, 2 * hd:3 * hd].T.astype(BF16)


def _qkv(x, w_qkv, *, dh, tile=512):
    bn, s, d = x.shape
    hd = w_qkv.shape[1] // 3
    row_spec = pl.BlockSpec((None, tile, hd), lambda b, i: (b, i, 0))
    return pl.pallas_call(
        functools.partial(_qkv_kernel, hd=hd, scale=dh ** -0.5),
        grid=(bn, s // tile),
        in_specs=[pl.BlockSpec((None, tile, d), lambda b, i: (b, i, 0)), _const_spec(w_qkv.shape)],
        out_specs=[row_spec, row_spec, pl.BlockSpec((None, hd, tile), lambda b, i: (b, 0, i))],
        out_shape=[jax.ShapeDtypeStruct((bn, s, hd), BF16), jax.ShapeDtypeStruct((bn, s, hd), BF16),
                   jax.ShapeDtypeStruct((bn, hd, s), BF16)],
        compiler_params=_params(2),
        name="qkv_proj",
    )(x, w_qkv.astype(BF16))


def _diff_attn_kernel(qi_tab, ki_tab, q_ref, k_ref, vt_ref, lam_ref, g_ref, o_ref,
                      m_ref, l_ref, acc_ref, *, blk, dh, lambda_init):
    t = pl.program_id(2)
    qi = qi_tab[t]
    ki = ki_tab[t]

    @pl.when(ki == 0)
    def _():
        m_ref[...] = jnp.full_like(m_ref, -1e30)
        l_ref[...] = jnp.zeros_like(l_ref)
        acc_ref[...] = jnp.zeros_like(acc_ref)

    q = q_ref[...]
    lane = lax.broadcasted_iota(jnp.int32, q.shape, 1)
    zero = jnp.zeros_like(q)
    qm = jnp.concatenate([jnp.where(lane < dh, q, zero), jnp.where(lane >= dh, q, zero)], axis=0)
    s_t = lax.dot_general(k_ref[...], qm, NT_DIMS, preferred_element_type=F32)
    kpos = ki * blk + lax.broadcasted_iota(jnp.int32, s_t.shape, 0)
    qpos = qi * blk + lax.broadcasted_iota(jnp.int32, s_t.shape, 1) % blk
    s_t = jnp.where(kpos <= qpos, s_t, -1e30)
    m_prev = m_ref[...]
    m_new = jnp.maximum(m_prev, jnp.max(s_t, axis=0, keepdims=True))
    corr = jnp.exp(m_prev - m_new)
    p = jnp.exp(s_t - m_new)
    l_ref[...] = corr * l_ref[...] + jnp.sum(p, axis=0, keepdims=True)
    acc_ref[...] = corr * acc_ref[...] + _dot(vt_ref[...], p.astype(BF16))
    m_ref[...] = m_new

    @pl.when(ki == qi)
    def _():
        a = acc_ref[...] * (1.0 / l_ref[...])
        lv = lam_ref[...]
        lam = (jnp.exp(jnp.sum(lv[0:1] * lv[1:2], axis=-1, keepdims=True))
               - jnp.exp(jnp.sum(lv[2:3] * lv[3:4], axis=-1, keepdims=True)) + lambda_init)
        o_t = a[:, 0:blk] - lam * a[:, blk:2 * blk]
        msq = jnp.mean(o_t * o_t, axis=0, keepdims=True)
        o_t = o_t * lax.rsqrt(msq + LN_EPS)
        o_ref[...] = (o_t.T * g_ref[...] * (1.0 - lambda_init)).astype(BF16)


def _diff_attn(q, k, vt, lam_q1, lam_k1, lam_q2, lam_k2, subln_g, lambda_init, *, blk=512):
    bn, s, hd = q.shape
    dh = lam_q1.shape[0]
    heads = hd // (2 * dh)
    nb = s // blk
    pairs = [(qi, ki) for qi in range(nb) for ki in range(qi + 1)]
    qi_tab = jnp.asarray([p[0] for p in pairs], jnp.int32)
    ki_tab = jnp.asarray([p[1] for p in pairs], jnp.int32)
    lamv = jnp.zeros((8, LANES), F32).at[0:4, 0:dh].set(jnp.stack([lam_q1, lam_k1, lam_q2, lam_k2]))
    grid_spec = pltpu.PrefetchScalarGridSpec(
        num_scalar_prefetch=2,
        grid=(bn, heads, len(pairs)),
        in_specs=[
            pl.BlockSpec((None, blk, 2 * dh), lambda b, h, t, qt, kt: (b, qt[t], h)),
            pl.BlockSpec((None, blk, 2 * dh), lambda b, h, t, qt, kt: (b, kt[t], h)),
            pl.BlockSpec((None, 2 * dh, blk), lambda b, h, t, qt, kt: (b, h, kt[t])),
            pl.BlockSpec((8, LANES), lambda b, h, t, qt, kt: (0, 0)),
            pl.BlockSpec((1, 2 * dh), lambda b, h, t, qt, kt: (0, 0)),
        ],
        out_specs=pl.BlockSpec((None, blk, 2 * dh), lambda b, h, t, qt, kt: (b, qt[t], h)),
        scratch_shapes=[pltpu.VMEM((1, 2 * blk), F32), pltpu.VMEM((1, 2 * blk), F32),
                        pltpu.VMEM((2 * dh, 2 * blk), F32)],
    )
    return pl.pallas_call(
        functools.partial(_diff_attn_kernel, blk=blk, dh=dh, lambda_init=lambda_init),
        grid_spec=grid_spec,
        out_shape=jax.ShapeDtypeStruct((bn, s, hd), BF16),
        compiler_params=_params(3),
        name="diff_attn",
    )(qi_tab, ki_tab, q, k, vt, lamv, subln_g[None, :])


def _attn_out_kernel(o_ref, x_ref, w_ref, ln_g_ref, ln_b_ref, wr_hi_ref, wr_lo_ref,
                     y_ref, yb_ref, logit_ref):
    m = _dot(o_ref[...], w_ref[...])
    y = _layer_norm(DEEPNORM_ALPHA * x_ref[...] + m, ln_g_ref[...], ln_b_ref[...])
    y_ref[...] = y
    yb_ref[...] = y.astype(BF16)
    y_hi = y.astype(BF16)
    y_lo = (y - y_hi.astype(F32)).astype(BF16)
    logit_ref[...] = (_dot(y_hi, wr_hi_ref[...]) + _dot(y_lo, wr_hi_ref[...])
                      + _dot(y_hi, wr_lo_ref[...]))


def _attn_out(o2d, x2d, w_out, ln_g, ln_b, w_router, *, tile=512):
    n, d = x2d.shape
    wr = jnp.pad(w_router, ((0, 0), (0, LANES - w_router.shape[1])))
    wr_hi = wr.astype(BF16)
    wr_lo = (wr - wr_hi.astype(F32)).astype(BF16)
    row = lambda t: t[None, :]
    tspec = pl.BlockSpec((tile, d), lambda i: (i, 0))
    return pl.pallas_call(
        _attn_out_kernel,
        grid=(n // tile,),
        in_specs=[tspec, tspec, _const_spec((d, d)), _const_spec((1, d)), _const_spec((1, d)),
                  _const_spec((d, LANES)), _const_spec((d, LANES))],
        out_specs=[tspec, tspec, pl.BlockSpec((tile, LANES), lambda i: (i, 0))],
        out_shape=[jax.ShapeDtypeStruct((n, d), F32), jax.ShapeDtypeStruct((n, d), BF16),
                   jax.ShapeDtypeStruct((n, LANES), F32)],
        compiler_params=_params(1),
        name="attn_out_router",
    )(o2d, x2d, w_out.astype(BF16), row(ln_g), row(ln_b), wr_hi, wr_lo)


def _moe_kernel(blk_e, n_used, xs_ref, gate_ref, wg_ref, wu_ref, wd_ref, y_ref, acc_ref, *, nf):
    m = pl.program_id(0)
    f = pl.program_id(1)

    @pl.when(m < n_used[0])
    def _():
        x = xs_ref[...]
        g = _dot(x, wg_ref[...])
        u = _dot(x, wu_ref[...])
        h = (g * _sigmoid(g) * u).astype(BF16)
        part = _dot(h, wd_ref[...])

        @pl.when(f == 0)
        def _():
            acc_ref[...] = part

        @pl.when(f > 0)
        def _():
            acc_ref[...] += part

        @pl.when(f == nf - 1)
        def _():
            y_ref[...] = acc_ref[...] * gate_ref[...]


def _moe_grouped(xs, row_gate, blk_e, n_used, wg, wu, wd, *, tm, tf):
    n_rows, d = xs.shape
    fdim = wg.shape[2]
    nf = fdim // tf
    n_blocks = n_rows // tm

    def row_map(m, f, be, nu):
        return (jnp.minimum(m, nu[0] - 1), 0)

    def f_eff(m, f, nu):
        return jnp.where(m < nu[0], f, nf - 1)

    grid_spec = pltpu.PrefetchScalarGridSpec(
        num_scalar_prefetch=2,
        grid=(n_blocks, nf),
        in_specs=[
            pl.BlockSpec((tm, d), row_map),
            pl.BlockSpec((tm, 1), row_map),
            pl.BlockSpec((None, d, tf), lambda m, f, be, nu: (be[m], 0, f_eff(m, f, nu))),
            pl.BlockSpec((None, d, tf), lambda m, f, be, nu: (be[m], 0, f_eff(m, f, nu))),
            pl.BlockSpec((None, tf, d), lambda m, f, be, nu: (be[m], f_eff(m, f, nu), 0)),
        ],
        out_specs=pl.BlockSpec((tm, d), row_map),
        scratch_shapes=[pltpu.VMEM((tm, d), F32)],
    )
    return pl.pallas_call(
        functools.partial(_moe_kernel, nf=nf),
        grid_spec=grid_spec,
        out_shape=jax.ShapeDtypeStruct((n_rows, d), F32),
        compiler_params=_params(2),
        name="moe_grouped",
    )(blk_e, n_used, xs, row_gate, wg, wu, wd)


def _route(logits, *, tm):
    n = logits.shape[0]
    top_val, top_idx = lax.top_k(logits, TOP_K)
    gates = jax.nn.softmax(top_val, axis=-1)
    flat_e = top_idx.reshape(-1)
    flat_tok = jnp.repeat(jnp.arange(n, dtype=jnp.int32), TOP_K)
    onehot = (flat_e[:, None] == jnp.arange(N_EXPERTS)[None, :]).astype(jnp.int32)
    ranks = jnp.cumsum(onehot, axis=0)
    counts = ranks[-1]
    rank = jnp.sum(ranks * onehot, axis=1) - 1
    padded = (counts + tm - 1) // tm * tm
    pend = jnp.cumsum(padded)
    pstart = pend - padded
    dest = (pstart[flat_e] + rank).astype(jnp.int32)
    n_rows = n * TOP_K + N_EXPERTS * tm
    n_blocks = n_rows // tm
    row_tok = jnp.zeros((n_rows,), jnp.int32).at[dest].set(flat_tok)
    row_gate = jnp.zeros((n_rows,), F32).at[dest].set(gates.reshape(-1))
    n_used = (pend[-1] // tm).astype(jnp.int32)
    blk_start = jnp.minimum(jnp.arange(n_blocks, dtype=jnp.int32), n_used - 1) * tm
    blk_e = jnp.minimum(jnp.searchsorted(pend, blk_start, side='right'), N_EXPERTS - 1).astype(jnp.int32)
    return row_tok, row_gate, dest.reshape(n, TOP_K), blk_e, n_used.reshape(1)


def _final_kernel(x_ref, y0_ref, y1_ref, p_ref, ln_g_ref, ln_b_ref, wpg_ref, bpg_ref, wpp_ref, o_ref):
    f = y0_ref[...] + y1_ref[...]
    y = _layer_norm(DEEPNORM_ALPHA * x_ref[...] + f, ln_g_ref[...], ln_b_ref[...])
    o_ref[...] = _ple(y, p_ref[...], wpg_ref, bpg_ref, wpp_ref)


def _final(x2d, y0, y1, p2d, ln_g, ln_b, wpg, bpg, wpp, *, tile=512):
    n, d = x2d.shape
    pd = p2d.shape[1]
    row = lambda t: t[None, :]
    tspec = pl.BlockSpec((tile, d), lambda i: (i, 0))
    return pl.pallas_call(
        _final_kernel,
        grid=(n // tile,),
        in_specs=[tspec, tspec, tspec, pl.BlockSpec((tile, pd), lambda i: (i, 0)),
                  _const_spec((1, d)), _const_spec((1, d)),
                  _const_spec((d, d)), _const_spec((1, d)), _const_spec((pd, d))],
        out_specs=tspec,
        out_shape=jax.ShapeDtypeStruct((n, d), F32),
        compiler_params=_params(1),
        name="moe_combine_ple",
    )(x2d, y0, y1, p2d, row(ln_g), row(ln_b), wpg.astype(BF16), row(bpg), wpp.astype(BF16))


def _even_layer(x, p_i, w_in, a_ln_g, a_ln_b, a_ws, a_bs, w_gate_up, b_gate, norm_g, w_out,
                ln1_g, ln1_b, wg, wu, wd, ln2_g, ln2_b, wpp, wpg, bpg):
    bn, s, d = x.shape
    x1 = _even_mixer(x, w_in, a_ln_g, a_ln_b, a_ws, a_bs, w_gate_up, b_gate, norm_g, w_out, ln1_g, ln1_b)
    x2 = _ffn_ple(x1.reshape(bn * s, d), p_i.reshape(bn * s, -1), wg, wu, wd, ln2_g, ln2_b, wpg, bpg, wpp)
    return x2.reshape(bn, s, d)


def _odd_layer(x, p_i, layer_idx, w_qkv, lam_q1, lam_k1, lam_q2, lam_k2, subln_g, w_out, ln1_g, ln1_b,
               w_router, ewg, ewu, ewd, ln2_g, ln2_b, wpp, wpg, bpg, *, moe_tm=512, moe_tf=1792):
    bn, s, d = x.shape
    n = bn * s
    lambda_init = 0.8 - 0.6 * math.exp(-0.3 * layer_idx)
    q, k, vt = _qkv(x, w_qkv, dh=lam_q1.shape[0])
    o = _diff_attn(q, k, vt, lam_q1, lam_k1, lam_q2, lam_k2, subln_g, lambda_init)
    x3, x3b, logits = _attn_out(o.reshape(n, d), x.reshape(n, d), w_out, ln1_g, ln1_b, w_router)
    row_tok, row_gate, dest, blk_e, n_used = _route(logits[:, :N_EXPERTS], tm=moe_tm)
    xs = jnp.take(x3b, row_tok, axis=0)
    y = _moe_grouped(xs, row_gate[:, None], blk_e, n_used,
                     ewg.astype(BF16), ewu.astype(BF16), ewd.astype(BF16), tm=moe_tm, tf=moe_tf)
    y0 = jnp.take(y, dest[:, 0], axis=0)
    y1 = jnp.take(y, dest[:, 1], axis=0)
    out = _final(x3, y0, y1, p_i.reshape(n, -1), ln2_g, ln2_b, wpg, bpg, wpp)
    return out.reshape(bn, s, d)


def kernel(x, p, e_w_in, e_a_ln_g, e_a_ln_b, e_a_ws, e_a_bs, e_b_w_gate_up, e_b_b_gate, e_b_norm_g, e_w_out, e_ln1_g, e_ln1_b, e_ffn_wg, e_ffn_wu, e_ffn_wd, e_ln2_g, e_ln2_b, o_w_qkv, o_lam_q1, o_lam_k1, o_lam_q2, o_lam_k2, o_subln_g, o_w_out, o_ln1_g, o_ln1_b, o_router, o_exp_wg, o_exp_wu, o_exp_wd, o_ln2_g, o_ln2_b, ple_w_proj, ple_w_gate, ple_b_gate):
    for i in range(DEPTH):
        j = i // 2
        if i % 2 == 0:
            x = _even_layer(x, p[i], e_w_in[j], e_a_ln_g[j], e_a_ln_b[j], e_a_ws[j], e_a_bs[j],
                            e_b_w_gate_up[j], e_b_b_gate[j], e_b_norm_g[j], e_w_out[j],
                            e_ln1_g[j], e_ln1_b[j], e_ffn_wg[j], e_ffn_wu[j], e_ffn_wd[j],
                            e_ln2_g[j], e_ln2_b[j], ple_w_proj[i], ple_w_gate[i], ple_b_gate[i])
        else:
            x = _odd_layer(x, p[i], i, o_w_qkv[j], o_lam_q1[j], o_lam_k1[j], o_lam_q2[j], o_lam_k2[j],
                           o_subln_g[j], o_w_out[j], o_ln1_g[j], o_ln1_b[j], o_router[j],
                           o_exp_wg[j], o_exp_wu[j], o_exp_wd[j], o_ln2_g[j], o_ln2_b[j],
                           ple_w_proj[i], ple_w_gate[i], ple_b_gate[i])
    return x
```

```python
import functools
import math

import jax
import jax.numpy as jnp
from jax import lax
from jax.experimental import pallas as pl
from jax.experimental.pallas import tpu as pltpu

F32 = jnp.float32
BF16 = jnp.bfloat16

DEPTH = 2
DEEPNORM_ALPHA = (2.0 * DEPTH) ** 0.25
LN_EPS = 1e-5
A_CHUNK = 128
A_GROUPS = 8
B_HEADS = 4
B_CHUNK = 64
B_TAU = 16.0
C_HEADS = 8
N_EXPERTS = 8
TOP_K = 2
LANES = 128
ONES_ROWS = 16
LOG2_E = 1.4426950408889634
VMEM_LIMIT = 56 * 1024 * 1024

NT_DIMS = (((1,), (1,)), ((), ()))
TN_DIMS = (((0,), (0,)), ((), ()))


def _dot(a, b):
    return jnp.dot(a, b, preferred_element_type=F32)


def _layer_norm(x, g, b):
    mu = jnp.mean(x, axis=-1, keepdims=True)
    xc = x - mu
    var = jnp.mean(xc * xc, axis=-1, keepdims=True)
    return xc * lax.rsqrt(var + LN_EPS) * g + b


def _sigmoid(x):
    return 1.0 / (1.0 + jnp.exp(-x))


def _split3(a):
    hi = a.astype(BF16)
    r1 = a - hi.astype(F32)
    mid = r1.astype(BF16)
    lo = (r1 - mid.astype(F32)).astype(BF16)
    return hi, mid, lo


def _col_reduce(op, final, x):
    parts = [x[i * 8:(i + 1) * 8] for i in range(x.shape[0] // 8)]
    while len(parts) > 1:
        nxt = [op(parts[i], parts[i + 1]) for i in range(0, len(parts) - 1, 2)]
        if len(parts) % 2:
            nxt.append(parts[-1])
        parts = nxt
    return final(parts[0], axis=0, keepdims=True)


def _const_spec(shape):
    zeros = (0,) * len(shape)
    return pl.BlockSpec(shape, lambda *_: zeros, pipeline_mode=pl.Buffered(1))


def _params(n_axes):
    return pltpu.CompilerParams(dimension_semantics=("arbitrary",) * n_axes,
                                vmem_limit_bytes=VMEM_LIMIT)


def _even_mixer_kernel(x_ref, w_in_ref, a_g_ref, a_b_ref, wcat_ref, abias_ref, mstack_ref,
                       w_up_ref, b_gate_ref, ng_ref, w_out_ref, ln_g_ref, ln_b_ref,
                       o_ref, state_ref, *, tile, aw, dkh, dvh):
    @pl.when(pl.program_id(1) == 0)
    def _():
        state_ref[...] = jnp.zeros_like(state_ref)

    hk = B_HEADS * dkh
    hv = B_HEADS * dvh
    x = x_ref[...]
    z = _dot(x.astype(BF16), w_in_ref[...])

    u = jax.nn.gelu(z[:, 0:aw])
    v = _layer_norm(jax.nn.gelu(z[:, aw:2 * aw]), a_g_ref[...], a_b_ref[...])
    gd = aw // A_GROUPS
    rows = lax.broadcasted_iota(jnp.int32, (A_CHUNK, A_GROUPS * A_CHUNK), 0)
    cols = lax.broadcasted_iota(jnp.int32, (A_CHUNK, A_GROUPS * A_CHUNK), 1)
    wcat = jnp.where((cols % A_CHUNK) <= rows, wcat_ref[...], 0.0).astype(BF16)
    r_bd = lax.broadcasted_iota(jnp.int32, (A_GROUPS * A_CHUNK, aw), 0) // A_CHUNK
    c_bd = lax.broadcasted_iota(jnp.int32, (A_GROUPS * A_CHUNK, aw), 1) // gd
    mask_bd = r_bd == c_bd
    ya_parts = []
    for c in range(tile // A_CHUNK):
        sl = slice(c * A_CHUNK, (c + 1) * A_CHUNK)
        v_rep = jnp.concatenate([v[sl]] * A_GROUPS, axis=0)
        v_bd = jnp.where(mask_bd, v_rep, 0.0).astype(BF16)
        sg = _dot(wcat, v_bd) + abias_ref[...]
        ya_parts.append(u[sl] * sg)
    ya = jnp.concatenate(ya_parts, axis=0)

    o0 = 2 * aw
    q = z[:, o0:o0 + hk] * (dkh ** -0.5)
    k = z[:, o0 + hk:o0 + 2 * hk]
    vv = z[:, o0 + 2 * hk:o0 + 2 * hk + hv]
    r = z[:, o0 + 2 * hk + hv:o0 + 2 * hk + 2 * hv]
    g_low = z[:, o0 + 2 * hk + 2 * hv:]
    pre = _dot(g_low.astype(BF16), w_up_ref[...]) + b_gate_ref[...]
    log_a = (jnp.minimum(pre, 0.0) - jnp.log1p(jnp.exp(-jnp.abs(pre)))) * (1.0 / B_TAU)
    la_hi, la_mid, la_lo = _split3(log_a)
    ms = mstack_ref[...]
    cs = _dot(ms, la_hi) + _dot(ms, la_mid) + _dot(ms, la_lo)
    b_cum = cs[0:tile]
    d_mid = cs[tile:2 * tile]
    d_last = cs[2 * tile:3 * tile]
    qe = (q * jnp.exp(d_mid)).astype(BF16)
    ke = k * jnp.exp(-d_mid)
    kd = (k * jnp.exp(d_last)).astype(BF16)
    qb = (q * jnp.exp(b_cum)).astype(BF16)
    dec = jnp.exp(b_cum + d_last)
    vvb = vv.astype(BF16)

    mask_kk = (lax.broadcasted_iota(jnp.int32, (hk, hk), 0) // dkh
               == lax.broadcasted_iota(jnp.int32, (hk, hk), 1) // dkh)
    mask_vbd = (lax.broadcasted_iota(jnp.int32, (hk, hv), 0) // dkh
                == lax.broadcasted_iota(jnp.int32, (hk, hv), 1) // dvh)
    mask_st = (lax.broadcasted_iota(jnp.int32, (hv, hk), 0) // dvh
               == lax.broadcasted_iota(jnp.int32, (hv, hk), 1) // dkh)
    causal = ((lax.broadcasted_iota(jnp.int32, (B_CHUNK, hk), 1) % B_CHUNK)
              <= lax.broadcasted_iota(jnp.int32, (B_CHUNK, hk), 0))

    st = state_ref[...]
    o_parts = []
    for c in range(tile // B_CHUNK):
        sl = slice(c * B_CHUNK, (c + 1) * B_CHUNK)
        ke_bd = jnp.where(mask_kk, jnp.concatenate([ke[sl]] * B_HEADS, axis=0), 0.0).astype(BF16)
        s_cat = lax.dot_general(qe[sl], ke_bd, NT_DIMS, preferred_element_type=F32)
        s_cat = jnp.where(causal, s_cat, 0.0).astype(BF16)
        v_bd = jnp.where(mask_vbd, jnp.concatenate([vv[sl]] * B_HEADS, axis=0), 0.0).astype(BF16)
        o_c = _dot(s_cat, v_bd) + lax.dot_general(qb[sl], st.astype(BF16), NT_DIMS,
                                                   preferred_element_type=F32)
        kv_t = lax.dot_general(vvb[sl], kd[sl], TN_DIMS, preferred_element_type=F32)
        st = dec[c * B_CHUNK:c * B_CHUNK + 1] * st + jnp.where(mask_st, kv_t, 0.0)
        o_parts.append(o_c)
    state_ref[...] = st
    o = jnp.concatenate(o_parts, axis=0)

    yb_parts = []
    for h in range(B_HEADS):
        oh = o[:, h * dvh:(h + 1) * dvh]
        msq = jnp.mean(oh * oh, axis=-1, keepdims=True)
        yb_parts.append(oh * lax.rsqrt(msq + LN_EPS))
    yb = jnp.concatenate(yb_parts, axis=1) * ng_ref[...] * (r * _sigmoid(r))

    y_cat = jnp.concatenate([ya, yb], axis=1).astype(BF16)
    m = _dot(y_cat, w_out_ref[...])
    o_ref[...] = _layer_norm(DEEPNORM_ALPHA * x + m, ln_g_ref[...], ln_b_ref[...])


def _gla_cumsum_matrices(tile):
    i = jnp.arange(tile)[:, None]
    j = jnp.arange(tile)[None, :]
    same = (i // B_CHUNK) == (j // B_CHUNK)
    m_cum = same & (j <= i)
    m_mid = same & (j <= (i // B_CHUNK) * B_CHUNK + B_CHUNK // 2 - 1)
    m_last = same
    f = lambda t: t.astype(F32)
    return jnp.concatenate([f(m_cum), f(m_cum) - f(m_mid), f(m_last) - f(m_cum)], axis=0).astype(BF16)


def _even_mixer(x, w_in, a_ln_g, a_ln_b, a_ws, a_bs, w_gate_up, b_gate, norm_g, w_out, ln_g, ln_b,
                *, tile=256):
    bn, s, d = x.shape
    aw = a_ln_g.shape[0]
    hk = w_gate_up.shape[1]
    dkh = hk // B_HEADS
    dvh = norm_g.shape[0]
    hv = B_HEADS * dvh
    rank = w_gate_up.shape[0]
    main = 2 * aw + 2 * hk + 2 * hv
    gd = aw // A_GROUPS
    w_in_p = jnp.concatenate([w_in[:, :main], jnp.pad(w_in[:, main:], ((0, 0), (0, LANES - rank)))],
                             axis=1).astype(BF16)
    w_up_p = jnp.pad(w_gate_up, ((0, LANES - rank), (0, 0))).astype(BF16)
    wcat = jnp.transpose(a_ws, (1, 0, 2)).reshape(A_CHUNK, A_GROUPS * A_CHUNK)
    abias = jnp.repeat(a_bs.T, gd, axis=1)
    mstack = _gla_cumsum_matrices(tile)
    ng = jnp.tile(norm_g, B_HEADS)[None, :]
    row = lambda t: t[None, :]
    kern = functools.partial(_even_mixer_kernel, tile=tile, aw=aw, dkh=dkh, dvh=dvh)
    tile_spec = pl.BlockSpec((None, tile, d), lambda b, i: (b, i, 0))
    return pl.pallas_call(
        kern,
        grid=(bn, s // tile),
        in_specs=[tile_spec, _const_spec(w_in_p.shape), _const_spec((1, aw)), _const_spec((1, aw)),
                  _const_spec(wcat.shape), _const_spec(abias.shape), _const_spec(mstack.shape),
                  _const_spec(w_up_p.shape), _const_spec((1, hk)), _const_spec((1, hv)),
                  _const_spec(w_out.shape), _const_spec((1, d)), _const_spec((1, d))],
        out_specs=tile_spec,
        out_shape=jax.ShapeDtypeStruct(x.shape, F32),
        scratch_shapes=[pltpu.VMEM((hv, hk), F32)],
        compiler_params=_params(2),
        name="even_mixer",
    )(x, w_in_p, row(a_ln_g), row(a_ln_b), wcat, abias, mstack, w_up_p, row(b_gate), ng,
      w_out.astype(BF16), row(ln_g), row(ln_b))


def _ple(y, p, wpg_ref, bpg_ref, wpp_ref):
    gate = _sigmoid(_dot(y.astype(BF16), wpg_ref[...]) + bpg_ref[...])
    return y + gate * _dot(p.astype(BF16), wpp_ref[...])


def _ffn_ple_kernel(x_ref, p_ref, wg_ref, wu_ref, wd_ref, ln_g_ref, ln_b_ref, wpg_ref, bpg_ref,
                    wpp_ref, o_ref, *, chunks):
    x = x_ref[...]
    xb = x.astype(BF16)
    acc = None
    for lo, hi in chunks:
        g = _dot(xb, wg_ref[:, lo:hi])
        u = _dot(xb, wu_ref[:, lo:hi])
        h = (g * _sigmoid(g) * u).astype(BF16)
        part = _dot(h, wd_ref[lo:hi, :])
        acc = part if acc is None else acc + part
    y = _layer_norm(DEEPNORM_ALPHA * x + acc, ln_g_ref[...], ln_b_ref[...])
    o_ref[...] = _ple(y, p_ref[...], wpg_ref, bpg_ref, wpp_ref)


def _ffn_ple(x2d, p2d, wg, wu, wd, ln_g, ln_b, wpg, bpg, wpp, *, tile=512, fchunk=1024):
    n, d = x2d.shape
    f = wg.shape[1]
    pd = p2d.shape[1]
    chunks = tuple((lo, min(lo + fchunk, f)) for lo in range(0, f, fchunk))
    row = lambda t: t[None, :]
    return pl.pallas_call(
        functools.partial(_ffn_ple_kernel, chunks=chunks),
        grid=(n // tile,),
        in_specs=[pl.BlockSpec((tile, d), lambda i: (i, 0)), pl.BlockSpec((tile, pd), lambda i: (i, 0)),
                  _const_spec((d, f)), _const_spec((d, f)), _const_spec((f, d)),
                  _const_spec((1, d)), _const_spec((1, d)),
                  _const_spec((d, d)), _const_spec((1, d)), _const_spec((pd, d))],
        out_specs=pl.BlockSpec((tile, d), lambda i: (i, 0)),
        out_shape=jax.ShapeDtypeStruct((n, d), F32),
        compiler_params=_params(1),
        name="ffn_ple",
    )(x2d, p2d, wg.astype(BF16), wu.astype(BF16), wd.astype(BF16), row(ln_g), row(ln_b),
      wpg.astype(BF16), row(bpg), wpp.astype(BF16))


def _qkv_kernel(x_ref, w_ref, q_ref, k_ref, vt_ref, *, hd, scale):
    z = _dot(x_ref[...].astype(BF16), w_ref[...])
    q_ref[...] = (z[:, 0:hd] * scale).astype(BF16)
    k_ref[...] = z[:, hd:2 * hd].astype(BF16)
    v_t = z[:, 2 * hd:3 * hd].T.astype(BF16)
    heads, dv_ext, tile = vt_ref.shape
    dv = dv_ext - ONES_ROWS
    ones = jnp.ones((ONES_ROWS, tile), BF16)
    for h in range(heads):
        vt_ref[h] = jnp.concatenate([v_t[h * dv:(h + 1) * dv, :], ones], axis=0)


def _qkv(x, w_qkv, *, dh, tile):
    bn, s, d = x.shape
    hd = w_qkv.shape[1] // 3
    heads = hd // (2 * dh)
    row_spec = pl.BlockSpec((None, tile, hd), lambda b, i: (b, i, 0))
    return pl.pallas_call(
        functools.partial(_qkv_kernel, hd=hd, scale=dh ** -0.5 * LOG2_E),
        grid=(bn, s // tile),
        in_specs=[pl.BlockSpec((None, tile, d), lambda b, i: (b, i, 0)), _const_spec(w_qkv.shape)],
        out_specs=[row_spec, row_spec,
                   pl.BlockSpec((None, heads, None, 2 * dh + ONES_ROWS, tile),
                                lambda b, i: (b, 0, i, 0, 0))],
        out_shape=[jax.ShapeDtypeStruct((bn, s, hd), BF16), jax.ShapeDtypeStruct((bn, s, hd), BF16),
                   jax.ShapeDtypeStruct((bn, heads, s // tile, 2 * dh + ONES_ROWS, tile), BF16)],
        compiler_params=_params(2),
        name="qkv_proj",
    )(x, w_qkv.astype(BF16))


def _diff_attn_kernel(q_ref, k_ref, vt_ref, lam_ref, g_ref, o_ref, m_ref, acc_ref,
                      qm_ref, sa_ref, sb_ref, *, blk, dh, strip, lambda_init):
    qi = pl.program_id(2)
    m_ref[...] = jnp.full_like(m_ref, -1e30)
    acc_ref[...] = jnp.zeros_like(acc_ref)

    q = q_ref[...]
    lane = lax.broadcasted_iota(jnp.int32, q.shape, 1)
    zero = jnp.zeros_like(q)
    qm_ref[0:blk] = jnp.where(lane < dh, q, zero)
    qm_ref[blk:2 * blk] = jnp.where(lane >= dh, q, zero)
    n_strips = 2 * blk // strip

    def scores(ki, s_ref):
        off = pl.multiple_of(ki * blk, blk)
        s_ref[...] = lax.dot_general(k_ref[pl.ds(off, blk), :], qm_ref[...], NT_DIMS,
                                     preferred_element_type=F32)

    def softmax_pv(s_ref, ki, masked):
        vt_blk = vt_ref[ki]
        for j in range(n_strips):
            cs = slice(j * strip, (j + 1) * strip)
            s_t = s_ref[:, cs]
            if masked:
                row = lax.broadcasted_iota(jnp.int32, s_t.shape, 0)
                col = lax.broadcasted_iota(jnp.int32, s_t.shape, 1)
                s_t = jnp.where(row <= (col + j * strip) % blk, s_t, -1e30)
            m_prev = m_ref[:, cs]
            m_new = jnp.maximum(m_prev, _col_reduce(jnp.maximum, jnp.max, s_t))
            corr = jnp.exp2(m_prev - m_new)
            p = jnp.exp2(s_t - m_new)
            acc_ref[:, cs] = corr * acc_ref[:, cs] + _dot(vt_blk, p.astype(BF16))
            m_ref[:, cs] = m_new

    scores(0, sa_ref)

    def pair(j, carry):
        k0 = 2 * j
        scores(k0 + 1, sb_ref)
        softmax_pv(sa_ref, k0, False)
        scores(k0 + 2, sa_ref)
        softmax_pv(sb_ref, k0 + 1, False)
        return carry

    lax.fori_loop(0, qi // 2, pair, 0)

    @pl.when(qi % 2 == 0)
    def _():
        softmax_pv(sa_ref, qi, True)

    @pl.when(qi % 2 == 1)
    def _():
        scores(qi, sb_ref)
        softmax_pv(sa_ref, qi - 1, False)
        softmax_pv(sb_ref, qi, True)

    a = acc_ref[0:2 * dh, :] * (1.0 / acc_ref[2 * dh:2 * dh + 1, :])
    lv = lam_ref[...]
    lam = (jnp.exp(jnp.sum(lv[0:1] * lv[1:2], axis=-1, keepdims=True))
           - jnp.exp(jnp.sum(lv[2:3] * lv[3:4], axis=-1, keepdims=True)) + lambda_init)
    o_t = a[:, 0:blk] - lam * a[:, blk:2 * blk]
    msq = jnp.mean(o_t * o_t, axis=0, keepdims=True)
    o_t = o_t * lax.rsqrt(msq + LN_EPS)
    o_ref[...] = (o_t.T * g_ref[...] * (1.0 - lambda_init)).astype(BF16)


def _diff_attn(q, k, vt, lam_q1, lam_k1, lam_q2, lam_k2, subln_g, lambda_init, *, blk, strip=512):
    bn, s, hd = q.shape
    dh = lam_q1.shape[0]
    heads = hd // (2 * dh)
    nb = s // blk
    lamv = jnp.zeros((8, LANES), F32).at[0:4, 0:dh].set(jnp.stack([lam_q1, lam_k1, lam_q2, lam_k2]))
    return pl.pallas_call(
        functools.partial(_diff_attn_kernel, blk=blk, dh=dh, strip=strip, lambda_init=lambda_init),
        grid=(bn, heads, nb),
        in_specs=[
            pl.BlockSpec((None, blk, 2 * dh), lambda b, h, i: (b, i, h)),
            pl.BlockSpec((None, s, 2 * dh), lambda b, h, i: (b, 0, h)),
            pl.BlockSpec((None, None, nb, 2 * dh + ONES_ROWS, blk), lambda b, h, i: (b, h, 0, 0, 0)),
            pl.BlockSpec((8, LANES), lambda b, h, i: (0, 0)),
            pl.BlockSpec((1, 2 * dh), lambda b, h, i: (0, 0)),
        ],
        out_specs=pl.BlockSpec((None, blk, 2 * dh), lambda b, h, i: (b, i, h)),
        out_shape=jax.ShapeDtypeStruct((bn, s, hd), BF16),
        scratch_shapes=[pltpu.VMEM((1, 2 * blk), F32),
                        pltpu.VMEM((2 * dh + ONES_ROWS, 2 * blk), F32), pltpu.VMEM((2 * blk, 2 * dh), BF16),
                        pltpu.VMEM((blk, 2 * blk), F32), pltpu.VMEM((blk, 2 * blk), F32)],
        compiler_params=_params(3),
        name="diff_attn",
    )(q, k, vt, lamv, subln_g[None, :])


def _attn_out_kernel(o_ref, x_ref, w_ref, ln_g_ref, ln_b_ref, wr_hi_ref, wr_lo_ref, lstrict_ref,
                     y_ref, route_ref, cnt_ref, base_ref):
    @pl.when(pl.program_id(0) == 0)
    def _():
        base_ref[...] = jnp.zeros_like(base_ref)

    m = _dot(o_ref[...], w_ref[...])
    y = _layer_norm(DEEPNORM_ALPHA * x_ref[...] + m, ln_g_ref[...], ln_b_ref[...])
    y_ref[...] = y
    y_hi = y.astype(BF16)
    y_lo = (y - y_hi.astype(F32)).astype(BF16)
    logits = (_dot(y_hi, wr_hi_ref[...]) + _dot(y_lo, wr_hi_ref[...]) + _dot(y_hi, wr_lo_ref[...]))

    lane = lax.broadcasted_iota(jnp.int32, logits.shape, 1).astype(F32)
    neg = -jnp.inf
    lg = jnp.where(lane < N_EXPERTS, logits, neg)
    v0 = jnp.max(lg, axis=-1, keepdims=True)
    i0 = jnp.min(jnp.where(lg == v0, lane, float(LANES)), axis=-1, keepdims=True)
    lg2 = jnp.where(lane == i0, neg, lg)
    v1 = jnp.max(lg2, axis=-1, keepdims=True)
    i1 = jnp.min(jnp.where(lg2 == v1, lane, float(LANES)), axis=-1, keepdims=True)
    e = jnp.exp(v1 - v0)
    g0 = 1.0 / (1.0 + e)
    g1 = e / (1.0 + e)
    oh0 = lane == i0
    oh1 = lane == i1
    c = jnp.where(oh0, 1.0, 0.0) + jnp.where(oh1, 1.0, 0.0)
    pre = _dot(lstrict_ref[...], c.astype(BF16)) + base_ref[...]
    r0 = jnp.sum(jnp.where(oh0, pre, 0.0), axis=-1, keepdims=True)
    r1 = jnp.sum(jnp.where(oh1, pre, 0.0), axis=-1, keepdims=True)
    base_ref[...] = base_ref[...] + jnp.sum(c, axis=0, keepdims=True)
    cnt_ref[...] = jnp.broadcast_to(base_ref[...], cnt_ref.shape)
    fields = (i0, i1, r0, r1, g0, g1)
    route = jnp.zeros_like(logits)
    for idx, val in enumerate(fields):
        route = jnp.where(lane == idx, val, route)
    route_ref[...] = route


def _attn_out(o2d, x2d, w_out, ln_g, ln_b, w_router, *, tile=512):
    n, d = x2d.shape
    wr = jnp.pad(w_router, ((0, 0), (0, LANES - w_router.shape[1])))
    wr_hi = wr.astype(BF16)
    wr_lo = (wr - wr_hi.astype(F32)).astype(BF16)
    lstrict = (jnp.arange(tile)[:, None] > jnp.arange(tile)[None, :]).astype(BF16)
    row = lambda t: t[None, :]
    tspec = pl.BlockSpec((tile, d), lambda i: (i, 0))
    return pl.pallas_call(
        _attn_out_kernel,
        grid=(n // tile,),
        in_specs=[tspec, tspec, _const_spec((d, d)), _const_spec((1, d)), _const_spec((1, d)),
                  _const_spec((d, LANES)), _const_spec((d, LANES)), _const_spec((tile, tile))],
        out_specs=[tspec, pl.BlockSpec((tile, LANES), lambda i: (i, 0)),
                   pl.BlockSpec((8, LANES), lambda i: (0, 0))],
        out_shape=[jax.ShapeDtypeStruct((n, d), F32), jax.ShapeDtypeStruct((n, LANES), F32),
                   jax.ShapeDtypeStruct((8, LANES), F32)],
        scratch_shapes=[pltpu.VMEM((1, LANES), F32)],
        compiler_params=_params(1),
        name="attn_out_router",
    )(o2d, x2d, w_out.astype(BF16), row(ln_g), row(ln_b), wr_hi, wr_lo, lstrict)


def _route_tables(route, cnt, *, tm):
    n = route.shape[0]
    e01 = route[:, 0:2].astype(jnp.int32)
    r01 = route[:, 2:4].astype(jnp.int32)
    counts = cnt[0, :N_EXPERTS].astype(jnp.int32)
    padded = (counts + tm - 1) // tm * tm
    pend = jnp.cumsum(padded)
    pstart = pend - padded
    start_of = jnp.sum(jnp.where(e01[:, :, None] == jnp.arange(N_EXPERTS)[None, None, :],
                                 pstart[None, None, :], 0), axis=-1)
    dest = start_of + r01
    n_rows = n * TOP_K + N_EXPERTS * tm
    n_blocks = n_rows // tm
    n_used = pend[-1] // tm
    blk_start = jnp.minimum(jnp.arange(n_blocks, dtype=jnp.int32), n_used - 1) * tm
    blk_e = jnp.minimum(jnp.sum(blk_start[:, None] >= pend[None, :], axis=1), N_EXPERTS - 1)
    return dest, blk_e.astype(jnp.int32), n_used.reshape(1).astype(jnp.int32), n_rows


def _moe_kernel(blk_e, n_used, xs_ref, wg_ref, wu_ref, wd_ref, y_ref, acc_ref, *, nf):
    m = pl.program_id(0)
    f = pl.program_id(1)

    @pl.when(m < n_used[0])
    def _():
        x = xs_ref[...].astype(BF16)
        g = _dot(x, wg_ref[...])
        u = _dot(x, wu_ref[...])
        h = (g * _sigmoid(g) * u).astype(BF16)
        part = _dot(h, wd_ref[...])

        @pl.when(f == 0)
        def _():
            acc_ref[...] = part

        @pl.when(f > 0)
        def _():
            acc_ref[...] += part

        @pl.when(f == nf - 1)
        def _():
            y_ref[...] = acc_ref[...]

    @pl.when((m >= n_used[0]) & (f == 0))
    def _():
        y_ref[...] = jnp.zeros_like(y_ref)


def _moe_grouped(xs, blk_e, n_used, wg, wu, wd, *, tm, tf):
    n_rows, d = xs.shape
    fdim = wg.shape[2]
    nf = fdim // tf
    n_blocks = n_rows // tm

    def f_eff(m, f, nu):
        return jnp.where(m < nu[0], f, nf - 1)

    grid_spec = pltpu.PrefetchScalarGridSpec(
        num_scalar_prefetch=2,
        grid=(n_blocks, nf),
        in_specs=[
            pl.BlockSpec((tm, d), lambda m, f, be, nu: (jnp.minimum(m, nu[0] - 1), 0)),
            pl.BlockSpec((None, d, tf), lambda m, f, be, nu: (be[m], 0, f_eff(m, f, nu))),
            pl.BlockSpec((None, d, tf), lambda m, f, be, nu: (be[m], 0, f_eff(m, f, nu))),
            pl.BlockSpec((None, tf, d), lambda m, f, be, nu: (be[m], f_eff(m, f, nu), 0)),
        ],
        out_specs=pl.BlockSpec((tm, d), lambda m, f, be, nu: (m, 0)),
        scratch_shapes=[pltpu.VMEM((tm, d), F32)],
    )
    return pl.pallas_call(
        functools.partial(_moe_kernel, nf=nf),
        grid_spec=grid_spec,
        out_shape=jax.ShapeDtypeStruct((n_rows, d), F32),
        compiler_params=_params(2),
        name="moe_grouped",
    )(blk_e, n_used, xs, wg, wu, wd)


def _final_kernel(x_ref, y0_ref, y1_ref, route_ref, p_ref, ln_g_ref, ln_b_ref, wpg_ref, bpg_ref,
                  wpp_ref, o_ref):
    route = route_ref[...]
    f = route[:, 4:5] * y0_ref[...] + route[:, 5:6] * y1_ref[...]
    y = _layer_norm(DEEPNORM_ALPHA * x_ref[...] + f, ln_g_ref[...], ln_b_ref[...])
    o_ref[...] = _ple(y, p_ref[...], wpg_ref, bpg_ref, wpp_ref)


def _final(x2d, y0, y1, route, p2d, ln_g, ln_b, wpg, bpg, wpp, *, tile=512):
    n, d = x2d.shape
    pd = p2d.shape[1]
    row = lambda t: t[None, :]
    tspec = pl.BlockSpec((tile, d), lambda i: (i, 0))
    return pl.pallas_call(
        _final_kernel,
        grid=(n // tile,),
        in_specs=[tspec, tspec, tspec, pl.BlockSpec((tile, LANES), lambda i: (i, 0)),
                  pl.BlockSpec((tile, pd), lambda i: (i, 0)),
                  _const_spec((1, d)), _const_spec((1, d)),
                  _const_spec((d, d)), _const_spec((1, d)), _const_spec((pd, d))],
        out_specs=tspec,
        out_shape=jax.ShapeDtypeStruct((n, d), F32),
        compiler_params=_params(1),
        name="moe_combine_ple",
    )(x2d, y0, y1, route, p2d, row(ln_g), row(ln_b), wpg.astype(BF16), row(bpg), wpp.astype(BF16))


def _even_layer(x, p_i, w_in, a_ln_g, a_ln_b, a_ws, a_bs, w_gate_up, b_gate, norm_g, w_out,
                ln1_g, ln1_b, wg, wu, wd, ln2_g, ln2_b, wpp, wpg, bpg):
    bn, s, d = x.shape
    x1 = _even_mixer(x, w_in, a_ln_g, a_ln_b, a_ws, a_bs, w_gate_up, b_gate, norm_g, w_out, ln1_g, ln1_b)
    x2 = _ffn_ple(x1.reshape(bn * s, d), p_i.reshape(bn * s, -1), wg, wu, wd, ln2_g, ln2_b, wpg, bpg, wpp)
    return x2.reshape(bn, s, d)


def _odd_layer(x, p_i, layer_idx, w_qkv, lam_q1, lam_k1, lam_q2, lam_k2, subln_g, w_out, ln1_g, ln1_b,
               w_router, ewg, ewu, ewd, ln2_g, ln2_b, wpp, wpg, bpg, *, attn_blk=512, moe_tm=512,
               moe_tf=1792):
    bn, s, d = x.shape
    n = bn * s
    lambda_init = 0.8 - 0.6 * math.exp(-0.3 * layer_idx)
    q, k, vt = _qkv(x, w_qkv, dh=lam_q1.shape[0], tile=attn_blk)
    o = _diff_attn(q, k, vt, lam_q1, lam_k1, lam_q2, lam_k2, subln_g, lambda_init, blk=attn_blk)
    x3, route, cnt = _attn_out(o.reshape(n, d), x.reshape(n, d), w_out, ln1_g, ln1_b, w_router)
    dest, blk_e, n_used, n_rows = _route_tables(route, cnt, tm=moe_tm)
    tok = jnp.repeat(jnp.arange(n, dtype=jnp.int32), TOP_K)
    row_tok = jnp.zeros((n_rows,), jnp.int32).at[dest.reshape(-1)].set(tok)
    xs = jnp.take(x3, row_tok, axis=0)
    y = _moe_grouped(xs, blk_e, n_used, ewg.astype(BF16), ewu.astype(BF16), ewd.astype(BF16),
                     tm=moe_tm, tf=moe_tf)
    y0 = jnp.take(y, dest[:, 0], axis=0)
    y1 = jnp.take(y, dest[:, 1], axis=0)
    out = _final(x3, y0, y1, route, p_i.reshape(n, -1), ln2_g, ln2_b, wpg, bpg, wpp)
    return out.reshape(bn, s, d)


def kernel(x, p, e_w_in, e_a_ln_g, e_a_ln_b, e_a_ws, e_a_bs, e_b_w_gate_up, e_b_b_gate, e_b_norm_g, e_w_out, e_ln1_g, e_ln1_b, e_ffn_wg, e_ffn_wu, e_ffn_wd, e_ln2_g, e_ln2_b, o_w_qkv, o_lam_q1, o_lam_k1, o_lam_q2, o_lam_k2, o_subln_g, o_w_out, o_ln1_g, o_ln1_b, o_router, o_exp_wg, o_exp_wu, o_exp_wd, o_ln2_g, o_ln2_b, ple_w_proj, ple_w_gate, ple_b_gate):
    for i in range(DEPTH):
        j = i // 2
        if i % 2 == 0:
            x = _even_layer(x, p[i], e_w_in[j], e_a_ln_g[j], e_a_ln_b[j], e_a_ws[j], e_a_bs[j],
                            e_b_w_gate_up[j], e_b_b_gate[j], e_b_norm_g[j], e_w_out[j],
                            e_ln1_g[j], e_ln1_b[j], e_ffn_wg[j], e_ffn_wu[j], e_ffn_wd[j],
                            e_ln2_g[j], e_ln2_b[j], ple_w_proj[i], ple_w_gate[i], ple_b_gate[i])
        else:
            x = _odd_layer(x, p[i], i, o_w_qkv[j], o_lam_q1[j], o_lam_k1[j], o_lam_q2[j], o_lam_k2[j],
                           o_subln_g[j], o_w_out[j], o_ln1_g[j], o_ln1_b[j], o_router[j],
                           o_exp_wg[j], o_exp_wu[j], o_exp_wd[j], o_ln2_g[j], o_ln2_b[j],
                           ple_w_proj[i], ple_w_gate[i], ple_b_gate[i])
    return x
```

```python
import functools
import math

import jax
import jax.numpy as jnp
from jax import lax
from jax.experimental import pallas as pl
from jax.experimental.pallas import tpu as pltpu
from jax.experimental.pallas import tpu_sc as plsc

F32 = jnp.float32
BF16 = jnp.bfloat16

DEPTH = 2
DEEPNORM_ALPHA = (2.0 * DEPTH) ** 0.25
LN_EPS = 1e-5
A_CHUNK = 128
A_GROUPS = 8
B_HEADS = 4
B_CHUNK = 64
B_TAU = 16.0
C_HEADS = 8
N_EXPERTS = 8
TOP_K = 2
LANES = 128
ONES_ROWS = 16
LOG2_E = 1.4426950408889634
VMEM_LIMIT = 56 * 1024 * 1024

NT_DIMS = (((1,), (1,)), ((), ()))
TN_DIMS = (((0,), (0,)), ((), ()))


def _dot(a, b):
    return jnp.dot(a, b, preferred_element_type=F32)


def _layer_norm(x, g, b):
    mu = jnp.mean(x, axis=-1, keepdims=True)
    xc = x - mu
    var = jnp.mean(xc * xc, axis=-1, keepdims=True)
    return xc * lax.rsqrt(var + LN_EPS) * g + b


def _sigmoid(x):
    return 1.0 / (1.0 + jnp.exp(-x))


def _split3(a):
    hi = a.astype(BF16)
    r1 = a - hi.astype(F32)
    mid = r1.astype(BF16)
    lo = (r1 - mid.astype(F32)).astype(BF16)
    return hi, mid, lo


def _col_reduce(op, final, x):
    parts = [x[i * 8:(i + 1) * 8] for i in range(x.shape[0] // 8)]
    while len(parts) > 1:
        nxt = [op(parts[i], parts[i + 1]) for i in range(0, len(parts) - 1, 2)]
        if len(parts) % 2:
            nxt.append(parts[-1])
        parts = nxt
    return final(parts[0], axis=0, keepdims=True)


def _const_spec(shape):
    zeros = (0,) * len(shape)
    return pl.BlockSpec(shape, lambda *_: zeros, pipeline_mode=pl.Buffered(1))


def _params(n_axes):
    return pltpu.CompilerParams(dimension_semantics=("arbitrary",) * n_axes,
                                vmem_limit_bytes=VMEM_LIMIT)


def _even_mixer_kernel(x_ref, w_in_ref, a_g_ref, a_b_ref, wcat_ref, abias_ref, mstack_ref,
                       w_up_ref, b_gate_ref, ng_ref, w_out_ref, ln_g_ref, ln_b_ref,
                       o_ref, state_ref, *, tile, aw, dkh, dvh):
    @pl.when(pl.program_id(1) == 0)
    def _():
        state_ref[...] = jnp.zeros_like(state_ref)

    hk = B_HEADS * dkh
    hv = B_HEADS * dvh
    x = x_ref[...]
    z = _dot(x.astype(BF16), w_in_ref[...])

    u = jax.nn.gelu(z[:, 0:aw])
    v = _layer_norm(jax.nn.gelu(z[:, aw:2 * aw]), a_g_ref[...], a_b_ref[...])
    gd = aw // A_GROUPS
    rows = lax.broadcasted_iota(jnp.int32, (A_CHUNK, A_GROUPS * A_CHUNK), 0)
    cols = lax.broadcasted_iota(jnp.int32, (A_CHUNK, A_GROUPS * A_CHUNK), 1)
    wcat = jnp.where((cols % A_CHUNK) <= rows, wcat_ref[...], 0.0).astype(BF16)
    r_bd = lax.broadcasted_iota(jnp.int32, (A_GROUPS * A_CHUNK, aw), 0) // A_CHUNK
    c_bd = lax.broadcasted_iota(jnp.int32, (A_GROUPS * A_CHUNK, aw), 1) // gd
    mask_bd = r_bd == c_bd
    ya_parts = []
    for c in range(tile // A_CHUNK):
        sl = slice(c * A_CHUNK, (c + 1) * A_CHUNK)
        v_rep = jnp.concatenate([v[sl]] * A_GROUPS, axis=0)
        v_bd = jnp.where(mask_bd, v_rep, 0.0).astype(BF16)
        sg = _dot(wcat, v_bd) + abias_ref[...]
        ya_parts.append(u[sl] * sg)
    ya = jnp.concatenate(ya_parts, axis=0)

    o0 = 2 * aw
    q = z[:, o0:o0 + hk] * (dkh ** -0.5)
    k = z[:, o0 + hk:o0 + 2 * hk]
    vv = z[:, o0 + 2 * hk:o0 + 2 * hk + hv]
    r = z[:, o0 + 2 * hk + hv:o0 + 2 * hk + 2 * hv]
    g_low = z[:, o0 + 2 * hk + 2 * hv:]
    pre = _dot(g_low.astype(BF16), w_up_ref[...]) + b_gate_ref[...]
    log_a = (jnp.minimum(pre, 0.0) - jnp.log1p(jnp.exp(-jnp.abs(pre)))) * (1.0 / B_TAU)
    la_hi, la_mid, la_lo = _split3(log_a)
    ms = mstack_ref[...]
    cs = _dot(ms, la_hi) + _dot(ms, la_mid) + _dot(ms, la_lo)
    b_cum = cs[0:tile]
    d_mid = cs[tile:2 * tile]
    d_last = cs[2 * tile:3 * tile]
    qe = (q * jnp.exp(d_mid)).astype(BF16)
    ke = k * jnp.exp(-d_mid)
    kd = (k * jnp.exp(d_last)).astype(BF16)
    qb = (q * jnp.exp(b_cum)).astype(BF16)
    dec = jnp.exp(b_cum + d_last)
    vvb = vv.astype(BF16)

    mask_kk = (lax.broadcasted_iota(jnp.int32, (hk, hk), 0) // dkh
               == lax.broadcasted_iota(jnp.int32, (hk, hk), 1) // dkh)
    mask_vbd = (lax.broadcasted_iota(jnp.int32, (hk, hv), 0) // dkh
                == lax.broadcasted_iota(jnp.int32, (hk, hv), 1) // dvh)
    mask_st = (lax.broadcasted_iota(jnp.int32, (hv, hk), 0) // dvh
               == lax.broadcasted_iota(jnp.int32, (hv, hk), 1) // dkh)
    causal = ((lax.broadcasted_iota(jnp.int32, (B_CHUNK, hk), 1) % B_CHUNK)
              <= lax.broadcasted_iota(jnp.int32, (B_CHUNK, hk), 0))

    st = state_ref[...]
    o_parts = []
    for c in range(tile // B_CHUNK):
        sl = slice(c * B_CHUNK, (c + 1) * B_CHUNK)
        ke_bd = jnp.where(mask_kk, jnp.concatenate([ke[sl]] * B_HEADS, axis=0), 0.0).astype(BF16)
        s_cat = lax.dot_general(qe[sl], ke_bd, NT_DIMS, preferred_element_type=F32)
        s_cat = jnp.where(causal, s_cat, 0.0).astype(BF16)
        v_bd = jnp.where(mask_vbd, jnp.concatenate([vv[sl]] * B_HEADS, axis=0), 0.0).astype(BF16)
        o_c = _dot(s_cat, v_bd) + lax.dot_general(qb[sl], st.astype(BF16), NT_DIMS,
                                                   preferred_element_type=F32)
        kv_t = lax.dot_general(vvb[sl], kd[sl], TN_DIMS, preferred_element_type=F32)
        st = dec[c * B_CHUNK:c * B_CHUNK + 1] * st + jnp.where(mask_st, kv_t, 0.0)
        o_parts.append(o_c)
    state_ref[...] = st
    o = jnp.concatenate(o_parts, axis=0)

    yb_parts = []
    for h in range(B_HEADS):
        oh = o[:, h * dvh:(h + 1) * dvh]
        msq = jnp.mean(oh * oh, axis=-1, keepdims=True)
        yb_parts.append(oh * lax.rsqrt(msq + LN_EPS))
    yb = jnp.concatenate(yb_parts, axis=1) * ng_ref[...] * (r * _sigmoid(r))

    y_cat = jnp.concatenate([ya, yb], axis=1).astype(BF16)
    m = _dot(y_cat, w_out_ref[...])
    o_ref[...] = _layer_norm(DEEPNORM_ALPHA * x + m, ln_g_ref[...], ln_b_ref[...])


def _gla_cumsum_matrices(tile):
    i = jnp.arange(tile)[:, None]
    j = jnp.arange(tile)[None, :]
    same = (i // B_CHUNK) == (j // B_CHUNK)
    m_cum = same & (j <= i)
    m_mid = same & (j <= (i // B_CHUNK) * B_CHUNK + B_CHUNK // 2 - 1)
    m_last = same
    f = lambda t: t.astype(F32)
    return jnp.concatenate([f(m_cum), f(m_cum) - f(m_mid), f(m_last) - f(m_cum)], axis=0).astype(BF16)


def _even_mixer(x, w_in, a_ln_g, a_ln_b, a_ws, a_bs, w_gate_up, b_gate, norm_g, w_out, ln_g, ln_b,
                *, tile=256):
    bn, s, d = x.shape
    aw = a_ln_g.shape[0]
    hk = w_gate_up.shape[1]
    dkh = hk // B_HEADS
    dvh = norm_g.shape[0]
    hv = B_HEADS * dvh
    rank = w_gate_up.shape[0]
    main = 2 * aw + 2 * hk + 2 * hv
    gd = aw // A_GROUPS
    w_in_p = jnp.concatenate([w_in[:, :main], jnp.pad(w_in[:, main:], ((0, 0), (0, LANES - rank)))],
                             axis=1).astype(BF16)
    w_up_p = jnp.pad(w_gate_up, ((0, LANES - rank), (0, 0))).astype(BF16)
    wcat = jnp.transpose(a_ws, (1, 0, 2)).reshape(A_CHUNK, A_GROUPS * A_CHUNK)
    abias = jnp.repeat(a_bs.T, gd, axis=1)
    mstack = _gla_cumsum_matrices(tile)
    ng = jnp.tile(norm_g, B_HEADS)[None, :]
    row = lambda t: t[None, :]
    kern = functools.partial(_even_mixer_kernel, tile=tile, aw=aw, dkh=dkh, dvh=dvh)
    tile_spec = pl.BlockSpec((None, tile, d), lambda b, i: (b, i, 0))
    return pl.pallas_call(
        kern,
        grid=(bn, s // tile),
        in_specs=[tile_spec, _const_spec(w_in_p.shape), _const_spec((1, aw)), _const_spec((1, aw)),
                  _const_spec(wcat.shape), _const_spec(abias.shape), _const_spec(mstack.shape),
                  _const_spec(w_up_p.shape), _const_spec((1, hk)), _const_spec((1, hv)),
                  _const_spec(w_out.shape), _const_spec((1, d)), _const_spec((1, d))],
        out_specs=tile_spec,
        out_shape=jax.ShapeDtypeStruct(x.shape, F32),
        scratch_shapes=[pltpu.VMEM((hv, hk), F32)],
        compiler_params=_params(2),
        name="even_mixer",
    )(x, w_in_p, row(a_ln_g), row(a_ln_b), wcat, abias, mstack, w_up_p, row(b_gate), ng,
      w_out.astype(BF16), row(ln_g), row(ln_b))


def _ple(y, p, wpg_ref, bpg_ref, wpp_ref):
    gate = _sigmoid(_dot(y.astype(BF16), wpg_ref[...]) + bpg_ref[...])
    return y + gate * _dot(p.astype(BF16), wpp_ref[...])


def _ffn_ple_kernel(x_ref, p_ref, wg_ref, wu_ref, wd_ref, ln_g_ref, ln_b_ref, wpg_ref, bpg_ref,
                    wpp_ref, o_ref, *, chunks):
    x = x_ref[...]
    xb = x.astype(BF16)
    acc = None
    for lo, hi in chunks:
        g = _dot(xb, wg_ref[:, lo:hi])
        u = _dot(xb, wu_ref[:, lo:hi])
        h = (g * _sigmoid(g) * u).astype(BF16)
        part = _dot(h, wd_ref[lo:hi, :])
        acc = part if acc is None else acc + part
    y = _layer_norm(DEEPNORM_ALPHA * x + acc, ln_g_ref[...], ln_b_ref[...])
    o_ref[...] = _ple(y, p_ref[...], wpg_ref, bpg_ref, wpp_ref)


def _ffn_ple(x2d, p2d, wg, wu, wd, ln_g, ln_b, wpg, bpg, wpp, *, tile=512, fchunk=1024):
    n, d = x2d.shape
    f = wg.shape[1]
    pd = p2d.shape[1]
    chunks = tuple((lo, min(lo + fchunk, f)) for lo in range(0, f, fchunk))
    row = lambda t: t[None, :]
    return pl.pallas_call(
        functools.partial(_ffn_ple_kernel, chunks=chunks),
        grid=(n // tile,),
        in_specs=[pl.BlockSpec((tile, d), lambda i: (i, 0)), pl.BlockSpec((tile, pd), lambda i: (i, 0)),
                  _const_spec((d, f)), _const_spec((d, f)), _const_spec((f, d)),
                  _const_spec((1, d)), _const_spec((1, d)),
                  _const_spec((d, d)), _const_spec((1, d)), _const_spec((pd, d))],
        out_specs=pl.BlockSpec((tile, d), lambda i: (i, 0)),
        out_shape=jax.ShapeDtypeStruct((n, d), F32),
        compiler_params=_params(1),
        name="ffn_ple",
    )(x2d, p2d, wg.astype(BF16), wu.astype(BF16), wd.astype(BF16), row(ln_g), row(ln_b),
      wpg.astype(BF16), row(bpg), wpp.astype(BF16))


def _qkv_kernel(x_ref, w_ref, q_ref, k_ref, vt_ref, *, hd, scale):
    z = _dot(x_ref[...].astype(BF16), w_ref[...])
    q_ref[...] = (z[:, 0:hd] * scale).astype(BF16)
    k_ref[...] = z[:, hd:2 * hd].astype(BF16)
    v_t = z[:, 2 * hd:3 * hd].T.astype(BF16)
    heads, dv_ext, tile = vt_ref.shape
    dv = dv_ext - ONES_ROWS
    ones = jnp.ones((ONES_ROWS, tile), BF16)
    for h in range(heads):
        vt_ref[h] = jnp.concatenate([v_t[h * dv:(h + 1) * dv, :], ones], axis=0)


def _qkv(x, w_qkv, *, dh, tile):
    bn, s, d = x.shape
    hd = w_qkv.shape[1] // 3
    heads = hd // (2 * dh)
    row_spec = pl.BlockSpec((None, tile, hd), lambda b, i: (b, i, 0))
    return pl.pallas_call(
        functools.partial(_qkv_kernel, hd=hd, scale=dh ** -0.5 * LOG2_E),
        grid=(bn, s // tile),
        in_specs=[pl.BlockSpec((None, tile, d), lambda b, i: (b, i, 0)), _const_spec(w_qkv.shape)],
        out_specs=[row_spec, row_spec,
                   pl.BlockSpec((None, heads, None, 2 * dh + ONES_ROWS, tile),
                                lambda b, i: (b, 0, i, 0, 0))],
        out_shape=[jax.ShapeDtypeStruct((bn, s, hd), BF16), jax.ShapeDtypeStruct((bn, s, hd), BF16),
                   jax.ShapeDtypeStruct((bn, heads, s // tile, 2 * dh + ONES_ROWS, tile), BF16)],
        compiler_params=_params(2),
        name="qkv_proj",
    )(x, w_qkv.astype(BF16))


def _diff_attn_kernel(q_ref, k_ref, vt_ref, lam_ref, g_ref, o_ref, m_ref, acc_ref,
                      qm_ref, sa_ref, sb_ref, *, blk, dh, strip, lambda_init):
    qi = pl.program_id(2)
    m_ref[...] = jnp.full_like(m_ref, -1e30)
    acc_ref[...] = jnp.zeros_like(acc_ref)

    q = q_ref[...]
    lane = lax.broadcasted_iota(jnp.int32, q.shape, 1)
    zero = jnp.zeros_like(q)
    qm_ref[0:blk] = jnp.where(lane < dh, q, zero)
    qm_ref[blk:2 * blk] = jnp.where(lane >= dh, q, zero)
    n_strips = 2 * blk // strip

    def scores(ki, s_ref):
        off = pl.multiple_of(ki * blk, blk)
        s_ref[...] = lax.dot_general(k_ref[pl.ds(off, blk), :], qm_ref[...], NT_DIMS,
                                     preferred_element_type=F32)

    def softmax_pv(s_ref, ki, masked):
        vt_blk = vt_ref[ki]
        for j in range(n_strips):
            cs = slice(j * strip, (j + 1) * strip)
            s_t = s_ref[:, cs]
            if masked:
                row = lax.broadcasted_iota(jnp.int32, s_t.shape, 0)
                col = lax.broadcasted_iota(jnp.int32, s_t.shape, 1)
                s_t = jnp.where(row <= (col + j * strip) % blk, s_t, -1e30)
            m_prev = m_ref[:, cs]
            m_new = jnp.maximum(m_prev, _col_reduce(jnp.maximum, jnp.max, s_t))
            corr = jnp.exp2(m_prev - m_new)
            p = jnp.exp2(s_t - m_new)
            acc_ref[:, cs] = corr * acc_ref[:, cs] + _dot(vt_blk, p.astype(BF16))
            m_ref[:, cs] = m_new

    scores(0, sa_ref)

    def pair(j, carry):
        k0 = 2 * j
        scores(k0 + 1, sb_ref)
        softmax_pv(sa_ref, k0, False)
        scores(k0 + 2, sa_ref)
        softmax_pv(sb_ref, k0 + 1, False)
        return carry

    lax.fori_loop(0, qi // 2, pair, 0)

    @pl.when(qi % 2 == 0)
    def _():
        softmax_pv(sa_ref, qi, True)

    @pl.when(qi % 2 == 1)
    def _():
        scores(qi, sb_ref)
        softmax_pv(sa_ref, qi - 1, False)
        softmax_pv(sb_ref, qi, True)

    a = acc_ref[0:2 * dh, :] * (1.0 / acc_ref[2 * dh:2 * dh + 1, :])
    lv = lam_ref[...]
    lam = (jnp.exp(jnp.sum(lv[0:1] * lv[1:2], axis=-1, keepdims=True))
           - jnp.exp(jnp.sum(lv[2:3] * lv[3:4], axis=-1, keepdims=True)) + lambda_init)
    o_t = a[:, 0:blk] - lam * a[:, blk:2 * blk]
    msq = jnp.mean(o_t * o_t, axis=0, keepdims=True)
    o_t = o_t * lax.rsqrt(msq + LN_EPS)
    o_ref[...] = (o_t.T * g_ref[...] * (1.0 - lambda_init)).astype(BF16)


def _diff_attn(q, k, vt, lam_q1, lam_k1, lam_q2, lam_k2, subln_g, lambda_init, *, blk, strip=512):
    bn, s, hd = q.shape
    dh = lam_q1.shape[0]
    heads = hd // (2 * dh)
    nb = s // blk
    lamv = jnp.zeros((8, LANES), F32).at[0:4, 0:dh].set(jnp.stack([lam_q1, lam_k1, lam_q2, lam_k2]))
    return pl.pallas_call(
        functools.partial(_diff_attn_kernel, blk=blk, dh=dh, strip=strip, lambda_init=lambda_init),
        grid=(bn, heads, nb),
        in_specs=[
            pl.BlockSpec((None, blk, 2 * dh), lambda b, h, i: (b, i, h)),
            pl.BlockSpec((None, s, 2 * dh), lambda b, h, i: (b, 0, h)),
            pl.BlockSpec((None, None, nb, 2 * dh + ONES_ROWS, blk), lambda b, h, i: (b, h, 0, 0, 0)),
            pl.BlockSpec((8, LANES), lambda b, h, i: (0, 0)),
            pl.BlockSpec((1, 2 * dh), lambda b, h, i: (0, 0)),
        ],
        out_specs=pl.BlockSpec((None, blk, 2 * dh), lambda b, h, i: (b, i, h)),
        out_shape=jax.ShapeDtypeStruct((bn, s, hd), BF16),
        scratch_shapes=[pltpu.VMEM((1, 2 * blk), F32),
                        pltpu.VMEM((2 * dh + ONES_ROWS, 2 * blk), F32), pltpu.VMEM((2 * blk, 2 * dh), BF16),
                        pltpu.VMEM((blk, 2 * blk), F32), pltpu.VMEM((blk, 2 * blk), F32)],
        compiler_params=_params(3),
        name="diff_attn",
    )(q, k, vt, lamv, subln_g[None, :])


def _attn_out_kernel(o_ref, x_ref, w_ref, ln_g_ref, ln_b_ref, wr_hi_ref, wr_lo_ref, lstrict_ref,
                     y_ref, route_ref, cnt_ref, base_ref):
    @pl.when(pl.program_id(0) == 0)
    def _():
        base_ref[...] = jnp.zeros_like(base_ref)

    m = _dot(o_ref[...], w_ref[...])
    y = _layer_norm(DEEPNORM_ALPHA * x_ref[...] + m, ln_g_ref[...], ln_b_ref[...])
    y_ref[...] = y
    y_hi = y.astype(BF16)
    y_lo = (y - y_hi.astype(F32)).astype(BF16)
    logits = (_dot(y_hi, wr_hi_ref[...]) + _dot(y_lo, wr_hi_ref[...]) + _dot(y_hi, wr_lo_ref[...]))

    lane = lax.broadcasted_iota(jnp.int32, logits.shape, 1).astype(F32)
    neg = -jnp.inf
    lg = jnp.where(lane < N_EXPERTS, logits, neg)
    v0 = jnp.max(lg, axis=-1, keepdims=True)
    i0 = jnp.min(jnp.where(lg == v0, lane, float(LANES)), axis=-1, keepdims=True)
    lg2 = jnp.where(lane == i0, neg, lg)
    v1 = jnp.max(lg2, axis=-1, keepdims=True)
    i1 = jnp.min(jnp.where(lg2 == v1, lane, float(LANES)), axis=-1, keepdims=True)
    e = jnp.exp(v1 - v0)
    g0 = 1.0 / (1.0 + e)
    g1 = e / (1.0 + e)
    oh0 = lane == i0
    oh1 = lane == i1
    c = jnp.where(oh0, 1.0, 0.0) + jnp.where(oh1, 1.0, 0.0)
    pre = _dot(lstrict_ref[...], c.astype(BF16)) + base_ref[...]
    r0 = jnp.sum(jnp.where(oh0, pre, 0.0), axis=-1, keepdims=True)
    r1 = jnp.sum(jnp.where(oh1, pre, 0.0), axis=-1, keepdims=True)
    base_ref[...] = base_ref[...] + jnp.sum(c, axis=0, keepdims=True)
    cnt_ref[...] = jnp.broadcast_to(base_ref[...], cnt_ref.shape)
    fields = (i0, i1, r0, r1, g0, g1)
    route = jnp.zeros_like(logits)
    for idx, val in enumerate(fields):
        route = jnp.where(lane == idx, val, route)
    route_ref[...] = route


def _attn_out(o2d, x2d, w_out, ln_g, ln_b, w_router, *, tile=512):
    n, d = x2d.shape
    wr = jnp.pad(w_router, ((0, 0), (0, LANES - w_router.shape[1])))
    wr_hi = wr.astype(BF16)
    wr_lo = (wr - wr_hi.astype(F32)).astype(BF16)
    lstrict = (jnp.arange(tile)[:, None] > jnp.arange(tile)[None, :]).astype(BF16)
    row = lambda t: t[None, :]
    tspec = pl.BlockSpec((tile, d), lambda i: (i, 0))
    return pl.pallas_call(
        _attn_out_kernel,
        grid=(n // tile,),
        in_specs=[tspec, tspec, _const_spec((d, d)), _const_spec((1, d)), _const_spec((1, d)),
                  _const_spec((d, LANES)), _const_spec((d, LANES)), _const_spec((tile, tile))],
        out_specs=[tspec, pl.BlockSpec((tile, LANES), lambda i: (i, 0)),
                   pl.BlockSpec((8, LANES), lambda i: (0, 0))],
        out_shape=[jax.ShapeDtypeStruct((n, d), F32), jax.ShapeDtypeStruct((n, LANES), F32),
                   jax.ShapeDtypeStruct((8, LANES), F32)],
        scratch_shapes=[pltpu.VMEM((1, LANES), F32)],
        compiler_params=_params(1),
        name="attn_out_router",
    )(o2d, x2d, w_out.astype(BF16), row(ln_g), row(ln_b), wr_hi, wr_lo, lstrict)


def _route_tables(route, cnt, *, tm):
    n = route.shape[0]
    e01 = route[:, 0:2].astype(jnp.int32)
    r01 = route[:, 2:4].astype(jnp.int32)
    counts = cnt[0, :N_EXPERTS].astype(jnp.int32)
    padded = (counts + tm - 1) // tm * tm
    pend = jnp.cumsum(padded)
    pstart = pend - padded
    start_of = jnp.sum(jnp.where(e01[:, :, None] == jnp.arange(N_EXPERTS)[None, None, :],
                                 pstart[None, None, :], 0), axis=-1)
    dest = start_of + r01
    n_rows = n * TOP_K + N_EXPERTS * tm
    n_blocks = n_rows // tm
    n_used = pend[-1] // tm
    blk_idx = jnp.arange(n_blocks, dtype=jnp.int32)
    blk_start = jnp.minimum(blk_idx, n_used - 1) * tm
    blk_e = jnp.minimum(jnp.sum(blk_start[:, None] >= pend[None, :], axis=1), N_EXPERTS - 1)
    valid = jnp.clip(pstart[blk_e] + counts[blk_e] - blk_idx * tm, 0, tm)
    return (dest, blk_e.astype(jnp.int32), n_used.reshape(1).astype(jnp.int32),
            valid.astype(jnp.int32), n_rows)


SC_CORES = 2
SC_SUBCORES = 16
SC_WORKERS = SC_CORES * SC_SUBCORES
SC_ROWS = 32


def _sc_mesh():
    return plsc.VectorSubcoreMesh(core_axis_name="c", subcore_axis_name="s",
                                  num_cores=SC_CORES, num_subcores=SC_SUBCORES)


def _sc_worker_id():
    return lax.axis_index("s") * SC_CORES + lax.axis_index("c")


def _sc_dispatch(x2d, dest, n_rows):
    n, d = x2d.shape
    per_w = n // SC_WORKERS
    n_chunks = per_w // SC_ROWS
    assert n == SC_WORKERS * n_chunks * SC_ROWS and n_chunks % 2 == 0
    d0 = dest[:, 0].reshape(SC_WORKERS, n_chunks, SC_ROWS)
    d1 = dest[:, 1].reshape(SC_WORKERS, n_chunks, SC_ROWS)

    def body(x_hbm, d0_hbm, d1_hbm, xs_hbm, d0_v, d1_v, rows_v, rsem, s0sem, s1sem):
        wid = _sc_worker_id()
        base = wid * per_w
        pltpu.sync_copy(d0_hbm.at[wid], d0_v)
        pltpu.sync_copy(d1_hbm.at[wid], d1_v)

        def read(c, slot):
            return pltpu.make_async_copy(x_hbm.at[pl.ds(base + c * SC_ROWS, SC_ROWS)],
                                         rows_v.at[slot], rsem.at[slot])

        def scat(idx_v, sem, c, slot):
            return pltpu.make_async_copy(rows_v.at[slot], xs_hbm.at[idx_v.at[c]], sem.at[slot])

        def start_scatters(c, slot):
            scat(d0_v, s0sem, c, slot).start()
            scat(d1_v, s1sem, c, slot).start()

        def wait_scatters(c, slot):
            scat(d0_v, s0sem, c, slot).wait()
            scat(d1_v, s1sem, c, slot).wait()

        read(0, 0).start()

        def pair(j, carry):
            c0 = 2 * j
            read(c0, 0).wait()

            @pl.when(j > 0)
            def _():
                wait_scatters(c0 - 1, 1)

            read(c0 + 1, 1).start()
            start_scatters(c0, 0)
            read(c0 + 1, 1).wait()
            wait_scatters(c0, 0)

            @pl.when(c0 + 2 < n_chunks)
            def _():
                read(c0 + 2, 0).start()

            start_scatters(c0 + 1, 1)
            return carry

        lax.fori_loop(0, n_chunks // 2, pair, 0)
        wait_scatters(n_chunks - 1, 1)

    return pl.kernel(
        body,
        out_type=jax.ShapeDtypeStruct((n_rows, d), x2d.dtype),
        mesh=_sc_mesh(),
        scratch_types=[pltpu.VMEM((n_chunks, SC_ROWS), jnp.int32), pltpu.VMEM((n_chunks, SC_ROWS), jnp.int32),
                       pltpu.VMEM((2, SC_ROWS, d), x2d.dtype), pltpu.SemaphoreType.DMA((2,)),
                       pltpu.SemaphoreType.DMA((2,)), pltpu.SemaphoreType.DMA((2,))],
        name="sc_dispatch",
    )(x2d, d0, d1)


def _sc_gather(table, idx):
    _, d = table.shape
    b = idx.shape[0]
    per_w = b // SC_WORKERS
    n_chunks = per_w // SC_ROWS
    assert b == SC_WORKERS * n_chunks * SC_ROWS and n_chunks % 2 == 0
    idx3 = idx.reshape(SC_WORKERS, n_chunks, SC_ROWS)

    def body(table_hbm, idx_hbm, out_hbm, idx_v, rows_v, gsem, wsem):
        wid = _sc_worker_id()
        base = wid * per_w
        pltpu.sync_copy(idx_hbm.at[wid], idx_v)

        def gather(c, slot):
            return pltpu.make_async_copy(table_hbm.at[idx_v.at[c]], rows_v.at[slot], gsem.at[slot])

        def write(c, slot):
            return pltpu.make_async_copy(rows_v.at[slot],
                                         out_hbm.at[pl.ds(base + c * SC_ROWS, SC_ROWS)], wsem.at[slot])

        gather(0, 0).start()

        def pair(j, carry):
            c0 = 2 * j
            gather(c0, 0).wait()

            @pl.when(j > 0)
            def _():
                write(c0 - 1, 1).wait()

            gather(c0 + 1, 1).start()
            write(c0, 0).start()
            gather(c0 + 1, 1).wait()
            write(c0, 0).wait()

            @pl.when(c0 + 2 < n_chunks)
            def _():
                gather(c0 + 2, 0).start()

            write(c0 + 1, 1).start()
            return carry

        lax.fori_loop(0, n_chunks // 2, pair, 0)
        write(n_chunks - 1, 1).wait()

    return pl.kernel(
        body,
        out_type=jax.ShapeDtypeStruct((b, d), table.dtype),
        mesh=_sc_mesh(),
        scratch_types=[pltpu.VMEM((n_chunks, SC_ROWS), jnp.int32), pltpu.VMEM((2, SC_ROWS, d), table.dtype),
                       pltpu.SemaphoreType.DMA((2,)), pltpu.SemaphoreType.DMA((2,))],
        name="sc_combine_gather",
    )(table, idx3)


def _moe_kernel(blk_e, n_used, valid, xs_ref, wg_ref, wu_ref, wd_ref, y_ref, acc_ref, *, nf):
    m = pl.program_id(0)
    f = pl.program_id(1)

    @pl.when(m < n_used[0])
    def _():
        row = lax.broadcasted_iota(jnp.int32, xs_ref.shape, 0)
        x = jnp.where(row < valid[m], xs_ref[...], 0.0).astype(BF16)
        g = _dot(x, wg_ref[...])
        u = _dot(x, wu_ref[...])
        h = (g * _sigmoid(g) * u).astype(BF16)
        part = _dot(h, wd_ref[...])

        @pl.when(f == 0)
        def _():
            acc_ref[...] = part

        @pl.when(f > 0)
        def _():
            acc_ref[...] += part

        @pl.when(f == nf - 1)
        def _():
            y_ref[...] = acc_ref[...]

    @pl.when((m >= n_used[0]) & (f == 0))
    def _():
        y_ref[...] = jnp.zeros_like(y_ref)


def _moe_grouped(xs, blk_e, n_used, valid, wg, wu, wd, *, tm, tf):
    n_rows, d = xs.shape
    fdim = wg.shape[2]
    nf = fdim // tf
    n_blocks = n_rows // tm

    def f_eff(m, f, nu):
        return jnp.where(m < nu[0], f, nf - 1)

    grid_spec = pltpu.PrefetchScalarGridSpec(
        num_scalar_prefetch=3,
        grid=(n_blocks, nf),
        in_specs=[
            pl.BlockSpec((tm, d), lambda m, f, be, nu, va: (jnp.minimum(m, nu[0] - 1), 0)),
            pl.BlockSpec((None, d, tf), lambda m, f, be, nu, va: (be[m], 0, f_eff(m, f, nu))),
            pl.BlockSpec((None, d, tf), lambda m, f, be, nu, va: (be[m], 0, f_eff(m, f, nu))),
            pl.BlockSpec((None, tf, d), lambda m, f, be, nu, va: (be[m], f_eff(m, f, nu), 0)),
        ],
        out_specs=pl.BlockSpec((tm, d), lambda m, f, be, nu, va: (m, 0)),
        scratch_shapes=[pltpu.VMEM((tm, d), F32)],
    )
    return pl.pallas_call(
        functools.partial(_moe_kernel, nf=nf),
        grid_spec=grid_spec,
        out_shape=jax.ShapeDtypeStruct((n_rows, d), F32),
        compiler_params=_params(2),
        name="moe_grouped",
    )(blk_e, n_used, valid, xs, wg, wu, wd)


def _final_kernel(x_ref, y0_ref, y1_ref, route_ref, p_ref, ln_g_ref, ln_b_ref, wpg_ref, bpg_ref,
                  wpp_ref, o_ref):
    route = route_ref[...]
    f = route[:, 4:5] * y0_ref[...] + route[:, 5:6] * y1_ref[...]
    y = _layer_norm(DEEPNORM_ALPHA * x_ref[...] + f, ln_g_ref[...], ln_b_ref[...])
    o_ref[...] = _ple(y, p_ref[...], wpg_ref, bpg_ref, wpp_ref)


def _final(x2d, y01, route, p2d, ln_g, ln_b, wpg, bpg, wpp, *, tile=512):
    n, d = x2d.shape
    pd = p2d.shape[1]
    row = lambda t: t[None, :]
    tspec = pl.BlockSpec((tile, d), lambda i: (i, 0))
    second = pl.BlockSpec((tile, d), lambda i: (i + n // tile, 0))
    y0 = y1 = y01
    return pl.pallas_call(
        _final_kernel,
        grid=(n // tile,),
        in_specs=[tspec, tspec, second, pl.BlockSpec((tile, LANES), lambda i: (i, 0)),
                  pl.BlockSpec((tile, pd), lambda i: (i, 0)),
                  _const_spec((1, d)), _const_spec((1, d)),
                  _const_spec((d, d)), _const_spec((1, d)), _const_spec((pd, d))],
        out_specs=tspec,
        out_shape=jax.ShapeDtypeStruct((n, d), F32),
        compiler_params=_params(1),
        name="moe_combine_ple",
    )(x2d, y0, y1, route, p2d, row(ln_g), row(ln_b), wpg.astype(BF16), row(bpg), wpp.astype(BF16))


def _even_layer(x, p_i, w_in, a_ln_g, a_ln_b, a_ws, a_bs, w_gate_up, b_gate, norm_g, w_out,
                ln1_g, ln1_b, wg, wu, wd, ln2_g, ln2_b, wpp, wpg, bpg):
    bn, s, d = x.shape
    x1 = _even_mixer(x, w_in, a_ln_g, a_ln_b, a_ws, a_bs, w_gate_up, b_gate, norm_g, w_out, ln1_g, ln1_b)
    x2 = _ffn_ple(x1.reshape(bn * s, d), p_i.reshape(bn * s, -1), wg, wu, wd, ln2_g, ln2_b, wpg, bpg, wpp)
    return x2.reshape(bn, s, d)


def _odd_layer(x, p_i, layer_idx, w_qkv, lam_q1, lam_k1, lam_q2, lam_k2, subln_g, w_out, ln1_g, ln1_b,
               w_router, ewg, ewu, ewd, ln2_g, ln2_b, wpp, wpg, bpg, *, attn_blk=512, moe_tm=512,
               moe_tf=1792):
    bn, s, d = x.shape
    n = bn * s
    lambda_init = 0.8 - 0.6 * math.exp(-0.3 * layer_idx)
    q, k, vt = _qkv(x, w_qkv, dh=lam_q1.shape[0], tile=attn_blk)
    o = _diff_attn(q, k, vt, lam_q1, lam_k1, lam_q2, lam_k2, subln_g, lambda_init, blk=attn_blk)
    x3, route, cnt = _attn_out(o.reshape(n, d), x.reshape(n, d), w_out, ln1_g, ln1_b, w_router)
    dest, blk_e, n_used, valid, n_rows = _route_tables(route, cnt, tm=moe_tm)
    xs = _sc_dispatch(x3, dest, n_rows)
    y = _moe_grouped(xs, blk_e, n_used, valid, ewg.astype(BF16), ewu.astype(BF16), ewd.astype(BF16),
                     tm=moe_tm, tf=moe_tf)
    y01 = _sc_gather(y, jnp.concatenate([dest[:, 0], dest[:, 1]]))
    out = _final(x3, y01, route, p_i.reshape(n, -1), ln2_g, ln2_b, wpg, bpg, wpp)
    return out.reshape(bn, s, d)


def kernel(x, p, e_w_in, e_a_ln_g, e_a_ln_b, e_a_ws, e_a_bs, e_b_w_gate_up, e_b_b_gate, e_b_norm_g, e_w_out, e_ln1_g, e_ln1_b, e_ffn_wg, e_ffn_wu, e_ffn_wd, e_ln2_g, e_ln2_b, o_w_qkv, o_lam_q1, o_lam_k1, o_lam_q2, o_lam_k2, o_subln_g, o_w_out, o_ln1_g, o_ln1_b, o_router, o_exp_wg, o_exp_wu, o_exp_wd, o_ln2_g, o_ln2_b, ple_w_proj, ple_w_gate, ple_b_gate):
    for i in range(DEPTH):
        j = i // 2
        if i % 2 == 0:
            x = _even_layer(x, p[i], e_w_in[j], e_a_ln_g[j], e_a_ln_b[j], e_a_ws[j], e_a_bs[j],
                            e_b_w_gate_up[j], e_b_b_gate[j], e_b_norm_g[j], e_w_out[j],
                            e_ln1_g[j], e_ln1_b[j], e_ffn_wg[j], e_ffn_wu[j], e_ffn_wd[j],
                            e_ln2_g[j], e_ln2_b[j], ple_w_proj[i], ple_w_gate[i], ple_b_gate[i])
        else:
            x = _odd_layer(x, p[i], i, o_w_qkv[j], o_lam_q1[j], o_lam_k1[j], o_lam_q2[j], o_lam_k2[j],
                           o_subln_g[j], o_w_out[j], o_ln1_g[j], o_ln1_b[j], o_router[j],
                           o_exp_wg[j], o_exp_wu[j], o_exp_wd[j], o_ln2_g[j], o_ln2_b[j],
                           ple_w_proj[i], ple_w_gate[i], ple_b_gate[i])
    return x
```

```python
import functools
import math

import jax
import jax.numpy as jnp
from jax import lax
from jax.experimental import pallas as pl
from jax.experimental.pallas import tpu as pltpu
from jax.experimental.pallas import tpu_sc as plsc

F32 = jnp.float32
BF16 = jnp.bfloat16

DEPTH = 2
DEEPNORM_ALPHA = (2.0 * DEPTH) ** 0.25
LN_EPS = 1e-5
A_CHUNK = 128
A_GROUPS = 8
B_HEADS = 4
B_CHUNK = 64
B_TAU = 16.0
C_HEADS = 8
N_EXPERTS = 8
TOP_K = 2
LANES = 128
ONES_ROWS = 16
LOG2_E = 1.4426950408889634
VMEM_LIMIT = 56 * 1024 * 1024

NT_DIMS = (((1,), (1,)), ((), ()))
TN_DIMS = (((0,), (0,)), ((), ()))


def _dot(a, b):
    return jnp.dot(a, b, preferred_element_type=F32)


def _layer_norm(x, g, b):
    mu = jnp.mean(x, axis=-1, keepdims=True)
    xc = x - mu
    var = jnp.mean(xc * xc, axis=-1, keepdims=True)
    return xc * lax.rsqrt(var + LN_EPS) * g + b


def _sigmoid(x):
    return 1.0 / (1.0 + jnp.exp(-x))


def _split3(a):
    hi = a.astype(BF16)
    r1 = a - hi.astype(F32)
    mid = r1.astype(BF16)
    lo = (r1 - mid.astype(F32)).astype(BF16)
    return hi, mid, lo


def _col_reduce(op, final, x):
    parts = [x[i * 8:(i + 1) * 8] for i in range(x.shape[0] // 8)]
    while len(parts) > 1:
        nxt = [op(parts[i], parts[i + 1]) for i in range(0, len(parts) - 1, 2)]
        if len(parts) % 2:
            nxt.append(parts[-1])
        parts = nxt
    return final(parts[0], axis=0, keepdims=True)


def _const_spec(shape):
    zeros = (0,) * len(shape)
    return pl.BlockSpec(shape, lambda *_: zeros, pipeline_mode=pl.Buffered(1))


def _params(n_axes):
    return pltpu.CompilerParams(dimension_semantics=("arbitrary",) * n_axes,
                                vmem_limit_bytes=VMEM_LIMIT)


def _even_mixer_kernel(x_ref, w_in_ref, a_g_ref, a_b_ref, wcat_ref, abias_ref, mstack_ref,
                       w_up_ref, b_gate_ref, ng_ref, w_out_ref, ln_g_ref, ln_b_ref,
                       o_ref, state_ref, *, tile, aw, dkh, dvh):
    @pl.when(pl.program_id(1) == 0)
    def _():
        state_ref[...] = jnp.zeros_like(state_ref)

    hk = B_HEADS * dkh
    hv = B_HEADS * dvh
    x = x_ref[...]
    z = _dot(x.astype(BF16), w_in_ref[...])

    u = jax.nn.gelu(z[:, 0:aw])
    v = _layer_norm(jax.nn.gelu(z[:, aw:2 * aw]), a_g_ref[...], a_b_ref[...])
    gd = aw // A_GROUPS
    rows = lax.broadcasted_iota(jnp.int32, (A_CHUNK, A_GROUPS * A_CHUNK), 0)
    cols = lax.broadcasted_iota(jnp.int32, (A_CHUNK, A_GROUPS * A_CHUNK), 1)
    wcat = jnp.where((cols % A_CHUNK) <= rows, wcat_ref[...], 0.0).astype(BF16)
    r_bd = lax.broadcasted_iota(jnp.int32, (A_GROUPS * A_CHUNK, aw), 0) // A_CHUNK
    c_bd = lax.broadcasted_iota(jnp.int32, (A_GROUPS * A_CHUNK, aw), 1) // gd
    mask_bd = r_bd == c_bd
    ya_parts = []
    for c in range(tile // A_CHUNK):
        sl = slice(c * A_CHUNK, (c + 1) * A_CHUNK)
        v_rep = jnp.concatenate([v[sl]] * A_GROUPS, axis=0)
        v_bd = jnp.where(mask_bd, v_rep, 0.0).astype(BF16)
        sg = _dot(wcat, v_bd) + abias_ref[...]
        ya_parts.append(u[sl] * sg)
    ya = jnp.concatenate(ya_parts, axis=0)

    o0 = 2 * aw
    q = z[:, o0:o0 + hk] * (dkh ** -0.5)
    k = z[:, o0 + hk:o0 + 2 * hk]
    vv = z[:, o0 + 2 * hk:o0 + 2 * hk + hv]
    r = z[:, o0 + 2 * hk + hv:o0 + 2 * hk + 2 * hv]
    g_low = z[:, o0 + 2 * hk + 2 * hv:]
    pre = _dot(g_low.astype(BF16), w_up_ref[...]) + b_gate_ref[...]
    log_a = (jnp.minimum(pre, 0.0) - jnp.log1p(jnp.exp(-jnp.abs(pre)))) * (1.0 / B_TAU)
    la_hi, la_mid, la_lo = _split3(log_a)
    ms = mstack_ref[...]
    cs = _dot(ms, la_hi) + _dot(ms, la_mid) + _dot(ms, la_lo)
    b_cum = cs[0:tile]
    d_mid = cs[tile:2 * tile]
    d_last = cs[2 * tile:3 * tile]
    qe = (q * jnp.exp(d_mid)).astype(BF16)
    ke = k * jnp.exp(-d_mid)
    kd = (k * jnp.exp(d_last)).astype(BF16)
    qb = (q * jnp.exp(b_cum)).astype(BF16)
    dec = jnp.exp(b_cum + d_last)
    vvb = vv.astype(BF16)

    mask_kk = (lax.broadcasted_iota(jnp.int32, (hk, hk), 0) // dkh
               == lax.broadcasted_iota(jnp.int32, (hk, hk), 1) // dkh)
    mask_vbd = (lax.broadcasted_iota(jnp.int32, (hk, hv), 0) // dkh
                == lax.broadcasted_iota(jnp.int32, (hk, hv), 1) // dvh)
    mask_st = (lax.broadcasted_iota(jnp.int32, (hv, hk), 0) // dvh
               == lax.broadcasted_iota(jnp.int32, (hv, hk), 1) // dkh)
    causal = ((lax.broadcasted_iota(jnp.int32, (B_CHUNK, hk), 1) % B_CHUNK)
              <= lax.broadcasted_iota(jnp.int32, (B_CHUNK, hk), 0))

    st = state_ref[...]
    o_parts = []
    for c in range(tile // B_CHUNK):
        sl = slice(c * B_CHUNK, (c + 1) * B_CHUNK)
        ke_bd = jnp.where(mask_kk, jnp.concatenate([ke[sl]] * B_HEADS, axis=0), 0.0).astype(BF16)
        s_cat = lax.dot_general(qe[sl], ke_bd, NT_DIMS, preferred_element_type=F32)
        s_cat = jnp.where(causal, s_cat, 0.0).astype(BF16)
        v_bd = jnp.where(mask_vbd, jnp.concatenate([vv[sl]] * B_HEADS, axis=0), 0.0).astype(BF16)
        o_c = _dot(s_cat, v_bd) + lax.dot_general(qb[sl], st.astype(BF16), NT_DIMS,
                                                   preferred_element_type=F32)
        kv_t = lax.dot_general(vvb[sl], kd[sl], TN_DIMS, preferred_element_type=F32)
        st = dec[c * B_CHUNK:c * B_CHUNK + 1] * st + jnp.where(mask_st, kv_t, 0.0)
        o_parts.append(o_c)
    state_ref[...] = st
    o = jnp.concatenate(o_parts, axis=0)

    yb_parts = []
    for h in range(B_HEADS):
        oh = o[:, h * dvh:(h + 1) * dvh]
        msq = jnp.mean(oh * oh, axis=-1, keepdims=True)
        yb_parts.append(oh * lax.rsqrt(msq + LN_EPS))
    yb = jnp.concatenate(yb_parts, axis=1) * ng_ref[...] * (r * _sigmoid(r))

    y_cat = jnp.concatenate([ya, yb], axis=1).astype(BF16)
    m = _dot(y_cat, w_out_ref[...])
    o_ref[...] = _layer_norm(DEEPNORM_ALPHA * x + m, ln_g_ref[...], ln_b_ref[...])


def _gla_cumsum_matrices(tile):
    i = jnp.arange(tile)[:, None]
    j = jnp.arange(tile)[None, :]
    same = (i // B_CHUNK) == (j // B_CHUNK)
    m_cum = same & (j <= i)
    m_mid = same & (j <= (i // B_CHUNK) * B_CHUNK + B_CHUNK // 2 - 1)
    m_last = same
    f = lambda t: t.astype(F32)
    return jnp.concatenate([f(m_cum), f(m_cum) - f(m_mid), f(m_last) - f(m_cum)], axis=0).astype(BF16)


def _even_mixer(x, w_in, a_ln_g, a_ln_b, a_ws, a_bs, w_gate_up, b_gate, norm_g, w_out, ln_g, ln_b,
                *, tile=256):
    bn, s, d = x.shape
    aw = a_ln_g.shape[0]
    hk = w_gate_up.shape[1]
    dkh = hk // B_HEADS
    dvh = norm_g.shape[0]
    hv = B_HEADS * dvh
    rank = w_gate_up.shape[0]
    main = 2 * aw + 2 * hk + 2 * hv
    gd = aw // A_GROUPS
    w_in_p = jnp.concatenate([w_in[:, :main], jnp.pad(w_in[:, main:], ((0, 0), (0, LANES - rank)))],
                             axis=1).astype(BF16)
    w_up_p = jnp.pad(w_gate_up, ((0, LANES - rank), (0, 0))).astype(BF16)
    wcat = jnp.transpose(a_ws, (1, 0, 2)).reshape(A_CHUNK, A_GROUPS * A_CHUNK)
    abias = jnp.repeat(a_bs.T, gd, axis=1)
    mstack = _gla_cumsum_matrices(tile)
    ng = jnp.tile(norm_g, B_HEADS)[None, :]
    row = lambda t: t[None, :]
    kern = functools.partial(_even_mixer_kernel, tile=tile, aw=aw, dkh=dkh, dvh=dvh)
    tile_spec = pl.BlockSpec((None, tile, d), lambda b, i: (b, i, 0))
    return pl.pallas_call(
        kern,
        grid=(bn, s // tile),
        in_specs=[tile_spec, _const_spec(w_in_p.shape), _const_spec((1, aw)), _const_spec((1, aw)),
                  _const_spec(wcat.shape), _const_spec(abias.shape), _const_spec(mstack.shape),
                  _const_spec(w_up_p.shape), _const_spec((1, hk)), _const_spec((1, hv)),
                  _const_spec(w_out.shape), _const_spec((1, d)), _const_spec((1, d))],
        out_specs=tile_spec,
        out_shape=jax.ShapeDtypeStruct(x.shape, F32),
        scratch_shapes=[pltpu.VMEM((hv, hk), F32)],
        compiler_params=_params(2),
        name="even_mixer",
    )(x, w_in_p, row(a_ln_g), row(a_ln_b), wcat, abias, mstack, w_up_p, row(b_gate), ng,
      w_out.astype(BF16), row(ln_g), row(ln_b))


def _ple(y, p, wpg_ref, bpg_ref, wpp_ref):
    gate = _sigmoid(_dot(y.astype(BF16), wpg_ref[...]) + bpg_ref[...])
    return y + gate * _dot(p.astype(BF16), wpp_ref[...])


def _ffn_ple_kernel(x_ref, p_ref, wg_ref, wu_ref, wd_ref, ln_g_ref, ln_b_ref, wpg_ref, bpg_ref,
                    wpp_ref, o_ref, *, chunks):
    x = x_ref[...]
    xb = x.astype(BF16)
    acc = None
    for lo, hi in chunks:
        g = _dot(xb, wg_ref[:, lo:hi])
        u = _dot(xb, wu_ref[:, lo:hi])
        h = (g * _sigmoid(g) * u).astype(BF16)
        part = _dot(h, wd_ref[lo:hi, :])
        acc = part if acc is None else acc + part
    y = _layer_norm(DEEPNORM_ALPHA * x + acc, ln_g_ref[...], ln_b_ref[...])
    o_ref[...] = _ple(y, p_ref[...], wpg_ref, bpg_ref, wpp_ref)


def _ffn_ple(x2d, p2d, wg, wu, wd, ln_g, ln_b, wpg, bpg, wpp, *, tile=512, fchunk=1024):
    n, d = x2d.shape
    f = wg.shape[1]
    pd = p2d.shape[1]
    chunks = tuple((lo, min(lo + fchunk, f)) for lo in range(0, f, fchunk))
    row = lambda t: t[None, :]
    return pl.pallas_call(
        functools.partial(_ffn_ple_kernel, chunks=chunks),
        grid=(n // tile,),
        in_specs=[pl.BlockSpec((tile, d), lambda i: (i, 0)), pl.BlockSpec((tile, pd), lambda i: (i, 0)),
                  _const_spec((d, f)), _const_spec((d, f)), _const_spec((f, d)),
                  _const_spec((1, d)), _const_spec((1, d)),
                  _const_spec((d, d)), _const_spec((1, d)), _const_spec((pd, d))],
        out_specs=pl.BlockSpec((tile, d), lambda i: (i, 0)),
        out_shape=jax.ShapeDtypeStruct((n, d), F32),
        compiler_params=_params(1),
        name="ffn_ple",
    )(x2d, p2d, wg.astype(BF16), wu.astype(BF16), wd.astype(BF16), row(ln_g), row(ln_b),
      wpg.astype(BF16), row(bpg), wpp.astype(BF16))


def _qkv_kernel(x_ref, w_ref, qt_ref, k_ref, vt_ref, *, hd, scale):
    z = _dot(x_ref[...].astype(BF16), w_ref[...])
    k_ref[...] = z[:, hd:2 * hd].astype(BF16)
    q_t = (z[:, 0:hd] * scale).T.astype(BF16)
    v_t = z[:, 2 * hd:3 * hd].T.astype(BF16)
    heads, dv_ext, tile = vt_ref.shape
    dv = dv_ext - ONES_ROWS
    ones = jnp.ones((ONES_ROWS, tile), BF16)
    for h in range(heads):
        qt_ref[h] = q_t[h * dv:(h + 1) * dv, :]
        vt_ref[h] = jnp.concatenate([v_t[h * dv:(h + 1) * dv, :], ones], axis=0)


def _qkv(x, w_qkv, *, dh, tile):
    bn, s, d = x.shape
    hd = w_qkv.shape[1] // 3
    heads = hd // (2 * dh)
    row_spec = pl.BlockSpec((None, tile, hd), lambda b, i: (b, i, 0))
    return pl.pallas_call(
        functools.partial(_qkv_kernel, hd=hd, scale=dh ** -0.5 * LOG2_E),
        grid=(bn, s // tile),
        in_specs=[pl.BlockSpec((None, tile, d), lambda b, i: (b, i, 0)), _const_spec(w_qkv.shape)],
        out_specs=[pl.BlockSpec((None, heads, None, 2 * dh, tile), lambda b, i: (b, 0, i, 0, 0)),
                   row_spec,
                   pl.BlockSpec((None, heads, None, 2 * dh + ONES_ROWS, tile),
                                lambda b, i: (b, 0, i, 0, 0))],
        out_shape=[jax.ShapeDtypeStruct((bn, heads, s // tile, 2 * dh, tile), BF16),
                   jax.ShapeDtypeStruct((bn, s, hd), BF16),
                   jax.ShapeDtypeStruct((bn, heads, s // tile, 2 * dh + ONES_ROWS, tile), BF16)],
        compiler_params=_params(2),
        name="qkv_proj",
    )(x, w_qkv.astype(BF16))


def _diff_attn_kernel(qt_ref, k_ref, vt_ref, lam_ref, g_ref, o_ref, m_ref, acc_ref,
                      qm_ref, sa_ref, sb_ref, *, blk, dh, strip, lambda_init):
    qi = pl.program_id(2)
    m_ref[...] = jnp.full_like(m_ref, -1e30)
    acc_ref[...] = jnp.zeros_like(acc_ref)

    q_t = qt_ref[...]
    comp = lax.broadcasted_iota(jnp.int32, q_t.shape, 0)
    zero = jnp.zeros_like(q_t)
    qm_ref[:, 0:blk] = jnp.where(comp < dh, q_t, zero)
    qm_ref[:, blk:2 * blk] = jnp.where(comp >= dh, q_t, zero)
    n_strips = 2 * blk // strip

    def scores(ki, s_ref):
        off = pl.multiple_of(ki * blk, blk)
        s_ref[...] = _dot(k_ref[pl.ds(off, blk), :], qm_ref[...])

    def softmax_pv(s_ref, ki, masked):
        vt_blk = vt_ref[ki]
        for j in range(n_strips):
            cs = slice(j * strip, (j + 1) * strip)
            s_t = s_ref[:, cs]
            if masked:
                row = lax.broadcasted_iota(jnp.int32, s_t.shape, 0)
                col = lax.broadcasted_iota(jnp.int32, s_t.shape, 1)
                s_t = jnp.where(row <= (col + j * strip) % blk, s_t, -1e30)
            m_prev = m_ref[:, cs]
            m_new = jnp.maximum(m_prev, _col_reduce(jnp.maximum, jnp.max, s_t))
            corr = jnp.exp2(m_prev - m_new)
            p = jnp.exp2(s_t - m_new)
            acc_ref[:, cs] = corr * acc_ref[:, cs] + _dot(vt_blk, p.astype(BF16))
            m_ref[:, cs] = m_new

    scores(0, sa_ref)

    def pair(j, carry):
        k0 = 2 * j
        scores(k0 + 1, sb_ref)
        softmax_pv(sa_ref, k0, False)
        scores(k0 + 2, sa_ref)
        softmax_pv(sb_ref, k0 + 1, False)
        return carry

    lax.fori_loop(0, qi // 2, pair, 0)

    @pl.when(qi % 2 == 0)
    def _():
        softmax_pv(sa_ref, qi, True)

    @pl.when(qi % 2 == 1)
    def _():
        scores(qi, sb_ref)
        softmax_pv(sa_ref, qi - 1, False)
        softmax_pv(sb_ref, qi, True)

    a = acc_ref[0:2 * dh, :] * (1.0 / acc_ref[2 * dh:2 * dh + 1, :])
    lv = lam_ref[...]
    lam = (jnp.exp(jnp.sum(lv[0:1] * lv[1:2], axis=-1, keepdims=True))
           - jnp.exp(jnp.sum(lv[2:3] * lv[3:4], axis=-1, keepdims=True)) + lambda_init)
    o_t = a[:, 0:blk] - lam * a[:, blk:2 * blk]
    msq = jnp.mean(o_t * o_t, axis=0, keepdims=True)
    o_t = o_t * lax.rsqrt(msq + LN_EPS)
    o_ref[...] = (o_t.T * g_ref[...] * (1.0 - lambda_init)).astype(BF16)


def _diff_attn(qt, k, vt, lam_q1, lam_k1, lam_q2, lam_k2, subln_g, lambda_init, *, blk, strip=512):
    bn, s, hd = k.shape
    dh = lam_q1.shape[0]
    heads = hd // (2 * dh)
    nb = s // blk
    lamv = jnp.zeros((8, LANES), F32).at[0:4, 0:dh].set(jnp.stack([lam_q1, lam_k1, lam_q2, lam_k2]))
    return pl.pallas_call(
        functools.partial(_diff_attn_kernel, blk=blk, dh=dh, strip=strip, lambda_init=lambda_init),
        grid=(bn, heads, nb),
        in_specs=[
            pl.BlockSpec((None, None, None, 2 * dh, blk), lambda b, h, i: (b, h, i, 0, 0)),
            pl.BlockSpec((None, s, 2 * dh), lambda b, h, i: (b, 0, h)),
            pl.BlockSpec((None, None, nb, 2 * dh + ONES_ROWS, blk), lambda b, h, i: (b, h, 0, 0, 0)),
            pl.BlockSpec((8, LANES), lambda b, h, i: (0, 0)),
            pl.BlockSpec((1, 2 * dh), lambda b, h, i: (0, 0)),
        ],
        out_specs=pl.BlockSpec((None, blk, 2 * dh), lambda b, h, i: (b, i, h)),
        out_shape=jax.ShapeDtypeStruct((bn, s, hd), BF16),
        scratch_shapes=[pltpu.VMEM((1, 2 * blk), F32),
                        pltpu.VMEM((2 * dh + ONES_ROWS, 2 * blk), F32), pltpu.VMEM((2 * dh, 2 * blk), BF16),
                        pltpu.VMEM((blk, 2 * blk), F32), pltpu.VMEM((blk, 2 * blk), F32)],
        compiler_params=_params(3),
        name="diff_attn",
    )(qt, k, vt, lamv, subln_g[None, :])


def _attn_out_kernel(o_ref, x_ref, w_ref, ln_g_ref, ln_b_ref, wr_hi_ref, wr_lo_ref, lstrict_ref,
                     y_ref, route_ref, cnt_ref, base_ref):
    @pl.when(pl.program_id(0) == 0)
    def _():
        base_ref[...] = jnp.zeros_like(base_ref)

    m = _dot(o_ref[...], w_ref[...])
    y = _layer_norm(DEEPNORM_ALPHA * x_ref[...] + m, ln_g_ref[...], ln_b_ref[...])
    y_ref[...] = y
    y_hi = y.astype(BF16)
    y_lo = (y - y_hi.astype(F32)).astype(BF16)
    logits = (_dot(y_hi, wr_hi_ref[...]) + _dot(y_lo, wr_hi_ref[...]) + _dot(y_hi, wr_lo_ref[...]))

    lane = lax.broadcasted_iota(jnp.int32, logits.shape, 1).astype(F32)
    neg = -jnp.inf
    lg = jnp.where(lane < N_EXPERTS, logits, neg)
    v0 = jnp.max(lg, axis=-1, keepdims=True)
    i0 = jnp.min(jnp.where(lg == v0, lane, float(LANES)), axis=-1, keepdims=True)
    lg2 = jnp.where(lane == i0, neg, lg)
    v1 = jnp.max(lg2, axis=-1, keepdims=True)
    i1 = jnp.min(jnp.where(lg2 == v1, lane, float(LANES)), axis=-1, keepdims=True)
    e = jnp.exp(v1 - v0)
    g0 = 1.0 / (1.0 + e)
    g1 = e / (1.0 + e)
    oh0 = lane == i0
    oh1 = lane == i1
    c = jnp.where(oh0, 1.0, 0.0) + jnp.where(oh1, 1.0, 0.0)
    pre = _dot(lstrict_ref[...], c.astype(BF16)) + base_ref[...]
    r0 = jnp.sum(jnp.where(oh0, pre, 0.0), axis=-1, keepdims=True)
    r1 = jnp.sum(jnp.where(oh1, pre, 0.0), axis=-1, keepdims=True)
    base_ref[...] = base_ref[...] + jnp.sum(c, axis=0, keepdims=True)
    cnt_ref[...] = jnp.broadcast_to(base_ref[...], cnt_ref.shape)
    fields = (i0, i1, r0, r1, g0, g1)
    route = jnp.zeros_like(logits)
    for idx, val in enumerate(fields):
        route = jnp.where(lane == idx, val, route)
    route_ref[...] = route


def _attn_out(o2d, x2d, w_out, ln_g, ln_b, w_router, *, tile=512):
    n, d = x2d.shape
    wr = jnp.pad(w_router, ((0, 0), (0, LANES - w_router.shape[1])))
    wr_hi = wr.astype(BF16)
    wr_lo = (wr - wr_hi.astype(F32)).astype(BF16)
    lstrict = (jnp.arange(tile)[:, None] > jnp.arange(tile)[None, :]).astype(BF16)
    row = lambda t: t[None, :]
    tspec = pl.BlockSpec((tile, d), lambda i: (i, 0))
    return pl.pallas_call(
        _attn_out_kernel,
        grid=(n // tile,),
        in_specs=[tspec, tspec, _const_spec((d, d)), _const_spec((1, d)), _const_spec((1, d)),
                  _const_spec((d, LANES)), _const_spec((d, LANES)), _const_spec((tile, tile))],
        out_specs=[tspec, pl.BlockSpec((tile, LANES), lambda i: (i, 0)),
                   pl.BlockSpec((8, LANES), lambda i: (0, 0))],
        out_shape=[jax.ShapeDtypeStruct((n, d), F32), jax.ShapeDtypeStruct((n, LANES), F32),
                   jax.ShapeDtypeStruct((8, LANES), F32)],
        scratch_shapes=[pltpu.VMEM((1, LANES), F32)],
        compiler_params=_params(1),
        name="attn_out_router",
    )(o2d, x2d, w_out.astype(BF16), row(ln_g), row(ln_b), wr_hi, wr_lo, lstrict)


def _route_tables(route, cnt, *, tm):
    n = route.shape[0]
    e01 = route[:, 0:2].astype(jnp.int32)
    r01 = route[:, 2:4].astype(jnp.int32)
    counts = cnt[0, :N_EXPERTS].astype(jnp.int32)
    padded = (counts + tm - 1) // tm * tm
    pend = jnp.cumsum(padded)
    pstart = pend - padded
    start_of = jnp.sum(jnp.where(e01[:, :, None] == jnp.arange(N_EXPERTS)[None, None, :],
                                 pstart[None, None, :], 0), axis=-1)
    dest = start_of + r01
    n_rows = n * TOP_K + N_EXPERTS * tm
    n_blocks = n_rows // tm
    n_used = pend[-1] // tm
    blk_idx = jnp.arange(n_blocks, dtype=jnp.int32)
    blk_start = jnp.minimum(blk_idx, n_used - 1) * tm
    blk_e = jnp.minimum(jnp.sum(blk_start[:, None] >= pend[None, :], axis=1), N_EXPERTS - 1)
    valid = jnp.clip(pstart[blk_e] + counts[blk_e] - blk_idx * tm, 0, tm)
    return (dest, blk_e.astype(jnp.int32), n_used.reshape(1).astype(jnp.int32),
            valid.astype(jnp.int32), n_rows)


SC_CORES = 2
SC_SUBCORES = 16
SC_WORKERS = SC_CORES * SC_SUBCORES
SC_ROWS = 32


def _sc_mesh():
    return plsc.VectorSubcoreMesh(core_axis_name="c", subcore_axis_name="s",
                                  num_cores=SC_CORES, num_subcores=SC_SUBCORES)


def _sc_worker_id():
    return lax.axis_index("s") * SC_CORES + lax.axis_index("c")


def _sc_dispatch(x2d, dest, n_rows):
    n, d = x2d.shape
    per_w = n // SC_WORKERS
    n_chunks = per_w // SC_ROWS
    assert n == SC_WORKERS * n_chunks * SC_ROWS and n_chunks % 2 == 0
    d0 = dest[:, 0].reshape(SC_WORKERS, n_chunks, SC_ROWS)
    d1 = dest[:, 1].reshape(SC_WORKERS, n_chunks, SC_ROWS)

    def body(x_hbm, d0_hbm, d1_hbm, xs_hbm, d0_v, d1_v, rows_v, rsem, s0sem, s1sem):
        wid = _sc_worker_id()
        base = wid * per_w
        pltpu.sync_copy(d0_hbm.at[wid], d0_v)
        pltpu.sync_copy(d1_hbm.at[wid], d1_v)

        def read(c, slot):
            return pltpu.make_async_copy(x_hbm.at[pl.ds(base + c * SC_ROWS, SC_ROWS)],
                                         rows_v.at[slot], rsem.at[slot])

        def scat(idx_v, sem, c, slot):
            return pltpu.make_async_copy(rows_v.at[slot], xs_hbm.at[idx_v.at[c]], sem.at[slot])

        def start_scatters(c, slot):
            scat(d0_v, s0sem, c, slot).start()
            scat(d1_v, s1sem, c, slot).start()

        def wait_scatters(c, slot):
            scat(d0_v, s0sem, c, slot).wait()
            scat(d1_v, s1sem, c, slot).wait()

        read(0, 0).start()

        def pair(j, carry):
            c0 = 2 * j
            read(c0, 0).wait()

            @pl.when(j > 0)
            def _():
                wait_scatters(c0 - 1, 1)

            read(c0 + 1, 1).start()
            start_scatters(c0, 0)
            read(c0 + 1, 1).wait()
            wait_scatters(c0, 0)

            @pl.when(c0 + 2 < n_chunks)
            def _():
                read(c0 + 2, 0).start()

            start_scatters(c0 + 1, 1)
            return carry

        lax.fori_loop(0, n_chunks // 2, pair, 0)
        wait_scatters(n_chunks - 1, 1)

    return pl.kernel(
        body,
        out_type=jax.ShapeDtypeStruct((n_rows, d), x2d.dtype),
        mesh=_sc_mesh(),
        scratch_types=[pltpu.VMEM((n_chunks, SC_ROWS), jnp.int32), pltpu.VMEM((n_chunks, SC_ROWS), jnp.int32),
                       pltpu.VMEM((2, SC_ROWS, d), x2d.dtype), pltpu.SemaphoreType.DMA((2,)),
                       pltpu.SemaphoreType.DMA((2,)), pltpu.SemaphoreType.DMA((2,))],
        name="sc_dispatch",
    )(x2d, d0, d1)


def _sc_gather(table, idx):
    _, d = table.shape
    b = idx.shape[0]
    per_w = b // SC_WORKERS
    n_chunks = per_w // SC_ROWS
    assert b == SC_WORKERS * n_chunks * SC_ROWS and n_chunks % 2 == 0
    idx3 = idx.reshape(SC_WORKERS, n_chunks, SC_ROWS)

    def body(table_hbm, idx_hbm, out_hbm, idx_v, rows_v, gsem, wsem):
        wid = _sc_worker_id()
        base = wid * per_w
        pltpu.sync_copy(idx_hbm.at[wid], idx_v)

        def gather(c, slot):
            return pltpu.make_async_copy(table_hbm.at[idx_v.at[c]], rows_v.at[slot], gsem.at[slot])

        def write(c, slot):
            return pltpu.make_async_copy(rows_v.at[slot],
                                         out_hbm.at[pl.ds(base + c * SC_ROWS, SC_ROWS)], wsem.at[slot])

        gather(0, 0).start()

        def pair(j, carry):
            c0 = 2 * j
            gather(c0, 0).wait()

            @pl.when(j > 0)
            def _():
                write(c0 - 1, 1).wait()

            gather(c0 + 1, 1).start()
            write(c0, 0).start()
            gather(c0 + 1, 1).wait()
            write(c0, 0).wait()

            @pl.when(c0 + 2 < n_chunks)
            def _():
                gather(c0 + 2, 0).start()

            write(c0 + 1, 1).start()
            return carry

        lax.fori_loop(0, n_chunks // 2, pair, 0)
        write(n_chunks - 1, 1).wait()

    return pl.kernel(
        body,
        out_type=jax.ShapeDtypeStruct((b, d), table.dtype),
        mesh=_sc_mesh(),
        scratch_types=[pltpu.VMEM((n_chunks, SC_ROWS), jnp.int32), pltpu.VMEM((2, SC_ROWS, d), table.dtype),
                       pltpu.SemaphoreType.DMA((2,)), pltpu.SemaphoreType.DMA((2,))],
        name="sc_combine_gather",
    )(table, idx3)


def _moe_kernel(blk_e, n_used, valid, xs_ref, wg_ref, wu_ref, wd_ref, y_ref, acc_ref, *, nf):
    m = pl.program_id(0)
    f = pl.program_id(1)

    @pl.when(m < n_used[0])
    def _():
        row = lax.broadcasted_iota(jnp.int32, xs_ref.shape, 0)
        x = jnp.where(row < valid[m], xs_ref[...], 0.0).astype(BF16)
        g = _dot(x, wg_ref[...])
        u = _dot(x, wu_ref[...])
        h = (g * _sigmoid(g) * u).astype(BF16)
        part = _dot(h, wd_ref[...])

        @pl.when(f == 0)
        def _():
            acc_ref[...] = part

        @pl.when(f > 0)
        def _():
            acc_ref[...] += part

        @pl.when(f == nf - 1)
        def _():
            y_ref[...] = acc_ref[...]

    @pl.when((m >= n_used[0]) & (f == 0))
    def _():
        y_ref[...] = jnp.zeros_like(y_ref)


def _moe_grouped(xs, blk_e, n_used, valid, wg, wu, wd, *, tm, tf):
    n_rows, d = xs.shape
    fdim = wg.shape[2]
    nf = fdim // tf
    n_blocks = n_rows // tm

    def f_eff(m, f, nu):
        return jnp.where(m < nu[0], f, nf - 1)

    grid_spec = pltpu.PrefetchScalarGridSpec(
        num_scalar_prefetch=3,
        grid=(n_blocks, nf),
        in_specs=[
            pl.BlockSpec((tm, d), lambda m, f, be, nu, va: (jnp.minimum(m, nu[0] - 1), 0)),
            pl.BlockSpec((None, d, tf), lambda m, f, be, nu, va: (be[m], 0, f_eff(m, f, nu))),
            pl.BlockSpec((None, d, tf), lambda m, f, be, nu, va: (be[m], 0, f_eff(m, f, nu))),
            pl.BlockSpec((None, tf, d), lambda m, f, be, nu, va: (be[m], f_eff(m, f, nu), 0)),
        ],
        out_specs=pl.BlockSpec((tm, d), lambda m, f, be, nu, va: (m, 0)),
        scratch_shapes=[pltpu.VMEM((tm, d), F32)],
    )
    return pl.pallas_call(
        functools.partial(_moe_kernel, nf=nf),
        grid_spec=grid_spec,
        out_shape=jax.ShapeDtypeStruct((n_rows, d), F32),
        compiler_params=_params(2),
        name="moe_grouped",
    )(blk_e, n_used, valid, xs, wg, wu, wd)


def _final_kernel(x_ref, y0_ref, y1_ref, route_ref, p_ref, ln_g_ref, ln_b_ref, wpg_ref, bpg_ref,
                  wpp_ref, o_ref):
    route = route_ref[...]
    f = route[:, 4:5] * y0_ref[...] + route[:, 5:6] * y1_ref[...]
    y = _layer_norm(DEEPNORM_ALPHA * x_ref[...] + f, ln_g_ref[...], ln_b_ref[...])
    o_ref[...] = _ple(y, p_ref[...], wpg_ref, bpg_ref, wpp_ref)


def _final(x2d, y01, route, p2d, ln_g, ln_b, wpg, bpg, wpp, *, tile=512):
    n, d = x2d.shape
    pd = p2d.shape[1]
    row = lambda t: t[None, :]
    tspec = pl.BlockSpec((tile, d), lambda i: (i, 0))
    second = pl.BlockSpec((tile, d), lambda i: (i + n // tile, 0))
    y0 = y1 = y01
    return pl.pallas_call(
        _final_kernel,
        grid=(n // tile,),
        in_specs=[tspec, tspec, second, pl.BlockSpec((tile, LANES), lambda i: (i, 0)),
                  pl.BlockSpec((tile, pd), lambda i: (i, 0)),
                  _const_spec((1, d)), _const_spec((1, d)),
                  _const_spec((d, d)), _const_spec((1, d)), _const_spec((pd, d))],
        out_specs=tspec,
        out_shape=jax.ShapeDtypeStruct((n, d), F32),
        compiler_params=_params(1),
        name="moe_combine_ple",
    )(x2d, y0, y1, route, p2d, row(ln_g), row(ln_b), wpg.astype(BF16), row(bpg), wpp.astype(BF16))


def _even_layer(x, p_i, w_in, a_ln_g, a_ln_b, a_ws, a_bs, w_gate_up, b_gate, norm_g, w_out,
                ln1_g, ln1_b, wg, wu, wd, ln2_g, ln2_b, wpp, wpg, bpg):
    bn, s, d = x.shape
    x1 = _even_mixer(x, w_in, a_ln_g, a_ln_b, a_ws, a_bs, w_gate_up, b_gate, norm_g, w_out, ln1_g, ln1_b)
    x2 = _ffn_ple(x1.reshape(bn * s, d), p_i.reshape(bn * s, -1), wg, wu, wd, ln2_g, ln2_b, wpg, bpg, wpp)
    return x2.reshape(bn, s, d)


def _odd_layer(x, p_i, layer_idx, w_qkv, lam_q1, lam_k1, lam_q2, lam_k2, subln_g, w_out, ln1_g, ln1_b,
               w_router, ewg, ewu, ewd, ln2_g, ln2_b, wpp, wpg, bpg, *, attn_blk=512, moe_tm=512,
               moe_tf=1792):
    bn, s, d = x.shape
    n = bn * s
    lambda_init = 0.8 - 0.6 * math.exp(-0.3 * layer_idx)
    q, k, vt = _qkv(x, w_qkv, dh=lam_q1.shape[0], tile=attn_blk)
    o = _diff_attn(q, k, vt, lam_q1, lam_k1, lam_q2, lam_k2, subln_g, lambda_init, blk=attn_blk)
    x3, route, cnt = _attn_out(o.reshape(n, d), x.reshape(n, d), w_out, ln1_g, ln1_b, w_router)
    dest, blk_e, n_used, valid, n_rows = _route_tables(route, cnt, tm=moe_tm)
    xs = _sc_dispatch(x3, dest, n_rows)
    y = _moe_grouped(xs, blk_e, n_used, valid, ewg.astype(BF16), ewu.astype(BF16), ewd.astype(BF16),
                     tm=moe_tm, tf=moe_tf)
    y01 = _sc_gather(y, jnp.concatenate([dest[:, 0], dest[:, 1]]))
    out = _final(x3, y01, route, p_i.reshape(n, -1), ln2_g, ln2_b, wpg, bpg, wpp)
    return out.reshape(bn, s, d)


def kernel(x, p, e_w_in, e_a_ln_g, e_a_ln_b, e_a_ws, e_a_bs, e_b_w_gate_up, e_b_b_gate, e_b_norm_g, e_w_out, e_ln1_g, e_ln1_b, e_ffn_wg, e_ffn_wu, e_ffn_wd, e_ln2_g, e_ln2_b, o_w_qkv, o_lam_q1, o_lam_k1, o_lam_q2, o_lam_k2, o_subln_g, o_w_out, o_ln1_g, o_ln1_b, o_router, o_exp_wg, o_exp_wu, o_exp_wd, o_ln2_g, o_ln2_b, ple_w_proj, ple_w_gate, ple_b_gate):
    for i in range(DEPTH):
        j = i // 2
        if i % 2 == 0:
            x = _even_layer(x, p[i], e_w_in[j], e_a_ln_g[j], e_a_ln_b[j], e_a_ws[j], e_a_bs[j],
                            e_b_w_gate_up[j], e_b_b_gate[j], e_b_norm_g[j], e_w_out[j],
                            e_ln1_g[j], e_ln1_b[j], e_ffn_wg[j], e_ffn_wu[j], e_ffn_wd[j],
                            e_ln2_g[j], e_ln2_b[j], ple_w_proj[i], ple_w_gate[i], ple_b_gate[i])
        else:
            x = _odd_layer(x, p[i], i, o_w_qkv[j], o_lam_q1[j], o_lam_k1[j], o_lam_q2[j], o_lam_k2[j],
                           o_subln_g[j], o_w_out[j], o_ln1_g[j], o_ln1_b[j], o_router[j],
                           o_exp_wg[j], o_exp_wu[j], o_exp_wd[j], o_ln2_g[j], o_ln2_b[j],
                           ple_w_proj[i], ple_w_gate[i], ple_b_gate[i])
    return x
```

```python
import functools
import math

import jax
import jax.numpy as jnp
from jax import lax
from jax.experimental import pallas as pl
from jax.experimental.pallas import tpu as pltpu
from jax.experimental.pallas import tpu_sc as plsc

F32 = jnp.float32
BF16 = jnp.bfloat16

DEPTH = 2
DEEPNORM_ALPHA = (2.0 * DEPTH) ** 0.25
LN_EPS = 1e-5
A_CHUNK = 128
A_GROUPS = 8
B_HEADS = 4
B_CHUNK = 64
B_TAU = 16.0
C_HEADS = 8
N_EXPERTS = 8
TOP_K = 2
LANES = 128
ONES_ROWS = 16
LOG2_E = 1.4426950408889634
VMEM_LIMIT = 56 * 1024 * 1024

NT_DIMS = (((1,), (1,)), ((), ()))
TN_DIMS = (((0,), (0,)), ((), ()))


def _dot(a, b):
    return jnp.dot(a, b, preferred_element_type=F32)


def _layer_norm(x, g, b):
    mu = jnp.mean(x, axis=-1, keepdims=True)
    xc = x - mu
    var = jnp.mean(xc * xc, axis=-1, keepdims=True)
    return xc * lax.rsqrt(var + LN_EPS) * g + b


def _sigmoid(x):
    return 1.0 / (1.0 + jnp.exp(-x))


def _split3(a):
    hi = a.astype(BF16)
    r1 = a - hi.astype(F32)
    mid = r1.astype(BF16)
    lo = (r1 - mid.astype(F32)).astype(BF16)
    return hi, mid, lo


def _col_reduce(op, final, x):
    parts = [x[i * 8:(i + 1) * 8] for i in range(x.shape[0] // 8)]
    while len(parts) > 1:
        nxt = [op(parts[i], parts[i + 1]) for i in range(0, len(parts) - 1, 2)]
        if len(parts) % 2:
            nxt.append(parts[-1])
        parts = nxt
    return final(parts[0], axis=0, keepdims=True)


def _const_spec(shape):
    zeros = (0,) * len(shape)
    return pl.BlockSpec(shape, lambda *_: zeros, pipeline_mode=pl.Buffered(1))


def _params(n_axes):
    return pltpu.CompilerParams(dimension_semantics=("arbitrary",) * n_axes,
                                vmem_limit_bytes=VMEM_LIMIT)


def _even_mixer_kernel(x_ref, w_in_ref, a_g_ref, a_b_ref, wcat_ref, abias_ref, mstack_ref,
                       w_up_ref, b_gate_ref, ng_ref, w_out_ref, ln_g_ref, ln_b_ref,
                       o_ref, state_ref, *, tile, aw, dkh, dvh):
    @pl.when(pl.program_id(1) == 0)
    def _():
        state_ref[...] = jnp.zeros_like(state_ref)

    hk = B_HEADS * dkh
    hv = B_HEADS * dvh
    x = x_ref[...]
    z = _dot(x.astype(BF16), w_in_ref[...])

    u = jax.nn.gelu(z[:, 0:aw])
    v = _layer_norm(jax.nn.gelu(z[:, aw:2 * aw]), a_g_ref[...], a_b_ref[...])
    gd = aw // A_GROUPS
    rows = lax.broadcasted_iota(jnp.int32, (A_CHUNK, A_GROUPS * A_CHUNK), 0)
    cols = lax.broadcasted_iota(jnp.int32, (A_CHUNK, A_GROUPS * A_CHUNK), 1)
    wcat = jnp.where((cols % A_CHUNK) <= rows, wcat_ref[...], 0.0).astype(BF16)
    r_bd = lax.broadcasted_iota(jnp.int32, (A_GROUPS * A_CHUNK, aw), 0) // A_CHUNK
    c_bd = lax.broadcasted_iota(jnp.int32, (A_GROUPS * A_CHUNK, aw), 1) // gd
    mask_bd = r_bd == c_bd
    ya_parts = []
    for c in range(tile // A_CHUNK):
        sl = slice(c * A_CHUNK, (c + 1) * A_CHUNK)
        v_rep = jnp.concatenate([v[sl]] * A_GROUPS, axis=0)
        v_bd = jnp.where(mask_bd, v_rep, 0.0).astype(BF16)
        sg = _dot(wcat, v_bd) + abias_ref[...]
        ya_parts.append(u[sl] * sg)
    ya = jnp.concatenate(ya_parts, axis=0)

    o0 = 2 * aw
    q = z[:, o0:o0 + hk] * (dkh ** -0.5)
    k = z[:, o0 + hk:o0 + 2 * hk]
    vv = z[:, o0 + 2 * hk:o0 + 2 * hk + hv]
    r = z[:, o0 + 2 * hk + hv:o0 + 2 * hk + 2 * hv]
    g_low = z[:, o0 + 2 * hk + 2 * hv:]
    pre = _dot(g_low.astype(BF16), w_up_ref[...]) + b_gate_ref[...]
    log_a = (jnp.minimum(pre, 0.0) - jnp.log1p(jnp.exp(-jnp.abs(pre)))) * (1.0 / B_TAU)
    la_hi, la_mid, la_lo = _split3(log_a)
    ms = mstack_ref[...]
    cs = _dot(ms, la_hi) + _dot(ms, la_mid) + _dot(ms, la_lo)
    b_cum = cs[0:tile]
    d_mid = cs[tile:2 * tile]
    d_last = cs[2 * tile:3 * tile]
    qe = (q * jnp.exp(d_mid)).astype(BF16)
    ke = k * jnp.exp(-d_mid)
    kd = (k * jnp.exp(d_last)).astype(BF16)
    qb = (q * jnp.exp(b_cum)).astype(BF16)
    dec = jnp.exp(b_cum + d_last)
    vvb = vv.astype(BF16)

    mask_kk = (lax.broadcasted_iota(jnp.int32, (hk, hk), 0) // dkh
               == lax.broadcasted_iota(jnp.int32, (hk, hk), 1) // dkh)
    mask_vbd = (lax.broadcasted_iota(jnp.int32, (hk, hv), 0) // dkh
                == lax.broadcasted_iota(jnp.int32, (hk, hv), 1) // dvh)
    mask_st = (lax.broadcasted_iota(jnp.int32, (hv, hk), 0) // dvh
               == lax.broadcasted_iota(jnp.int32, (hv, hk), 1) // dkh)
    causal = ((lax.broadcasted_iota(jnp.int32, (B_CHUNK, hk), 1) % B_CHUNK)
              <= lax.broadcasted_iota(jnp.int32, (B_CHUNK, hk), 0))

    st = state_ref[...]
    o_parts = []
    for c in range(tile // B_CHUNK):
        sl = slice(c * B_CHUNK, (c + 1) * B_CHUNK)
        ke_bd = jnp.where(mask_kk, jnp.concatenate([ke[sl]] * B_HEADS, axis=0), 0.0).astype(BF16)
        s_cat = lax.dot_general(qe[sl], ke_bd, NT_DIMS, preferred_element_type=F32)
        s_cat = jnp.where(causal, s_cat, 0.0).astype(BF16)
        v_bd = jnp.where(mask_vbd, jnp.concatenate([vv[sl]] * B_HEADS, axis=0), 0.0).astype(BF16)
        o_c = _dot(s_cat, v_bd) + lax.dot_general(qb[sl], st.astype(BF16), NT_DIMS,
                                                   preferred_element_type=F32)
        kv_t = lax.dot_general(vvb[sl], kd[sl], TN_DIMS, preferred_element_type=F32)
        st = dec[c * B_CHUNK:c * B_CHUNK + 1] * st + jnp.where(mask_st, kv_t, 0.0)
        o_parts.append(o_c)
    state_ref[...] = st
    o = jnp.concatenate(o_parts, axis=0)

    yb_parts = []
    for h in range(B_HEADS):
        oh = o[:, h * dvh:(h + 1) * dvh]
        msq = jnp.mean(oh * oh, axis=-1, keepdims=True)
        yb_parts.append(oh * lax.rsqrt(msq + LN_EPS))
    yb = jnp.concatenate(yb_parts, axis=1) * ng_ref[...] * (r * _sigmoid(r))

    y_cat = jnp.concatenate([ya, yb], axis=1).astype(BF16)
    m = _dot(y_cat, w_out_ref[...])
    o_ref[...] = _layer_norm(DEEPNORM_ALPHA * x + m, ln_g_ref[...], ln_b_ref[...])


def _gla_cumsum_matrices(tile):
    i = jnp.arange(tile)[:, None]
    j = jnp.arange(tile)[None, :]
    same = (i // B_CHUNK) == (j // B_CHUNK)
    m_cum = same & (j <= i)
    m_mid = same & (j <= (i // B_CHUNK) * B_CHUNK + B_CHUNK // 2 - 1)
    m_last = same
    f = lambda t: t.astype(F32)
    return jnp.concatenate([f(m_cum), f(m_cum) - f(m_mid), f(m_last) - f(m_cum)], axis=0).astype(BF16)


def _even_mixer(x, w_in, a_ln_g, a_ln_b, a_ws, a_bs, w_gate_up, b_gate, norm_g, w_out, ln_g, ln_b,
                *, tile=256):
    bn, s, d = x.shape
    aw = a_ln_g.shape[0]
    hk = w_gate_up.shape[1]
    dkh = hk // B_HEADS
    dvh = norm_g.shape[0]
    hv = B_HEADS * dvh
    rank = w_gate_up.shape[0]
    main = 2 * aw + 2 * hk + 2 * hv
    gd = aw // A_GROUPS
    w_in_p = jnp.concatenate([w_in[:, :main], jnp.pad(w_in[:, main:], ((0, 0), (0, LANES - rank)))],
                             axis=1).astype(BF16)
    w_up_p = jnp.pad(w_gate_up, ((0, LANES - rank), (0, 0))).astype(BF16)
    wcat = jnp.transpose(a_ws, (1, 0, 2)).reshape(A_CHUNK, A_GROUPS * A_CHUNK)
    abias = jnp.repeat(a_bs.T, gd, axis=1)
    mstack = _gla_cumsum_matrices(tile)
    ng = jnp.tile(norm_g, B_HEADS)[None, :]
    row = lambda t: t[None, :]
    kern = functools.partial(_even_mixer_kernel, tile=tile, aw=aw, dkh=dkh, dvh=dvh)
    tile_spec = pl.BlockSpec((None, tile, d), lambda b, i: (b, i, 0))
    return pl.pallas_call(
        kern,
        grid=(bn, s // tile),
        in_specs=[tile_spec, _const_spec(w_in_p.shape), _const_spec((1, aw)), _const_spec((1, aw)),
                  _const_spec(wcat.shape), _const_spec(abias.shape), _const_spec(mstack.shape),
                  _const_spec(w_up_p.shape), _const_spec((1, hk)), _const_spec((1, hv)),
                  _const_spec(w_out.shape), _const_spec((1, d)), _const_spec((1, d))],
        out_specs=tile_spec,
        out_shape=jax.ShapeDtypeStruct(x.shape, F32),
        scratch_shapes=[pltpu.VMEM((hv, hk), F32)],
        compiler_params=_params(2),
        name="even_mixer",
    )(x, w_in_p, row(a_ln_g), row(a_ln_b), wcat, abias, mstack, w_up_p, row(b_gate), ng,
      w_out.astype(BF16), row(ln_g), row(ln_b))


def _ple(y, p, wpg_ref, bpg_ref, wpp_ref):
    gate = _sigmoid(_dot(y.astype(BF16), wpg_ref[...]) + bpg_ref[...])
    return y + gate * _dot(p.astype(BF16), wpp_ref[...])


def _ffn_ple_kernel(x_ref, p_ref, wg_ref, wu_ref, wd_ref, ln_g_ref, ln_b_ref, wpg_ref, bpg_ref,
                    wpp_ref, o_ref, *, chunks):
    x = x_ref[...]
    xb = x.astype(BF16)
    acc = None
    for lo, hi in chunks:
        g = _dot(xb, wg_ref[:, lo:hi])
        u = _dot(xb, wu_ref[:, lo:hi])
        h = (g * _sigmoid(g) * u).astype(BF16)
        part = _dot(h, wd_ref[lo:hi, :])
        acc = part if acc is None else acc + part
    y = _layer_norm(DEEPNORM_ALPHA * x + acc, ln_g_ref[...], ln_b_ref[...])
    o_ref[...] = _ple(y, p_ref[...], wpg_ref, bpg_ref, wpp_ref)


def _ffn_ple(x2d, p2d, wg, wu, wd, ln_g, ln_b, wpg, bpg, wpp, *, tile=512, fchunk=1024):
    n, d = x2d.shape
    f = wg.shape[1]
    pd = p2d.shape[1]
    chunks = tuple((lo, min(lo + fchunk, f)) for lo in range(0, f, fchunk))
    row = lambda t: t[None, :]
    return pl.pallas_call(
        functools.partial(_ffn_ple_kernel, chunks=chunks),
        grid=(n // tile,),
        in_specs=[pl.BlockSpec((tile, d), lambda i: (i, 0)), pl.BlockSpec((tile, pd), lambda i: (i, 0)),
                  _const_spec((d, f)), _const_spec((d, f)), _const_spec((f, d)),
                  _const_spec((1, d)), _const_spec((1, d)),
                  _const_spec((d, d)), _const_spec((1, d)), _const_spec((pd, d))],
        out_specs=pl.BlockSpec((tile, d), lambda i: (i, 0)),
        out_shape=jax.ShapeDtypeStruct((n, d), F32),
        compiler_params=_params(1),
        name="ffn_ple",
    )(x2d, p2d, wg.astype(BF16), wu.astype(BF16), wd.astype(BF16), row(ln_g), row(ln_b),
      wpg.astype(BF16), row(bpg), wpp.astype(BF16))


def _qkv_kernel(x_ref, w_ref, qt_ref, k_ref, vt_ref, *, hd, scale):
    z = _dot(x_ref[...].astype(BF16), w_ref[...])
    k_ref[...] = z[:, hd:2 * hd].astype(BF16)
    q_t = (z[:, 0:hd] * scale).T.astype(BF16)
    v_t = z[:, 2 * hd:3 * hd].T.astype(BF16)
    heads, dv_ext, tile = vt_ref.shape
    dv = dv_ext - ONES_ROWS
    ones = jnp.ones((ONES_ROWS, tile), BF16)
    for h in range(heads):
        qt_ref[h] = q_t[h * dv:(h + 1) * dv, :]
        vt_ref[h] = jnp.concatenate([v_t[h * dv:(h + 1) * dv, :], ones], axis=0)


def _qkv(x, w_qkv, *, dh, tile):
    bn, s, d = x.shape
    hd = w_qkv.shape[1] // 3
    heads = hd // (2 * dh)
    row_spec = pl.BlockSpec((None, tile, hd), lambda b, i: (b, i, 0))
    return pl.pallas_call(
        functools.partial(_qkv_kernel, hd=hd, scale=dh ** -0.5 * LOG2_E),
        grid=(bn, s // tile),
        in_specs=[pl.BlockSpec((None, tile, d), lambda b, i: (b, i, 0)), _const_spec(w_qkv.shape)],
        out_specs=[pl.BlockSpec((None, heads, None, 2 * dh, tile), lambda b, i: (b, 0, i, 0, 0)),
                   row_spec,
                   pl.BlockSpec((None, heads, None, 2 * dh + ONES_ROWS, tile),
                                lambda b, i: (b, 0, i, 0, 0))],
        out_shape=[jax.ShapeDtypeStruct((bn, heads, s // tile, 2 * dh, tile), BF16),
                   jax.ShapeDtypeStruct((bn, s, hd), BF16),
                   jax.ShapeDtypeStruct((bn, heads, s // tile, 2 * dh + ONES_ROWS, tile), BF16)],
        compiler_params=_params(2),
        name="qkv_proj",
    )(x, w_qkv.astype(BF16))


def _diff_attn_kernel(qt_ref, k_ref, vt_ref, lam_ref, g_ref, o_ref, m_ref, acc_ref,
                      qm_ref, sa_ref, sb_ref, *, blk, dh, lambda_init):
    qi = pl.program_id(2)
    m_ref[...] = jnp.full_like(m_ref, -1e30)
    acc_ref[...] = jnp.zeros_like(acc_ref)

    q_t = qt_ref[...]
    comp = lax.broadcasted_iota(jnp.int32, q_t.shape, 0)
    zero = jnp.zeros_like(q_t)
    qm_ref[0] = jnp.where(comp < dh, q_t, zero)
    qm_ref[1] = jnp.where(comp >= dh, q_t, zero)

    def scores(ki, s_ref):
        off = pl.multiple_of(ki * blk, blk)
        k_blk = k_ref[pl.ds(off, blk), :]
        for c in range(2):
            s_ref[c] = _dot(k_blk, qm_ref[c])

    def softmax_pv(s_ref, ki, masked):
        vt_blk = vt_ref[ki]
        for c in range(2):
            s_t = s_ref[c]
            if masked:
                row = lax.broadcasted_iota(jnp.int32, s_t.shape, 0)
                col = lax.broadcasted_iota(jnp.int32, s_t.shape, 1)
                s_t = jnp.where(row <= col, s_t, -1e30)
            m_prev = m_ref[c]
            m_new = jnp.maximum(m_prev, _col_reduce(jnp.maximum, jnp.max, s_t))
            corr = jnp.exp2(m_prev - m_new)
            p = jnp.exp2(s_t - m_new)
            acc_ref[c] = corr * acc_ref[c] + _dot(vt_blk, p.astype(BF16))
            m_ref[c] = m_new

    scores(0, sa_ref)

    def pair(j, carry):
        k0 = 2 * j
        scores(k0 + 1, sb_ref)
        softmax_pv(sa_ref, k0, False)
        scores(k0 + 2, sa_ref)
        softmax_pv(sb_ref, k0 + 1, False)
        return carry

    lax.fori_loop(0, qi // 2, pair, 0)

    @pl.when(qi % 2 == 0)
    def _():
        softmax_pv(sa_ref, qi, True)

    @pl.when(qi % 2 == 1)
    def _():
        scores(qi, sb_ref)
        softmax_pv(sa_ref, qi - 1, False)
        softmax_pv(sb_ref, qi, True)

    a1 = acc_ref[0, 0:2 * dh, :] * (1.0 / acc_ref[0, 2 * dh:2 * dh + 1, :])
    a2 = acc_ref[1, 0:2 * dh, :] * (1.0 / acc_ref[1, 2 * dh:2 * dh + 1, :])
    lv = lam_ref[...]
    lam = (jnp.exp(jnp.sum(lv[0:1] * lv[1:2], axis=-1, keepdims=True))
           - jnp.exp(jnp.sum(lv[2:3] * lv[3:4], axis=-1, keepdims=True)) + lambda_init)
    o_t = a1 - lam * a2
    msq = jnp.mean(o_t * o_t, axis=0, keepdims=True)
    o_t = o_t * lax.rsqrt(msq + LN_EPS)
    o_ref[...] = (o_t.T * g_ref[...] * (1.0 - lambda_init)).astype(BF16)


def _diff_attn(qt, k, vt, lam_q1, lam_k1, lam_q2, lam_k2, subln_g, lambda_init, *, blk):
    bn, s, hd = k.shape
    dh = lam_q1.shape[0]
    heads = hd // (2 * dh)
    nb = s // blk
    lamv = jnp.zeros((8, LANES), F32).at[0:4, 0:dh].set(jnp.stack([lam_q1, lam_k1, lam_q2, lam_k2]))
    return pl.pallas_call(
        functools.partial(_diff_attn_kernel, blk=blk, dh=dh, lambda_init=lambda_init),
        grid=(bn, heads, nb),
        in_specs=[
            pl.BlockSpec((None, None, None, 2 * dh, blk), lambda b, h, i: (b, h, i, 0, 0)),
            pl.BlockSpec((None, s, 2 * dh), lambda b, h, i: (b, 0, h)),
            pl.BlockSpec((None, None, nb, 2 * dh + ONES_ROWS, blk), lambda b, h, i: (b, h, 0, 0, 0)),
            pl.BlockSpec((8, LANES), lambda b, h, i: (0, 0)),
            pl.BlockSpec((1, 2 * dh), lambda b, h, i: (0, 0)),
        ],
        out_specs=pl.BlockSpec((None, blk, 2 * dh), lambda b, h, i: (b, i, h)),
        out_shape=jax.ShapeDtypeStruct((bn, s, hd), BF16),
        scratch_shapes=[pltpu.VMEM((2, 1, blk), F32),
                        pltpu.VMEM((2, 2 * dh + ONES_ROWS, blk), F32), pltpu.VMEM((2, 2 * dh, blk), BF16),
                        pltpu.VMEM((2, blk, blk), F32), pltpu.VMEM((2, blk, blk), F32)],
        compiler_params=_params(3),
        name="diff_attn",
    )(qt, k, vt, lamv, subln_g[None, :])


def _attn_out_kernel(o_ref, x_ref, w_ref, ln_g_ref, ln_b_ref, wr_hi_ref, wr_lo_ref, lstrict_ref,
                     y_ref, route_ref, cnt_ref, base_ref):
    @pl.when(pl.program_id(0) == 0)
    def _():
        base_ref[...] = jnp.zeros_like(base_ref)

    m = _dot(o_ref[...], w_ref[...])
    y = _layer_norm(DEEPNORM_ALPHA * x_ref[...] + m, ln_g_ref[...], ln_b_ref[...])
    y_ref[...] = y
    y_hi = y.astype(BF16)
    y_lo = (y - y_hi.astype(F32)).astype(BF16)
    logits = (_dot(y_hi, wr_hi_ref[...]) + _dot(y_lo, wr_hi_ref[...]) + _dot(y_hi, wr_lo_ref[...]))

    lane = lax.broadcasted_iota(jnp.int32, logits.shape, 1).astype(F32)
    neg = -jnp.inf
    lg = jnp.where(lane < N_EXPERTS, logits, neg)
    v0 = jnp.max(lg, axis=-1, keepdims=True)
    i0 = jnp.min(jnp.where(lg == v0, lane, float(LANES)), axis=-1, keepdims=True)
    lg2 = jnp.where(lane == i0, neg, lg)
    v1 = jnp.max(lg2, axis=-1, keepdims=True)
    i1 = jnp.min(jnp.where(lg2 == v1, lane, float(LANES)), axis=-1, keepdims=True)
    e = jnp.exp(v1 - v0)
    g0 = 1.0 / (1.0 + e)
    g1 = e / (1.0 + e)
    oh0 = lane == i0
    oh1 = lane == i1
    c = jnp.where(oh0, 1.0, 0.0) + jnp.where(oh1, 1.0, 0.0)
    pre = _dot(lstrict_ref[...], c.astype(BF16)) + base_ref[...]
    r0 = jnp.sum(jnp.where(oh0, pre, 0.0), axis=-1, keepdims=True)
    r1 = jnp.sum(jnp.where(oh1, pre, 0.0), axis=-1, keepdims=True)
    base_ref[...] = base_ref[...] + jnp.sum(c, axis=0, keepdims=True)
    cnt_ref[...] = jnp.broadcast_to(base_ref[...], cnt_ref.shape)
    fields = (i0, i1, r0, r1, g0, g1)
    route = jnp.zeros_like(logits)
    for idx, val in enumerate(fields):
        route = jnp.where(lane == idx, val, route)
    route_ref[...] = route


def _attn_out(o2d, x2d, w_out, ln_g, ln_b, w_router, *, tile=512):
    n, d = x2d.shape
    wr = jnp.pad(w_router, ((0, 0), (0, LANES - w_router.shape[1])))
    wr_hi = wr.astype(BF16)
    wr_lo = (wr - wr_hi.astype(F32)).astype(BF16)
    lstrict = (jnp.arange(tile)[:, None] > jnp.arange(tile)[None, :]).astype(BF16)
    row = lambda t: t[None, :]
    tspec = pl.BlockSpec((tile, d), lambda i: (i, 0))
    return pl.pallas_call(
        _attn_out_kernel,
        grid=(n // tile,),
        in_specs=[tspec, tspec, _const_spec((d, d)), _const_spec((1, d)), _const_spec((1, d)),
                  _const_spec((d, LANES)), _const_spec((d, LANES)), _const_spec((tile, tile))],
        out_specs=[tspec, pl.BlockSpec((tile, LANES), lambda i: (i, 0)),
                   pl.BlockSpec((8, LANES), lambda i: (0, 0))],
        out_shape=[jax.ShapeDtypeStruct((n, d), F32), jax.ShapeDtypeStruct((n, LANES), F32),
                   jax.ShapeDtypeStruct((8, LANES), F32)],
        scratch_shapes=[pltpu.VMEM((1, LANES), F32)],
        compiler_params=_params(1),
        name="attn_out_router",
    )(o2d, x2d, w_out.astype(BF16), row(ln_g), row(ln_b), wr_hi, wr_lo, lstrict)


def _route_tables(route, cnt, *, tm):
    n = route.shape[0]
    e01 = route[:, 0:2].astype(jnp.int32)
    r01 = route[:, 2:4].astype(jnp.int32)
    counts = cnt[0, :N_EXPERTS].astype(jnp.int32)
    padded = (counts + tm - 1) // tm * tm
    pend = jnp.cumsum(padded)
    pstart = pend - padded
    start_of = jnp.sum(jnp.where(e01[:, :, None] == jnp.arange(N_EXPERTS)[None, None, :],
                                 pstart[None, None, :], 0), axis=-1)
    dest = start_of + r01
    n_rows = n * TOP_K + N_EXPERTS * tm
    n_blocks = n_rows // tm
    n_used = pend[-1] // tm
    blk_idx = jnp.arange(n_blocks, dtype=jnp.int32)
    blk_start = jnp.minimum(blk_idx, n_used - 1) * tm
    blk_e = jnp.minimum(jnp.sum(blk_start[:, None] >= pend[None, :], axis=1), N_EXPERTS - 1)
    valid = jnp.clip(pstart[blk_e] + counts[blk_e] - blk_idx * tm, 0, tm)
    return (dest, blk_e.astype(jnp.int32), n_used.reshape(1).astype(jnp.int32),
            valid.astype(jnp.int32), n_rows)


SC_CORES = 2
SC_SUBCORES = 16
SC_WORKERS = SC_CORES * SC_SUBCORES
SC_ROWS = 32


def _sc_mesh():
    return plsc.VectorSubcoreMesh(core_axis_name="c", subcore_axis_name="s",
                                  num_cores=SC_CORES, num_subcores=SC_SUBCORES)


def _sc_worker_id():
    return lax.axis_index("s") * SC_CORES + lax.axis_index("c")


def _sc_dispatch(x2d, dest, n_rows):
    n, d = x2d.shape
    per_w = n // SC_WORKERS
    n_chunks = per_w // SC_ROWS
    assert n == SC_WORKERS * n_chunks * SC_ROWS and n_chunks % 2 == 0
    d0 = dest[:, 0].reshape(SC_WORKERS, n_chunks, SC_ROWS)
    d1 = dest[:, 1].reshape(SC_WORKERS, n_chunks, SC_ROWS)

    def body(x_hbm, d0_hbm, d1_hbm, xs_hbm, d0_v, d1_v, rows_v, rsem, s0sem, s1sem):
        wid = _sc_worker_id()
        base = wid * per_w
        pltpu.sync_copy(d0_hbm.at[wid], d0_v)
        pltpu.sync_copy(d1_hbm.at[wid], d1_v)

        def read(c, slot):
            return pltpu.make_async_copy(x_hbm.at[pl.ds(base + c * SC_ROWS, SC_ROWS)],
                                         rows_v.at[slot], rsem.at[slot])

        def scat(idx_v, sem, c, slot):
            return pltpu.make_async_copy(rows_v.at[slot], xs_hbm.at[idx_v.at[c]], sem.at[slot])

        def start_scatters(c, slot):
            scat(d0_v, s0sem, c, slot).start()
            scat(d1_v, s1sem, c, slot).start()

        def wait_scatters(c, slot):
            scat(d0_v, s0sem, c, slot).wait()
            scat(d1_v, s1sem, c, slot).wait()

        read(0, 0).start()

        def pair(j, carry):
            c0 = 2 * j
            read(c0, 0).wait()

            @pl.when(j > 0)
            def _():
                wait_scatters(c0 - 1, 1)

            read(c0 + 1, 1).start()
            start_scatters(c0, 0)
            read(c0 + 1, 1).wait()
            wait_scatters(c0, 0)

            @pl.when(c0 + 2 < n_chunks)
            def _():
                read(c0 + 2, 0).start()

            start_scatters(c0 + 1, 1)
            return carry

        lax.fori_loop(0, n_chunks // 2, pair, 0)
        wait_scatters(n_chunks - 1, 1)

    return pl.kernel(
        body,
        out_type=jax.ShapeDtypeStruct((n_rows, d), x2d.dtype),
        mesh=_sc_mesh(),
        scratch_types=[pltpu.VMEM((n_chunks, SC_ROWS), jnp.int32), pltpu.VMEM((n_chunks, SC_ROWS), jnp.int32),
                       pltpu.VMEM((2, SC_ROWS, d), x2d.dtype), pltpu.SemaphoreType.DMA((2,)),
                       pltpu.SemaphoreType.DMA((2,)), pltpu.SemaphoreType.DMA((2,))],
        name="sc_dispatch",
    )(x2d, d0, d1)


def _sc_gather(table, idx):
    _, d = table.shape
    b = idx.shape[0]
    per_w = b // SC_WORKERS
    n_chunks = per_w // SC_ROWS
    assert b == SC_WORKERS * n_chunks * SC_ROWS and n_chunks % 2 == 0
    idx3 = idx.reshape(SC_WORKERS, n_chunks, SC_ROWS)

    def body(table_hbm, idx_hbm, out_hbm, idx_v, rows_v, gsem, wsem):
        wid = _sc_worker_id()
        base = wid * per_w
        pltpu.sync_copy(idx_hbm.at[wid], idx_v)

        def gather(c, slot):
            return pltpu.make_async_copy(table_hbm.at[idx_v.at[c]], rows_v.at[slot], gsem.at[slot])

        def write(c, slot):
            return pltpu.make_async_copy(rows_v.at[slot],
                                         out_hbm.at[pl.ds(base + c * SC_ROWS, SC_ROWS)], wsem.at[slot])

        gather(0, 0).start()

        def pair(j, carry):
            c0 = 2 * j
            gather(c0, 0).wait()

            @pl.when(j > 0)
            def _():
                write(c0 - 1, 1).wait()

            gather(c0 + 1, 1).start()
            write(c0, 0).start()
            gather(c0 + 1, 1).wait()
            write(c0, 0).wait()

            @pl.when(c0 + 2 < n_chunks)
            def _():
                gather(c0 + 2, 0).start()

            write(c0 + 1, 1).start()
            return carry

        lax.fori_loop(0, n_chunks // 2, pair, 0)
        write(n_chunks - 1, 1).wait()

    return pl.kernel(
        body,
        out_type=jax.ShapeDtypeStruct((b, d), table.dtype),
        mesh=_sc_mesh(),
        scratch_types=[pltpu.VMEM((n_chunks, SC_ROWS), jnp.int32), pltpu.VMEM((2, SC_ROWS, d), table.dtype),
                       pltpu.SemaphoreType.DMA((2,)), pltpu.SemaphoreType.DMA((2,))],
        name="sc_combine_gather",
    )(table, idx3)


def _moe_kernel(blk_e, n_used, valid, xs_ref, wg_ref, wu_ref, wd_ref, y_ref, acc_ref, *, nf):
    m = pl.program_id(0)
    f = pl.program_id(1)

    @pl.when(m < n_used[0])
    def _():
        row = lax.broadcasted_iota(jnp.int32, xs_ref.shape, 0)
        x = jnp.where(row < valid[m], xs_ref[...], 0.0).astype(BF16)
        g = _dot(x, wg_ref[...])
        u = _dot(x, wu_ref[...])
        h = (g * _sigmoid(g) * u).astype(BF16)
        part = _dot(h, wd_ref[...])

        @pl.when(f == 0)
        def _():
            acc_ref[...] = part

        @pl.when(f > 0)
        def _():
            acc_ref[...] += part

        @pl.when(f == nf - 1)
        def _():
            y_ref[...] = acc_ref[...]

    @pl.when((m >= n_used[0]) & (f == 0))
    def _():
        y_ref[...] = jnp.zeros_like(y_ref)


def _moe_grouped(xs, blk_e, n_used, valid, wg, wu, wd, *, tm, tf):
    n_rows, d = xs.shape
    fdim = wg.shape[2]
    nf = fdim // tf
    n_blocks = n_rows // tm

    def f_eff(m, f, nu):
        return jnp.where(m < nu[0], f, nf - 1)

    grid_spec = pltpu.PrefetchScalarGridSpec(
        num_scalar_prefetch=3,
        grid=(n_blocks, nf),
        in_specs=[
            pl.BlockSpec((tm, d), lambda m, f, be, nu, va: (jnp.minimum(m, nu[0] - 1), 0)),
            pl.BlockSpec((None, d, tf), lambda m, f, be, nu, va: (be[m], 0, f_eff(m, f, nu))),
            pl.BlockSpec((None, d, tf), lambda m, f, be, nu, va: (be[m], 0, f_eff(m, f, nu))),
            pl.BlockSpec((None, tf, d), lambda m, f, be, nu, va: (be[m], f_eff(m, f, nu), 0)),
        ],
        out_specs=pl.BlockSpec((tm, d), lambda m, f, be, nu, va: (m, 0)),
        scratch_shapes=[pltpu.VMEM((tm, d), F32)],
    )
    return pl.pallas_call(
        functools.partial(_moe_kernel, nf=nf),
        grid_spec=grid_spec,
        out_shape=jax.ShapeDtypeStruct((n_rows, d), F32),
        compiler_params=_params(2),
        name="moe_grouped",
    )(blk_e, n_used, valid, xs, wg, wu, wd)


def _final_kernel(x_ref, y0_ref, y1_ref, route_ref, p_ref, ln_g_ref, ln_b_ref, wpg_ref, bpg_ref,
                  wpp_ref, o_ref):
    route = route_ref[...]
    f = route[:, 4:5] * y0_ref[...] + route[:, 5:6] * y1_ref[...]
    y = _layer_norm(DEEPNORM_ALPHA * x_ref[...] + f, ln_g_ref[...], ln_b_ref[...])
    o_ref[...] = _ple(y, p_ref[...], wpg_ref, bpg_ref, wpp_ref)


def _final(x2d, y01, route, p2d, ln_g, ln_b, wpg, bpg, wpp, *, tile=512):
    n, d = x2d.shape
    pd = p2d.shape[1]
    row = lambda t: t[None, :]
    tspec = pl.BlockSpec((tile, d), lambda i: (i, 0))
    second = pl.BlockSpec((tile, d), lambda i: (i + n // tile, 0))
    y0 = y1 = y01
    return pl.pallas_call(
        _final_kernel,
        grid=(n // tile,),
        in_specs=[tspec, tspec, second, pl.BlockSpec((tile, LANES), lambda i: (i, 0)),
                  pl.BlockSpec((tile, pd), lambda i: (i, 0)),
                  _const_spec((1, d)), _const_spec((1, d)),
                  _const_spec((d, d)), _const_spec((1, d)), _const_spec((pd, d))],
        out_specs=tspec,
        out_shape=jax.ShapeDtypeStruct((n, d), F32),
        compiler_params=_params(1),
        name="moe_combine_ple",
    )(x2d, y0, y1, route, p2d, row(ln_g), row(ln_b), wpg.astype(BF16), row(bpg), wpp.astype(BF16))


def _even_layer(x, p_i, w_in, a_ln_g, a_ln_b, a_ws, a_bs, w_gate_up, b_gate, norm_g, w_out,
                ln1_g, ln1_b, wg, wu, wd, ln2_g, ln2_b, wpp, wpg, bpg):
    bn, s, d = x.shape
    x1 = _even_mixer(x, w_in, a_ln_g, a_ln_b, a_ws, a_bs, w_gate_up, b_gate, norm_g, w_out, ln1_g, ln1_b)
    x2 = _ffn_ple(x1.reshape(bn * s, d), p_i.reshape(bn * s, -1), wg, wu, wd, ln2_g, ln2_b, wpg, bpg, wpp)
    return x2.reshape(bn, s, d)


def _odd_layer(x, p_i, layer_idx, w_qkv, lam_q1, lam_k1, lam_q2, lam_k2, subln_g, w_out, ln1_g, ln1_b,
               w_router, ewg, ewu, ewd, ln2_g, ln2_b, wpp, wpg, bpg, *, attn_blk=512, moe_tm=512,
               moe_tf=1792):
    bn, s, d = x.shape
    n = bn * s
    lambda_init = 0.8 - 0.6 * math.exp(-0.3 * layer_idx)
    q, k, vt = _qkv(x, w_qkv, dh=lam_q1.shape[0], tile=attn_blk)
    o = _diff_attn(q, k, vt, lam_q1, lam_k1, lam_q2, lam_k2, subln_g, lambda_init, blk=attn_blk)
    x3, route, cnt = _attn_out(o.reshape(n, d), x.reshape(n, d), w_out, ln1_g, ln1_b, w_router)
    dest, blk_e, n_used, valid, n_rows = _route_tables(route, cnt, tm=moe_tm)
    xs = _sc_dispatch(x3, dest, n_rows)
    y = _moe_grouped(xs, blk_e, n_used, valid, ewg.astype(BF16), ewu.astype(BF16), ewd.astype(BF16),
                     tm=moe_tm, tf=moe_tf)
    y01 = _sc_gather(y, jnp.concatenate([dest[:, 0], dest[:, 1]]))
    out = _final(x3, y01, route, p_i.reshape(n, -1), ln2_g, ln2_b, wpg, bpg, wpp)
    return out.reshape(bn, s, d)


def kernel(x, p, e_w_in, e_a_ln_g, e_a_ln_b, e_a_ws, e_a_bs, e_b_w_gate_up, e_b_b_gate, e_b_norm_g, e_w_out, e_ln1_g, e_ln1_b, e_ffn_wg, e_ffn_wu, e_ffn_wd, e_ln2_g, e_ln2_b, o_w_qkv, o_lam_q1, o_lam_k1, o_lam_q2, o_lam_k2, o_subln_g, o_w_out, o_ln1_g, o_ln1_b, o_router, o_exp_wg, o_exp_wu, o_exp_wd, o_ln2_g, o_ln2_b, ple_w_proj, ple_w_gate, ple_b_gate):
    for i in range(DEPTH):
        j = i // 2
        if i % 2 == 0:
            x = _even_layer(x, p[i], e_w_in[j], e_a_ln_g[j], e_a_ln_b[j], e_a_ws[j], e_a_bs[j],
                            e_b_w_gate_up[j], e_b_b_gate[j], e_b_norm_g[j], e_w_out[j],
                            e_ln1_g[j], e_ln1_b[j], e_ffn_wg[j], e_ffn_wu[j], e_ffn_wd[j],
                            e_ln2_g[j], e_ln2_b[j], ple_w_proj[i], ple_w_gate[i], ple_b_gate[i])
        else:
            x = _odd_layer(x, p[i], i, o_w_qkv[j], o_lam_q1[j], o_lam_k1[j], o_lam_q2[j], o_lam_k2[j],
                           o_subln_g[j], o_w_out[j], o_ln1_g[j], o_ln1_b[j], o_router[j],
                           o_exp_wg[j], o_exp_wu[j], o_exp_wd[j], o_ln2_g[j], o_ln2_b[j],
                           ple_w_proj[i], ple_w_gate[i], ple_b_gate[i])
    return x
```

```python
import functools
import math

import jax
import jax.numpy as jnp
from jax import lax
from jax.experimental import pallas as pl
from jax.experimental.pallas import tpu as pltpu
from jax.experimental.pallas import tpu_sc as plsc

F32 = jnp.float32
BF16 = jnp.bfloat16

DEPTH = 2
DEEPNORM_ALPHA = (2.0 * DEPTH) ** 0.25
LN_EPS = 1e-5
A_CHUNK = 128
A_GROUPS = 8
B_HEADS = 4
B_CHUNK = 64
B_TAU = 16.0
C_HEADS = 8
N_EXPERTS = 8
TOP_K = 2
LANES = 128
LOG2_E = 1.4426950408889634
VMEM_LIMIT = 56 * 1024 * 1024

NT_DIMS = (((1,), (1,)), ((), ()))
TN_DIMS = (((0,), (0,)), ((), ()))


def _dot(a, b):
    return jnp.dot(a, b, preferred_element_type=F32)


def _layer_norm(x, g, b):
    mu = jnp.mean(x, axis=-1, keepdims=True)
    xc = x - mu
    var = jnp.mean(xc * xc, axis=-1, keepdims=True)
    return xc * lax.rsqrt(var + LN_EPS) * g + b


def _sigmoid(x):
    return 1.0 / (1.0 + jnp.exp(-x))


def _split3(a):
    hi = a.astype(BF16)
    r1 = a - hi.astype(F32)
    mid = r1.astype(BF16)
    lo = (r1 - mid.astype(F32)).astype(BF16)
    return hi, mid, lo


def _const_spec(shape):
    zeros = (0,) * len(shape)
    return pl.BlockSpec(shape, lambda *_: zeros, pipeline_mode=pl.Buffered(1))


def _params(n_axes):
    return pltpu.CompilerParams(dimension_semantics=("arbitrary",) * n_axes,
                                vmem_limit_bytes=VMEM_LIMIT)


def _even_mixer_kernel(x_ref, w_in_ref, a_g_ref, a_b_ref, wcat_ref, abias_ref, mstack_ref,
                       w_up_ref, b_gate_ref, ng_ref, w_out_ref, ln_g_ref, ln_b_ref,
                       o_ref, state_ref, *, tile, aw, dkh, dvh):
    @pl.when(pl.program_id(1) == 0)
    def _():
        state_ref[...] = jnp.zeros_like(state_ref)

    hk = B_HEADS * dkh
    hv = B_HEADS * dvh
    x = x_ref[...]
    z = _dot(x.astype(BF16), w_in_ref[...])

    u = jax.nn.gelu(z[:, 0:aw])
    v = _layer_norm(jax.nn.gelu(z[:, aw:2 * aw]), a_g_ref[...], a_b_ref[...])
    gd = aw // A_GROUPS
    rows = lax.broadcasted_iota(jnp.int32, (A_CHUNK, A_GROUPS * A_CHUNK), 0)
    cols = lax.broadcasted_iota(jnp.int32, (A_CHUNK, A_GROUPS * A_CHUNK), 1)
    wcat = jnp.where((cols % A_CHUNK) <= rows, wcat_ref[...], 0.0).astype(BF16)
    r_bd = lax.broadcasted_iota(jnp.int32, (A_GROUPS * A_CHUNK, aw), 0) // A_CHUNK
    c_bd = lax.broadcasted_iota(jnp.int32, (A_GROUPS * A_CHUNK, aw), 1) // gd
    mask_bd = r_bd == c_bd
    ya_parts = []
    for c in range(tile // A_CHUNK):
        sl = slice(c * A_CHUNK, (c + 1) * A_CHUNK)
        v_rep = jnp.concatenate([v[sl]] * A_GROUPS, axis=0)
        v_bd = jnp.where(mask_bd, v_rep, 0.0).astype(BF16)
        sg = _dot(wcat, v_bd) + abias_ref[...]
        ya_parts.append(u[sl] * sg)
    ya = jnp.concatenate(ya_parts, axis=0)

    o0 = 2 * aw
    q = z[:, o0:o0 + hk] * (dkh ** -0.5)
    k = z[:, o0 + hk:o0 + 2 * hk]
    vv = z[:, o0 + 2 * hk:o0 + 2 * hk + hv]
    r = z[:, o0 + 2 * hk + hv:o0 + 2 * hk + 2 * hv]
    g_low = z[:, o0 + 2 * hk + 2 * hv:]
    pre = _dot(g_low.astype(BF16), w_up_ref[...]) + b_gate_ref[...]
    log_a = (jnp.minimum(pre, 0.0) - jnp.log1p(jnp.exp(-jnp.abs(pre)))) * (1.0 / B_TAU)
    la_hi, la_mid, la_lo = _split3(log_a)
    ms = mstack_ref[...]
    cs = _dot(ms, la_hi) + _dot(ms, la_mid) + _dot(ms, la_lo)
    b_cum = cs[0:tile]
    d_mid = cs[tile:2 * tile]
    d_last = cs[2 * tile:3 * tile]
    qe = (q * jnp.exp(d_mid)).astype(BF16)
    ke = k * jnp.exp(-d_mid)
    kd = (k * jnp.exp(d_last)).astype(BF16)
    qb = (q * jnp.exp(b_cum)).astype(BF16)
    dec = jnp.exp(b_cum + d_last)
    vvb = vv.astype(BF16)

    mask_kk = (lax.broadcasted_iota(jnp.int32, (hk, hk), 0) // dkh
               == lax.broadcasted_iota(jnp.int32, (hk, hk), 1) // dkh)
    mask_vbd = (lax.broadcasted_iota(jnp.int32, (hk, hv), 0) // dkh
                == lax.broadcasted_iota(jnp.int32, (hk, hv), 1) // dvh)
    mask_st = (lax.broadcasted_iota(jnp.int32, (hv, hk), 0) // dvh
               == lax.broadcasted_iota(jnp.int32, (hv, hk), 1) // dkh)
    causal = ((lax.broadcasted_iota(jnp.int32, (B_CHUNK, hk), 1) % B_CHUNK)
              <= lax.broadcasted_iota(jnp.int32, (B_CHUNK, hk), 0))

    st = state_ref[...]
    o_parts = []
    for c in range(tile // B_CHUNK):
        sl = slice(c * B_CHUNK, (c + 1) * B_CHUNK)
        ke_bd = jnp.where(mask_kk, jnp.concatenate([ke[sl]] * B_HEADS, axis=0), 0.0).astype(BF16)
        s_cat = lax.dot_general(qe[sl], ke_bd, NT_DIMS, preferred_element_type=F32)
        s_cat = jnp.where(causal, s_cat, 0.0).astype(BF16)
        v_bd = jnp.where(mask_vbd, jnp.concatenate([vv[sl]] * B_HEADS, axis=0), 0.0).astype(BF16)
        o_c = _dot(s_cat, v_bd) + lax.dot_general(qb[sl], st.astype(BF16), NT_DIMS,
                                                   preferred_element_type=F32)
        kv_t = lax.dot_general(vvb[sl], kd[sl], TN_DIMS, preferred_element_type=F32)
        st = dec[c * B_CHUNK:c * B_CHUNK + 1] * st + jnp.where(mask_st, kv_t, 0.0)
        o_parts.append(o_c)
    state_ref[...] = st
    o = jnp.concatenate(o_parts, axis=0)

    yb_parts = []
    for h in range(B_HEADS):
        oh = o[:, h * dvh:(h + 1) * dvh]
        msq = jnp.mean(oh * oh, axis=-1, keepdims=True)
        yb_parts.append(oh * lax.rsqrt(msq + LN_EPS))
    yb = jnp.concatenate(yb_parts, axis=1) * ng_ref[...] * (r * _sigmoid(r))

    y_cat = jnp.concatenate([ya, yb], axis=1).astype(BF16)
    m = _dot(y_cat, w_out_ref[...])
    o_ref[...] = _layer_norm(DEEPNORM_ALPHA * x + m, ln_g_ref[...], ln_b_ref[...])


def _gla_cumsum_matrices(tile):
    i = jnp.arange(tile)[:, None]
    j = jnp.arange(tile)[None, :]
    same = (i // B_CHUNK) == (j // B_CHUNK)
    m_cum = same & (j <= i)
    m_mid = same & (j <= (i // B_CHUNK) * B_CHUNK + B_CHUNK // 2 - 1)
    m_last = same
    f = lambda t: t.astype(F32)
    return jnp.concatenate([f(m_cum), f(m_cum) - f(m_mid), f(m_last) - f(m_cum)], axis=0).astype(BF16)


def _even_mixer(x, w_in, a_ln_g, a_ln_b, a_ws, a_bs, w_gate_up, b_gate, norm_g, w_out, ln_g, ln_b,
                *, tile=256):
    bn, s, d = x.shape
    aw = a_ln_g.shape[0]
    hk = w_gate_up.shape[1]
    dkh = hk // B_HEADS
    dvh = norm_g.shape[0]
    hv = B_HEADS * dvh
    rank = w_gate_up.shape[0]
    main = 2 * aw + 2 * hk + 2 * hv
    gd = aw // A_GROUPS
    w_in_p = jnp.concatenate([w_in[:, :main], jnp.pad(w_in[:, main:], ((0, 0), (0, LANES - rank)))],
                             axis=1).astype(BF16)
    w_up_p = jnp.pad(w_gate_up, ((0, LANES - rank), (0, 0))).astype(BF16)
    wcat = jnp.transpose(a_ws, (1, 0, 2)).reshape(A_CHUNK, A_GROUPS * A_CHUNK)
    abias = jnp.repeat(a_bs.T, gd, axis=1)
    mstack = _gla_cumsum_matrices(tile)
    ng = jnp.tile(norm_g, B_HEADS)[None, :]
    row = lambda t: t[None, :]
    kern = functools.partial(_even_mixer_kernel, tile=tile, aw=aw, dkh=dkh, dvh=dvh)
    tile_spec = pl.BlockSpec((None, tile, d), lambda b, i: (b, i, 0))
    return pl.pallas_call(
        kern,
        grid=(bn, s // tile),
        in_specs=[tile_spec, _const_spec(w_in_p.shape), _const_spec((1, aw)), _const_spec((1, aw)),
                  _const_spec(wcat.shape), _const_spec(abias.shape), _const_spec(mstack.shape),
                  _const_spec(w_up_p.shape), _const_spec((1, hk)), _const_spec((1, hv)),
                  _const_spec(w_out.shape), _const_spec((1, d)), _const_spec((1, d))],
        out_specs=tile_spec,
        out_shape=jax.ShapeDtypeStruct(x.shape, F32),
        scratch_shapes=[pltpu.VMEM((hv, hk), F32)],
        compiler_params=_params(2),
        name="even_mixer",
    )(x, w_in_p, row(a_ln_g), row(a_ln_b), wcat, abias, mstack, w_up_p, row(b_gate), ng,
      w_out.astype(BF16), row(ln_g), row(ln_b))


def _ple(y, p, wpg_ref, bpg_ref, wpp_ref):
    gate = _sigmoid(_dot(y.astype(BF16), wpg_ref[...]) + bpg_ref[...])
    return y + gate * _dot(p.astype(BF16), wpp_ref[...])


def _ffn_ple_kernel(x_ref, p_ref, wg_ref, wu_ref, wd_ref, ln_g_ref, ln_b_ref, wpg_ref, bpg_ref,
                    wpp_ref, o_ref, *, chunks):
    x = x_ref[...]
    xb = x.astype(BF16)
    acc = None
    for lo, hi in chunks:
        g = _dot(xb, wg_ref[:, lo:hi])
        u = _dot(xb, wu_ref[:, lo:hi])
        h = (g * _sigmoid(g) * u).astype(BF16)
        part = _dot(h, wd_ref[lo:hi, :])
        acc = part if acc is None else acc + part
    y = _layer_norm(DEEPNORM_ALPHA * x + acc, ln_g_ref[...], ln_b_ref[...])
    o_ref[...] = _ple(y, p_ref[...], wpg_ref, bpg_ref, wpp_ref)


def _ffn_ple(x2d, p2d, wg, wu, wd, ln_g, ln_b, wpg, bpg, wpp, *, tile=512, fchunk=1024):
    n, d = x2d.shape
    f = wg.shape[1]
    pd = p2d.shape[1]
    chunks = tuple((lo, min(lo + fchunk, f)) for lo in range(0, f, fchunk))
    row = lambda t: t[None, :]
    return pl.pallas_call(
        functools.partial(_ffn_ple_kernel, chunks=chunks),
        grid=(n // tile,),
        in_specs=[pl.BlockSpec((tile, d), lambda i: (i, 0)), pl.BlockSpec((tile, pd), lambda i: (i, 0)),
                  _const_spec((d, f)), _const_spec((d, f)), _const_spec((f, d)),
                  _const_spec((1, d)), _const_spec((1, d)),
                  _const_spec((d, d)), _const_spec((1, d)), _const_spec((pd, d))],
        out_specs=pl.BlockSpec((tile, d), lambda i: (i, 0)),
        out_shape=jax.ShapeDtypeStruct((n, d), F32),
        compiler_params=_params(1),
        name="ffn_ple",
    )(x2d, p2d, wg.astype(BF16), wu.astype(BF16), wd.astype(BF16), row(ln_g), row(ln_b),
      wpg.astype(BF16), row(bpg), wpp.astype(BF16))


def _qkv_kernel(x_ref, w_ref, q_ref, kt_ref, v_ref, *, hd, scale):
    z = _dot(x_ref[...].astype(BF16), w_ref[...])
    q_ref[...] = (z[:, 0:hd] * scale).astype(BF16)
    k_t = z[:, hd:2 * hd].T.astype(BF16)
    heads, dk2, tile = kt_ref.shape
    ones = jnp.ones((tile, dk2), BF16)
    for h in range(heads):
        kt_ref[h] = k_t[h * dk2:(h + 1) * dk2, :]
        v_ref[:, 2 * h * dk2:(2 * h + 1) * dk2] = z[:, 2 * hd + h * dk2:2 * hd + (h + 1) * dk2].astype(BF16)
        v_ref[:, (2 * h + 1) * dk2:(2 * h + 2) * dk2] = ones


def _qkv(x, w_qkv, *, dh, tile):
    bn, s, d = x.shape
    hd = w_qkv.shape[1] // 3
    heads = hd // (2 * dh)
    return pl.pallas_call(
        functools.partial(_qkv_kernel, hd=hd, scale=dh ** -0.5 * LOG2_E),
        grid=(bn, s // tile),
        in_specs=[pl.BlockSpec((None, tile, d), lambda b, i: (b, i, 0)), _const_spec(w_qkv.shape)],
        out_specs=[pl.BlockSpec((None, tile, hd), lambda b, i: (b, i, 0)),
                   pl.BlockSpec((None, heads, None, 2 * dh, tile), lambda b, i: (b, 0, i, 0, 0)),
                   pl.BlockSpec((None, tile, 2 * hd), lambda b, i: (b, i, 0))],
        out_shape=[jax.ShapeDtypeStruct((bn, s, hd), BF16),
                   jax.ShapeDtypeStruct((bn, heads, s // tile, 2 * dh, tile), BF16),
                   jax.ShapeDtypeStruct((bn, s, 2 * hd), BF16)],
        compiler_params=_params(2),
        name="qkv_proj",
    )(x, w_qkv.astype(BF16))


def _diff_attn_kernel(q_ref, kt_ref, v_ref, lam_ref, g_ref, o_ref, m_ref, acc_ref,
                      qm_ref, sa_ref, sb_ref, *, blk, dh, lambda_init):
    qi = pl.program_id(2)
    m_ref[...] = jnp.full_like(m_ref, -1e30)
    acc_ref[...] = jnp.zeros_like(acc_ref)

    q = q_ref[...]
    lane = lax.broadcasted_iota(jnp.int32, q.shape, 1)
    zero = jnp.zeros_like(q)
    qm_ref[0] = jnp.where(lane < dh, q, zero)
    qm_ref[1] = jnp.where(lane >= dh, q, zero)
    reps = blk // LANES

    def scores(ki, s_ref):
        kt_blk = kt_ref[ki]
        for c in range(2):
            s_ref[c] = _dot(qm_ref[c], kt_blk)

    def softmax_pv(s_ref, ki, masked):
        off = pl.multiple_of(ki * blk, blk)
        v_blk = v_ref[pl.ds(off, blk), :]
        for c in range(2):
            s = s_ref[c]
            if masked:
                row = lax.broadcasted_iota(jnp.int32, s.shape, 0)
                col = lax.broadcasted_iota(jnp.int32, s.shape, 1)
                s = jnp.where(col <= row, s, -1e30)
            m_prev = m_ref[c]
            m_new = jnp.maximum(m_prev, jnp.max(s, axis=-1, keepdims=True))
            corr = jnp.exp2(m_prev - m_new)
            p = jnp.exp2(s - jnp.concatenate([m_new] * reps, axis=1))
            acc_ref[c] = jnp.concatenate([corr, corr], axis=1) * acc_ref[c] + _dot(p.astype(BF16), v_blk)
            m_ref[c] = m_new

    scores(0, sa_ref)

    def pair(j, carry):
        k0 = 2 * j
        scores(k0 + 1, sb_ref)
        softmax_pv(sa_ref, k0, False)
        scores(k0 + 2, sa_ref)
        softmax_pv(sb_ref, k0 + 1, False)
        return carry

    lax.fori_loop(0, qi // 2, pair, 0)

    @pl.when(qi % 2 == 0)
    def _():
        softmax_pv(sa_ref, qi, True)

    @pl.when(qi % 2 == 1)
    def _():
        scores(qi, sb_ref)
        softmax_pv(sa_ref, qi - 1, False)
        softmax_pv(sb_ref, qi, True)

    a1 = acc_ref[0, :, 0:2 * dh] * (1.0 / acc_ref[0, :, 2 * dh:4 * dh])
    a2 = acc_ref[1, :, 0:2 * dh] * (1.0 / acc_ref[1, :, 2 * dh:4 * dh])
    lv = lam_ref[...]
    lam = (jnp.exp(jnp.sum(lv[0:1] * lv[1:2], axis=-1, keepdims=True))
           - jnp.exp(jnp.sum(lv[2:3] * lv[3:4], axis=-1, keepdims=True)) + lambda_init)
    o = a1 - lam * a2
    msq = jnp.mean(o * o, axis=-1, keepdims=True)
    o_ref[...] = (o * lax.rsqrt(msq + LN_EPS) * g_ref[...] * (1.0 - lambda_init)).astype(BF16)


def _diff_attn(q, kt, v_ext, lam_q1, lam_k1, lam_q2, lam_k2, subln_g, lambda_init, *, blk):
    bn, s, hd = q.shape
    dh = lam_q1.shape[0]
    heads = hd // (2 * dh)
    nb = s // blk
    lamv = jnp.zeros((8, LANES), F32).at[0:4, 0:dh].set(jnp.stack([lam_q1, lam_k1, lam_q2, lam_k2]))
    return pl.pallas_call(
        functools.partial(_diff_attn_kernel, blk=blk, dh=dh, lambda_init=lambda_init),
        grid=(bn, heads, nb),
        in_specs=[
            pl.BlockSpec((None, blk, 2 * dh), lambda b, h, i: (b, i, h)),
            pl.BlockSpec((None, None, nb, 2 * dh, blk), lambda b, h, i: (b, h, 0, 0, 0)),
            pl.BlockSpec((None, s, 4 * dh), lambda b, h, i: (b, 0, h)),
            pl.BlockSpec((8, LANES), lambda b, h, i: (0, 0)),
            pl.BlockSpec((1, 2 * dh), lambda b, h, i: (0, 0)),
        ],
        out_specs=pl.BlockSpec((None, blk, 2 * dh), lambda b, h, i: (b, i, h)),
        out_shape=jax.ShapeDtypeStruct((bn, s, hd), BF16),
        scratch_shapes=[pltpu.VMEM((2, blk, LANES), F32),
                        pltpu.VMEM((2, blk, 4 * dh), F32), pltpu.VMEM((2, blk, 2 * dh), BF16),
                        pltpu.VMEM((2, blk, blk), F32), pltpu.VMEM((2, blk, blk), F32)],
        compiler_params=_params(3),
        name="diff_attn",
    )(q, kt, v_ext, lamv, subln_g[None, :])


def _attn_out_kernel(o_ref, x_ref, w_ref, ln_g_ref, ln_b_ref, wr_hi_ref, wr_lo_ref, lstrict_ref,
                     y_ref, route_ref, cnt_ref, base_ref):
    @pl.when(pl.program_id(0) == 0)
    def _():
        base_ref[...] = jnp.zeros_like(base_ref)

    m = _dot(o_ref[...], w_ref[...])
    y = _layer_norm(DEEPNORM_ALPHA * x_ref[...] + m, ln_g_ref[...], ln_b_ref[...])
    y_ref[...] = y
    y_hi = y.astype(BF16)
    y_lo = (y - y_hi.astype(F32)).astype(BF16)
    logits = (_dot(y_hi, wr_hi_ref[...]) + _dot(y_lo, wr_hi_ref[...]) + _dot(y_hi, wr_lo_ref[...]))

    lane = lax.broadcasted_iota(jnp.int32, logits.shape, 1).astype(F32)
    neg = -jnp.inf
    lg = jnp.where(lane < N_EXPERTS, logits, neg)
    v0 = jnp.max(lg, axis=-1, keepdims=True)
    i0 = jnp.min(jnp.where(lg == v0, lane, float(LANES)), axis=-1, keepdims=True)
    lg2 = jnp.where(lane == i0, neg, lg)
    v1 = jnp.max(lg2, axis=-1, keepdims=True)
    i1 = jnp.min(jnp.where(lg2 == v1, lane, float(LANES)), axis=-1, keepdims=True)
    e = jnp.exp(v1 - v0)
    g0 = 1.0 / (1.0 + e)
    g1 = e / (1.0 + e)
    oh0 = lane == i0
    oh1 = lane == i1
    c = jnp.where(oh0, 1.0, 0.0) + jnp.where(oh1, 1.0, 0.0)
    pre = _dot(lstrict_ref[...], c.astype(BF16)) + base_ref[...]
    r0 = jnp.sum(jnp.where(oh0, pre, 0.0), axis=-1, keepdims=True)
    r1 = jnp.sum(jnp.where(oh1, pre, 0.0), axis=-1, keepdims=True)
    base_ref[...] = base_ref[...] + jnp.sum(c, axis=0, keepdims=True)
    cnt_ref[...] = jnp.broadcast_to(base_ref[...], cnt_ref.shape)
    fields = (i0, i1, r0, r1, g0, g1)
    route = jnp.zeros_like(logits)
    for idx, val in enumerate(fields):
        route = jnp.where(lane == idx, val, route)
    route_ref[...] = route


def _attn_out(o2d, x2d, w_out, ln_g, ln_b, w_router, *, tile=512):
    n, d = x2d.shape
    wr = jnp.pad(w_router, ((0, 0), (0, LANES - w_router.shape[1])))
    wr_hi = wr.astype(BF16)
    wr_lo = (wr - wr_hi.astype(F32)).astype(BF16)
    lstrict = (jnp.arange(tile)[:, None] > jnp.arange(tile)[None, :]).astype(BF16)
    row = lambda t: t[None, :]
    tspec = pl.BlockSpec((tile, d), lambda i: (i, 0))
    return pl.pallas_call(
        _attn_out_kernel,
        grid=(n // tile,),
        in_specs=[tspec, tspec, _const_spec((d, d)), _const_spec((1, d)), _const_spec((1, d)),
                  _const_spec((d, LANES)), _const_spec((d, LANES)), _const_spec((tile, tile))],
        out_specs=[tspec, pl.BlockSpec((tile, LANES), lambda i: (i, 0)),
                   pl.BlockSpec((8, LANES), lambda i: (0, 0))],
        out_shape=[jax.ShapeDtypeStruct((n, d), F32), jax.ShapeDtypeStruct((n, LANES), F32),
                   jax.ShapeDtypeStruct((8, LANES), F32)],
        scratch_shapes=[pltpu.VMEM((1, LANES), F32)],
        compiler_params=_params(1),
        name="attn_out_router",
    )(o2d, x2d, w_out.astype(BF16), row(ln_g), row(ln_b), wr_hi, wr_lo, lstrict)


def _route_tables(route, cnt, *, tm):
    n = route.shape[0]
    e01 = route[:, 0:2].astype(jnp.int32)
    r01 = route[:, 2:4].astype(jnp.int32)
    counts = cnt[0, :N_EXPERTS].astype(jnp.int32)
    padded = (counts + tm - 1) // tm * tm
    pend = jnp.cumsum(padded)
    pstart = pend - padded
    start_of = jnp.sum(jnp.where(e01[:, :, None] == jnp.arange(N_EXPERTS)[None, None, :],
                                 pstart[None, None, :], 0), axis=-1)
    dest = start_of + r01
    n_rows = n * TOP_K + N_EXPERTS * tm
    n_blocks = n_rows // tm
    n_used = pend[-1] // tm
    blk_idx = jnp.arange(n_blocks, dtype=jnp.int32)
    blk_start = jnp.minimum(blk_idx, n_used - 1) * tm
    blk_e = jnp.minimum(jnp.sum(blk_start[:, None] >= pend[None, :], axis=1), N_EXPERTS - 1)
    valid = jnp.clip(pstart[blk_e] + counts[blk_e] - blk_idx * tm, 0, tm)
    return (dest, blk_e.astype(jnp.int32), n_used.reshape(1).astype(jnp.int32),
            valid.astype(jnp.int32), n_rows)


SC_CORES = 2
SC_SUBCORES = 16
SC_WORKERS = SC_CORES * SC_SUBCORES
SC_ROWS = 32


def _sc_mesh():
    return plsc.VectorSubcoreMesh(core_axis_name="c", subcore_axis_name="s",
                                  num_cores=SC_CORES, num_subcores=SC_SUBCORES)


def _sc_worker_id():
    return lax.axis_index("s") * SC_CORES + lax.axis_index("c")


def _sc_dispatch(x2d, dest, n_rows):
    n, d = x2d.shape
    per_w = n // SC_WORKERS
    n_chunks = per_w // SC_ROWS
    assert n == SC_WORKERS * n_chunks * SC_ROWS and n_chunks % 2 == 0
    d0 = dest[:, 0].reshape(SC_WORKERS, n_chunks, SC_ROWS)
    d1 = dest[:, 1].reshape(SC_WORKERS, n_chunks, SC_ROWS)

    def body(x_hbm, d0_hbm, d1_hbm, xs_hbm, d0_v, d1_v, rows_v, rsem, s0sem, s1sem):
        wid = _sc_worker_id()
        base = wid * per_w
        pltpu.sync_copy(d0_hbm.at[wid], d0_v)
        pltpu.sync_copy(d1_hbm.at[wid], d1_v)

        def read(c, slot):
            return pltpu.make_async_copy(x_hbm.at[pl.ds(base + c * SC_ROWS, SC_ROWS)],
                                         rows_v.at[slot], rsem.at[slot])

        def scat(idx_v, sem, c, slot):
            return pltpu.make_async_copy(rows_v.at[slot], xs_hbm.at[idx_v.at[c]], sem.at[slot])

        def start_scatters(c, slot):
            scat(d0_v, s0sem, c, slot).start()
            scat(d1_v, s1sem, c, slot).start()

        def wait_scatters(c, slot):
            scat(d0_v, s0sem, c, slot).wait()
            scat(d1_v, s1sem, c, slot).wait()

        read(0, 0).start()

        def pair(j, carry):
            c0 = 2 * j
            read(c0, 0).wait()

            @pl.when(j > 0)
            def _():
                wait_scatters(c0 - 1, 1)

            read(c0 + 1, 1).start()
            start_scatters(c0, 0)
            read(c0 + 1, 1).wait()
            wait_scatters(c0, 0)

            @pl.when(c0 + 2 < n_chunks)
            def _():
                read(c0 + 2, 0).start()

            start_scatters(c0 + 1, 1)
            return carry

        lax.fori_loop(0, n_chunks // 2, pair, 0)
        wait_scatters(n_chunks - 1, 1)

    return pl.kernel(
        body,
        out_type=jax.ShapeDtypeStruct((n_rows, d), x2d.dtype),
        mesh=_sc_mesh(),
        scratch_types=[pltpu.VMEM((n_chunks, SC_ROWS), jnp.int32), pltpu.VMEM((n_chunks, SC_ROWS), jnp.int32),
                       pltpu.VMEM((2, SC_ROWS, d), x2d.dtype), pltpu.SemaphoreType.DMA((2,)),
                       pltpu.SemaphoreType.DMA((2,)), pltpu.SemaphoreType.DMA((2,))],
        name="sc_dispatch",
    )(x2d, d0, d1)


def _sc_gather(table, idx):
    _, d = table.shape
    b = idx.shape[0]
    per_w = b // SC_WORKERS
    n_chunks = per_w // SC_ROWS
    assert b == SC_WORKERS * n_chunks * SC_ROWS and n_chunks % 2 == 0
    idx3 = idx.reshape(SC_WORKERS, n_chunks, SC_ROWS)

    def body(table_hbm, idx_hbm, out_hbm, idx_v, rows_v, gsem, wsem):
        wid = _sc_worker_id()
        base = wid * per_w
        pltpu.sync_copy(idx_hbm.at[wid], idx_v)

        def gather(c, slot):
            return pltpu.make_async_copy(table_hbm.at[idx_v.at[c]], rows_v.at[slot], gsem.at[slot])

        def write(c, slot):
            return pltpu.make_async_copy(rows_v.at[slot],
                                         out_hbm.at[pl.ds(base + c * SC_ROWS, SC_ROWS)], wsem.at[slot])

        gather(0, 0).start()

        def pair(j, carry):
            c0 = 2 * j
            gather(c0, 0).wait()

            @pl.when(j > 0)
            def _():
                write(c0 - 1, 1).wait()

            gather(c0 + 1, 1).start()
            write(c0, 0).start()
            gather(c0 + 1, 1).wait()
            write(c0, 0).wait()

            @pl.when(c0 + 2 < n_chunks)
            def _():
                gather(c0 + 2, 0).start()

            write(c0 + 1, 1).start()
            return carry

        lax.fori_loop(0, n_chunks // 2, pair, 0)
        write(n_chunks - 1, 1).wait()

    return pl.kernel(
        body,
        out_type=jax.ShapeDtypeStruct((b, d), table.dtype),
        mesh=_sc_mesh(),
        scratch_types=[pltpu.VMEM((n_chunks, SC_ROWS), jnp.int32), pltpu.VMEM((2, SC_ROWS, d), table.dtype),
                       pltpu.SemaphoreType.DMA((2,)), pltpu.SemaphoreType.DMA((2,))],
        name="sc_combine_gather",
    )(table, idx3)


def _moe_kernel(blk_e, n_used, valid, xs_ref, wg_ref, wu_ref, wd_ref, y_ref, acc_ref, *, nf):
    m = pl.program_id(0)
    f = pl.program_id(1)

    @pl.when(m < n_used[0])
    def _():
        row = lax.broadcasted_iota(jnp.int32, xs_ref.shape, 0)
        x = jnp.where(row < valid[m], xs_ref[...], 0.0).astype(BF16)
        g = _dot(x, wg_ref[...])
        u = _dot(x, wu_ref[...])
        h = (g * _sigmoid(g) * u).astype(BF16)
        part = _dot(h, wd_ref[...])

        @pl.when(f == 0)
        def _():
            acc_ref[...] = part

        @pl.when(f > 0)
        def _():
            acc_ref[...] += part

        @pl.when(f == nf - 1)
        def _():
            y_ref[...] = acc_ref[...]

    @pl.when((m >= n_used[0]) & (f == 0))
    def _():
        y_ref[...] = jnp.zeros_like(y_ref)


def _moe_grouped(xs, blk_e, n_used, valid, wg, wu, wd, *, tm, tf):
    n_rows, d = xs.shape
    fdim = wg.shape[2]
    nf = fdim // tf
    n_blocks = n_rows // tm

    def f_eff(m, f, nu):
        return jnp.where(m < nu[0], f, nf - 1)

    grid_spec = pltpu.PrefetchScalarGridSpec(
        num_scalar_prefetch=3,
        grid=(n_blocks, nf),
        in_specs=[
            pl.BlockSpec((tm, d), lambda m, f, be, nu, va: (jnp.minimum(m, nu[0] - 1), 0)),
            pl.BlockSpec((None, d, tf), lambda m, f, be, nu, va: (be[m], 0, f_eff(m, f, nu))),
            pl.BlockSpec((None, d, tf), lambda m, f, be, nu, va: (be[m], 0, f_eff(m, f, nu))),
            pl.BlockSpec((None, tf, d), lambda m, f, be, nu, va: (be[m], f_eff(m, f, nu), 0)),
        ],
        out_specs=pl.BlockSpec((tm, d), lambda m, f, be, nu, va: (m, 0)),
        scratch_shapes=[pltpu.VMEM((tm, d), F32)],
    )
    return pl.pallas_call(
        functools.partial(_moe_kernel, nf=nf),
        grid_spec=grid_spec,
        out_shape=jax.ShapeDtypeStruct((n_rows, d), F32),
        compiler_params=_params(2),
        name="moe_grouped",
    )(blk_e, n_used, valid, xs, wg, wu, wd)


def _final_kernel(x_ref, y0_ref, y1_ref, route_ref, p_ref, ln_g_ref, ln_b_ref, wpg_ref, bpg_ref,
                  wpp_ref, o_ref):
    route = route_ref[...]
    f = route[:, 4:5] * y0_ref[...] + route[:, 5:6] * y1_ref[...]
    y = _layer_norm(DEEPNORM_ALPHA * x_ref[...] + f, ln_g_ref[...], ln_b_ref[...])
    o_ref[...] = _ple(y, p_ref[...], wpg_ref, bpg_ref, wpp_ref)


def _final(x2d, y01, route, p2d, ln_g, ln_b, wpg, bpg, wpp, *, tile=512):
    n, d = x2d.shape
    pd = p2d.shape[1]
    row = lambda t: t[None, :]
    tspec = pl.BlockSpec((tile, d), lambda i: (i, 0))
    second = pl.BlockSpec((tile, d), lambda i: (i + n // tile, 0))
    y0 = y1 = y01
    return pl.pallas_call(
        _final_kernel,
        grid=(n // tile,),
        in_specs=[tspec, tspec, second, pl.BlockSpec((tile, LANES), lambda i: (i, 0)),
                  pl.BlockSpec((tile, pd), lambda i: (i, 0)),
                  _const_spec((1, d)), _const_spec((1, d)),
                  _const_spec((d, d)), _const_spec((1, d)), _const_spec((pd, d))],
        out_specs=tspec,
        out_shape=jax.ShapeDtypeStruct((n, d), F32),
        compiler_params=_params(1),
        name="moe_combine_ple",
    )(x2d, y0, y1, route, p2d, row(ln_g), row(ln_b), wpg.astype(BF16), row(bpg), wpp.astype(BF16))


def _even_layer(x, p_i, w_in, a_ln_g, a_ln_b, a_ws, a_bs, w_gate_up, b_gate, norm_g, w_out,
                ln1_g, ln1_b, wg, wu, wd, ln2_g, ln2_b, wpp, wpg, bpg):
    bn, s, d = x.shape
    x1 = _even_mixer(x, w_in, a_ln_g, a_ln_b, a_ws, a_bs, w_gate_up, b_gate, norm_g, w_out, ln1_g, ln1_b)
    x2 = _ffn_ple(x1.reshape(bn * s, d), p_i.reshape(bn * s, -1), wg, wu, wd, ln2_g, ln2_b, wpg, bpg, wpp)
    return x2.reshape(bn, s, d)


def _odd_layer(x, p_i, layer_idx, w_qkv, lam_q1, lam_k1, lam_q2, lam_k2, subln_g, w_out, ln1_g, ln1_b,
               w_router, ewg, ewu, ewd, ln2_g, ln2_b, wpp, wpg, bpg, *, attn_blk=512, moe_tm=512,
               moe_tf=1792):
    bn, s, d = x.shape
    n = bn * s
    lambda_init = 0.8 - 0.6 * math.exp(-0.3 * layer_idx)
    q, kt, v_ext = _qkv(x, w_qkv, dh=lam_q1.shape[0], tile=attn_blk)
    o = _diff_attn(q, kt, v_ext, lam_q1, lam_k1, lam_q2, lam_k2, subln_g, lambda_init, blk=attn_blk)
    x3, route, cnt = _attn_out(o.reshape(n, d), x.reshape(n, d), w_out, ln1_g, ln1_b, w_router)
    dest, blk_e, n_used, valid, n_rows = _route_tables(route, cnt, tm=moe_tm)
    xs = _sc_dispatch(x3, dest, n_rows)
    y = _moe_grouped(xs, blk_e, n_used, valid, ewg.astype(BF16), ewu.astype(BF16), ewd.astype(BF16),
                     tm=moe_tm, tf=moe_tf)
    y01 = _sc_gather(y, jnp.concatenate([dest[:, 0], dest[:, 1]]))
    out = _final(x3, y01, route, p_i.reshape(n, -1), ln2_g, ln2_b, wpg, bpg, wpp)
    return out.reshape(bn, s, d)


def kernel(x, p, e_w_in, e_a_ln_g, e_a_ln_b, e_a_ws, e_a_bs, e_b_w_gate_up, e_b_b_gate, e_b_norm_g, e_w_out, e_ln1_g, e_ln1_b, e_ffn_wg, e_ffn_wu, e_ffn_wd, e_ln2_g, e_ln2_b, o_w_qkv, o_lam_q1, o_lam_k1, o_lam_q2, o_lam_k2, o_subln_g, o_w_out, o_ln1_g, o_ln1_b, o_router, o_exp_wg, o_exp_wu, o_exp_wd, o_ln2_g, o_ln2_b, ple_w_proj, ple_w_gate, ple_b_gate):
    for i in range(DEPTH):
        j = i // 2
        if i % 2 == 0:
            x = _even_layer(x, p[i], e_w_in[j], e_a_ln_g[j], e_a_ln_b[j], e_a_ws[j], e_a_bs[j],
                            e_b_w_gate_up[j], e_b_b_gate[j], e_b_norm_g[j], e_w_out[j],
                            e_ln1_g[j], e_ln1_b[j], e_ffn_wg[j], e_ffn_wu[j], e_ffn_wd[j],
                            e_ln2_g[j], e_ln2_b[j], ple_w_proj[i], ple_w_gate[i], ple_b_gate[i])
        else:
            x = _odd_layer(x, p[i], i, o_w_qkv[j], o_lam_q1[j], o_lam_k1[j], o_lam_q2[j], o_lam_k2[j],
                           o_subln_g[j], o_w_out[j], o_ln1_g[j], o_ln1_b[j], o_router[j],
                           o_exp_wg[j], o_exp_wu[j], o_exp_wd[j], o_ln2_g[j], o_ln2_b[j],
                           ple_w_proj[i], ple_w_gate[i], ple_b_gate[i])
    return x
```

```python
import functools
import math

import jax
import jax.numpy as jnp
from jax import lax
from jax.experimental import pallas as pl
from jax.experimental.pallas import tpu as pltpu
from jax.experimental.pallas import tpu_sc as plsc

F32 = jnp.float32
BF16 = jnp.bfloat16

DEPTH = 2
DEEPNORM_ALPHA = (2.0 * DEPTH) ** 0.25
LN_EPS = 1e-5
A_CHUNK = 128
A_GROUPS = 8
B_HEADS = 4
B_CHUNK = 64
B_TAU = 16.0
C_HEADS = 8
N_EXPERTS = 8
TOP_K = 2
LANES = 128
LOG2_E = 1.4426950408889634
VMEM_LIMIT = 56 * 1024 * 1024

NT_DIMS = (((1,), (1,)), ((), ()))
TN_DIMS = (((0,), (0,)), ((), ()))


def _dot(a, b):
    return jnp.dot(a, b, preferred_element_type=F32)


def _layer_norm(x, g, b):
    mu = jnp.mean(x, axis=-1, keepdims=True)
    xc = x - mu
    var = jnp.mean(xc * xc, axis=-1, keepdims=True)
    return xc * lax.rsqrt(var + LN_EPS) * g + b


def _sigmoid(x):
    return 1.0 / (1.0 + jnp.exp(-x))


def _split3(a):
    hi = a.astype(BF16)
    r1 = a - hi.astype(F32)
    mid = r1.astype(BF16)
    lo = (r1 - mid.astype(F32)).astype(BF16)
    return hi, mid, lo


def _const_spec(shape):
    zeros = (0,) * len(shape)
    return pl.BlockSpec(shape, lambda *_: zeros, pipeline_mode=pl.Buffered(1))


def _params(n_axes):
    return pltpu.CompilerParams(dimension_semantics=("arbitrary",) * n_axes,
                                vmem_limit_bytes=VMEM_LIMIT)


def _even_mixer_kernel(x_ref, w_in_ref, a_g_ref, a_b_ref, wcat_ref, abias_ref, mstack_ref,
                       w_up_ref, b_gate_ref, ng_ref, w_out_ref, ln_g_ref, ln_b_ref,
                       o_ref, state_ref, *, tile, aw, dkh, dvh):
    @pl.when(pl.program_id(1) == 0)
    def _():
        state_ref[...] = jnp.zeros_like(state_ref)

    hk = B_HEADS * dkh
    hv = B_HEADS * dvh
    x = x_ref[...]
    z = _dot(x.astype(BF16), w_in_ref[...])

    u = jax.nn.gelu(z[:, 0:aw])
    v = _layer_norm(jax.nn.gelu(z[:, aw:2 * aw]), a_g_ref[...], a_b_ref[...])
    gd = aw // A_GROUPS
    rows = lax.broadcasted_iota(jnp.int32, (A_CHUNK, A_GROUPS * A_CHUNK), 0)
    cols = lax.broadcasted_iota(jnp.int32, (A_CHUNK, A_GROUPS * A_CHUNK), 1)
    wcat = jnp.where((cols % A_CHUNK) <= rows, wcat_ref[...], 0.0).astype(BF16)
    r_bd = lax.broadcasted_iota(jnp.int32, (A_GROUPS * A_CHUNK, aw), 0) // A_CHUNK
    c_bd = lax.broadcasted_iota(jnp.int32, (A_GROUPS * A_CHUNK, aw), 1) // gd
    mask_bd = r_bd == c_bd
    ya_parts = []
    for c in range(tile // A_CHUNK):
        sl = slice(c * A_CHUNK, (c + 1) * A_CHUNK)
        v_rep = jnp.concatenate([v[sl]] * A_GROUPS, axis=0)
        v_bd = jnp.where(mask_bd, v_rep, 0.0).astype(BF16)
        sg = _dot(wcat, v_bd) + abias_ref[...]
        ya_parts.append(u[sl] * sg)
    ya = jnp.concatenate(ya_parts, axis=0)

    o0 = 2 * aw
    q = z[:, o0:o0 + hk] * (dkh ** -0.5)
    k = z[:, o0 + hk:o0 + 2 * hk]
    vv = z[:, o0 + 2 * hk:o0 + 2 * hk + hv]
    r = z[:, o0 + 2 * hk + hv:o0 + 2 * hk + 2 * hv]
    g_low = z[:, o0 + 2 * hk + 2 * hv:]
    pre = _dot(g_low.astype(BF16), w_up_ref[...]) + b_gate_ref[...]
    log_a = (jnp.minimum(pre, 0.0) - jnp.log1p(jnp.exp(-jnp.abs(pre)))) * (1.0 / B_TAU)
    la_hi, la_mid, la_lo = _split3(log_a)
    ms = mstack_ref[...]
    cs = _dot(ms, la_hi) + _dot(ms, la_mid) + _dot(ms, la_lo)
    b_cum = cs[0:tile]
    d_mid = cs[tile:2 * tile]
    d_last = cs[2 * tile:3 * tile]
    qe = (q * jnp.exp(d_mid)).astype(BF16)
    ke = k * jnp.exp(-d_mid)
    kd = (k * jnp.exp(d_last)).astype(BF16)
    qb = (q * jnp.exp(b_cum)).astype(BF16)
    dec = jnp.exp(b_cum + d_last)
    vvb = vv.astype(BF16)

    mask_kk = (lax.broadcasted_iota(jnp.int32, (hk, hk), 0) // dkh
               == lax.broadcasted_iota(jnp.int32, (hk, hk), 1) // dkh)
    mask_vbd = (lax.broadcasted_iota(jnp.int32, (hk, hv), 0) // dkh
                == lax.broadcasted_iota(jnp.int32, (hk, hv), 1) // dvh)
    mask_st = (lax.broadcasted_iota(jnp.int32, (hv, hk), 0) // dvh
               == lax.broadcasted_iota(jnp.int32, (hv, hk), 1) // dkh)
    causal = ((lax.broadcasted_iota(jnp.int32, (B_CHUNK, hk), 1) % B_CHUNK)
              <= lax.broadcasted_iota(jnp.int32, (B_CHUNK, hk), 0))

    st = state_ref[...]
    o_parts = []
    for c in range(tile // B_CHUNK):
        sl = slice(c * B_CHUNK, (c + 1) * B_CHUNK)
        ke_bd = jnp.where(mask_kk, jnp.concatenate([ke[sl]] * B_HEADS, axis=0), 0.0).astype(BF16)
        s_cat = lax.dot_general(qe[sl], ke_bd, NT_DIMS, preferred_element_type=F32)
        s_cat = jnp.where(causal, s_cat, 0.0).astype(BF16)
        v_bd = jnp.where(mask_vbd, jnp.concatenate([vv[sl]] * B_HEADS, axis=0), 0.0).astype(BF16)
        o_c = _dot(s_cat, v_bd) + lax.dot_general(qb[sl], st.astype(BF16), NT_DIMS,
                                                   preferred_element_type=F32)
        kv_t = lax.dot_general(vvb[sl], kd[sl], TN_DIMS, preferred_element_type=F32)
        st = dec[c * B_CHUNK:c * B_CHUNK + 1] * st + jnp.where(mask_st, kv_t, 0.0)
        o_parts.append(o_c)
    state_ref[...] = st
    o = jnp.concatenate(o_parts, axis=0)

    yb_parts = []
    for h in range(B_HEADS):
        oh = o[:, h * dvh:(h + 1) * dvh]
        msq = jnp.mean(oh * oh, axis=-1, keepdims=True)
        yb_parts.append(oh * lax.rsqrt(msq + LN_EPS))
    yb = jnp.concatenate(yb_parts, axis=1) * ng_ref[...] * (r * _sigmoid(r))

    y_cat = jnp.concatenate([ya, yb], axis=1).astype(BF16)
    m = _dot(y_cat, w_out_ref[...])
    o_ref[...] = _layer_norm(DEEPNORM_ALPHA * x + m, ln_g_ref[...], ln_b_ref[...])


def _gla_cumsum_matrices(tile):
    i = jnp.arange(tile)[:, None]
    j = jnp.arange(tile)[None, :]
    same = (i // B_CHUNK) == (j // B_CHUNK)
    m_cum = same & (j <= i)
    m_mid = same & (j <= (i // B_CHUNK) * B_CHUNK + B_CHUNK // 2 - 1)
    m_last = same
    f = lambda t: t.astype(F32)
    return jnp.concatenate([f(m_cum), f(m_cum) - f(m_mid), f(m_last) - f(m_cum)], axis=0).astype(BF16)


def _even_mixer(x, w_in, a_ln_g, a_ln_b, a_ws, a_bs, w_gate_up, b_gate, norm_g, w_out, ln_g, ln_b,
                *, tile=256):
    bn, s, d = x.shape
    aw = a_ln_g.shape[0]
    hk = w_gate_up.shape[1]
    dkh = hk // B_HEADS
    dvh = norm_g.shape[0]
    hv = B_HEADS * dvh
    rank = w_gate_up.shape[0]
    main = 2 * aw + 2 * hk + 2 * hv
    gd = aw // A_GROUPS
    w_in_p = jnp.concatenate([w_in[:, :main], jnp.pad(w_in[:, main:], ((0, 0), (0, LANES - rank)))],
                             axis=1).astype(BF16)
    w_up_p = jnp.pad(w_gate_up, ((0, LANES - rank), (0, 0))).astype(BF16)
    wcat = jnp.transpose(a_ws, (1, 0, 2)).reshape(A_CHUNK, A_GROUPS * A_CHUNK)
    abias = jnp.repeat(a_bs.T, gd, axis=1)
    mstack = _gla_cumsum_matrices(tile)
    ng = jnp.tile(norm_g, B_HEADS)[None, :]
    row = lambda t: t[None, :]
    kern = functools.partial(_even_mixer_kernel, tile=tile, aw=aw, dkh=dkh, dvh=dvh)
    tile_spec = pl.BlockSpec((None, tile, d), lambda b, i: (b, i, 0))
    return pl.pallas_call(
        kern,
        grid=(bn, s // tile),
        in_specs=[tile_spec, _const_spec(w_in_p.shape), _const_spec((1, aw)), _const_spec((1, aw)),
                  _const_spec(wcat.shape), _const_spec(abias.shape), _const_spec(mstack.shape),
                  _const_spec(w_up_p.shape), _const_spec((1, hk)), _const_spec((1, hv)),
                  _const_spec(w_out.shape), _const_spec((1, d)), _const_spec((1, d))],
        out_specs=tile_spec,
        out_shape=jax.ShapeDtypeStruct(x.shape, F32),
        scratch_shapes=[pltpu.VMEM((hv, hk), F32)],
        compiler_params=_params(2),
        name="even_mixer",
    )(x, w_in_p, row(a_ln_g), row(a_ln_b), wcat, abias, mstack, w_up_p, row(b_gate), ng,
      w_out.astype(BF16), row(ln_g), row(ln_b))


def _ple(y, p, wpg_ref, bpg_ref, wpp_ref):
    gate = _sigmoid(_dot(y.astype(BF16), wpg_ref[...]) + bpg_ref[...])
    return y + gate * _dot(p.astype(BF16), wpp_ref[...])


def _ffn_ple_kernel(x_ref, p_ref, wg_ref, wu_ref, wd_ref, ln_g_ref, ln_b_ref, wpg_ref, bpg_ref,
                    wpp_ref, o_ref, *, chunks):
    x = x_ref[...]
    xb = x.astype(BF16)
    acc = None
    for lo, hi in chunks:
        g = _dot(xb, wg_ref[:, lo:hi])
        u = _dot(xb, wu_ref[:, lo:hi])
        h = (g * _sigmoid(g) * u).astype(BF16)
        part = _dot(h, wd_ref[lo:hi, :])
        acc = part if acc is None else acc + part
    y = _layer_norm(DEEPNORM_ALPHA * x + acc, ln_g_ref[...], ln_b_ref[...])
    o_ref[...] = _ple(y, p_ref[...], wpg_ref, bpg_ref, wpp_ref)


def _ffn_ple(x2d, p2d, wg, wu, wd, ln_g, ln_b, wpg, bpg, wpp, *, tile=512, fchunk=1024):
    n, d = x2d.shape
    f = wg.shape[1]
    pd = p2d.shape[1]
    chunks = tuple((lo, min(lo + fchunk, f)) for lo in range(0, f, fchunk))
    row = lambda t: t[None, :]
    return pl.pallas_call(
        functools.partial(_ffn_ple_kernel, chunks=chunks),
        grid=(n // tile,),
        in_specs=[pl.BlockSpec((tile, d), lambda i: (i, 0)), pl.BlockSpec((tile, pd), lambda i: (i, 0)),
                  _const_spec((d, f)), _const_spec((d, f)), _const_spec((f, d)),
                  _const_spec((1, d)), _const_spec((1, d)),
                  _const_spec((d, d)), _const_spec((1, d)), _const_spec((pd, d))],
        out_specs=pl.BlockSpec((tile, d), lambda i: (i, 0)),
        out_shape=jax.ShapeDtypeStruct((n, d), F32),
        compiler_params=_params(1),
        name="ffn_ple",
    )(x2d, p2d, wg.astype(BF16), wu.astype(BF16), wd.astype(BF16), row(ln_g), row(ln_b),
      wpg.astype(BF16), row(bpg), wpp.astype(BF16))


def _qkv_kernel(x_ref, w_ref, q_ref, kt_ref, v_ref, *, hd, scale):
    z = _dot(x_ref[...].astype(BF16), w_ref[...])
    q_ref[...] = (z[:, 0:hd] * scale).astype(BF16)
    k_t = z[:, hd:2 * hd].T.astype(BF16)
    heads, dk2, tile = kt_ref.shape
    ones = jnp.ones((tile, dk2), BF16)
    for h in range(heads):
        kt_ref[h] = k_t[h * dk2:(h + 1) * dk2, :]
        v_ref[:, 2 * h * dk2:(2 * h + 1) * dk2] = z[:, 2 * hd + h * dk2:2 * hd + (h + 1) * dk2].astype(BF16)
        v_ref[:, (2 * h + 1) * dk2:(2 * h + 2) * dk2] = ones


def _qkv(x, w_qkv, *, dh, tile):
    bn, s, d = x.shape
    hd = w_qkv.shape[1] // 3
    heads = hd // (2 * dh)
    return pl.pallas_call(
        functools.partial(_qkv_kernel, hd=hd, scale=dh ** -0.5 * LOG2_E),
        grid=(bn, s // tile),
        in_specs=[pl.BlockSpec((None, tile, d), lambda b, i: (b, i, 0)), _const_spec(w_qkv.shape)],
        out_specs=[pl.BlockSpec((None, tile, hd), lambda b, i: (b, i, 0)),
                   pl.BlockSpec((None, heads, None, 2 * dh, tile), lambda b, i: (b, 0, i, 0, 0)),
                   pl.BlockSpec((None, tile, 2 * hd), lambda b, i: (b, i, 0))],
        out_shape=[jax.ShapeDtypeStruct((bn, s, hd), BF16),
                   jax.ShapeDtypeStruct((bn, heads, s // tile, 2 * dh, tile), BF16),
                   jax.ShapeDtypeStruct((bn, s, 2 * hd), BF16)],
        compiler_params=_params(2),
        name="qkv_proj",
    )(x, w_qkv.astype(BF16))


def _diff_attn_kernel(q_ref, kt_ref, v_ref, lam_ref, g_ref, o_ref, m_ref, acc_ref,
                      qm_ref, sa_ref, sb_ref, *, blk, dh, lambda_init):
    qi = pl.program_id(2)
    m_ref[...] = jnp.full_like(m_ref, -1e30)
    acc_ref[...] = jnp.zeros_like(acc_ref)

    q = q_ref[...]
    lane = lax.broadcasted_iota(jnp.int32, q.shape, 1)
    zero = jnp.zeros_like(q)
    qm_ref[0] = jnp.where(lane < dh, q, zero)
    qm_ref[1] = jnp.where(lane >= dh, q, zero)
    reps = blk // LANES

    def scores(ki, s_ref):
        kt_blk = kt_ref[ki]
        for c in range(2):
            s_ref[c] = _dot(qm_ref[c], kt_blk)

    def softmax_pv(s_ref, ki, masked):
        off = pl.multiple_of(ki * blk, blk)
        v_blk = v_ref[pl.ds(off, blk), :]
        for c in range(2):
            s = s_ref[c]
            if masked:
                row = lax.broadcasted_iota(jnp.int32, s.shape, 0)
                col = lax.broadcasted_iota(jnp.int32, s.shape, 1)
                s = jnp.where(col <= row, s, -1e30)
            m_prev = m_ref[c]
            m_new = jnp.maximum(m_prev, jnp.max(s, axis=-1, keepdims=True))
            corr = jnp.exp2(m_prev - m_new)
            p = jnp.exp2(s - jnp.concatenate([m_new] * reps, axis=1))
            acc_ref[c] = jnp.concatenate([corr, corr], axis=1) * acc_ref[c] + _dot(p.astype(BF16), v_blk)
            m_ref[c] = m_new

    scores(0, sa_ref)

    def pair(j, carry):
        k0 = 2 * j
        scores(k0 + 1, sb_ref)
        softmax_pv(sa_ref, k0, False)
        scores(k0 + 2, sa_ref)
        softmax_pv(sb_ref, k0 + 1, False)
        return carry

    lax.fori_loop(0, qi // 2, pair, 0)

    @pl.when(qi % 2 == 0)
    def _():
        softmax_pv(sa_ref, qi, True)

    @pl.when(qi % 2 == 1)
    def _():
        scores(qi, sb_ref)
        softmax_pv(sa_ref, qi - 1, False)
        softmax_pv(sb_ref, qi, True)

    a1 = acc_ref[0, :, 0:2 * dh] * (1.0 / acc_ref[0, :, 2 * dh:4 * dh])
    a2 = acc_ref[1, :, 0:2 * dh] * (1.0 / acc_ref[1, :, 2 * dh:4 * dh])
    lv = lam_ref[...]
    lam = (jnp.exp(jnp.sum(lv[0:1] * lv[1:2], axis=-1, keepdims=True))
           - jnp.exp(jnp.sum(lv[2:3] * lv[3:4], axis=-1, keepdims=True)) + lambda_init)
    o = a1 - lam * a2
    msq = jnp.mean(o * o, axis=-1, keepdims=True)
    o_ref[...] = (o * lax.rsqrt(msq + LN_EPS) * g_ref[...] * (1.0 - lambda_init)).astype(BF16)


def _diff_attn(q, kt, v_ext, lam_q1, lam_k1, lam_q2, lam_k2, subln_g, lambda_init, *, blk):
    bn, s, hd = q.shape
    dh = lam_q1.shape[0]
    heads = hd // (2 * dh)
    nb = s // blk
    lamv = jnp.zeros((8, LANES), F32).at[0:4, 0:dh].set(jnp.stack([lam_q1, lam_k1, lam_q2, lam_k2]))
    return pl.pallas_call(
        functools.partial(_diff_attn_kernel, blk=blk, dh=dh, lambda_init=lambda_init),
        grid=(bn, heads, nb),
        in_specs=[
            pl.BlockSpec((None, blk, 2 * dh), lambda b, h, i: (b, i, h)),
            pl.BlockSpec((None, None, nb, 2 * dh, blk), lambda b, h, i: (b, h, 0, 0, 0)),
            pl.BlockSpec((None, s, 4 * dh), lambda b, h, i: (b, 0, h)),
            pl.BlockSpec((8, LANES), lambda b, h, i: (0, 0)),
            pl.BlockSpec((1, 2 * dh), lambda b, h, i: (0, 0)),
        ],
        out_specs=pl.BlockSpec((None, blk, 2 * dh), lambda b, h, i: (b, i, h)),
        out_shape=jax.ShapeDtypeStruct((bn, s, hd), BF16),
        scratch_shapes=[pltpu.VMEM((2, blk, LANES), F32),
                        pltpu.VMEM((2, blk, 4 * dh), F32), pltpu.VMEM((2, blk, 2 * dh), BF16),
                        pltpu.VMEM((2, blk, blk), F32), pltpu.VMEM((2, blk, blk), F32)],
        compiler_params=_params(3),
        name="diff_attn",
    )(q, kt, v_ext, lamv, subln_g[None, :])


def _attn_out_kernel(o_ref, x_ref, w_ref, ln_g_ref, ln_b_ref, wr_hi_ref, wr_lo_ref, lstrict_ref,
                     y_ref, route_ref, cnt_ref, base_ref):
    @pl.when(pl.program_id(0) == 0)
    def _():
        base_ref[...] = jnp.zeros_like(base_ref)

    m = _dot(o_ref[...], w_ref[...])
    y = _layer_norm(DEEPNORM_ALPHA * x_ref[...] + m, ln_g_ref[...], ln_b_ref[...])
    y_ref[...] = y
    y_hi = y.astype(BF16)
    y_lo = (y - y_hi.astype(F32)).astype(BF16)
    logits = (_dot(y_hi, wr_hi_ref[...]) + _dot(y_lo, wr_hi_ref[...]) + _dot(y_hi, wr_lo_ref[...]))

    lane = lax.broadcasted_iota(jnp.int32, logits.shape, 1).astype(F32)
    neg = -jnp.inf
    lg = jnp.where(lane < N_EXPERTS, logits, neg)
    v0 = jnp.max(lg, axis=-1, keepdims=True)
    i0 = jnp.min(jnp.where(lg == v0, lane, float(LANES)), axis=-1, keepdims=True)
    lg2 = jnp.where(lane == i0, neg, lg)
    v1 = jnp.max(lg2, axis=-1, keepdims=True)
    i1 = jnp.min(jnp.where(lg2 == v1, lane, float(LANES)), axis=-1, keepdims=True)
    e = jnp.exp(v1 - v0)
    g0 = 1.0 / (1.0 + e)
    g1 = e / (1.0 + e)
    oh0 = lane == i0
    oh1 = lane == i1
    c = jnp.where(oh0, 1.0, 0.0) + jnp.where(oh1, 1.0, 0.0)
    pre = _dot(lstrict_ref[...], c.astype(BF16)) + base_ref[...]
    r0 = jnp.sum(jnp.where(oh0, pre, 0.0), axis=-1, keepdims=True)
    r1 = jnp.sum(jnp.where(oh1, pre, 0.0), axis=-1, keepdims=True)
    base_ref[...] = base_ref[...] + jnp.sum(c, axis=0, keepdims=True)
    cnt_ref[...] = jnp.broadcast_to(base_ref[...], cnt_ref.shape)
    fields = (i0, i1, r0, r1, g0, g1)
    route = jnp.zeros_like(logits)
    for idx, val in enumerate(fields):
        route = jnp.where(lane == idx, val, route)
    route_ref[...] = route


def _attn_out(o2d, x2d, w_out, ln_g, ln_b, w_router, *, tile=512):
    n, d = x2d.shape
    wr = jnp.pad(w_router, ((0, 0), (0, LANES - w_router.shape[1])))
    wr_hi = wr.astype(BF16)
    wr_lo = (wr - wr_hi.astype(F32)).astype(BF16)
    lstrict = (jnp.arange(tile)[:, None] > jnp.arange(tile)[None, :]).astype(BF16)
    row = lambda t: t[None, :]
    tspec = pl.BlockSpec((tile, d), lambda i: (i, 0))
    return pl.pallas_call(
        _attn_out_kernel,
        grid=(n // tile,),
        in_specs=[tspec, tspec, _const_spec((d, d)), _const_spec((1, d)), _const_spec((1, d)),
                  _const_spec((d, LANES)), _const_spec((d, LANES)), _const_spec((tile, tile))],
        out_specs=[tspec, pl.BlockSpec((tile, LANES), lambda i: (i, 0)),
                   pl.BlockSpec((8, LANES), lambda i: (0, 0))],
        out_shape=[jax.ShapeDtypeStruct((n, d), F32), jax.ShapeDtypeStruct((n, LANES), F32),
                   jax.ShapeDtypeStruct((8, LANES), F32)],
        scratch_shapes=[pltpu.VMEM((1, LANES), F32)],
        compiler_params=_params(1),
        name="attn_out_router",
    )(o2d, x2d, w_out.astype(BF16), row(ln_g), row(ln_b), wr_hi, wr_lo, lstrict)


def _route_tables(route, cnt, *, tm):
    n = route.shape[0]
    e01 = route[:, 0:2].astype(jnp.int32)
    r01 = route[:, 2:4].astype(jnp.int32)
    counts = cnt[0, :N_EXPERTS].astype(jnp.int32)
    padded = (counts + tm - 1) // tm * tm
    pend = jnp.cumsum(padded)
    pstart = pend - padded
    start_of = jnp.sum(jnp.where(e01[:, :, None] == jnp.arange(N_EXPERTS)[None, None, :],
                                 pstart[None, None, :], 0), axis=-1)
    dest = start_of + r01
    n_rows = n * TOP_K + N_EXPERTS * tm
    n_blocks = n_rows // tm
    n_used = pend[-1] // tm
    blk_idx = jnp.arange(n_blocks, dtype=jnp.int32)
    blk_start = jnp.minimum(blk_idx, n_used - 1) * tm
    blk_e = jnp.minimum(jnp.sum(blk_start[:, None] >= pend[None, :], axis=1), N_EXPERTS - 1)
    valid = jnp.clip(pstart[blk_e] + counts[blk_e] - blk_idx * tm, 0, tm)
    return (dest, blk_e.astype(jnp.int32), n_used.reshape(1).astype(jnp.int32),
            valid.astype(jnp.int32), n_rows)


SC_CORES = 2
SC_SUBCORES = 16
SC_WORKERS = SC_CORES * SC_SUBCORES
SC_ROWS = 32


def _sc_mesh():
    return plsc.VectorSubcoreMesh(core_axis_name="c", subcore_axis_name="s",
                                  num_cores=SC_CORES, num_subcores=SC_SUBCORES)


def _sc_worker_id():
    return lax.axis_index("s") * SC_CORES + lax.axis_index("c")


def _sc_dispatch(x2d, dest, n_rows):
    n, d = x2d.shape
    per_w = n // SC_WORKERS
    n_chunks = per_w // SC_ROWS
    assert n == SC_WORKERS * n_chunks * SC_ROWS and n_chunks % 2 == 0
    d0 = dest[:, 0].reshape(SC_WORKERS, n_chunks, SC_ROWS)
    d1 = dest[:, 1].reshape(SC_WORKERS, n_chunks, SC_ROWS)

    def body(x_hbm, d0_hbm, d1_hbm, xs_hbm, d0_v, d1_v, rows_v, rsem, s0sem, s1sem):
        wid = _sc_worker_id()
        base = wid * per_w
        pltpu.sync_copy(d0_hbm.at[wid], d0_v)
        pltpu.sync_copy(d1_hbm.at[wid], d1_v)

        def read(c, slot):
            return pltpu.make_async_copy(x_hbm.at[pl.ds(base + c * SC_ROWS, SC_ROWS)],
                                         rows_v.at[slot], rsem.at[slot])

        def scat(idx_v, sem, c, slot):
            return pltpu.make_async_copy(rows_v.at[slot], xs_hbm.at[idx_v.at[c]], sem.at[slot])

        def start_scatters(c, slot):
            scat(d0_v, s0sem, c, slot).start()
            scat(d1_v, s1sem, c, slot).start()

        def wait_scatters(c, slot):
            scat(d0_v, s0sem, c, slot).wait()
            scat(d1_v, s1sem, c, slot).wait()

        read(0, 0).start()

        def pair(j, carry):
            c0 = 2 * j
            read(c0, 0).wait()

            @pl.when(j > 0)
            def _():
                wait_scatters(c0 - 1, 1)

            read(c0 + 1, 1).start()
            start_scatters(c0, 0)
            read(c0 + 1, 1).wait()
            wait_scatters(c0, 0)

            @pl.when(c0 + 2 < n_chunks)
            def _():
                read(c0 + 2, 0).start()

            start_scatters(c0 + 1, 1)
            return carry

        lax.fori_loop(0, n_chunks // 2, pair, 0)
        wait_scatters(n_chunks - 1, 1)

    return pl.kernel(
        body,
        out_type=jax.ShapeDtypeStruct((n_rows, d), x2d.dtype),
        mesh=_sc_mesh(),
        scratch_types=[pltpu.VMEM((n_chunks, SC_ROWS), jnp.int32), pltpu.VMEM((n_chunks, SC_ROWS), jnp.int32),
                       pltpu.VMEM((2, SC_ROWS, d), x2d.dtype), pltpu.SemaphoreType.DMA((2,)),
                       pltpu.SemaphoreType.DMA((2,)), pltpu.SemaphoreType.DMA((2,))],
        name="sc_dispatch",
    )(x2d, d0, d1)


def _sc_gather(table, idx):
    _, d = table.shape
    b = idx.shape[0]
    per_w = b // SC_WORKERS
    n_chunks = per_w // SC_ROWS
    assert b == SC_WORKERS * n_chunks * SC_ROWS and n_chunks % 2 == 0
    idx3 = idx.reshape(SC_WORKERS, n_chunks, SC_ROWS)

    def body(table_hbm, idx_hbm, out_hbm, idx_v, rows_v, gsem, wsem):
        wid = _sc_worker_id()
        base = wid * per_w
        pltpu.sync_copy(idx_hbm.at[wid], idx_v)

        def gather(c, slot):
            return pltpu.make_async_copy(table_hbm.at[idx_v.at[c]], rows_v.at[slot], gsem.at[slot])

        def write(c, slot):
            return pltpu.make_async_copy(rows_v.at[slot],
                                         out_hbm.at[pl.ds(base + c * SC_ROWS, SC_ROWS)], wsem.at[slot])

        gather(0, 0).start()

        def pair(j, carry):
            c0 = 2 * j
            gather(c0, 0).wait()

            @pl.when(j > 0)
            def _():
                write(c0 - 1, 1).wait()

            gather(c0 + 1, 1).start()
            write(c0, 0).start()
            gather(c0 + 1, 1).wait()
            write(c0, 0).wait()

            @pl.when(c0 + 2 < n_chunks)
            def _():
                gather(c0 + 2, 0).start()

            write(c0 + 1, 1).start()
            return carry

        lax.fori_loop(0, n_chunks // 2, pair, 0)
        write(n_chunks - 1, 1).wait()

    return pl.kernel(
        body,
        out_type=jax.ShapeDtypeStruct((b, d), table.dtype),
        mesh=_sc_mesh(),
        scratch_types=[pltpu.VMEM((n_chunks, SC_ROWS), jnp.int32), pltpu.VMEM((2, SC_ROWS, d), table.dtype),
                       pltpu.SemaphoreType.DMA((2,)), pltpu.SemaphoreType.DMA((2,))],
        name="sc_combine_gather",
    )(table, idx3)


def _moe_kernel(blk_e, n_used, valid, xs_ref, wg_ref, wu_ref, wd_ref, y_ref, *, chunks):
    m = pl.program_id(0)
    f = pl.program_id(1)

    @pl.when(f == 0)
    def _():
        y_ref[...] = jnp.zeros_like(y_ref)

    @pl.when(m < n_used[0])
    def _():
        row = lax.broadcasted_iota(jnp.int32, xs_ref.shape, 0)
        x = jnp.where(row < valid[m], xs_ref[...], 0.0).astype(BF16)
        for lo, hi in chunks:
            g = _dot(x, wg_ref[:, lo:hi])
            u = _dot(x, wu_ref[:, lo:hi])
            h = (g * _sigmoid(g) * u).astype(BF16)
            y_ref[...] += _dot(h, wd_ref[lo:hi, :])


def _moe_grouped(xs, blk_e, n_used, valid, wg, wu, wd, *, tm, tf, fchunk=1024):
    n_rows, d = xs.shape
    fdim = wg.shape[2]
    nf = fdim // tf
    assert nf * tf == fdim
    n_blocks = n_rows // tm
    chunks = tuple((lo, min(lo + fchunk, tf)) for lo in range(0, tf, fchunk))

    def f_eff(m, f, nu):
        return jnp.where(m < nu[0], f, nf - 1)

    grid_spec = pltpu.PrefetchScalarGridSpec(
        num_scalar_prefetch=3,
        grid=(n_blocks, nf),
        in_specs=[
            pl.BlockSpec((tm, d), lambda m, f, be, nu, va: (jnp.minimum(m, nu[0] - 1), 0)),
            pl.BlockSpec((None, d, tf), lambda m, f, be, nu, va: (be[m], 0, f_eff(m, f, nu))),
            pl.BlockSpec((None, d, tf), lambda m, f, be, nu, va: (be[m], 0, f_eff(m, f, nu))),
            pl.BlockSpec((None, tf, d), lambda m, f, be, nu, va: (be[m], f_eff(m, f, nu), 0)),
        ],
        out_specs=pl.BlockSpec((tm, d), lambda m, f, be, nu, va: (m, 0)),
    )
    return pl.pallas_call(
        functools.partial(_moe_kernel, chunks=chunks),
        grid_spec=grid_spec,
        out_shape=jax.ShapeDtypeStruct((n_rows, d), F32),
        compiler_params=_params(2),
        name="moe_grouped",
    )(blk_e, n_used, valid, xs, wg, wu, wd)


def _final_kernel(x_ref, y0_ref, y1_ref, route_ref, p_ref, ln_g_ref, ln_b_ref, wpg_ref, bpg_ref,
                  wpp_ref, o_ref):
    route = route_ref[...]
    f = route[:, 4:5] * y0_ref[...] + route[:, 5:6] * y1_ref[...]
    y = _layer_norm(DEEPNORM_ALPHA * x_ref[...] + f, ln_g_ref[...], ln_b_ref[...])
    o_ref[...] = _ple(y, p_ref[...], wpg_ref, bpg_ref, wpp_ref)


def _final(x2d, y01, route, p2d, ln_g, ln_b, wpg, bpg, wpp, *, tile=512):
    n, d = x2d.shape
    pd = p2d.shape[1]
    row = lambda t: t[None, :]
    tspec = pl.BlockSpec((tile, d), lambda i: (i, 0))
    second = pl.BlockSpec((tile, d), lambda i: (i + n // tile, 0))
    y0 = y1 = y01
    return pl.pallas_call(
        _final_kernel,
        grid=(n // tile,),
        in_specs=[tspec, tspec, second, pl.BlockSpec((tile, LANES), lambda i: (i, 0)),
                  pl.BlockSpec((tile, pd), lambda i: (i, 0)),
                  _const_spec((1, d)), _const_spec((1, d)),
                  _const_spec((d, d)), _const_spec((1, d)), _const_spec((pd, d))],
        out_specs=tspec,
        out_shape=jax.ShapeDtypeStruct((n, d), F32),
        compiler_params=_params(1),
        name="moe_combine_ple",
    )(x2d, y0, y1, route, p2d, row(ln_g), row(ln_b), wpg.astype(BF16), row(bpg), wpp.astype(BF16))


def _even_layer(x, p_i, w_in, a_ln_g, a_ln_b, a_ws, a_bs, w_gate_up, b_gate, norm_g, w_out,
                ln1_g, ln1_b, wg, wu, wd, ln2_g, ln2_b, wpp, wpg, bpg):
    bn, s, d = x.shape
    x1 = _even_mixer(x, w_in, a_ln_g, a_ln_b, a_ws, a_bs, w_gate_up, b_gate, norm_g, w_out, ln1_g, ln1_b)
    x2 = _ffn_ple(x1.reshape(bn * s, d), p_i.reshape(bn * s, -1), wg, wu, wd, ln2_g, ln2_b, wpg, bpg, wpp)
    return x2.reshape(bn, s, d)


def _odd_layer(x, p_i, layer_idx, w_qkv, lam_q1, lam_k1, lam_q2, lam_k2, subln_g, w_out, ln1_g, ln1_b,
               w_router, ewg, ewu, ewd, ln2_g, ln2_b, wpp, wpg, bpg, *, attn_blk=512, moe_tm=512,
               moe_tf=1792):
    bn, s, d = x.shape
    n = bn * s
    lambda_init = 0.8 - 0.6 * math.exp(-0.3 * layer_idx)
    q, kt, v_ext = _qkv(x, w_qkv, dh=lam_q1.shape[0], tile=attn_blk)
    o = _diff_attn(q, kt, v_ext, lam_q1, lam_k1, lam_q2, lam_k2, subln_g, lambda_init, blk=attn_blk)
    x3, route, cnt = _attn_out(o.reshape(n, d), x.reshape(n, d), w_out, ln1_g, ln1_b, w_router)
    dest, blk_e, n_used, valid, n_rows = _route_tables(route, cnt, tm=moe_tm)
    xs = _sc_dispatch(x3, dest, n_rows)
    y = _moe_grouped(xs, blk_e, n_used, valid, ewg.astype(BF16), ewu.astype(BF16), ewd.astype(BF16),
                     tm=moe_tm, tf=moe_tf)
    y01 = _sc_gather(y, jnp.concatenate([dest[:, 0], dest[:, 1]]))
    out = _final(x3, y01, route, p_i.reshape(n, -1), ln2_g, ln2_b, wpg, bpg, wpp)
    return out.reshape(bn, s, d)


def kernel(x, p, e_w_in, e_a_ln_g, e_a_ln_b, e_a_ws, e_a_bs, e_b_w_gate_up, e_b_b_gate, e_b_norm_g, e_w_out, e_ln1_g, e_ln1_b, e_ffn_wg, e_ffn_wu, e_ffn_wd, e_ln2_g, e_ln2_b, o_w_qkv, o_lam_q1, o_lam_k1, o_lam_q2, o_lam_k2, o_subln_g, o_w_out, o_ln1_g, o_ln1_b, o_router, o_exp_wg, o_exp_wu, o_exp_wd, o_ln2_g, o_ln2_b, ple_w_proj, ple_w_gate, ple_b_gate):
    for i in range(DEPTH):
        j = i // 2
        if i % 2 == 0:
            x = _even_layer(x, p[i], e_w_in[j], e_a_ln_g[j], e_a_ln_b[j], e_a_ws[j], e_a_bs[j],
                            e_b_w_gate_up[j], e_b_b_gate[j], e_b_norm_g[j], e_w_out[j],
                            e_ln1_g[j], e_ln1_b[j], e_ffn_wg[j], e_ffn_wu[j], e_ffn_wd[j],
                            e_ln2_g[j], e_ln2_b[j], ple_w_proj[i], ple_w_gate[i], ple_b_gate[i])
        else:
            x = _odd_layer(x, p[i], i, o_w_qkv[j], o_lam_q1[j], o_lam_k1[j], o_lam_q2[j], o_lam_k2[j],
                           o_subln_g[j], o_w_out[j], o_ln1_g[j], o_ln1_b[j], o_router[j],
                           o_exp_wg[j], o_exp_wu[j], o_exp_wd[j], o_ln2_g[j], o_ln2_b[j],
                           ple_w_proj[i], ple_w_gate[i], ple_b_gate[i])
    return x
```

```python
import functools
import math

import jax
import jax.numpy as jnp
from jax import lax
from jax.experimental import pallas as pl
from jax.experimental.pallas import tpu as pltpu
from jax.experimental.pallas import tpu_sc as plsc

F32 = jnp.float32
BF16 = jnp.bfloat16

DEPTH = 2
DEEPNORM_ALPHA = (2.0 * DEPTH) ** 0.25
LN_EPS = 1e-5
A_CHUNK = 128
A_GROUPS = 8
B_HEADS = 4
B_CHUNK = 64
B_TAU = 16.0
CS_ROWS = 256
C_HEADS = 8
N_EXPERTS = 8
TOP_K = 2
LANES = 128
LOG2_E = 1.4426950408889634
VMEM_LIMIT = 56 * 1024 * 1024

NT_DIMS = (((1,), (1,)), ((), ()))
TN_DIMS = (((0,), (0,)), ((), ()))


def _dot(a, b):
    return jnp.dot(a, b, preferred_element_type=F32)


def _layer_norm(x, g, b):
    mu = jnp.mean(x, axis=-1, keepdims=True)
    xc = x - mu
    var = jnp.mean(xc * xc, axis=-1, keepdims=True)
    return xc * lax.rsqrt(var + LN_EPS) * g + b


def _sigmoid(x):
    return 1.0 / (1.0 + jnp.exp(-x))


def _split3(a):
    hi = a.astype(BF16)
    r1 = a - hi.astype(F32)
    mid = r1.astype(BF16)
    lo = (r1 - mid.astype(F32)).astype(BF16)
    return hi, mid, lo


def _const_spec(shape):
    zeros = (0,) * len(shape)
    return pl.BlockSpec(shape, lambda *_: zeros, pipeline_mode=pl.Buffered(1))


def _params(n_axes):
    return pltpu.CompilerParams(dimension_semantics=("arbitrary",) * n_axes,
                                vmem_limit_bytes=VMEM_LIMIT)


def _even_mixer_kernel(x_ref, w_in_ref, a_g_ref, a_b_ref, wcat_ref, abias_ref, mstack_ref,
                       w_up_ref, b_gate_ref, ng_ref, w_out_ref, ln_g_ref, ln_b_ref,
                       o_ref, state_ref, *, tile, aw, dkh, dvh):
    @pl.when(pl.program_id(1) == 0)
    def _():
        state_ref[...] = jnp.zeros_like(state_ref)

    hk = B_HEADS * dkh
    hv = B_HEADS * dvh
    x = x_ref[...]
    z = _dot(x.astype(BF16), w_in_ref[...])

    u = jax.nn.gelu(z[:, 0:aw])
    v = _layer_norm(jax.nn.gelu(z[:, aw:2 * aw]), a_g_ref[...], a_b_ref[...])
    gd = aw // A_GROUPS
    rows = lax.broadcasted_iota(jnp.int32, (A_CHUNK, A_GROUPS * A_CHUNK), 0)
    cols = lax.broadcasted_iota(jnp.int32, (A_CHUNK, A_GROUPS * A_CHUNK), 1)
    wcat = jnp.where((cols % A_CHUNK) <= rows, wcat_ref[...], 0.0).astype(BF16)
    r_bd = lax.broadcasted_iota(jnp.int32, (A_GROUPS * A_CHUNK, aw), 0) // A_CHUNK
    c_bd = lax.broadcasted_iota(jnp.int32, (A_GROUPS * A_CHUNK, aw), 1) // gd
    mask_bd = r_bd == c_bd
    ya_parts = []
    for c in range(tile // A_CHUNK):
        sl = slice(c * A_CHUNK, (c + 1) * A_CHUNK)
        v_rep = jnp.concatenate([v[sl]] * A_GROUPS, axis=0)
        v_bd = jnp.where(mask_bd, v_rep, 0.0).astype(BF16)
        sg = _dot(wcat, v_bd) + abias_ref[...]
        ya_parts.append(u[sl] * sg)
    ya = jnp.concatenate(ya_parts, axis=0)

    o0 = 2 * aw
    q = z[:, o0:o0 + hk] * (dkh ** -0.5)
    k = z[:, o0 + hk:o0 + 2 * hk]
    vv = z[:, o0 + 2 * hk:o0 + 2 * hk + hv]
    r = z[:, o0 + 2 * hk + hv:o0 + 2 * hk + 2 * hv]
    g_low = z[:, o0 + 2 * hk + 2 * hv:]
    pre = _dot(g_low.astype(BF16), w_up_ref[...]) + b_gate_ref[...]
    log_a = (jnp.minimum(pre, 0.0) - jnp.log1p(jnp.exp(-jnp.abs(pre)))) * (1.0 / B_TAU)
    la_hi, la_mid, la_lo = _split3(log_a)
    ms = mstack_ref[...]
    b_parts, mid_parts, last_parts = [], [], []
    for t in range(tile // CS_ROWS):
        rs = slice(t * CS_ROWS, (t + 1) * CS_ROWS)
        cs = _dot(ms, la_hi[rs]) + _dot(ms, la_mid[rs]) + _dot(ms, la_lo[rs])
        b_parts.append(cs[0:CS_ROWS])
        mid_parts.append(cs[CS_ROWS:2 * CS_ROWS])
        last_parts.append(cs[2 * CS_ROWS:3 * CS_ROWS])
    b_cum = jnp.concatenate(b_parts, axis=0)
    d_mid = jnp.concatenate(mid_parts, axis=0)
    d_last = jnp.concatenate(last_parts, axis=0)
    qe = (q * jnp.exp(d_mid)).astype(BF16)
    ke = k * jnp.exp(-d_mid)
    kd = (k * jnp.exp(d_last)).astype(BF16)
    qb = (q * jnp.exp(b_cum)).astype(BF16)
    dec = jnp.exp(b_cum + d_last)
    vvb = vv.astype(BF16)

    mask_kk = (lax.broadcasted_iota(jnp.int32, (hk, hk), 0) // dkh
               == lax.broadcasted_iota(jnp.int32, (hk, hk), 1) // dkh)
    mask_vbd = (lax.broadcasted_iota(jnp.int32, (hk, hv), 0) // dkh
                == lax.broadcasted_iota(jnp.int32, (hk, hv), 1) // dvh)
    mask_st = (lax.broadcasted_iota(jnp.int32, (hv, hk), 0) // dvh
               == lax.broadcasted_iota(jnp.int32, (hv, hk), 1) // dkh)
    causal = ((lax.broadcasted_iota(jnp.int32, (B_CHUNK, hk), 1) % B_CHUNK)
              <= lax.broadcasted_iota(jnp.int32, (B_CHUNK, hk), 0))

    st = state_ref[...]
    o_parts = []
    for c in range(tile // B_CHUNK):
        sl = slice(c * B_CHUNK, (c + 1) * B_CHUNK)
        ke_bd = jnp.where(mask_kk, jnp.concatenate([ke[sl]] * B_HEADS, axis=0), 0.0).astype(BF16)
        s_cat = lax.dot_general(qe[sl], ke_bd, NT_DIMS, preferred_element_type=F32)
        s_cat = jnp.where(causal, s_cat, 0.0).astype(BF16)
        v_bd = jnp.where(mask_vbd, jnp.concatenate([vv[sl]] * B_HEADS, axis=0), 0.0).astype(BF16)
        o_c = _dot(s_cat, v_bd) + lax.dot_general(qb[sl], st.astype(BF16), NT_DIMS,
                                                   preferred_element_type=F32)
        kv_t = lax.dot_general(vvb[sl], kd[sl], TN_DIMS, preferred_element_type=F32)
        st = dec[c * B_CHUNK:c * B_CHUNK + 1] * st + jnp.where(mask_st, kv_t, 0.0)
        o_parts.append(o_c)
    state_ref[...] = st
    o = jnp.concatenate(o_parts, axis=0)

    yb_parts = []
    for h in range(B_HEADS):
        oh = o[:, h * dvh:(h + 1) * dvh]
        msq = jnp.mean(oh * oh, axis=-1, keepdims=True)
        yb_parts.append(oh * lax.rsqrt(msq + LN_EPS))
    yb = jnp.concatenate(yb_parts, axis=1) * ng_ref[...] * (r * _sigmoid(r))

    y_cat = jnp.concatenate([ya, yb], axis=1).astype(BF16)
    m = _dot(y_cat, w_out_ref[...])
    o_ref[...] = _layer_norm(DEEPNORM_ALPHA * x + m, ln_g_ref[...], ln_b_ref[...])


def _gla_cumsum_matrices(tile):
    i = jnp.arange(tile)[:, None]
    j = jnp.arange(tile)[None, :]
    same = (i // B_CHUNK) == (j // B_CHUNK)
    m_cum = same & (j <= i)
    m_mid = same & (j <= (i // B_CHUNK) * B_CHUNK + B_CHUNK // 2 - 1)
    m_last = same
    f = lambda t: t.astype(F32)
    return jnp.concatenate([f(m_cum), f(m_cum) - f(m_mid), f(m_last) - f(m_cum)], axis=0).astype(BF16)


def _even_mixer(x, w_in, a_ln_g, a_ln_b, a_ws, a_bs, w_gate_up, b_gate, norm_g, w_out, ln_g, ln_b,
                *, tile=512):
    bn, s, d = x.shape
    aw = a_ln_g.shape[0]
    hk = w_gate_up.shape[1]
    dkh = hk // B_HEADS
    dvh = norm_g.shape[0]
    hv = B_HEADS * dvh
    rank = w_gate_up.shape[0]
    main = 2 * aw + 2 * hk + 2 * hv
    gd = aw // A_GROUPS
    w_in_p = jnp.concatenate([w_in[:, :main], jnp.pad(w_in[:, main:], ((0, 0), (0, LANES - rank)))],
                             axis=1).astype(BF16)
    w_up_p = jnp.pad(w_gate_up, ((0, LANES - rank), (0, 0))).astype(BF16)
    wcat = jnp.transpose(a_ws, (1, 0, 2)).reshape(A_CHUNK, A_GROUPS * A_CHUNK)
    abias = jnp.repeat(a_bs.T, gd, axis=1)
    mstack = _gla_cumsum_matrices(CS_ROWS)
    ng = jnp.tile(norm_g, B_HEADS)[None, :]
    row = lambda t: t[None, :]
    kern = functools.partial(_even_mixer_kernel, tile=tile, aw=aw, dkh=dkh, dvh=dvh)
    tile_spec = pl.BlockSpec((None, tile, d), lambda b, i: (b, i, 0))
    return pl.pallas_call(
        kern,
        grid=(bn, s // tile),
        in_specs=[tile_spec, _const_spec(w_in_p.shape), _const_spec((1, aw)), _const_spec((1, aw)),
                  _const_spec(wcat.shape), _const_spec(abias.shape), _const_spec(mstack.shape),
                  _const_spec(w_up_p.shape), _const_spec((1, hk)), _const_spec((1, hv)),
                  _const_spec(w_out.shape), _const_spec((1, d)), _const_spec((1, d))],
        out_specs=tile_spec,
        out_shape=jax.ShapeDtypeStruct(x.shape, F32),
        scratch_shapes=[pltpu.VMEM((hv, hk), F32)],
        compiler_params=_params(2),
        name="even_mixer",
    )(x, w_in_p, row(a_ln_g), row(a_ln_b), wcat, abias, mstack, w_up_p, row(b_gate), ng,
      w_out.astype(BF16), row(ln_g), row(ln_b))


def _ple(y, p, wpg_ref, bpg_ref, wpp_ref):
    gate = _sigmoid(_dot(y.astype(BF16), wpg_ref[...]) + bpg_ref[...])
    return y + gate * _dot(p.astype(BF16), wpp_ref[...])


def _ffn_ple_kernel(x_ref, p_ref, wg_ref, wu_ref, wd_ref, ln_g_ref, ln_b_ref, wpg_ref, bpg_ref,
                    wpp_ref, o_ref, *, chunks):
    x = x_ref[...]
    xb = x.astype(BF16)
    acc = None
    for lo, hi in chunks:
        g = _dot(xb, wg_ref[:, lo:hi])
        u = _dot(xb, wu_ref[:, lo:hi])
        h = (g * _sigmoid(g) * u).astype(BF16)
        part = _dot(h, wd_ref[lo:hi, :])
        acc = part if acc is None else acc + part
    y = _layer_norm(DEEPNORM_ALPHA * x + acc, ln_g_ref[...], ln_b_ref[...])
    o_ref[...] = _ple(y, p_ref[...], wpg_ref, bpg_ref, wpp_ref)


def _ffn_ple(x2d, p2d, wg, wu, wd, ln_g, ln_b, wpg, bpg, wpp, *, tile=512, fchunk=1024):
    n, d = x2d.shape
    f = wg.shape[1]
    pd = p2d.shape[1]
    chunks = tuple((lo, min(lo + fchunk, f)) for lo in range(0, f, fchunk))
    row = lambda t: t[None, :]
    return pl.pallas_call(
        functools.partial(_ffn_ple_kernel, chunks=chunks),
        grid=(n // tile,),
        in_specs=[pl.BlockSpec((tile, d), lambda i: (i, 0)), pl.BlockSpec((tile, pd), lambda i: (i, 0)),
                  _const_spec((d, f)), _const_spec((d, f)), _const_spec((f, d)),
                  _const_spec((1, d)), _const_spec((1, d)),
                  _const_spec((d, d)), _const_spec((1, d)), _const_spec((pd, d))],
        out_specs=pl.BlockSpec((tile, d), lambda i: (i, 0)),
        out_shape=jax.ShapeDtypeStruct((n, d), F32),
        compiler_params=_params(1),
        name="ffn_ple",
    )(x2d, p2d, wg.astype(BF16), wu.astype(BF16), wd.astype(BF16), row(ln_g), row(ln_b),
      wpg.astype(BF16), row(bpg), wpp.astype(BF16))


def _qkv_kernel(x_ref, w_ref, q_ref, kt_ref, v_ref, *, hd, scale):
    z = _dot(x_ref[...].astype(BF16), w_ref[...])
    q_ref[...] = (z[:, 0:hd] * scale).astype(BF16)
    k_t = z[:, hd:2 * hd].T.astype(BF16)
    heads, dk2, tile = kt_ref.shape
    ones = jnp.ones((tile, dk2), BF16)
    for h in range(heads):
        kt_ref[h] = k_t[h * dk2:(h + 1) * dk2, :]
        v_ref[:, 2 * h * dk2:(2 * h + 1) * dk2] = z[:, 2 * hd + h * dk2:2 * hd + (h + 1) * dk2].astype(BF16)
        v_ref[:, (2 * h + 1) * dk2:(2 * h + 2) * dk2] = ones


def _qkv(x, w_qkv, *, dh, tile):
    bn, s, d = x.shape
    hd = w_qkv.shape[1] // 3
    heads = hd // (2 * dh)
    return pl.pallas_call(
        functools.partial(_qkv_kernel, hd=hd, scale=dh ** -0.5 * LOG2_E),
        grid=(bn, s // tile),
        in_specs=[pl.BlockSpec((None, tile, d), lambda b, i: (b, i, 0)), _const_spec(w_qkv.shape)],
        out_specs=[pl.BlockSpec((None, tile, hd), lambda b, i: (b, i, 0)),
                   pl.BlockSpec((None, heads, None, 2 * dh, tile), lambda b, i: (b, 0, i, 0, 0)),
                   pl.BlockSpec((None, tile, 2 * hd), lambda b, i: (b, i, 0))],
        out_shape=[jax.ShapeDtypeStruct((bn, s, hd), BF16),
                   jax.ShapeDtypeStruct((bn, heads, s // tile, 2 * dh, tile), BF16),
                   jax.ShapeDtypeStruct((bn, s, 2 * hd), BF16)],
        compiler_params=_params(2),
        name="qkv_proj",
    )(x, w_qkv.astype(BF16))


def _diff_attn_kernel(q_ref, kt_ref, v_ref, lam_ref, g_ref, o_ref, m_ref, acc_ref,
                      qm_ref, sa_ref, sb_ref, *, blk, dh, lambda_init):
    qi = pl.program_id(2)
    m_ref[...] = jnp.full_like(m_ref, -1e30)
    acc_ref[...] = jnp.zeros_like(acc_ref)

    q = q_ref[...]
    lane = lax.broadcasted_iota(jnp.int32, q.shape, 1)
    zero = jnp.zeros_like(q)
    qm_ref[0] = jnp.where(lane < dh, q, zero)
    qm_ref[1] = jnp.where(lane >= dh, q, zero)

    def scores(ki, s_ref):
        kt_blk = kt_ref[ki]
        for c in range(2):
            s_ref[c] = _dot(qm_ref[c], kt_blk)

    def update(c, rows, s, v_part):
        m_prev = m_ref[c, rows]
        m_new = jnp.maximum(m_prev, jnp.max(s, axis=-1, keepdims=True))
        corr = jnp.exp2(m_prev - m_new)
        p = jnp.exp2(s - jnp.concatenate([m_new] * (s.shape[1] // LANES), axis=1))
        acc_ref[c, rows] = (jnp.concatenate([corr, corr], axis=1) * acc_ref[c, rows]
                            + _dot(p.astype(BF16), v_part))
        m_ref[c, rows] = m_new

    def softmax_pv(s_ref, ki, masked):
        off = pl.multiple_of(ki * blk, blk)
        v_blk = v_ref[pl.ds(off, blk), :]
        half = blk // 2
        for c in range(2):
            if not masked:
                update(c, slice(0, blk), s_ref[c], v_blk)
                continue
            s_top = s_ref[c, 0:half, 0:half]
            row = lax.broadcasted_iota(jnp.int32, s_top.shape, 0)
            col = lax.broadcasted_iota(jnp.int32, s_top.shape, 1)
            update(c, slice(0, half), jnp.where(col <= row, s_top, -1e30), v_blk[0:half])
            s_bot = s_ref[c, half:blk, :]
            row = lax.broadcasted_iota(jnp.int32, s_bot.shape, 0) + half
            col = lax.broadcasted_iota(jnp.int32, s_bot.shape, 1)
            update(c, slice(half, blk), jnp.where(col <= row, s_bot, -1e30), v_blk)

    scores(0, sa_ref)

    def pair(j, carry):
        k0 = 2 * j
        scores(k0 + 1, sb_ref)
        softmax_pv(sa_ref, k0, False)
        scores(k0 + 2, sa_ref)
        softmax_pv(sb_ref, k0 + 1, False)
        return carry

    lax.fori_loop(0, qi // 2, pair, 0)

    @pl.when(qi % 2 == 0)
    def _():
        softmax_pv(sa_ref, qi, True)

    @pl.when(qi % 2 == 1)
    def _():
        scores(qi, sb_ref)
        softmax_pv(sa_ref, qi - 1, False)
        softmax_pv(sb_ref, qi, True)

    a1 = acc_ref[0, :, 0:2 * dh] * (1.0 / acc_ref[0, :, 2 * dh:4 * dh])
    a2 = acc_ref[1, :, 0:2 * dh] * (1.0 / acc_ref[1, :, 2 * dh:4 * dh])
    lv = lam_ref[...]
    lam = (jnp.exp(jnp.sum(lv[0:1] * lv[1:2], axis=-1, keepdims=True))
           - jnp.exp(jnp.sum(lv[2:3] * lv[3:4], axis=-1, keepdims=True)) + lambda_init)
    o = a1 - lam * a2
    msq = jnp.mean(o * o, axis=-1, keepdims=True)
    o_ref[...] = (o * lax.rsqrt(msq + LN_EPS) * g_ref[...] * (1.0 - lambda_init)).astype(BF16)


def _diff_attn(q, kt, v_ext, lam_q1, lam_k1, lam_q2, lam_k2, subln_g, lambda_init, *, blk):
    bn, s, hd = q.shape
    dh = lam_q1.shape[0]
    heads = hd // (2 * dh)
    nb = s // blk
    lamv = jnp.zeros((8, LANES), F32).at[0:4, 0:dh].set(jnp.stack([lam_q1, lam_k1, lam_q2, lam_k2]))
    return pl.pallas_call(
        functools.partial(_diff_attn_kernel, blk=blk, dh=dh, lambda_init=lambda_init),
        grid=(bn, heads, nb),
        in_specs=[
            pl.BlockSpec((None, blk, 2 * dh), lambda b, h, i: (b, i, h)),
            pl.BlockSpec((None, None, nb, 2 * dh, blk), lambda b, h, i: (b, h, 0, 0, 0)),
            pl.BlockSpec((None, s, 4 * dh), lambda b, h, i: (b, 0, h)),
            pl.BlockSpec((8, LANES), lambda b, h, i: (0, 0)),
            pl.BlockSpec((1, 2 * dh), lambda b, h, i: (0, 0)),
        ],
        out_specs=pl.BlockSpec((None, blk, 2 * dh), lambda b, h, i: (b, i, h)),
        out_shape=jax.ShapeDtypeStruct((bn, s, hd), BF16),
        scratch_shapes=[pltpu.VMEM((2, blk, LANES), F32),
                        pltpu.VMEM((2, blk, 4 * dh), F32), pltpu.VMEM((2, blk, 2 * dh), BF16),
                        pltpu.VMEM((2, blk, blk), F32), pltpu.VMEM((2, blk, blk), F32)],
        compiler_params=_params(3),
        name="diff_attn",
    )(q, kt, v_ext, lamv, subln_g[None, :])


def _attn_out_kernel(o_ref, x_ref, w_ref, ln_g_ref, ln_b_ref, wr_hi_ref, wr_lo_ref, lstrict_ref,
                     y_ref, route_ref, cnt_ref, base_ref):
    @pl.when(pl.program_id(0) == 0)
    def _():
        base_ref[...] = jnp.zeros_like(base_ref)

    m = _dot(o_ref[...], w_ref[...])
    y = _layer_norm(DEEPNORM_ALPHA * x_ref[...] + m, ln_g_ref[...], ln_b_ref[...])
    y_ref[...] = y
    y_hi = y.astype(BF16)
    y_lo = (y - y_hi.astype(F32)).astype(BF16)
    logits = (_dot(y_hi, wr_hi_ref[...]) + _dot(y_lo, wr_hi_ref[...]) + _dot(y_hi, wr_lo_ref[...]))

    lane = lax.broadcasted_iota(jnp.int32, logits.shape, 1).astype(F32)
    neg = -jnp.inf
    lg = jnp.where(lane < N_EXPERTS, logits, neg)
    v0 = jnp.max(lg, axis=-1, keepdims=True)
    i0 = jnp.min(jnp.where(lg == v0, lane, float(LANES)), axis=-1, keepdims=True)
    lg2 = jnp.where(lane == i0, neg, lg)
    v1 = jnp.max(lg2, axis=-1, keepdims=True)
    i1 = jnp.min(jnp.where(lg2 == v1, lane, float(LANES)), axis=-1, keepdims=True)
    e = jnp.exp(v1 - v0)
    g0 = 1.0 / (1.0 + e)
    g1 = e / (1.0 + e)
    oh0 = lane == i0
    oh1 = lane == i1
    c = jnp.where(oh0, 1.0, 0.0) + jnp.where(oh1, 1.0, 0.0)
    pre = _dot(lstrict_ref[...], c.astype(BF16)) + base_ref[...]
    r0 = jnp.sum(jnp.where(oh0, pre, 0.0), axis=-1, keepdims=True)
    r1 = jnp.sum(jnp.where(oh1, pre, 0.0), axis=-1, keepdims=True)
    base_ref[...] = base_ref[...] + jnp.sum(c, axis=0, keepdims=True)
    cnt_ref[...] = jnp.broadcast_to(base_ref[...], cnt_ref.shape)
    fields = (i0, i1, r0, r1, g0, g1)
    route = jnp.zeros_like(logits)
    for idx, val in enumerate(fields):
        route = jnp.where(lane == idx, val, route)
    route_ref[...] = route


def _attn_out(o2d, x2d, w_out, ln_g, ln_b, w_router, *, tile=512):
    n, d = x2d.shape
    wr = jnp.pad(w_router, ((0, 0), (0, LANES - w_router.shape[1])))
    wr_hi = wr.astype(BF16)
    wr_lo = (wr - wr_hi.astype(F32)).astype(BF16)
    lstrict = (jnp.arange(tile)[:, None] > jnp.arange(tile)[None, :]).astype(BF16)
    row = lambda t: t[None, :]
    tspec = pl.BlockSpec((tile, d), lambda i: (i, 0))
    return pl.pallas_call(
        _attn_out_kernel,
        grid=(n // tile,),
        in_specs=[tspec, tspec, _const_spec((d, d)), _const_spec((1, d)), _const_spec((1, d)),
                  _const_spec((d, LANES)), _const_spec((d, LANES)), _const_spec((tile, tile))],
        out_specs=[tspec, pl.BlockSpec((tile, LANES), lambda i: (i, 0)),
                   pl.BlockSpec((8, LANES), lambda i: (0, 0))],
        out_shape=[jax.ShapeDtypeStruct((n, d), F32), jax.ShapeDtypeStruct((n, LANES), F32),
                   jax.ShapeDtypeStruct((8, LANES), F32)],
        scratch_shapes=[pltpu.VMEM((1, LANES), F32)],
        compiler_params=_params(1),
        name="attn_out_router",
    )(o2d, x2d, w_out.astype(BF16), row(ln_g), row(ln_b), wr_hi, wr_lo, lstrict)


def _route_tables(route, cnt, *, tm):
    n = route.shape[0]
    e01 = route[:, 0:2].astype(jnp.int32)
    r01 = route[:, 2:4].astype(jnp.int32)
    counts = cnt[0, :N_EXPERTS].astype(jnp.int32)
    padded = (counts + tm - 1) // tm * tm
    pend = jnp.cumsum(padded)
    pstart = pend - padded
    start_of = jnp.sum(jnp.where(e01[:, :, None] == jnp.arange(N_EXPERTS)[None, None, :],
                                 pstart[None, None, :], 0), axis=-1)
    dest = start_of + r01
    n_rows = n * TOP_K + N_EXPERTS * tm
    n_blocks = n_rows // tm
    n_used = pend[-1] // tm
    blk_idx = jnp.arange(n_blocks, dtype=jnp.int32)
    blk_start = jnp.minimum(blk_idx, n_used - 1) * tm
    blk_e = jnp.minimum(jnp.sum(blk_start[:, None] >= pend[None, :], axis=1), N_EXPERTS - 1)
    valid = jnp.clip(pstart[blk_e] + counts[blk_e] - blk_idx * tm, 0, tm)
    return (dest, blk_e.astype(jnp.int32), n_used.reshape(1).astype(jnp.int32),
            valid.astype(jnp.int32), n_rows)


SC_CORES = 2
SC_SUBCORES = 16
SC_WORKERS = SC_CORES * SC_SUBCORES
SC_ROWS = 32


def _sc_mesh():
    return plsc.VectorSubcoreMesh(core_axis_name="c", subcore_axis_name="s",
                                  num_cores=SC_CORES, num_subcores=SC_SUBCORES)


def _sc_worker_id():
    return lax.axis_index("s") * SC_CORES + lax.axis_index("c")


def _sc_dispatch(x2d, dest, n_rows):
    n, d = x2d.shape
    per_w = n // SC_WORKERS
    n_chunks = per_w // SC_ROWS
    assert n == SC_WORKERS * n_chunks * SC_ROWS and n_chunks % 2 == 0
    d0 = dest[:, 0].reshape(SC_WORKERS, n_chunks, SC_ROWS)
    d1 = dest[:, 1].reshape(SC_WORKERS, n_chunks, SC_ROWS)

    def body(x_hbm, d0_hbm, d1_hbm, xs_hbm, d0_v, d1_v, rows_v, rsem, s0sem, s1sem):
        wid = _sc_worker_id()
        base = wid * per_w
        pltpu.sync_copy(d0_hbm.at[wid], d0_v)
        pltpu.sync_copy(d1_hbm.at[wid], d1_v)

        def read(c, slot):
            return pltpu.make_async_copy(x_hbm.at[pl.ds(base + c * SC_ROWS, SC_ROWS)],
                                         rows_v.at[slot], rsem.at[slot])

        def scat(idx_v, sem, c, slot):
            return pltpu.make_async_copy(rows_v.at[slot], xs_hbm.at[idx_v.at[c]], sem.at[slot])

        def start_scatters(c, slot):
            scat(d0_v, s0sem, c, slot).start()
            scat(d1_v, s1sem, c, slot).start()

        def wait_scatters(c, slot):
            scat(d0_v, s0sem, c, slot).wait()
            scat(d1_v, s1sem, c, slot).wait()

        read(0, 0).start()

        def pair(j, carry):
            c0 = 2 * j
            read(c0, 0).wait()

            @pl.when(j > 0)
            def _():
                wait_scatters(c0 - 1, 1)

            read(c0 + 1, 1).start()
            start_scatters(c0, 0)
            read(c0 + 1, 1).wait()
            wait_scatters(c0, 0)

            @pl.when(c0 + 2 < n_chunks)
            def _():
                read(c0 + 2, 0).start()

            start_scatters(c0 + 1, 1)
            return carry

        lax.fori_loop(0, n_chunks // 2, pair, 0)
        wait_scatters(n_chunks - 1, 1)

    return pl.kernel(
        body,
        out_type=jax.ShapeDtypeStruct((n_rows, d), x2d.dtype),
        mesh=_sc_mesh(),
        scratch_types=[pltpu.VMEM((n_chunks, SC_ROWS), jnp.int32), pltpu.VMEM((n_chunks, SC_ROWS), jnp.int32),
                       pltpu.VMEM((2, SC_ROWS, d), x2d.dtype), pltpu.SemaphoreType.DMA((2,)),
                       pltpu.SemaphoreType.DMA((2,)), pltpu.SemaphoreType.DMA((2,))],
        name="sc_dispatch",
    )(x2d, d0, d1)


def _sc_gather(table, idx):
    _, d = table.shape
    b = idx.shape[0]
    per_w = b // SC_WORKERS
    n_chunks = per_w // SC_ROWS
    assert b == SC_WORKERS * n_chunks * SC_ROWS and n_chunks % 2 == 0
    idx3 = idx.reshape(SC_WORKERS, n_chunks, SC_ROWS)

    def body(table_hbm, idx_hbm, out_hbm, idx_v, rows_v, gsem, wsem):
        wid = _sc_worker_id()
        base = wid * per_w
        pltpu.sync_copy(idx_hbm.at[wid], idx_v)

        def gather(c, slot):
            return pltpu.make_async_copy(table_hbm.at[idx_v.at[c]], rows_v.at[slot], gsem.at[slot])

        def write(c, slot):
            return pltpu.make_async_copy(rows_v.at[slot],
                                         out_hbm.at[pl.ds(base + c * SC_ROWS, SC_ROWS)], wsem.at[slot])

        gather(0, 0).start()

        def pair(j, carry):
            c0 = 2 * j
            gather(c0, 0).wait()

            @pl.when(j > 0)
            def _():
                write(c0 - 1, 1).wait()

            gather(c0 + 1, 1).start()
            write(c0, 0).start()
            gather(c0 + 1, 1).wait()
            write(c0, 0).wait()

            @pl.when(c0 + 2 < n_chunks)
            def _():
                gather(c0 + 2, 0).start()

            write(c0 + 1, 1).start()
            return carry

        lax.fori_loop(0, n_chunks // 2, pair, 0)
        write(n_chunks - 1, 1).wait()

    return pl.kernel(
        body,
        out_type=jax.ShapeDtypeStruct((b, d), table.dtype),
        mesh=_sc_mesh(),
        scratch_types=[pltpu.VMEM((n_chunks, SC_ROWS), jnp.int32), pltpu.VMEM((2, SC_ROWS, d), table.dtype),
                       pltpu.SemaphoreType.DMA((2,)), pltpu.SemaphoreType.DMA((2,))],
        name="sc_combine_gather",
    )(table, idx3)


def _moe_kernel(blk_e, n_used, valid, xs_ref, wg_ref, wu_ref, wd_ref, y_ref, *, chunks):
    m = pl.program_id(0)
    f = pl.program_id(1)

    @pl.when(f == 0)
    def _():
        y_ref[...] = jnp.zeros_like(y_ref)

    @pl.when(m < n_used[0])
    def _():
        row = lax.broadcasted_iota(jnp.int32, xs_ref.shape, 0)
        x = jnp.where(row < valid[m], xs_ref[...], 0.0).astype(BF16)
        for lo, hi in chunks:
            g = _dot(x, wg_ref[:, lo:hi])
            u = _dot(x, wu_ref[:, lo:hi])
            h = (g * _sigmoid(g) * u).astype(BF16)
            y_ref[...] += _dot(h, wd_ref[lo:hi, :])


def _moe_grouped(xs, blk_e, n_used, valid, wg, wu, wd, *, tm, tf, fchunk=1024):
    n_rows, d = xs.shape
    fdim = wg.shape[2]
    nf = fdim // tf
    assert nf * tf == fdim
    n_blocks = n_rows // tm
    chunks = tuple((lo, min(lo + fchunk, tf)) for lo in range(0, tf, fchunk))

    def f_eff(m, f, nu):
        mm = jnp.minimum(m, nu[0] - 1)
        snake = jnp.where(mm % 2 == 0, f, nf - 1 - f)
        return jnp.where(m < nu[0], snake, jnp.where(mm % 2 == 0, nf - 1, 0))

    grid_spec = pltpu.PrefetchScalarGridSpec(
        num_scalar_prefetch=3,
        grid=(n_blocks, nf),
        in_specs=[
            pl.BlockSpec((tm, d), lambda m, f, be, nu, va: (jnp.minimum(m, nu[0] - 1), 0)),
            pl.BlockSpec((None, d, tf), lambda m, f, be, nu, va: (be[m], 0, f_eff(m, f, nu))),
            pl.BlockSpec((None, d, tf), lambda m, f, be, nu, va: (be[m], 0, f_eff(m, f, nu))),
            pl.BlockSpec((None, tf, d), lambda m, f, be, nu, va: (be[m], f_eff(m, f, nu), 0)),
        ],
        out_specs=pl.BlockSpec((tm, d), lambda m, f, be, nu, va: (m, 0)),
    )
    return pl.pallas_call(
        functools.partial(_moe_kernel, chunks=chunks),
        grid_spec=grid_spec,
        out_shape=jax.ShapeDtypeStruct((n_rows, d), F32),
        compiler_params=_params(2),
        name="moe_grouped",
    )(blk_e, n_used, valid, xs, wg, wu, wd)


def _final_kernel(x_ref, y0_ref, y1_ref, route_ref, p_ref, ln_g_ref, ln_b_ref, wpg_ref, bpg_ref,
                  wpp_ref, o_ref):
    route = route_ref[...]
    f = route[:, 4:5] * y0_ref[...] + route[:, 5:6] * y1_ref[...]
    y = _layer_norm(DEEPNORM_ALPHA * x_ref[...] + f, ln_g_ref[...], ln_b_ref[...])
    o_ref[...] = _ple(y, p_ref[...], wpg_ref, bpg_ref, wpp_ref)


def _final(x2d, y01, route, p2d, ln_g, ln_b, wpg, bpg, wpp, *, tile=512):
    n, d = x2d.shape
    pd = p2d.shape[1]
    row = lambda t: t[None, :]
    tspec = pl.BlockSpec((tile, d), lambda i: (i, 0))
    second = pl.BlockSpec((tile, d), lambda i: (i + n // tile, 0))
    y0 = y1 = y01
    return pl.pallas_call(
        _final_kernel,
        grid=(n // tile,),
        in_specs=[tspec, tspec, second, pl.BlockSpec((tile, LANES), lambda i: (i, 0)),
                  pl.BlockSpec((tile, pd), lambda i: (i, 0)),
                  _const_spec((1, d)), _const_spec((1, d)),
                  _const_spec((d, d)), _const_spec((1, d)), _const_spec((pd, d))],
        out_specs=tspec,
        out_shape=jax.ShapeDtypeStruct((n, d), F32),
        compiler_params=_params(1),
        name="moe_combine_ple",
    )(x2d, y0, y1, route, p2d, row(ln_g), row(ln_b), wpg.astype(BF16), row(bpg), wpp.astype(BF16))


def _even_layer(x, p_i, w_in, a_ln_g, a_ln_b, a_ws, a_bs, w_gate_up, b_gate, norm_g, w_out,
                ln1_g, ln1_b, wg, wu, wd, ln2_g, ln2_b, wpp, wpg, bpg):
    bn, s, d = x.shape
    x1 = _even_mixer(x, w_in, a_ln_g, a_ln_b, a_ws, a_bs, w_gate_up, b_gate, norm_g, w_out, ln1_g, ln1_b)
    x2 = _ffn_ple(x1.reshape(bn * s, d), p_i.reshape(bn * s, -1), wg, wu, wd, ln2_g, ln2_b, wpg, bpg, wpp)
    return x2.reshape(bn, s, d)


def _odd_layer(x, p_i, layer_idx, w_qkv, lam_q1, lam_k1, lam_q2, lam_k2, subln_g, w_out, ln1_g, ln1_b,
               w_router, ewg, ewu, ewd, ln2_g, ln2_b, wpp, wpg, bpg, *, attn_blk=512, moe_tm=512,
               moe_tf=1792):
    bn, s, d = x.shape
    n = bn * s
    lambda_init = 0.8 - 0.6 * math.exp(-0.3 * layer_idx)
    q, kt, v_ext = _qkv(x, w_qkv, dh=lam_q1.shape[0], tile=attn_blk)
    o = _diff_attn(q, kt, v_ext, lam_q1, lam_k1, lam_q2, lam_k2, subln_g, lambda_init, blk=attn_blk)
    x3, route, cnt = _attn_out(o.reshape(n, d), x.reshape(n, d), w_out, ln1_g, ln1_b, w_router)
    dest, blk_e, n_used, valid, n_rows = _route_tables(route, cnt, tm=moe_tm)
    xs = _sc_dispatch(x3, dest, n_rows)
    y = _moe_grouped(xs, blk_e, n_used, valid, ewg.astype(BF16), ewu.astype(BF16), ewd.astype(BF16),
                     tm=moe_tm, tf=moe_tf)
    y01 = _sc_gather(y, jnp.concatenate([dest[:, 0], dest[:, 1]]))
    out = _final(x3, y01, route, p_i.reshape(n, -1), ln2_g, ln2_b, wpg, bpg, wpp)
    return out.reshape(bn, s, d)


def kernel(x, p, e_w_in, e_a_ln_g, e_a_ln_b, e_a_ws, e_a_bs, e_b_w_gate_up, e_b_b_gate, e_b_norm_g, e_w_out, e_ln1_g, e_ln1_b, e_ffn_wg, e_ffn_wu, e_ffn_wd, e_ln2_g, e_ln2_b, o_w_qkv, o_lam_q1, o_lam_k1, o_lam_q2, o_lam_k2, o_subln_g, o_w_out, o_ln1_g, o_ln1_b, o_router, o_exp_wg, o_exp_wu, o_exp_wd, o_ln2_g, o_ln2_b, ple_w_proj, ple_w_gate, ple_b_gate):
    for i in range(DEPTH):
        j = i // 2
        if i % 2 == 0:
            x = _even_layer(x, p[i], e_w_in[j], e_a_ln_g[j], e_a_ln_b[j], e_a_ws[j], e_a_bs[j],
                            e_b_w_gate_up[j], e_b_b_gate[j], e_b_norm_g[j], e_w_out[j],
                            e_ln1_g[j], e_ln1_b[j], e_ffn_wg[j], e_ffn_wu[j], e_ffn_wd[j],
                            e_ln2_g[j], e_ln2_b[j], ple_w_proj[i], ple_w_gate[i], ple_b_gate[i])
        else:
            x = _odd_layer(x, p[i], i, o_w_qkv[j], o_lam_q1[j], o_lam_k1[j], o_lam_q2[j], o_lam_k2[j],
                           o_subln_g[j], o_w_out[j], o_ln1_g[j], o_ln1_b[j], o_router[j],
                           o_exp_wg[j], o_exp_wu[j], o_exp_wd[j], o_ln2_g[j], o_ln2_b[j],
                           ple_w_proj[i], ple_w_gate[i], ple_b_gate[i])
    return x
```

```python
import functools
import math

import jax
import jax.numpy as jnp
from jax import lax
from jax.experimental import pallas as pl
from jax.experimental.pallas import tpu as pltpu
from jax.experimental.pallas import tpu_sc as plsc

F32 = jnp.float32
BF16 = jnp.bfloat16

DEPTH = 2
DEEPNORM_ALPHA = (2.0 * DEPTH) ** 0.25
LN_EPS = 1e-5
A_CHUNK = 128
A_GROUPS = 8
B_HEADS = 4
B_CHUNK = 64
B_TAU = 16.0
CS_ROWS = 256
C_HEADS = 8
N_EXPERTS = 8
TOP_K = 2
LANES = 128
LOG2_E = 1.4426950408889634
VMEM_LIMIT = 56 * 1024 * 1024

NT_DIMS = (((1,), (1,)), ((), ()))
TN_DIMS = (((0,), (0,)), ((), ()))


def _dot(a, b):
    return jnp.dot(a, b, preferred_element_type=F32)


def _layer_norm(x, g, b):
    mu = jnp.mean(x, axis=-1, keepdims=True)
    xc = x - mu
    var = jnp.mean(xc * xc, axis=-1, keepdims=True)
    return xc * lax.rsqrt(var + LN_EPS) * g + b


def _sigmoid(x):
    return 1.0 / (1.0 + jnp.exp(-x))


def _split3(a):
    hi = a.astype(BF16)
    r1 = a - hi.astype(F32)
    mid = r1.astype(BF16)
    lo = (r1 - mid.astype(F32)).astype(BF16)
    return hi, mid, lo


def _const_spec(shape):
    zeros = (0,) * len(shape)
    return pl.BlockSpec(shape, lambda *_: zeros, pipeline_mode=pl.Buffered(1))


def _params(n_axes):
    return pltpu.CompilerParams(dimension_semantics=("arbitrary",) * n_axes,
                                vmem_limit_bytes=VMEM_LIMIT)


def _even_mixer_kernel(x_ref, w_in_ref, a_g_ref, a_b_ref, wcat_ref, abias_ref, mstack_ref,
                       w_up_ref, b_gate_ref, ng_ref, w_out_ref, ln_g_ref, ln_b_ref,
                       o_ref, state_ref, *, tile, aw, dkh, dvh):
    @pl.when(pl.program_id(1) == 0)
    def _():
        state_ref[...] = jnp.zeros_like(state_ref)

    hk = B_HEADS * dkh
    hv = B_HEADS * dvh
    x = x_ref[...]
    z = _dot(x.astype(BF16), w_in_ref[...])

    u = jax.nn.gelu(z[:, 0:aw])
    v = _layer_norm(jax.nn.gelu(z[:, aw:2 * aw]), a_g_ref[...], a_b_ref[...])
    gd = aw // A_GROUPS
    rows = lax.broadcasted_iota(jnp.int32, (A_CHUNK, A_GROUPS * A_CHUNK), 0)
    cols = lax.broadcasted_iota(jnp.int32, (A_CHUNK, A_GROUPS * A_CHUNK), 1)
    wcat = jnp.where((cols % A_CHUNK) <= rows, wcat_ref[...], 0.0).astype(BF16)
    r_bd = lax.broadcasted_iota(jnp.int32, (A_GROUPS * A_CHUNK, aw), 0) // A_CHUNK
    c_bd = lax.broadcasted_iota(jnp.int32, (A_GROUPS * A_CHUNK, aw), 1) // gd
    mask_bd = r_bd == c_bd
    ya_parts = []
    for c in range(tile // A_CHUNK):
        sl = slice(c * A_CHUNK, (c + 1) * A_CHUNK)
        v_rep = jnp.concatenate([v[sl]] * A_GROUPS, axis=0)
        v_bd = jnp.where(mask_bd, v_rep, 0.0).astype(BF16)
        sg = _dot(wcat, v_bd) + abias_ref[...]
        ya_parts.append(u[sl] * sg)
    ya = jnp.concatenate(ya_parts, axis=0)

    o0 = 2 * aw
    q = z[:, o0:o0 + hk] * (dkh ** -0.5)
    k = z[:, o0 + hk:o0 + 2 * hk]
    vv = z[:, o0 + 2 * hk:o0 + 2 * hk + hv]
    r = z[:, o0 + 2 * hk + hv:o0 + 2 * hk + 2 * hv]
    g_low = z[:, o0 + 2 * hk + 2 * hv:]
    pre = _dot(g_low.astype(BF16), w_up_ref[...]) + b_gate_ref[...]
    log_a = (jnp.minimum(pre, 0.0) - jnp.log1p(jnp.exp(-jnp.abs(pre)))) * (1.0 / B_TAU)
    la_hi, la_mid, la_lo = _split3(log_a)
    ms = mstack_ref[...]
    b_parts, mid_parts, last_parts = [], [], []
    for t in range(tile // CS_ROWS):
        rs = slice(t * CS_ROWS, (t + 1) * CS_ROWS)
        cs = _dot(ms, la_hi[rs]) + _dot(ms, la_mid[rs]) + _dot(ms, la_lo[rs])
        b_parts.append(cs[0:CS_ROWS])
        mid_parts.append(cs[CS_ROWS:2 * CS_ROWS])
        last_parts.append(cs[2 * CS_ROWS:3 * CS_ROWS])
    b_cum = jnp.concatenate(b_parts, axis=0)
    d_mid = jnp.concatenate(mid_parts, axis=0)
    d_last = jnp.concatenate(last_parts, axis=0)
    qe = (q * jnp.exp(d_mid)).astype(BF16)
    ke = k * jnp.exp(-d_mid)
    kd = (k * jnp.exp(d_last)).astype(BF16)
    qb = (q * jnp.exp(b_cum)).astype(BF16)
    dec = jnp.exp(b_cum + d_last)
    vvb = vv.astype(BF16)

    mask_kk = (lax.broadcasted_iota(jnp.int32, (hk, hk), 0) // dkh
               == lax.broadcasted_iota(jnp.int32, (hk, hk), 1) // dkh)
    mask_vbd = (lax.broadcasted_iota(jnp.int32, (hk, hv), 0) // dkh
                == lax.broadcasted_iota(jnp.int32, (hk, hv), 1) // dvh)
    mask_st = (lax.broadcasted_iota(jnp.int32, (hv, hk), 0) // dvh
               == lax.broadcasted_iota(jnp.int32, (hv, hk), 1) // dkh)
    causal = ((lax.broadcasted_iota(jnp.int32, (B_CHUNK, hk), 1) % B_CHUNK)
              <= lax.broadcasted_iota(jnp.int32, (B_CHUNK, hk), 0))

    st = state_ref[...]
    o_parts = []
    for c in range(tile // B_CHUNK):
        sl = slice(c * B_CHUNK, (c + 1) * B_CHUNK)
        ke_bd = jnp.where(mask_kk, jnp.concatenate([ke[sl]] * B_HEADS, axis=0), 0.0).astype(BF16)
        s_cat = lax.dot_general(qe[sl], ke_bd, NT_DIMS, preferred_element_type=F32)
        s_cat = jnp.where(causal, s_cat, 0.0).astype(BF16)
        v_bd = jnp.where(mask_vbd, jnp.concatenate([vv[sl]] * B_HEADS, axis=0), 0.0).astype(BF16)
        o_c = _dot(s_cat, v_bd) + lax.dot_general(qb[sl], st.astype(BF16), NT_DIMS,
                                                   preferred_element_type=F32)
        kv_t = lax.dot_general(vvb[sl], kd[sl], TN_DIMS, preferred_element_type=F32)
        st = dec[c * B_CHUNK:c * B_CHUNK + 1] * st + jnp.where(mask_st, kv_t, 0.0)
        o_parts.append(o_c)
    state_ref[...] = st
    o = jnp.concatenate(o_parts, axis=0)

    yb_parts = []
    for h in range(B_HEADS):
        oh = o[:, h * dvh:(h + 1) * dvh]
        msq = jnp.mean(oh * oh, axis=-1, keepdims=True)
        yb_parts.append(oh * lax.rsqrt(msq + LN_EPS))
    yb = jnp.concatenate(yb_parts, axis=1) * ng_ref[...] * (r * _sigmoid(r))

    y_cat = jnp.concatenate([ya, yb], axis=1).astype(BF16)
    m = _dot(y_cat, w_out_ref[...])
    o_ref[...] = _layer_norm(DEEPNORM_ALPHA * x + m, ln_g_ref[...], ln_b_ref[...])


def _gla_cumsum_matrices(tile):
    i = jnp.arange(tile)[:, None]
    j = jnp.arange(tile)[None, :]
    same = (i // B_CHUNK) == (j // B_CHUNK)
    m_cum = same & (j <= i)
    m_mid = same & (j <= (i // B_CHUNK) * B_CHUNK + B_CHUNK // 2 - 1)
    m_last = same
    f = lambda t: t.astype(F32)
    return jnp.concatenate([f(m_cum), f(m_cum) - f(m_mid), f(m_last) - f(m_cum)], axis=0).astype(BF16)


def _even_mixer(x, w_in, a_ln_g, a_ln_b, a_ws, a_bs, w_gate_up, b_gate, norm_g, w_out, ln_g, ln_b,
                *, tile=512):
    bn, s, d = x.shape
    aw = a_ln_g.shape[0]
    hk = w_gate_up.shape[1]
    dkh = hk // B_HEADS
    dvh = norm_g.shape[0]
    hv = B_HEADS * dvh
    rank = w_gate_up.shape[0]
    main = 2 * aw + 2 * hk + 2 * hv
    gd = aw // A_GROUPS
    w_in_p = jnp.concatenate([w_in[:, :main], jnp.pad(w_in[:, main:], ((0, 0), (0, LANES - rank)))],
                             axis=1).astype(BF16)
    w_up_p = jnp.pad(w_gate_up, ((0, LANES - rank), (0, 0))).astype(BF16)
    wcat = jnp.transpose(a_ws, (1, 0, 2)).reshape(A_CHUNK, A_GROUPS * A_CHUNK)
    abias = jnp.repeat(a_bs.T, gd, axis=1)
    mstack = _gla_cumsum_matrices(CS_ROWS)
    ng = jnp.tile(norm_g, B_HEADS)[None, :]
    row = lambda t: t[None, :]
    kern = functools.partial(_even_mixer_kernel, tile=tile, aw=aw, dkh=dkh, dvh=dvh)
    tile_spec = pl.BlockSpec((None, tile, d), lambda b, i: (b, i, 0))
    return pl.pallas_call(
        kern,
        grid=(bn, s // tile),
        in_specs=[tile_spec, _const_spec(w_in_p.shape), _const_spec((1, aw)), _const_spec((1, aw)),
                  _const_spec(wcat.shape), _const_spec(abias.shape), _const_spec(mstack.shape),
                  _const_spec(w_up_p.shape), _const_spec((1, hk)), _const_spec((1, hv)),
                  _const_spec(w_out.shape), _const_spec((1, d)), _const_spec((1, d))],
        out_specs=tile_spec,
        out_shape=jax.ShapeDtypeStruct(x.shape, F32),
        scratch_shapes=[pltpu.VMEM((hv, hk), F32)],
        compiler_params=_params(2),
        name="even_mixer",
    )(x, w_in_p, row(a_ln_g), row(a_ln_b), wcat, abias, mstack, w_up_p, row(b_gate), ng,
      w_out.astype(BF16), row(ln_g), row(ln_b))


def _ple(y, p, wpg_ref, bpg_ref, wpp_ref):
    gate = _sigmoid(_dot(y.astype(BF16), wpg_ref[...]) + bpg_ref[...])
    return y + gate * _dot(p.astype(BF16), wpp_ref[...])


def _ffn_ple_kernel(x_ref, p_ref, wg_ref, wu_ref, wd_ref, ln_g_ref, ln_b_ref, wpg_ref, bpg_ref,
                    wpp_ref, o_ref, *, chunks):
    x = x_ref[...]
    xb = x.astype(BF16)
    acc = None
    for lo, hi in chunks:
        g = _dot(xb, wg_ref[:, lo:hi])
        u = _dot(xb, wu_ref[:, lo:hi])
        h = (g * _sigmoid(g) * u).astype(BF16)
        part = _dot(h, wd_ref[lo:hi, :])
        acc = part if acc is None else acc + part
    y = _layer_norm(DEEPNORM_ALPHA * x + acc, ln_g_ref[...], ln_b_ref[...])
    o_ref[...] = _ple(y, p_ref[...], wpg_ref, bpg_ref, wpp_ref)


def _ffn_ple(x2d, p2d, wg, wu, wd, ln_g, ln_b, wpg, bpg, wpp, *, tile=512, fchunk=1024):
    n, d = x2d.shape
    f = wg.shape[1]
    pd = p2d.shape[1]
    chunks = tuple((lo, min(lo + fchunk, f)) for lo in range(0, f, fchunk))
    row = lambda t: t[None, :]
    return pl.pallas_call(
        functools.partial(_ffn_ple_kernel, chunks=chunks),
        grid=(n // tile,),
        in_specs=[pl.BlockSpec((tile, d), lambda i: (i, 0)), pl.BlockSpec((tile, pd), lambda i: (i, 0)),
                  _const_spec((d, f)), _const_spec((d, f)), _const_spec((f, d)),
                  _const_spec((1, d)), _const_spec((1, d)),
                  _const_spec((d, d)), _const_spec((1, d)), _const_spec((pd, d))],
        out_specs=pl.BlockSpec((tile, d), lambda i: (i, 0)),
        out_shape=jax.ShapeDtypeStruct((n, d), F32),
        compiler_params=_params(1),
        name="ffn_ple",
    )(x2d, p2d, wg.astype(BF16), wu.astype(BF16), wd.astype(BF16), row(ln_g), row(ln_b),
      wpg.astype(BF16), row(bpg), wpp.astype(BF16))


def _qkv_kernel(x_ref, w_ref, q_ref, kt_ref, v_ref, *, hd, scale):
    z = _dot(x_ref[...].astype(BF16), w_ref[...])
    q_ref[...] = (z[:, 0:hd] * scale).astype(BF16)
    k_t = z[:, hd:2 * hd].T.astype(BF16)
    heads, dk2, tile = kt_ref.shape
    ones = jnp.ones((tile, dk2), BF16)
    for h in range(heads):
        kt_ref[h] = k_t[h * dk2:(h + 1) * dk2, :]
        v_ref[:, 2 * h * dk2:(2 * h + 1) * dk2] = z[:, 2 * hd + h * dk2:2 * hd + (h + 1) * dk2].astype(BF16)
        v_ref[:, (2 * h + 1) * dk2:(2 * h + 2) * dk2] = ones


def _qkv(x, w_qkv, *, dh, tile):
    bn, s, d = x.shape
    hd = w_qkv.shape[1] // 3
    heads = hd // (2 * dh)
    return pl.pallas_call(
        functools.partial(_qkv_kernel, hd=hd, scale=dh ** -0.5 * LOG2_E),
        grid=(bn, s // tile),
        in_specs=[pl.BlockSpec((None, tile, d), lambda b, i: (b, i, 0)), _const_spec(w_qkv.shape)],
        out_specs=[pl.BlockSpec((None, tile, hd), lambda b, i: (b, i, 0)),
                   pl.BlockSpec((None, heads, None, 2 * dh, tile), lambda b, i: (b, 0, i, 0, 0)),
                   pl.BlockSpec((None, tile, 2 * hd), lambda b, i: (b, i, 0))],
        out_shape=[jax.ShapeDtypeStruct((bn, s, hd), BF16),
                   jax.ShapeDtypeStruct((bn, heads, s // tile, 2 * dh, tile), BF16),
                   jax.ShapeDtypeStruct((bn, s, 2 * hd), BF16)],
        compiler_params=_params(2),
        name="qkv_proj",
    )(x, w_qkv.astype(BF16))


def _diff_attn_kernel(q_ref, kt_ref, v_ref, lam_ref, g_ref, o_ref, m_ref, acc_ref,
                      qm_ref, sa_ref, sb_ref, *, blk, dh, lambda_init):
    qi = pl.program_id(2)
    m_ref[...] = jnp.full_like(m_ref, -1e30)
    acc_ref[...] = jnp.zeros_like(acc_ref)

    q = q_ref[...]
    lane = lax.broadcasted_iota(jnp.int32, q.shape, 1)
    zero = jnp.zeros_like(q)
    qm_ref[0] = jnp.where(lane < dh, q, zero)
    qm_ref[1] = jnp.where(lane >= dh, q, zero)

    def scores(ki, s_ref):
        kt_blk = kt_ref[ki]
        for c in range(2):
            s_ref[c] = _dot(qm_ref[c], kt_blk)

    def update(c, rows, s, v_part):
        m_prev = m_ref[c, rows]
        m_new = jnp.maximum(m_prev, jnp.max(s, axis=-1, keepdims=True))
        corr = jnp.exp2(m_prev - m_new)
        p = jnp.exp2(s - jnp.concatenate([m_new] * (s.shape[1] // LANES), axis=1))
        acc_ref[c, rows] = (jnp.concatenate([corr, corr], axis=1) * acc_ref[c, rows]
                            + _dot(p.astype(BF16), v_part))
        m_ref[c, rows] = m_new

    def softmax_pv(s_ref, ki, masked):
        off = pl.multiple_of(ki * blk, blk)
        v_blk = v_ref[pl.ds(off, blk), :]
        half = blk // 2
        for c in range(2):
            if not masked:
                update(c, slice(0, blk), s_ref[c], v_blk)
                continue
            s_top = s_ref[c, 0:half, 0:half]
            row = lax.broadcasted_iota(jnp.int32, s_top.shape, 0)
            col = lax.broadcasted_iota(jnp.int32, s_top.shape, 1)
            update(c, slice(0, half), jnp.where(col <= row, s_top, -1e30), v_blk[0:half])
            s_bot = s_ref[c, half:blk, :]
            row = lax.broadcasted_iota(jnp.int32, s_bot.shape, 0) + half
            col = lax.broadcasted_iota(jnp.int32, s_bot.shape, 1)
            update(c, slice(half, blk), jnp.where(col <= row, s_bot, -1e30), v_blk)

    scores(0, sa_ref)

    def pair(j, carry):
        k0 = 2 * j
        scores(k0 + 1, sb_ref)
        softmax_pv(sa_ref, k0, False)
        scores(k0 + 2, sa_ref)
        softmax_pv(sb_ref, k0 + 1, False)
        return carry

    lax.fori_loop(0, qi // 2, pair, 0)

    @pl.when(qi % 2 == 0)
    def _():
        softmax_pv(sa_ref, qi, True)

    @pl.when(qi % 2 == 1)
    def _():
        scores(qi, sb_ref)
        softmax_pv(sa_ref, qi - 1, False)
        softmax_pv(sb_ref, qi, True)

    a1 = acc_ref[0, :, 0:2 * dh] * (1.0 / acc_ref[0, :, 2 * dh:4 * dh])
    a2 = acc_ref[1, :, 0:2 * dh] * (1.0 / acc_ref[1, :, 2 * dh:4 * dh])
    lv = lam_ref[...]
    lam = (jnp.exp(jnp.sum(lv[0:1] * lv[1:2], axis=-1, keepdims=True))
           - jnp.exp(jnp.sum(lv[2:3] * lv[3:4], axis=-1, keepdims=True)) + lambda_init)
    o = a1 - lam * a2
    msq = jnp.mean(o * o, axis=-1, keepdims=True)
    o_ref[...] = (o * lax.rsqrt(msq + LN_EPS) * g_ref[...] * (1.0 - lambda_init)).astype(BF16)


def _diff_attn(q, kt, v_ext, lam_q1, lam_k1, lam_q2, lam_k2, subln_g, lambda_init, *, blk):
    bn, s, hd = q.shape
    dh = lam_q1.shape[0]
    heads = hd // (2 * dh)
    nb = s // blk
    lamv = jnp.zeros((8, LANES), F32).at[0:4, 0:dh].set(jnp.stack([lam_q1, lam_k1, lam_q2, lam_k2]))
    return pl.pallas_call(
        functools.partial(_diff_attn_kernel, blk=blk, dh=dh, lambda_init=lambda_init),
        grid=(bn, heads, nb),
        in_specs=[
            pl.BlockSpec((None, blk, 2 * dh), lambda b, h, i: (b, i, h)),
            pl.BlockSpec((None, None, nb, 2 * dh, blk), lambda b, h, i: (b, h, 0, 0, 0)),
            pl.BlockSpec((None, s, 4 * dh), lambda b, h, i: (b, 0, h)),
            pl.BlockSpec((8, LANES), lambda b, h, i: (0, 0)),
            pl.BlockSpec((1, 2 * dh), lambda b, h, i: (0, 0)),
        ],
        out_specs=pl.BlockSpec((None, blk, 2 * dh), lambda b, h, i: (b, i, h)),
        out_shape=jax.ShapeDtypeStruct((bn, s, hd), BF16),
        scratch_shapes=[pltpu.VMEM((2, blk, LANES), F32),
                        pltpu.VMEM((2, blk, 4 * dh), F32), pltpu.VMEM((2, blk, 2 * dh), BF16),
                        pltpu.VMEM((2, blk, blk), F32), pltpu.VMEM((2, blk, blk), F32)],
        compiler_params=_params(3),
        name="diff_attn",
    )(q, kt, v_ext, lamv, subln_g[None, :])


def _attn_out_kernel(o_ref, x_ref, w_ref, ln_g_ref, ln_b_ref, wr_hi_ref, wr_lo_ref, lstrict_ref,
                     y_ref, route_ref, cnt_ref, base_ref):
    @pl.when(pl.program_id(0) == 0)
    def _():
        base_ref[...] = jnp.zeros_like(base_ref)

    m = _dot(o_ref[...], w_ref[...])
    y = _layer_norm(DEEPNORM_ALPHA * x_ref[...] + m, ln_g_ref[...], ln_b_ref[...])
    y_ref[...] = y
    y_hi = y.astype(BF16)
    y_lo = (y - y_hi.astype(F32)).astype(BF16)
    logits = (_dot(y_hi, wr_hi_ref[...]) + _dot(y_lo, wr_hi_ref[...]) + _dot(y_hi, wr_lo_ref[...]))

    lane = lax.broadcasted_iota(jnp.int32, logits.shape, 1).astype(F32)
    neg = -jnp.inf
    lg = jnp.where(lane < N_EXPERTS, logits, neg)
    v0 = jnp.max(lg, axis=-1, keepdims=True)
    i0 = jnp.min(jnp.where(lg == v0, lane, float(LANES)), axis=-1, keepdims=True)
    lg2 = jnp.where(lane == i0, neg, lg)
    v1 = jnp.max(lg2, axis=-1, keepdims=True)
    i1 = jnp.min(jnp.where(lg2 == v1, lane, float(LANES)), axis=-1, keepdims=True)
    e = jnp.exp(v1 - v0)
    g0 = 1.0 / (1.0 + e)
    g1 = e / (1.0 + e)
    oh0 = lane == i0
    oh1 = lane == i1
    c = jnp.where(oh0, 1.0, 0.0) + jnp.where(oh1, 1.0, 0.0)
    pre = _dot(lstrict_ref[...], c.astype(BF16)) + base_ref[...]
    r0 = jnp.sum(jnp.where(oh0, pre, 0.0), axis=-1, keepdims=True)
    r1 = jnp.sum(jnp.where(oh1, pre, 0.0), axis=-1, keepdims=True)
    base_ref[...] = base_ref[...] + jnp.sum(c, axis=0, keepdims=True)
    cnt_ref[...] = jnp.broadcast_to(base_ref[...], cnt_ref.shape)
    fields = (i0, i1, r0, r1, g0, g1)
    route = jnp.zeros_like(logits)
    for idx, val in enumerate(fields):
        route = jnp.where(lane == idx, val, route)
    route_ref[...] = route


def _attn_out(o2d, x2d, w_out, ln_g, ln_b, w_router, *, tile=512):
    n, d = x2d.shape
    wr = jnp.pad(w_router, ((0, 0), (0, LANES - w_router.shape[1])))
    wr_hi = wr.astype(BF16)
    wr_lo = (wr - wr_hi.astype(F32)).astype(BF16)
    lstrict = (jnp.arange(tile)[:, None] > jnp.arange(tile)[None, :]).astype(BF16)
    row = lambda t: t[None, :]
    tspec = pl.BlockSpec((tile, d), lambda i: (i, 0))
    return pl.pallas_call(
        _attn_out_kernel,
        grid=(n // tile,),
        in_specs=[tspec, tspec, _const_spec((d, d)), _const_spec((1, d)), _const_spec((1, d)),
                  _const_spec((d, LANES)), _const_spec((d, LANES)), _const_spec((tile, tile))],
        out_specs=[tspec, pl.BlockSpec((tile, LANES), lambda i: (i, 0)),
                   pl.BlockSpec((8, LANES), lambda i: (0, 0))],
        out_shape=[jax.ShapeDtypeStruct((n, d), F32), jax.ShapeDtypeStruct((n, LANES), F32),
                   jax.ShapeDtypeStruct((8, LANES), F32)],
        scratch_shapes=[pltpu.VMEM((1, LANES), F32)],
        compiler_params=_params(1),
        name="attn_out_router",
    )(o2d, x2d, w_out.astype(BF16), row(ln_g), row(ln_b), wr_hi, wr_lo, lstrict)


def _route_tables(route, cnt, *, tm):
    n = route.shape[0]
    e01 = route[:, 0:2].astype(jnp.int32)
    r01 = route[:, 2:4].astype(jnp.int32)
    counts = cnt[0, :N_EXPERTS].astype(jnp.int32)
    padded = (counts + tm - 1) // tm * tm
    pend = jnp.cumsum(padded)
    pstart = pend - padded
    start_of = jnp.sum(jnp.where(e01[:, :, None] == jnp.arange(N_EXPERTS)[None, None, :],
                                 pstart[None, None, :], 0), axis=-1)
    dest = start_of + r01
    n_rows = n * TOP_K + N_EXPERTS * tm
    n_blocks = n_rows // tm
    n_used = pend[-1] // tm
    blk_idx = jnp.arange(n_blocks, dtype=jnp.int32)
    blk_start = jnp.minimum(blk_idx, n_used - 1) * tm
    blk_e = jnp.minimum(jnp.sum(blk_start[:, None] >= pend[None, :], axis=1), N_EXPERTS - 1)
    valid = jnp.clip(pstart[blk_e] + counts[blk_e] - blk_idx * tm, 0, tm)
    return (dest, blk_e.astype(jnp.int32), n_used.reshape(1).astype(jnp.int32),
            valid.astype(jnp.int32), n_rows)


SC_CORES = 2
SC_SUBCORES = 16
SC_WORKERS = SC_CORES * SC_SUBCORES
SC_ROWS = 32


def _sc_mesh():
    return plsc.VectorSubcoreMesh(core_axis_name="c", subcore_axis_name="s",
                                  num_cores=SC_CORES, num_subcores=SC_SUBCORES)


def _sc_worker_id():
    return lax.axis_index("s") * SC_CORES + lax.axis_index("c")


def _sc_dispatch(x2d, dest, n_rows):
    n, d = x2d.shape
    per_w = n // SC_WORKERS
    n_chunks = per_w // SC_ROWS
    assert n == SC_WORKERS * n_chunks * SC_ROWS and n_chunks % 2 == 0
    d0 = dest[:, 0].reshape(SC_WORKERS, n_chunks, SC_ROWS)
    d1 = dest[:, 1].reshape(SC_WORKERS, n_chunks, SC_ROWS)

    def body(x_hbm, d0_hbm, d1_hbm, xs_hbm, d0_v, d1_v, rows_v, rsem, s0sem, s1sem):
        wid = _sc_worker_id()
        base = wid * per_w
        pltpu.sync_copy(d0_hbm.at[wid], d0_v)
        pltpu.sync_copy(d1_hbm.at[wid], d1_v)

        def read(c, slot):
            return pltpu.make_async_copy(x_hbm.at[pl.ds(base + c * SC_ROWS, SC_ROWS)],
                                         rows_v.at[slot], rsem.at[slot])

        def scat(idx_v, sem, c, slot):
            return pltpu.make_async_copy(rows_v.at[slot], xs_hbm.at[idx_v.at[c]], sem.at[slot])

        def start_scatters(c, slot):
            scat(d0_v, s0sem, c, slot).start()
            scat(d1_v, s1sem, c, slot).start()

        def wait_scatters(c, slot):
            scat(d0_v, s0sem, c, slot).wait()
            scat(d1_v, s1sem, c, slot).wait()

        read(0, 0).start()

        def pair(j, carry):
            c0 = 2 * j
            read(c0, 0).wait()

            @pl.when(j > 0)
            def _():
                wait_scatters(c0 - 1, 1)

            read(c0 + 1, 1).start()
            start_scatters(c0, 0)
            read(c0 + 1, 1).wait()
            wait_scatters(c0, 0)

            @pl.when(c0 + 2 < n_chunks)
            def _():
                read(c0 + 2, 0).start()

            start_scatters(c0 + 1, 1)
            return carry

        lax.fori_loop(0, n_chunks // 2, pair, 0)
        wait_scatters(n_chunks - 1, 1)

    return pl.kernel(
        body,
        out_type=jax.ShapeDtypeStruct((n_rows, d), x2d.dtype),
        mesh=_sc_mesh(),
        scratch_types=[pltpu.VMEM((n_chunks, SC_ROWS), jnp.int32), pltpu.VMEM((n_chunks, SC_ROWS), jnp.int32),
                       pltpu.VMEM((2, SC_ROWS, d), x2d.dtype), pltpu.SemaphoreType.DMA((2,)),
                       pltpu.SemaphoreType.DMA((2,)), pltpu.SemaphoreType.DMA((2,))],
        name="sc_dispatch",
    )(x2d, d0, d1)


def _sc_gather(table, idx):
    _, d = table.shape
    b = idx.shape[0]
    per_w = b // SC_WORKERS
    n_chunks = per_w // SC_ROWS
    assert b == SC_WORKERS * n_chunks * SC_ROWS and n_chunks % 2 == 0
    idx3 = idx.reshape(SC_WORKERS, n_chunks, SC_ROWS)

    def body(table_hbm, idx_hbm, out_hbm, idx_v, rows_v, gsem, wsem):
        wid = _sc_worker_id()
        base = wid * per_w
        pltpu.sync_copy(idx_hbm.at[wid], idx_v)

        def gather(c, slot):
            return pltpu.make_async_copy(table_hbm.at[idx_v.at[c]], rows_v.at[slot], gsem.at[slot])

        def write(c, slot):
            return pltpu.make_async_copy(rows_v.at[slot],
                                         out_hbm.at[pl.ds(base + c * SC_ROWS, SC_ROWS)], wsem.at[slot])

        gather(0, 0).start()

        def pair(j, carry):
            c0 = 2 * j
            gather(c0, 0).wait()

            @pl.when(j > 0)
            def _():
                write(c0 - 1, 1).wait()

            gather(c0 + 1, 1).start()
            write(c0, 0).start()
            gather(c0 + 1, 1).wait()
            write(c0, 0).wait()

            @pl.when(c0 + 2 < n_chunks)
            def _():
                gather(c0 + 2, 0).start()

            write(c0 + 1, 1).start()
            return carry

        lax.fori_loop(0, n_chunks // 2, pair, 0)
        write(n_chunks - 1, 1).wait()

    return pl.kernel(
        body,
        out_type=jax.ShapeDtypeStruct((b, d), table.dtype),
        mesh=_sc_mesh(),
        scratch_types=[pltpu.VMEM((n_chunks, SC_ROWS), jnp.int32), pltpu.VMEM((2, SC_ROWS, d), table.dtype),
                       pltpu.SemaphoreType.DMA((2,)), pltpu.SemaphoreType.DMA((2,))],
        name="sc_combine_gather",
    )(table, idx3)


def _moe_kernel(blk_e, n_used, valid, xs_ref, wg_ref, wu_ref, wd_ref, y_ref, *, chunks):
    m = pl.program_id(0)
    f = pl.program_id(1)

    @pl.when(f == 0)
    def _():
        y_ref[...] = jnp.zeros_like(y_ref)

    @pl.when(m < n_used[0])
    def _():
        row = lax.broadcasted_iota(jnp.int32, xs_ref.shape, 0)
        x = jnp.where(row < valid[m], xs_ref[...], 0.0).astype(BF16)
        for lo, hi in chunks:
            g = _dot(x, wg_ref[:, lo:hi])
            u = _dot(x, wu_ref[:, lo:hi])
            h = (g * _sigmoid(g) * u).astype(BF16)
            y_ref[...] += _dot(h, wd_ref[lo:hi, :])


def _moe_grouped(xs, blk_e, n_used, valid, wg, wu, wd, *, tm, tf, fchunk=1024):
    n_rows, d = xs.shape
    fdim = wg.shape[2]
    nf = fdim // tf
    assert nf * tf == fdim
    n_blocks = n_rows // tm
    chunks = tuple((lo, min(lo + fchunk, tf)) for lo in range(0, tf, fchunk))

    def f_eff(m, f, nu):
        mm = jnp.minimum(m, nu[0] - 1)
        snake = jnp.where(mm % 2 == 0, f, nf - 1 - f)
        return jnp.where(m < nu[0], snake, jnp.where(mm % 2 == 0, nf - 1, 0))

    grid_spec = pltpu.PrefetchScalarGridSpec(
        num_scalar_prefetch=3,
        grid=(n_blocks, nf),
        in_specs=[
            pl.BlockSpec((tm, d), lambda m, f, be, nu, va: (jnp.minimum(m, nu[0] - 1), 0)),
            pl.BlockSpec((None, d, tf), lambda m, f, be, nu, va: (be[m], 0, f_eff(m, f, nu))),
            pl.BlockSpec((None, d, tf), lambda m, f, be, nu, va: (be[m], 0, f_eff(m, f, nu))),
            pl.BlockSpec((None, tf, d), lambda m, f, be, nu, va: (be[m], f_eff(m, f, nu), 0)),
        ],
        out_specs=pl.BlockSpec((tm, d), lambda m, f, be, nu, va: (m, 0)),
    )
    return pl.pallas_call(
        functools.partial(_moe_kernel, chunks=chunks),
        grid_spec=grid_spec,
        out_shape=jax.ShapeDtypeStruct((n_rows, d), F32),
        compiler_params=_params(2),
        name="moe_grouped",
    )(blk_e, n_used, valid, xs, wg, wu, wd)


def _final_kernel(x_ref, y0_ref, y1_ref, route_ref, p_ref, ln_g_ref, ln_b_ref, wpg_ref, bpg_ref,
                  wpp_ref, o_ref):
    route = route_ref[...]
    f = route[:, 4:5] * y0_ref[...] + route[:, 5:6] * y1_ref[...]
    y = _layer_norm(DEEPNORM_ALPHA * x_ref[...] + f, ln_g_ref[...], ln_b_ref[...])
    o_ref[...] = _ple(y, p_ref[...], wpg_ref, bpg_ref, wpp_ref)


def _final(x2d, y01, route, p2d, ln_g, ln_b, wpg, bpg, wpp, *, tile=512):
    n, d = x2d.shape
    pd = p2d.shape[1]
    row = lambda t: t[None, :]
    tspec = pl.BlockSpec((tile, d), lambda i: (i, 0))
    second = pl.BlockSpec((tile, d), lambda i: (i + n // tile, 0))
    y0 = y1 = y01
    return pl.pallas_call(
        _final_kernel,
        grid=(n // tile,),
        in_specs=[tspec, tspec, second, pl.BlockSpec((tile, LANES), lambda i: (i, 0)),
                  pl.BlockSpec((tile, pd), lambda i: (i, 0)),
                  _const_spec((1, d)), _const_spec((1, d)),
                  _const_spec((d, d)), _const_spec((1, d)), _const_spec((pd, d))],
        out_specs=tspec,
        out_shape=jax.ShapeDtypeStruct((n, d), F32),
        compiler_params=_params(1),
        name="moe_combine_ple",
    )(x2d, y0, y1, route, p2d, row(ln_g), row(ln_b), wpg.astype(BF16), row(bpg), wpp.astype(BF16))


def _even_layer(x, p_i, w_in, a_ln_g, a_ln_b, a_ws, a_bs, w_gate_up, b_gate, norm_g, w_out,
                ln1_g, ln1_b, wg, wu, wd, ln2_g, ln2_b, wpp, wpg, bpg):
    bn, s, d = x.shape
    x1 = _even_mixer(x, w_in, a_ln_g, a_ln_b, a_ws, a_bs, w_gate_up, b_gate, norm_g, w_out, ln1_g, ln1_b)
    x2 = _ffn_ple(x1.reshape(bn * s, d), p_i.reshape(bn * s, -1), wg, wu, wd, ln2_g, ln2_b, wpg, bpg, wpp)
    return x2.reshape(bn, s, d)


def _odd_layer(x, p_i, layer_idx, w_qkv, lam_q1, lam_k1, lam_q2, lam_k2, subln_g, w_out, ln1_g, ln1_b,
               w_router, ewg, ewu, ewd, ln2_g, ln2_b, wpp, wpg, bpg, *, attn_blk=1024, moe_tm=512,
               moe_tf=1792):
    bn, s, d = x.shape
    n = bn * s
    lambda_init = 0.8 - 0.6 * math.exp(-0.3 * layer_idx)
    q, kt, v_ext = _qkv(x, w_qkv, dh=lam_q1.shape[0], tile=attn_blk)
    o = _diff_attn(q, kt, v_ext, lam_q1, lam_k1, lam_q2, lam_k2, subln_g, lambda_init, blk=attn_blk)
    x3, route, cnt = _attn_out(o.reshape(n, d), x.reshape(n, d), w_out, ln1_g, ln1_b, w_router)
    dest, blk_e, n_used, valid, n_rows = _route_tables(route, cnt, tm=moe_tm)
    xs = _sc_dispatch(x3, dest, n_rows)
    y = _moe_grouped(xs, blk_e, n_used, valid, ewg.astype(BF16), ewu.astype(BF16), ewd.astype(BF16),
                     tm=moe_tm, tf=moe_tf)
    y01 = _sc_gather(y, jnp.concatenate([dest[:, 0], dest[:, 1]]))
    out = _final(x3, y01, route, p_i.reshape(n, -1), ln2_g, ln2_b, wpg, bpg, wpp)
    return out.reshape(bn, s, d)


def kernel(x, p, e_w_in, e_a_ln_g, e_a_ln_b, e_a_ws, e_a_bs, e_b_w_gate_up, e_b_b_gate, e_b_norm_g, e_w_out, e_ln1_g, e_ln1_b, e_ffn_wg, e_ffn_wu, e_ffn_wd, e_ln2_g, e_ln2_b, o_w_qkv, o_lam_q1, o_lam_k1, o_lam_q2, o_lam_k2, o_subln_g, o_w_out, o_ln1_g, o_ln1_b, o_router, o_exp_wg, o_exp_wu, o_exp_wd, o_ln2_g, o_ln2_b, ple_w_proj, ple_w_gate, ple_b_gate):
    for i in range(DEPTH):
        j = i // 2
        if i % 2 == 0:
            x = _even_layer(x, p[i], e_w_in[j], e_a_ln_g[j], e_a_ln_b[j], e_a_ws[j], e_a_bs[j],
                            e_b_w_gate_up[j], e_b_b_gate[j], e_b_norm_g[j], e_w_out[j],
                            e_ln1_g[j], e_ln1_b[j], e_ffn_wg[j], e_ffn_wu[j], e_ffn_wd[j],
                            e_ln2_g[j], e_ln2_b[j], ple_w_proj[i], ple_w_gate[i], ple_b_gate[i])
        else:
            x = _odd_layer(x, p[i], i, o_w_qkv[j], o_lam_q1[j], o_lam_k1[j], o_lam_q2[j], o_lam_k2[j],
                           o_subln_g[j], o_w_out[j], o_ln1_g[j], o_ln1_b[j], o_router[j],
                           o_exp_wg[j], o_exp_wu[j], o_exp_wd[j], o_ln2_g[j], o_ln2_b[j],
                           ple_w_proj[i], ple_w_gate[i], ple_b_gate[i])
    return x
```

```python
import functools
import math

import jax
import jax.numpy as jnp
from jax import lax
from jax.experimental import pallas as pl
from jax.experimental.pallas import tpu as pltpu
from jax.experimental.pallas import tpu_sc as plsc

F32 = jnp.float32
BF16 = jnp.bfloat16

DEPTH = 2
DEEPNORM_ALPHA = (2.0 * DEPTH) ** 0.25
LN_EPS = 1e-5
A_CHUNK = 128
A_GROUPS = 8
B_HEADS = 4
B_CHUNK = 64
B_TAU = 16.0
CS_ROWS = 256
C_HEADS = 8
N_EXPERTS = 8
TOP_K = 2
LANES = 128
LOG2_E = 1.4426950408889634
VMEM_LIMIT = 56 * 1024 * 1024

NT_DIMS = (((1,), (1,)), ((), ()))
TN_DIMS = (((0,), (0,)), ((), ()))


def _dot(a, b):
    return jnp.dot(a, b, preferred_element_type=F32)


def _layer_norm(x, g, b):
    mu = jnp.mean(x, axis=-1, keepdims=True)
    xc = x - mu
    var = jnp.mean(xc * xc, axis=-1, keepdims=True)
    return xc * lax.rsqrt(var + LN_EPS) * g + b


def _sigmoid(x):
    return 1.0 / (1.0 + jnp.exp(-x))


def _split3(a):
    hi = a.astype(BF16)
    r1 = a - hi.astype(F32)
    mid = r1.astype(BF16)
    lo = (r1 - mid.astype(F32)).astype(BF16)
    return hi, mid, lo


def _const_spec(shape):
    zeros = (0,) * len(shape)
    return pl.BlockSpec(shape, lambda *_: zeros, pipeline_mode=pl.Buffered(1))


def _params(n_axes):
    return pltpu.CompilerParams(dimension_semantics=("arbitrary",) * n_axes,
                                vmem_limit_bytes=VMEM_LIMIT)


def _even_mixer_kernel(x_ref, w_in_ref, a_g_ref, a_b_ref, wcat_ref, abias_ref, mstack_ref,
                       w_up_ref, b_gate_ref, ng_ref, w_out_ref, ln_g_ref, ln_b_ref,
                       o_ref, state_ref, *, tile, aw, dkh, dvh):
    @pl.when(pl.program_id(1) == 0)
    def _():
        state_ref[...] = jnp.zeros_like(state_ref)

    hk = B_HEADS * dkh
    hv = B_HEADS * dvh
    x = x_ref[...]
    z = _dot(x.astype(BF16), w_in_ref[...])

    u = jax.nn.gelu(z[:, 0:aw])
    v = _layer_norm(jax.nn.gelu(z[:, aw:2 * aw]), a_g_ref[...], a_b_ref[...])
    gd = aw // A_GROUPS
    rows = lax.broadcasted_iota(jnp.int32, (A_CHUNK, A_GROUPS * A_CHUNK), 0)
    cols = lax.broadcasted_iota(jnp.int32, (A_CHUNK, A_GROUPS * A_CHUNK), 1)
    wcat = jnp.where((cols % A_CHUNK) <= rows, wcat_ref[...], 0.0).astype(BF16)
    r_bd = lax.broadcasted_iota(jnp.int32, (A_GROUPS * A_CHUNK, aw), 0) // A_CHUNK
    c_bd = lax.broadcasted_iota(jnp.int32, (A_GROUPS * A_CHUNK, aw), 1) // gd
    mask_bd = r_bd == c_bd
    ya_parts = []
    for c in range(tile // A_CHUNK):
        sl = slice(c * A_CHUNK, (c + 1) * A_CHUNK)
        v_rep = jnp.concatenate([v[sl]] * A_GROUPS, axis=0)
        v_bd = jnp.where(mask_bd, v_rep, 0.0).astype(BF16)
        sg = _dot(wcat, v_bd) + abias_ref[...]
        ya_parts.append(u[sl] * sg)
    ya = jnp.concatenate(ya_parts, axis=0)

    o0 = 2 * aw
    q = z[:, o0:o0 + hk] * (dkh ** -0.5)
    k = z[:, o0 + hk:o0 + 2 * hk]
    vv = z[:, o0 + 2 * hk:o0 + 2 * hk + hv]
    r = z[:, o0 + 2 * hk + hv:o0 + 2 * hk + 2 * hv]
    g_low = z[:, o0 + 2 * hk + 2 * hv:]
    pre = _dot(g_low.astype(BF16), w_up_ref[...]) + b_gate_ref[...]
    log_a = (jnp.minimum(pre, 0.0) - jnp.log1p(jnp.exp(-jnp.abs(pre)))) * (1.0 / B_TAU)
    la_hi, la_mid, la_lo = _split3(log_a)
    ms = mstack_ref[...]
    b_parts, mid_parts, last_parts = [], [], []
    for t in range(tile // CS_ROWS):
        rs = slice(t * CS_ROWS, (t + 1) * CS_ROWS)
        cs = _dot(ms, la_hi[rs]) + _dot(ms, la_mid[rs]) + _dot(ms, la_lo[rs])
        b_parts.append(cs[0:CS_ROWS])
        mid_parts.append(cs[CS_ROWS:2 * CS_ROWS])
        last_parts.append(cs[2 * CS_ROWS:3 * CS_ROWS])
    b_cum = jnp.concatenate(b_parts, axis=0)
    d_mid = jnp.concatenate(mid_parts, axis=0)
    d_last = jnp.concatenate(last_parts, axis=0)
    qe = (q * jnp.exp(d_mid)).astype(BF16)
    ke = k * jnp.exp(-d_mid)
    kd = (k * jnp.exp(d_last)).astype(BF16)
    qb = (q * jnp.exp(b_cum)).astype(BF16)
    dec = jnp.exp(b_cum + d_last)
    vvb = vv.astype(BF16)

    mask_kk = (lax.broadcasted_iota(jnp.int32, (hk, hk), 0) // dkh
               == lax.broadcasted_iota(jnp.int32, (hk, hk), 1) // dkh)
    mask_vbd = (lax.broadcasted_iota(jnp.int32, (hk, hv), 0) // dkh
                == lax.broadcasted_iota(jnp.int32, (hk, hv), 1) // dvh)
    mask_st = (lax.broadcasted_iota(jnp.int32, (hv, hk), 0) // dvh
               == lax.broadcasted_iota(jnp.int32, (hv, hk), 1) // dkh)
    causal = ((lax.broadcasted_iota(jnp.int32, (B_CHUNK, hk), 1) % B_CHUNK)
              <= lax.broadcasted_iota(jnp.int32, (B_CHUNK, hk), 0))

    st = state_ref[...]
    o_parts = []
    for c in range(tile // B_CHUNK):
        sl = slice(c * B_CHUNK, (c + 1) * B_CHUNK)
        ke_bd = jnp.where(mask_kk, jnp.concatenate([ke[sl]] * B_HEADS, axis=0), 0.0).astype(BF16)
        s_cat = lax.dot_general(qe[sl], ke_bd, NT_DIMS, preferred_element_type=F32)
        s_cat = jnp.where(causal, s_cat, 0.0).astype(BF16)
        v_bd = jnp.where(mask_vbd, jnp.concatenate([vv[sl]] * B_HEADS, axis=0), 0.0).astype(BF16)
        o_c = _dot(s_cat, v_bd) + lax.dot_general(qb[sl], st.astype(BF16), NT_DIMS,
                                                   preferred_element_type=F32)
        kv_t = lax.dot_general(vvb[sl], kd[sl], TN_DIMS, preferred_element_type=F32)
        st = dec[c * B_CHUNK:c * B_CHUNK + 1] * st + jnp.where(mask_st, kv_t, 0.0)
        o_parts.append(o_c)
    state_ref[...] = st
    o = jnp.concatenate(o_parts, axis=0)

    yb_parts = []
    for h in range(B_HEADS):
        oh = o[:, h * dvh:(h + 1) * dvh]
        msq = jnp.mean(oh * oh, axis=-1, keepdims=True)
        yb_parts.append(oh * lax.rsqrt(msq + LN_EPS))
    yb = jnp.concatenate(yb_parts, axis=1) * ng_ref[...] * (r * _sigmoid(r))

    y_cat = jnp.concatenate([ya, yb], axis=1).astype(BF16)
    m = _dot(y_cat, w_out_ref[...])
    o_ref[...] = _layer_norm(DEEPNORM_ALPHA * x + m, ln_g_ref[...], ln_b_ref[...])


def _gla_cumsum_matrices(tile):
    i = jnp.arange(tile)[:, None]
    j = jnp.arange(tile)[None, :]
    same = (i // B_CHUNK) == (j // B_CHUNK)
    m_cum = same & (j <= i)
    m_mid = same & (j <= (i // B_CHUNK) * B_CHUNK + B_CHUNK // 2 - 1)
    m_last = same
    f = lambda t: t.astype(F32)
    return jnp.concatenate([f(m_cum), f(m_cum) - f(m_mid), f(m_last) - f(m_cum)], axis=0).astype(BF16)


def _even_mixer(x, w_in, a_ln_g, a_ln_b, a_ws, a_bs, w_gate_up, b_gate, norm_g, w_out, ln_g, ln_b,
                *, tile=1024):
    bn, s, d = x.shape
    aw = a_ln_g.shape[0]
    hk = w_gate_up.shape[1]
    dkh = hk // B_HEADS
    dvh = norm_g.shape[0]
    hv = B_HEADS * dvh
    rank = w_gate_up.shape[0]
    main = 2 * aw + 2 * hk + 2 * hv
    gd = aw // A_GROUPS
    w_in_p = jnp.concatenate([w_in[:, :main], jnp.pad(w_in[:, main:], ((0, 0), (0, LANES - rank)))],
                             axis=1).astype(BF16)
    w_up_p = jnp.pad(w_gate_up, ((0, LANES - rank), (0, 0))).astype(BF16)
    wcat = jnp.transpose(a_ws, (1, 0, 2)).reshape(A_CHUNK, A_GROUPS * A_CHUNK)
    abias = jnp.repeat(a_bs.T, gd, axis=1)
    mstack = _gla_cumsum_matrices(CS_ROWS)
    ng = jnp.tile(norm_g, B_HEADS)[None, :]
    row = lambda t: t[None, :]
    kern = functools.partial(_even_mixer_kernel, tile=tile, aw=aw, dkh=dkh, dvh=dvh)
    tile_spec = pl.BlockSpec((None, tile, d), lambda b, i: (b, i, 0))
    return pl.pallas_call(
        kern,
        grid=(bn, s // tile),
        in_specs=[tile_spec, _const_spec(w_in_p.shape), _const_spec((1, aw)), _const_spec((1, aw)),
                  _const_spec(wcat.shape), _const_spec(abias.shape), _const_spec(mstack.shape),
                  _const_spec(w_up_p.shape), _const_spec((1, hk)), _const_spec((1, hv)),
                  _const_spec(w_out.shape), _const_spec((1, d)), _const_spec((1, d))],
        out_specs=tile_spec,
        out_shape=jax.ShapeDtypeStruct(x.shape, F32),
        scratch_shapes=[pltpu.VMEM((hv, hk), F32)],
        compiler_params=_params(2),
        name="even_mixer",
    )(x, w_in_p, row(a_ln_g), row(a_ln_b), wcat, abias, mstack, w_up_p, row(b_gate), ng,
      w_out.astype(BF16), row(ln_g), row(ln_b))


def _ple(y, p, wpg_ref, bpg_ref, wpp_ref):
    gate = _sigmoid(_dot(y.astype(BF16), wpg_ref[...]) + bpg_ref[...])
    return y + gate * _dot(p.astype(BF16), wpp_ref[...])


def _ffn_ple_kernel(x_ref, p_ref, wg_ref, wu_ref, wd_ref, ln_g_ref, ln_b_ref, wpg_ref, bpg_ref,
                    wpp_ref, o_ref, *, chunks):
    x = x_ref[...]
    xb = x.astype(BF16)
    acc = None
    for lo, hi in chunks:
        g = _dot(xb, wg_ref[:, lo:hi])
        u = _dot(xb, wu_ref[:, lo:hi])
        h = (g * _sigmoid(g) * u).astype(BF16)
        part = _dot(h, wd_ref[lo:hi, :])
        acc = part if acc is None else acc + part
    y = _layer_norm(DEEPNORM_ALPHA * x + acc, ln_g_ref[...], ln_b_ref[...])
    o_ref[...] = _ple(y, p_ref[...], wpg_ref, bpg_ref, wpp_ref)


def _ffn_ple(x2d, p2d, wg, wu, wd, ln_g, ln_b, wpg, bpg, wpp, *, tile=1024, fchunk=1024):
    n, d = x2d.shape
    f = wg.shape[1]
    pd = p2d.shape[1]
    chunks = tuple((lo, min(lo + fchunk, f)) for lo in range(0, f, fchunk))
    row = lambda t: t[None, :]
    return pl.pallas_call(
        functools.partial(_ffn_ple_kernel, chunks=chunks),
        grid=(n // tile,),
        in_specs=[pl.BlockSpec((tile, d), lambda i: (i, 0)), pl.BlockSpec((tile, pd), lambda i: (i, 0)),
                  _const_spec((d, f)), _const_spec((d, f)), _const_spec((f, d)),
                  _const_spec((1, d)), _const_spec((1, d)),
                  _const_spec((d, d)), _const_spec((1, d)), _const_spec((pd, d))],
        out_specs=pl.BlockSpec((tile, d), lambda i: (i, 0)),
        out_shape=jax.ShapeDtypeStruct((n, d), F32),
        compiler_params=_params(1),
        name="ffn_ple",
    )(x2d, p2d, wg.astype(BF16), wu.astype(BF16), wd.astype(BF16), row(ln_g), row(ln_b),
      wpg.astype(BF16), row(bpg), wpp.astype(BF16))


def _qkv_kernel(x_ref, w_ref, q_ref, kt_ref, v_ref, *, hd, scale):
    z = _dot(x_ref[...].astype(BF16), w_ref[...])
    q_ref[...] = (z[:, 0:hd] * scale).astype(BF16)
    k_t = z[:, hd:2 * hd].T.astype(BF16)
    heads, dk2, tile = kt_ref.shape
    ones = jnp.ones((tile, dk2), BF16)
    for h in range(heads):
        kt_ref[h] = k_t[h * dk2:(h + 1) * dk2, :]
        v_ref[:, 2 * h * dk2:(2 * h + 1) * dk2] = z[:, 2 * hd + h * dk2:2 * hd + (h + 1) * dk2].astype(BF16)
        v_ref[:, (2 * h + 1) * dk2:(2 * h + 2) * dk2] = ones


def _qkv(x, w_qkv, *, dh, tile):
    bn, s, d = x.shape
    hd = w_qkv.shape[1] // 3
    heads = hd // (2 * dh)
    return pl.pallas_call(
        functools.partial(_qkv_kernel, hd=hd, scale=dh ** -0.5 * LOG2_E),
        grid=(bn, s // tile),
        in_specs=[pl.BlockSpec((None, tile, d), lambda b, i: (b, i, 0)), _const_spec(w_qkv.shape)],
        out_specs=[pl.BlockSpec((None, tile, hd), lambda b, i: (b, i, 0)),
                   pl.BlockSpec((None, heads, None, 2 * dh, tile), lambda b, i: (b, 0, i, 0, 0)),
                   pl.BlockSpec((None, tile, 2 * hd), lambda b, i: (b, i, 0))],
        out_shape=[jax.ShapeDtypeStruct((bn, s, hd), BF16),
                   jax.ShapeDtypeStruct((bn, heads, s // tile, 2 * dh, tile), BF16),
                   jax.ShapeDtypeStruct((bn, s, 2 * hd), BF16)],
        compiler_params=_params(2),
        name="qkv_proj",
    )(x, w_qkv.astype(BF16))


def _diff_attn_kernel(q_ref, kt_ref, v_ref, lam_ref, g_ref, o_ref, m_ref, acc_ref,
                      qm_ref, sa_ref, sb_ref, *, blk, dh, lambda_init):
    qi = pl.program_id(2)
    m_ref[...] = jnp.full_like(m_ref, -1e30)
    acc_ref[...] = jnp.zeros_like(acc_ref)

    q = q_ref[...]
    lane = lax.broadcasted_iota(jnp.int32, q.shape, 1)
    zero = jnp.zeros_like(q)
    qm_ref[0] = jnp.where(lane < dh, q, zero)
    qm_ref[1] = jnp.where(lane >= dh, q, zero)

    def scores(ki, s_ref):
        kt_blk = kt_ref[ki]
        for c in range(2):
            s_ref[c] = _dot(qm_ref[c], kt_blk)

    def update(c, rows, s, v_part):
        m_prev = m_ref[c, rows]
        m_new = jnp.maximum(m_prev, jnp.max(s, axis=-1, keepdims=True))
        corr = jnp.exp2(m_prev - m_new)
        p = jnp.exp2(s - jnp.concatenate([m_new] * (s.shape[1] // LANES), axis=1))
        acc_ref[c, rows] = (jnp.concatenate([corr, corr], axis=1) * acc_ref[c, rows]
                            + _dot(p.astype(BF16), v_part))
        m_ref[c, rows] = m_new

    def softmax_pv(s_ref, ki, masked):
        off = pl.multiple_of(ki * blk, blk)
        v_blk = v_ref[pl.ds(off, blk), :]
        half = blk // 2
        for c in range(2):
            if not masked:
                update(c, slice(0, blk), s_ref[c], v_blk)
                continue
            s_top = s_ref[c, 0:half, 0:half]
            row = lax.broadcasted_iota(jnp.int32, s_top.shape, 0)
            col = lax.broadcasted_iota(jnp.int32, s_top.shape, 1)
            update(c, slice(0, half), jnp.where(col <= row, s_top, -1e30), v_blk[0:half])
            s_bot = s_ref[c, half:blk, :]
            row = lax.broadcasted_iota(jnp.int32, s_bot.shape, 0) + half
            col = lax.broadcasted_iota(jnp.int32, s_bot.shape, 1)
            update(c, slice(half, blk), jnp.where(col <= row, s_bot, -1e30), v_blk)

    scores(0, sa_ref)

    def pair(j, carry):
        k0 = 2 * j
        scores(k0 + 1, sb_ref)
        softmax_pv(sa_ref, k0, False)
        scores(k0 + 2, sa_ref)
        softmax_pv(sb_ref, k0 + 1, False)
        return carry

    lax.fori_loop(0, qi // 2, pair, 0)

    @pl.when(qi % 2 == 0)
    def _():
        softmax_pv(sa_ref, qi, True)

    @pl.when(qi % 2 == 1)
    def _():
        scores(qi, sb_ref)
        softmax_pv(sa_ref, qi - 1, False)
        softmax_pv(sb_ref, qi, True)

    a1 = acc_ref[0, :, 0:2 * dh] * (1.0 / acc_ref[0, :, 2 * dh:4 * dh])
    a2 = acc_ref[1, :, 0:2 * dh] * (1.0 / acc_ref[1, :, 2 * dh:4 * dh])
    lv = lam_ref[...]
    lam = (jnp.exp(jnp.sum(lv[0:1] * lv[1:2], axis=-1, keepdims=True))
           - jnp.exp(jnp.sum(lv[2:3] * lv[3:4], axis=-1, keepdims=True)) + lambda_init)
    o = a1 - lam * a2
    msq = jnp.mean(o * o, axis=-1, keepdims=True)
    o_ref[...] = (o * lax.rsqrt(msq + LN_EPS) * g_ref[...] * (1.0 - lambda_init)).astype(BF16)


def _diff_attn(q, kt, v_ext, lam_q1, lam_k1, lam_q2, lam_k2, subln_g, lambda_init, *, blk):
    bn, s, hd = q.shape
    dh = lam_q1.shape[0]
    heads = hd // (2 * dh)
    nb = s // blk
    lamv = jnp.zeros((8, LANES), F32).at[0:4, 0:dh].set(jnp.stack([lam_q1, lam_k1, lam_q2, lam_k2]))
    return pl.pallas_call(
        functools.partial(_diff_attn_kernel, blk=blk, dh=dh, lambda_init=lambda_init),
        grid=(bn, heads, nb),
        in_specs=[
            pl.BlockSpec((None, blk, 2 * dh), lambda b, h, i: (b, i, h)),
            pl.BlockSpec((None, None, nb, 2 * dh, blk), lambda b, h, i: (b, h, 0, 0, 0)),
            pl.BlockSpec((None, s, 4 * dh), lambda b, h, i: (b, 0, h)),
            pl.BlockSpec((8, LANES), lambda b, h, i: (0, 0)),
            pl.BlockSpec((1, 2 * dh), lambda b, h, i: (0, 0)),
        ],
        out_specs=pl.BlockSpec((None, blk, 2 * dh), lambda b, h, i: (b, i, h)),
        out_shape=jax.ShapeDtypeStruct((bn, s, hd), BF16),
        scratch_shapes=[pltpu.VMEM((2, blk, LANES), F32),
                        pltpu.VMEM((2, blk, 4 * dh), F32), pltpu.VMEM((2, blk, 2 * dh), BF16),
                        pltpu.VMEM((2, blk, blk), F32), pltpu.VMEM((2, blk, blk), F32)],
        compiler_params=_params(3),
        name="diff_attn",
    )(q, kt, v_ext, lamv, subln_g[None, :])


def _attn_out_kernel(o_ref, x_ref, w_ref, ln_g_ref, ln_b_ref, wr_ref, lstrict_ref,
                     y_ref, route_ref, cnt_ref, base_ref):
    @pl.when(pl.program_id(0) == 0)
    def _():
        base_ref[...] = jnp.zeros_like(base_ref)

    m = _dot(o_ref[...], w_ref[...])
    y = _layer_norm(DEEPNORM_ALPHA * x_ref[...] + m, ln_g_ref[...], ln_b_ref[...])
    y_ref[...] = y
    y_hi = y.astype(BF16)
    y_lo = (y - y_hi.astype(F32)).astype(BF16)
    hi_terms = _dot(y_hi, wr_ref[...])
    logits = hi_terms[:, 0:LANES] + hi_terms[:, LANES:2 * LANES] + _dot(y_lo, wr_ref[:, 0:LANES])

    lane = lax.broadcasted_iota(jnp.int32, logits.shape, 1).astype(F32)
    neg = -jnp.inf
    lg = jnp.where(lane < N_EXPERTS, logits, neg)
    v0 = jnp.max(lg, axis=-1, keepdims=True)
    i0 = jnp.min(jnp.where(lg == v0, lane, float(LANES)), axis=-1, keepdims=True)
    lg2 = jnp.where(lane == i0, neg, lg)
    v1 = jnp.max(lg2, axis=-1, keepdims=True)
    i1 = jnp.min(jnp.where(lg2 == v1, lane, float(LANES)), axis=-1, keepdims=True)
    e = jnp.exp(v1 - v0)
    g0 = 1.0 / (1.0 + e)
    g1 = e / (1.0 + e)
    oh0 = lane == i0
    oh1 = lane == i1
    c = jnp.where(oh0, 1.0, 0.0) + jnp.where(oh1, 1.0, 0.0)
    pre = _dot(lstrict_ref[...], c.astype(BF16)) + base_ref[...]
    r0 = jnp.sum(jnp.where(oh0, pre, 0.0), axis=-1, keepdims=True)
    r1 = jnp.sum(jnp.where(oh1, pre, 0.0), axis=-1, keepdims=True)
    base_ref[...] = base_ref[...] + jnp.sum(c, axis=0, keepdims=True)
    cnt_ref[...] = jnp.broadcast_to(base_ref[...], cnt_ref.shape)
    fields = (i0, i1, r0, r1, g0, g1)
    route = jnp.zeros_like(logits)
    for idx, val in enumerate(fields):
        route = jnp.where(lane == idx, val, route)
    route_ref[...] = route


def _attn_out(o2d, x2d, w_out, ln_g, ln_b, w_router, *, tile=1024):
    n, d = x2d.shape
    wr = jnp.pad(w_router, ((0, 0), (0, LANES - w_router.shape[1])))
    wr_hi = wr.astype(BF16)
    wr_lo = (wr - wr_hi.astype(F32)).astype(BF16)
    wr_split = jnp.concatenate([wr_hi, wr_lo], axis=1)
    lstrict = (jnp.arange(tile)[:, None] > jnp.arange(tile)[None, :]).astype(BF16)
    row = lambda t: t[None, :]
    tspec = pl.BlockSpec((tile, d), lambda i: (i, 0))
    return pl.pallas_call(
        _attn_out_kernel,
        grid=(n // tile,),
        in_specs=[tspec, tspec, _const_spec((d, d)), _const_spec((1, d)), _const_spec((1, d)),
                  _const_spec((d, 2 * LANES)), _const_spec((tile, tile))],
        out_specs=[tspec, pl.BlockSpec((tile, LANES), lambda i: (i, 0)),
                   pl.BlockSpec((8, LANES), lambda i: (0, 0))],
        out_shape=[jax.ShapeDtypeStruct((n, d), F32), jax.ShapeDtypeStruct((n, LANES), F32),
                   jax.ShapeDtypeStruct((8, LANES), F32)],
        scratch_shapes=[pltpu.VMEM((1, LANES), F32)],
        compiler_params=_params(1),
        name="attn_out_router",
    )(o2d, x2d, w_out.astype(BF16), row(ln_g), row(ln_b), wr_split, lstrict)


def _route_tables(route, cnt, *, tm):
    n = route.shape[0]
    e01 = route[:, 0:2].astype(jnp.int32)
    r01 = route[:, 2:4].astype(jnp.int32)
    counts = cnt[0, :N_EXPERTS].astype(jnp.int32)
    padded = (counts + tm - 1) // tm * tm
    pend = jnp.cumsum(padded)
    pstart = pend - padded
    start_of = jnp.sum(jnp.where(e01[:, :, None] == jnp.arange(N_EXPERTS)[None, None, :],
                                 pstart[None, None, :], 0), axis=-1)
    dest = start_of + r01
    n_rows = n * TOP_K + N_EXPERTS * tm
    n_blocks = n_rows // tm
    n_used = pend[-1] // tm
    blk_idx = jnp.arange(n_blocks, dtype=jnp.int32)
    blk_start = jnp.minimum(blk_idx, n_used - 1) * tm
    blk_e = jnp.minimum(jnp.sum(blk_start[:, None] >= pend[None, :], axis=1), N_EXPERTS - 1)
    valid = jnp.clip(pstart[blk_e] + counts[blk_e] - blk_idx * tm, 0, tm)
    return (dest, blk_e.astype(jnp.int32), n_used.reshape(1).astype(jnp.int32),
            valid.astype(jnp.int32), n_rows)


SC_CORES = 2
SC_SUBCORES = 16
SC_WORKERS = SC_CORES * SC_SUBCORES
SC_ROWS = 32


def _sc_mesh():
    return plsc.VectorSubcoreMesh(core_axis_name="c", subcore_axis_name="s",
                                  num_cores=SC_CORES, num_subcores=SC_SUBCORES)


def _sc_worker_id():
    return lax.axis_index("s") * SC_CORES + lax.axis_index("c")


def _sc_dispatch(x2d, dest, n_rows):
    n, d = x2d.shape
    per_w = n // SC_WORKERS
    n_chunks = per_w // SC_ROWS
    assert n == SC_WORKERS * n_chunks * SC_ROWS and n_chunks % 2 == 0
    d0 = dest[:, 0].reshape(SC_WORKERS, n_chunks, SC_ROWS)
    d1 = dest[:, 1].reshape(SC_WORKERS, n_chunks, SC_ROWS)

    def body(x_hbm, d0_hbm, d1_hbm, xs_hbm, d0_v, d1_v, rows_v, rsem, s0sem, s1sem):
        wid = _sc_worker_id()
        base = wid * per_w
        pltpu.sync_copy(d0_hbm.at[wid], d0_v)
        pltpu.sync_copy(d1_hbm.at[wid], d1_v)

        def read(c, slot):
            return pltpu.make_async_copy(x_hbm.at[pl.ds(base + c * SC_ROWS, SC_ROWS)],
                                         rows_v.at[slot], rsem.at[slot])

        def scat(idx_v, sem, c, slot):
            return pltpu.make_async_copy(rows_v.at[slot], xs_hbm.at[idx_v.at[c]], sem.at[slot])

        def start_scatters(c, slot):
            scat(d0_v, s0sem, c, slot).start()
            scat(d1_v, s1sem, c, slot).start()

        def wait_scatters(c, slot):
            scat(d0_v, s0sem, c, slot).wait()
            scat(d1_v, s1sem, c, slot).wait()

        read(0, 0).start()

        def pair(j, carry):
            c0 = 2 * j
            read(c0, 0).wait()

            @pl.when(j > 0)
            def _():
                wait_scatters(c0 - 1, 1)

            read(c0 + 1, 1).start()
            start_scatters(c0, 0)
            read(c0 + 1, 1).wait()
            wait_scatters(c0, 0)

            @pl.when(c0 + 2 < n_chunks)
            def _():
                read(c0 + 2, 0).start()

            start_scatters(c0 + 1, 1)
            return carry

        lax.fori_loop(0, n_chunks // 2, pair, 0)
        wait_scatters(n_chunks - 1, 1)

    return pl.kernel(
        body,
        out_type=jax.ShapeDtypeStruct((n_rows, d), x2d.dtype),
        mesh=_sc_mesh(),
        scratch_types=[pltpu.VMEM((n_chunks, SC_ROWS), jnp.int32), pltpu.VMEM((n_chunks, SC_ROWS), jnp.int32),
                       pltpu.VMEM((2, SC_ROWS, d), x2d.dtype), pltpu.SemaphoreType.DMA((2,)),
                       pltpu.SemaphoreType.DMA((2,)), pltpu.SemaphoreType.DMA((2,))],
        name="sc_dispatch",
    )(x2d, d0, d1)


def _sc_gather(table, idx):
    _, d = table.shape
    b = idx.shape[0]
    per_w = b // SC_WORKERS
    n_chunks = per_w // SC_ROWS
    assert b == SC_WORKERS * n_chunks * SC_ROWS and n_chunks % 2 == 0
    idx3 = idx.reshape(SC_WORKERS, n_chunks, SC_ROWS)

    def body(table_hbm, idx_hbm, out_hbm, idx_v, rows_v, gsem, wsem):
        wid = _sc_worker_id()
        base = wid * per_w
        pltpu.sync_copy(idx_hbm.at[wid], idx_v)

        def gather(c, slot):
            return pltpu.make_async_copy(table_hbm.at[idx_v.at[c]], rows_v.at[slot], gsem.at[slot])

        def write(c, slot):
            return pltpu.make_async_copy(rows_v.at[slot],
                                         out_hbm.at[pl.ds(base + c * SC_ROWS, SC_ROWS)], wsem.at[slot])

        gather(0, 0).start()

        def pair(j, carry):
            c0 = 2 * j
            gather(c0, 0).wait()

            @pl.when(j > 0)
            def _():
                write(c0 - 1, 1).wait()

            gather(c0 + 1, 1).start()
            write(c0, 0).start()
            gather(c0 + 1, 1).wait()
            write(c0, 0).wait()

            @pl.when(c0 + 2 < n_chunks)
            def _():
                gather(c0 + 2, 0).start()

            write(c0 + 1, 1).start()
            return carry

        lax.fori_loop(0, n_chunks // 2, pair, 0)
        write(n_chunks - 1, 1).wait()

    return pl.kernel(
        body,
        out_type=jax.ShapeDtypeStruct((b, d), table.dtype),
        mesh=_sc_mesh(),
        scratch_types=[pltpu.VMEM((n_chunks, SC_ROWS), jnp.int32), pltpu.VMEM((2, SC_ROWS, d), table.dtype),
                       pltpu.SemaphoreType.DMA((2,)), pltpu.SemaphoreType.DMA((2,))],
        name="sc_combine_gather",
    )(table, idx3)


def _moe_kernel(blk_e, n_used, valid, xs_ref, wg_ref, wu_ref, wd_ref, y_ref, *, chunks):
    m = pl.program_id(0)
    f = pl.program_id(1)

    @pl.when(f == 0)
    def _():
        y_ref[...] = jnp.zeros_like(y_ref)

    @pl.when(m < n_used[0])
    def _():
        row = lax.broadcasted_iota(jnp.int32, xs_ref.shape, 0)
        x = jnp.where(row < valid[m], xs_ref[...], 0.0).astype(BF16)
        for lo, hi in chunks:
            g = _dot(x, wg_ref[:, lo:hi])
            u = _dot(x, wu_ref[:, lo:hi])
            h = (g * _sigmoid(g) * u).astype(BF16)
            y_ref[...] += _dot(h, wd_ref[lo:hi, :])


def _moe_grouped(xs, blk_e, n_used, valid, wg, wu, wd, *, tm, tf, fchunk=1024):
    n_rows, d = xs.shape
    fdim = wg.shape[2]
    nf = fdim // tf
    assert nf * tf == fdim
    n_blocks = n_rows // tm
    chunks = tuple((lo, min(lo + fchunk, tf)) for lo in range(0, tf, fchunk))

    def f_eff(m, f, nu):
        mm = jnp.minimum(m, nu[0] - 1)
        snake = jnp.where(mm % 2 == 0, f, nf - 1 - f)
        return jnp.where(m < nu[0], snake, jnp.where(mm % 2 == 0, nf - 1, 0))

    grid_spec = pltpu.PrefetchScalarGridSpec(
        num_scalar_prefetch=3,
        grid=(n_blocks, nf),
        in_specs=[
            pl.BlockSpec((tm, d), lambda m, f, be, nu, va: (jnp.minimum(m, nu[0] - 1), 0)),
            pl.BlockSpec((None, d, tf), lambda m, f, be, nu, va: (be[m], 0, f_eff(m, f, nu))),
            pl.BlockSpec((None, d, tf), lambda m, f, be, nu, va: (be[m], 0, f_eff(m, f, nu))),
            pl.BlockSpec((None, tf, d), lambda m, f, be, nu, va: (be[m], f_eff(m, f, nu), 0)),
        ],
        out_specs=pl.BlockSpec((tm, d), lambda m, f, be, nu, va: (m, 0)),
    )
    return pl.pallas_call(
        functools.partial(_moe_kernel, chunks=chunks),
        grid_spec=grid_spec,
        out_shape=jax.ShapeDtypeStruct((n_rows, d), F32),
        compiler_params=_params(2),
        name="moe_grouped",
    )(blk_e, n_used, valid, xs, wg, wu, wd)


def _final_kernel(x_ref, y0_ref, y1_ref, route_ref, p_ref, ln_g_ref, ln_b_ref, wpg_ref, bpg_ref,
                  wpp_ref, *rest):
    o_ref = rest[-1]
    route = route_ref[...]
    f = route[:, 4:5] * y0_ref[...] + route[:, 5:6] * y1_ref[...]
    y = _layer_norm(DEEPNORM_ALPHA * x_ref[...] + f, ln_g_ref[...], ln_b_ref[...])
    o_ref[...] = _ple(y, p_ref[...], wpg_ref, bpg_ref, wpp_ref)


def _final_part(x2d, y01, route, p2d, ln_g, ln_b, wpg, bpg, wpp, *, part, n_parts, prev, tile):
    n, d = x2d.shape
    pd = p2d.shape[1]
    nblk = n // n_parts // tile
    off = part * nblk
    row = lambda t: t[None, :]
    tok = lambda width: pl.BlockSpec((tile, width), lambda i: (i + off, 0))
    in_specs = [tok(d), pl.BlockSpec((tile, d), lambda i: (i, 0)),
                pl.BlockSpec((tile, d), lambda i: (i + nblk, 0)), tok(LANES), tok(pd),
                _const_spec((1, d)), _const_spec((1, d)),
                _const_spec((d, d)), _const_spec((1, d)), _const_spec((pd, d))]
    args = [x2d, y01, y01, route, p2d, row(ln_g), row(ln_b), wpg.astype(BF16), row(bpg), wpp.astype(BF16)]
    aliases = {}
    if prev is not None:
        in_specs.append(pl.BlockSpec(memory_space=pl.ANY))
        args.append(prev)
        aliases = {len(args) - 1: 0}
    return pl.pallas_call(
        _final_kernel,
        grid=(nblk,),
        in_specs=in_specs,
        out_specs=tok(d),
        out_shape=jax.ShapeDtypeStruct((n, d), F32),
        input_output_aliases=aliases,
        compiler_params=_params(1),
        name="moe_combine_ple",
    )(*args)


def _combine(x2d, y, dest, route, p2d, ln_g, ln_b, wpg, bpg, wpp, *, n_parts=2, tile=1024):
    n = x2d.shape[0]
    step = n // n_parts
    tile = min(tile, step)
    out = None
    for part in range(n_parts):
        d_part = dest[part * step:(part + 1) * step]
        y01 = _sc_gather(y, jnp.concatenate([d_part[:, 0], d_part[:, 1]]))
        out = _final_part(x2d, y01, route, p2d, ln_g, ln_b, wpg, bpg, wpp,
                          part=part, n_parts=n_parts, prev=out, tile=tile)
    return out


def _even_layer(x, p_i, w_in, a_ln_g, a_ln_b, a_ws, a_bs, w_gate_up, b_gate, norm_g, w_out,
                ln1_g, ln1_b, wg, wu, wd, ln2_g, ln2_b, wpp, wpg, bpg):
    bn, s, d = x.shape
    x1 = _even_mixer(x, w_in, a_ln_g, a_ln_b, a_ws, a_bs, w_gate_up, b_gate, norm_g, w_out, ln1_g, ln1_b)
    x2 = _ffn_ple(x1.reshape(bn * s, d), p_i.reshape(bn * s, -1), wg, wu, wd, ln2_g, ln2_b, wpg, bpg, wpp)
    return x2.reshape(bn, s, d)


def _odd_layer(x, p_i, layer_idx, w_qkv, lam_q1, lam_k1, lam_q2, lam_k2, subln_g, w_out, ln1_g, ln1_b,
               w_router, ewg, ewu, ewd, ln2_g, ln2_b, wpp, wpg, bpg, *, attn_blk=1024, moe_tm=512,
               moe_tf=1792):
    bn, s, d = x.shape
    n = bn * s
    lambda_init = 0.8 - 0.6 * math.exp(-0.3 * layer_idx)
    q, kt, v_ext = _qkv(x, w_qkv, dh=lam_q1.shape[0], tile=attn_blk)
    o = _diff_attn(q, kt, v_ext, lam_q1, lam_k1, lam_q2, lam_k2, subln_g, lambda_init, blk=attn_blk)
    x3, route, cnt = _attn_out(o.reshape(n, d), x.reshape(n, d), w_out, ln1_g, ln1_b, w_router)
    dest, blk_e, n_used, valid, n_rows = _route_tables(route, cnt, tm=moe_tm)
    xs = _sc_dispatch(x3, dest, n_rows)
    y = _moe_grouped(xs, blk_e, n_used, valid, ewg.astype(BF16), ewu.astype(BF16), ewd.astype(BF16),
                     tm=moe_tm, tf=moe_tf)
    out = _combine(x3, y, dest, route, p_i.reshape(n, -1), ln2_g, ln2_b, wpg, bpg, wpp)
    return out.reshape(bn, s, d)


def kernel(x, p, e_w_in, e_a_ln_g, e_a_ln_b, e_a_ws, e_a_bs, e_b_w_gate_up, e_b_b_gate, e_b_norm_g, e_w_out, e_ln1_g, e_ln1_b, e_ffn_wg, e_ffn_wu, e_ffn_wd, e_ln2_g, e_ln2_b, o_w_qkv, o_lam_q1, o_lam_k1, o_lam_q2, o_lam_k2, o_subln_g, o_w_out, o_ln1_g, o_ln1_b, o_router, o_exp_wg, o_exp_wu, o_exp_wd, o_ln2_g, o_ln2_b, ple_w_proj, ple_w_gate, ple_b_gate):
    for i in range(DEPTH):
        j = i // 2
        if i % 2 == 0:
            x = _even_layer(x, p[i], e_w_in[j], e_a_ln_g[j], e_a_ln_b[j], e_a_ws[j], e_a_bs[j],
                            e_b_w_gate_up[j], e_b_b_gate[j], e_b_norm_g[j], e_w_out[j],
                            e_ln1_g[j], e_ln1_b[j], e_ffn_wg[j], e_ffn_wu[j], e_ffn_wd[j],
                            e_ln2_g[j], e_ln2_b[j], ple_w_proj[i], ple_w_gate[i], ple_b_gate[i])
        else:
            x = _odd_layer(x, p[i], i, o_w_qkv[j], o_lam_q1[j], o_lam_k1[j], o_lam_q2[j], o_lam_k2[j],
                           o_subln_g[j], o_w_out[j], o_ln1_g[j], o_ln1_b[j], o_router[j],
                           o_exp_wg[j], o_exp_wu[j], o_exp_wd[j], o_ln2_g[j], o_ln2_b[j],
                           ple_w_proj[i], ple_w_gate[i], ple_b_gate[i])
    return x
```

```python
import functools
import math

import jax
import jax.numpy as jnp
from jax import lax
from jax.experimental import pallas as pl
from jax.experimental.pallas import tpu as pltpu
from jax.experimental.pallas import tpu_sc as plsc

F32 = jnp.float32
BF16 = jnp.bfloat16

DEPTH = 2
DEEPNORM_ALPHA = (2.0 * DEPTH) ** 0.25
LN_EPS = 1e-5
A_CHUNK = 128
A_GROUPS = 8
B_HEADS = 4
B_CHUNK = 64
B_TAU = 16.0
CS_ROWS = 256
C_HEADS = 8
N_EXPERTS = 8
TOP_K = 2
LANES = 128
LOG2_E = 1.4426950408889634
VMEM_LIMIT = 56 * 1024 * 1024

NT_DIMS = (((1,), (1,)), ((), ()))
TN_DIMS = (((0,), (0,)), ((), ()))


def _dot(a, b):
    return jnp.dot(a, b, preferred_element_type=F32)


def _layer_norm(x, g, b):
    mu = jnp.mean(x, axis=-1, keepdims=True)
    xc = x - mu
    var = jnp.mean(xc * xc, axis=-1, keepdims=True)
    return xc * lax.rsqrt(var + LN_EPS) * g + b


def _sigmoid(x):
    return 1.0 / (1.0 + jnp.exp(-x))


def _split3(a):
    hi = a.astype(BF16)
    r1 = a - hi.astype(F32)
    mid = r1.astype(BF16)
    lo = (r1 - mid.astype(F32)).astype(BF16)
    return hi, mid, lo


def _const_spec(shape):
    zeros = (0,) * len(shape)
    return pl.BlockSpec(shape, lambda *_: zeros, pipeline_mode=pl.Buffered(1))


def _params(n_axes):
    return pltpu.CompilerParams(dimension_semantics=("arbitrary",) * n_axes,
                                vmem_limit_bytes=VMEM_LIMIT)


def _even_mixer_kernel(x_ref, w_in_ref, a_g_ref, a_b_ref, wcat_ref, abias_ref, mstack_ref,
                       w_up_ref, b_gate_ref, ng_ref, w_out_ref, ln_g_ref, ln_b_ref,
                       o_ref, state_ref, *, tile, aw, dkh, dvh):
    @pl.when(pl.program_id(1) == 0)
    def _():
        state_ref[...] = jnp.zeros_like(state_ref)

    hk = B_HEADS * dkh
    hv = B_HEADS * dvh
    x = x_ref[...]
    z = _dot(x.astype(BF16), w_in_ref[...])

    u = jax.nn.gelu(z[:, 0:aw])
    v = _layer_norm(jax.nn.gelu(z[:, aw:2 * aw]), a_g_ref[...], a_b_ref[...])
    gd = aw // A_GROUPS
    rows = lax.broadcasted_iota(jnp.int32, (A_CHUNK, A_GROUPS * A_CHUNK), 0)
    cols = lax.broadcasted_iota(jnp.int32, (A_CHUNK, A_GROUPS * A_CHUNK), 1)
    wcat = jnp.where((cols % A_CHUNK) <= rows, wcat_ref[...], 0.0).astype(BF16)
    r_bd = lax.broadcasted_iota(jnp.int32, (A_GROUPS * A_CHUNK, aw), 0) // A_CHUNK
    c_bd = lax.broadcasted_iota(jnp.int32, (A_GROUPS * A_CHUNK, aw), 1) // gd
    mask_bd = r_bd == c_bd
    ya_parts = []
    for c in range(tile // A_CHUNK):
        sl = slice(c * A_CHUNK, (c + 1) * A_CHUNK)
        v_rep = jnp.concatenate([v[sl]] * A_GROUPS, axis=0)
        v_bd = jnp.where(mask_bd, v_rep, 0.0).astype(BF16)
        sg = _dot(wcat, v_bd) + abias_ref[...]
        ya_parts.append(u[sl] * sg)
    ya = jnp.concatenate(ya_parts, axis=0)

    o0 = 2 * aw
    q = z[:, o0:o0 + hk] * (dkh ** -0.5)
    k = z[:, o0 + hk:o0 + 2 * hk]
    vv = z[:, o0 + 2 * hk:o0 + 2 * hk + hv]
    r = z[:, o0 + 2 * hk + hv:o0 + 2 * hk + 2 * hv]
    g_low = z[:, o0 + 2 * hk + 2 * hv:]
    pre = _dot(g_low.astype(BF16), w_up_ref[...]) + b_gate_ref[...]
    log_a = (jnp.minimum(pre, 0.0) - jnp.log1p(jnp.exp(-jnp.abs(pre)))) * (1.0 / B_TAU)
    la_hi, la_mid, la_lo = _split3(log_a)
    ms = mstack_ref[...]
    b_parts, mid_parts, last_parts = [], [], []
    for t in range(tile // CS_ROWS):
        rs = slice(t * CS_ROWS, (t + 1) * CS_ROWS)
        cs = _dot(ms, la_hi[rs]) + _dot(ms, la_mid[rs]) + _dot(ms, la_lo[rs])
        b_parts.append(cs[0:CS_ROWS])
        mid_parts.append(cs[CS_ROWS:2 * CS_ROWS])
        last_parts.append(cs[2 * CS_ROWS:3 * CS_ROWS])
    b_cum = jnp.concatenate(b_parts, axis=0)
    d_mid = jnp.concatenate(mid_parts, axis=0)
    d_last = jnp.concatenate(last_parts, axis=0)
    qe = (q * jnp.exp(d_mid)).astype(BF16)
    ke = k * jnp.exp(-d_mid)
    kd = (k * jnp.exp(d_last)).astype(BF16)
    qb = (q * jnp.exp(b_cum)).astype(BF16)
    dec = jnp.exp(b_cum + d_last)
    vvb = vv.astype(BF16)

    mask_kk = (lax.broadcasted_iota(jnp.int32, (hk, hk), 0) // dkh
               == lax.broadcasted_iota(jnp.int32, (hk, hk), 1) // dkh)
    mask_vbd = (lax.broadcasted_iota(jnp.int32, (hk, hv), 0) // dkh
                == lax.broadcasted_iota(jnp.int32, (hk, hv), 1) // dvh)
    mask_st = (lax.broadcasted_iota(jnp.int32, (hv, hk), 0) // dvh
               == lax.broadcasted_iota(jnp.int32, (hv, hk), 1) // dkh)
    causal = ((lax.broadcasted_iota(jnp.int32, (B_CHUNK, hk), 1) % B_CHUNK)
              <= lax.broadcasted_iota(jnp.int32, (B_CHUNK, hk), 0))

    st = state_ref[...]
    o_parts = []
    for c in range(tile // B_CHUNK):
        sl = slice(c * B_CHUNK, (c + 1) * B_CHUNK)
        ke_bd = jnp.where(mask_kk, jnp.concatenate([ke[sl]] * B_HEADS, axis=0), 0.0).astype(BF16)
        s_cat = lax.dot_general(qe[sl], ke_bd, NT_DIMS, preferred_element_type=F32)
        s_cat = jnp.where(causal, s_cat, 0.0).astype(BF16)
        v_bd = jnp.where(mask_vbd, jnp.concatenate([vv[sl]] * B_HEADS, axis=0), 0.0).astype(BF16)
        o_c = _dot(s_cat, v_bd) + lax.dot_general(qb[sl], st.astype(BF16), NT_DIMS,
                                                   preferred_element_type=F32)
        kv_t = lax.dot_general(vvb[sl], kd[sl], TN_DIMS, preferred_element_type=F32)
        st = dec[c * B_CHUNK:c * B_CHUNK + 1] * st + jnp.where(mask_st, kv_t, 0.0)
        o_parts.append(o_c)
    state_ref[...] = st
    o = jnp.concatenate(o_parts, axis=0)

    yb_parts = []
    for h in range(B_HEADS):
        oh = o[:, h * dvh:(h + 1) * dvh]
        msq = jnp.mean(oh * oh, axis=-1, keepdims=True)
        yb_parts.append(oh * lax.rsqrt(msq + LN_EPS))
    yb = jnp.concatenate(yb_parts, axis=1) * ng_ref[...] * (r * _sigmoid(r))

    y_cat = jnp.concatenate([ya, yb], axis=1).astype(BF16)
    m = _dot(y_cat, w_out_ref[...])
    o_ref[...] = _layer_norm(DEEPNORM_ALPHA * x + m, ln_g_ref[...], ln_b_ref[...])


def _gla_cumsum_matrices(tile):
    i = jnp.arange(tile)[:, None]
    j = jnp.arange(tile)[None, :]
    same = (i // B_CHUNK) == (j // B_CHUNK)
    m_cum = same & (j <= i)
    m_mid = same & (j <= (i // B_CHUNK) * B_CHUNK + B_CHUNK // 2 - 1)
    m_last = same
    f = lambda t: t.astype(F32)
    return jnp.concatenate([f(m_cum), f(m_cum) - f(m_mid), f(m_last) - f(m_cum)], axis=0).astype(BF16)


def _even_mixer(x, w_in, a_ln_g, a_ln_b, a_ws, a_bs, w_gate_up, b_gate, norm_g, w_out, ln_g, ln_b,
                *, tile=1024):
    bn, s, d = x.shape
    aw = a_ln_g.shape[0]
    hk = w_gate_up.shape[1]
    dkh = hk // B_HEADS
    dvh = norm_g.shape[0]
    hv = B_HEADS * dvh
    rank = w_gate_up.shape[0]
    main = 2 * aw + 2 * hk + 2 * hv
    gd = aw // A_GROUPS
    w_in_p = jnp.concatenate([w_in[:, :main], jnp.pad(w_in[:, main:], ((0, 0), (0, LANES - rank)))],
                             axis=1).astype(BF16)
    w_up_p = jnp.pad(w_gate_up, ((0, LANES - rank), (0, 0))).astype(BF16)
    wcat = jnp.transpose(a_ws, (1, 0, 2)).reshape(A_CHUNK, A_GROUPS * A_CHUNK)
    abias = jnp.repeat(a_bs.T, gd, axis=1)
    mstack = _gla_cumsum_matrices(CS_ROWS)
    ng = jnp.tile(norm_g, B_HEADS)[None, :]
    row = lambda t: t[None, :]
    kern = functools.partial(_even_mixer_kernel, tile=tile, aw=aw, dkh=dkh, dvh=dvh)
    tile_spec = pl.BlockSpec((None, tile, d), lambda b, i: (b, i, 0))
    return pl.pallas_call(
        kern,
        grid=(bn, s // tile),
        in_specs=[tile_spec, _const_spec(w_in_p.shape), _const_spec((1, aw)), _const_spec((1, aw)),
                  _const_spec(wcat.shape), _const_spec(abias.shape), _const_spec(mstack.shape),
                  _const_spec(w_up_p.shape), _const_spec((1, hk)), _const_spec((1, hv)),
                  _const_spec(w_out.shape), _const_spec((1, d)), _const_spec((1, d))],
        out_specs=tile_spec,
        out_shape=jax.ShapeDtypeStruct(x.shape, F32),
        scratch_shapes=[pltpu.VMEM((hv, hk), F32)],
        compiler_params=_params(2),
        name="even_mixer",
    )(x, w_in_p, row(a_ln_g), row(a_ln_b), wcat, abias, mstack, w_up_p, row(b_gate), ng,
      w_out.astype(BF16), row(ln_g), row(ln_b))


def _ple(y, p, wpg_ref, bpg_ref, wpp_ref):
    gate = _sigmoid(_dot(y.astype(BF16), wpg_ref[...]) + bpg_ref[...])
    return y + gate * _dot(p.astype(BF16), wpp_ref[...])


def _ffn_ple_kernel(x_ref, p_ref, wg_ref, wu_ref, wd_ref, ln_g_ref, ln_b_ref, wpg_ref, bpg_ref,
                    wpp_ref, o_ref, *, chunks):
    x = x_ref[...]
    xb = x.astype(BF16)
    acc = None
    for lo, hi in chunks:
        g = _dot(xb, wg_ref[:, lo:hi])
        u = _dot(xb, wu_ref[:, lo:hi])
        h = (g * _sigmoid(g) * u).astype(BF16)
        part = _dot(h, wd_ref[lo:hi, :])
        acc = part if acc is None else acc + part
    y = _layer_norm(DEEPNORM_ALPHA * x + acc, ln_g_ref[...], ln_b_ref[...])
    o_ref[...] = _ple(y, p_ref[...], wpg_ref, bpg_ref, wpp_ref)


def _ffn_ple(x2d, p_all, layer, wg, wu, wd, ln_g, ln_b, wpg, bpg, wpp, *, tile=512, fchunk=1024):
    n, d = x2d.shape
    f = wg.shape[1]
    pd = p_all.shape[2]
    chunks = tuple((lo, min(lo + fchunk, f)) for lo in range(0, f, fchunk))
    row = lambda t: t[None, :]
    return pl.pallas_call(
        functools.partial(_ffn_ple_kernel, chunks=chunks),
        grid=(n // tile,),
        in_specs=[pl.BlockSpec((tile, d), lambda i: (i, 0)),
                  pl.BlockSpec((None, tile, pd), lambda i: (layer, i, 0)),
                  _const_spec((d, f)), _const_spec((d, f)), _const_spec((f, d)),
                  _const_spec((1, d)), _const_spec((1, d)),
                  _const_spec((d, d)), _const_spec((1, d)), _const_spec((pd, d))],
        out_specs=pl.BlockSpec((tile, d), lambda i: (i, 0)),
        out_shape=jax.ShapeDtypeStruct((n, d), F32),
        compiler_params=_params(1),
        name="ffn_ple",
    )(x2d, p_all, wg.astype(BF16), wu.astype(BF16), wd.astype(BF16), row(ln_g), row(ln_b),
      wpg.astype(BF16), row(bpg), wpp.astype(BF16))


def _qkv_kernel(x_ref, w_ref, q_ref, kt_ref, v_ref, *, hd, scale):
    z = _dot(x_ref[...].astype(BF16), w_ref[...])
    q_ref[...] = (z[:, 0:hd] * scale).astype(BF16)
    k_t = z[:, hd:2 * hd].T.astype(BF16)
    heads, dk2, tile = kt_ref.shape
    ones = jnp.ones((tile, dk2), BF16)
    for h in range(heads):
        kt_ref[h] = k_t[h * dk2:(h + 1) * dk2, :]
        v_ref[:, 2 * h * dk2:(2 * h + 1) * dk2] = z[:, 2 * hd + h * dk2:2 * hd + (h + 1) * dk2].astype(BF16)
        v_ref[:, (2 * h + 1) * dk2:(2 * h + 2) * dk2] = ones


def _qkv(x, w_qkv, *, dh, tile):
    bn, s, d = x.shape
    hd = w_qkv.shape[1] // 3
    heads = hd // (2 * dh)
    return pl.pallas_call(
        functools.partial(_qkv_kernel, hd=hd, scale=dh ** -0.5 * LOG2_E),
        grid=(bn, s // tile),
        in_specs=[pl.BlockSpec((None, tile, d), lambda b, i: (b, i, 0)), _const_spec(w_qkv.shape)],
        out_specs=[pl.BlockSpec((None, tile, hd), lambda b, i: (b, i, 0)),
                   pl.BlockSpec((None, heads, None, 2 * dh, tile), lambda b, i: (b, 0, i, 0, 0)),
                   pl.BlockSpec((None, tile, 2 * hd), lambda b, i: (b, i, 0))],
        out_shape=[jax.ShapeDtypeStruct((bn, s, hd), BF16),
                   jax.ShapeDtypeStruct((bn, heads, s // tile, 2 * dh, tile), BF16),
                   jax.ShapeDtypeStruct((bn, s, 2 * hd), BF16)],
        compiler_params=_params(2),
        name="qkv_proj",
    )(x, w_qkv.astype(BF16))


def _diff_attn_kernel(q_ref, kt_ref, v_ref, lam_ref, g_ref, o_ref, m_ref, acc_ref,
                      qm_ref, sa_ref, sb_ref, *, blk, dh, lambda_init):
    qi = pl.program_id(2)
    m_ref[...] = jnp.full_like(m_ref, -1e30)
    acc_ref[...] = jnp.zeros_like(acc_ref)

    q = q_ref[...]
    lane = lax.broadcasted_iota(jnp.int32, q.shape, 1)
    zero = jnp.zeros_like(q)
    qm_ref[0] = jnp.where(lane < dh, q, zero)
    qm_ref[1] = jnp.where(lane >= dh, q, zero)

    def scores(ki, s_ref):
        kt_blk = kt_ref[ki]
        for c in range(2):
            s_ref[c] = _dot(qm_ref[c], kt_blk)

    def update(c, rows, s, v_part):
        m_prev = m_ref[c, rows]
        m_new = jnp.maximum(m_prev, jnp.max(s, axis=-1, keepdims=True))
        corr = jnp.exp2(m_prev - m_new)
        p = jnp.exp2(s - jnp.concatenate([m_new] * (s.shape[1] // LANES), axis=1))
        acc_ref[c, rows] = (jnp.concatenate([corr, corr], axis=1) * acc_ref[c, rows]
                            + _dot(p.astype(BF16), v_part))
        m_ref[c, rows] = m_new

    def softmax_pv(s_ref, ki, masked):
        off = pl.multiple_of(ki * blk, blk)
        v_blk = v_ref[pl.ds(off, blk), :]
        half = blk // 2
        for c in range(2):
            if not masked:
                update(c, slice(0, blk), s_ref[c], v_blk)
                continue
            s_top = s_ref[c, 0:half, 0:half]
            row = lax.broadcasted_iota(jnp.int32, s_top.shape, 0)
            col = lax.broadcasted_iota(jnp.int32, s_top.shape, 1)
            update(c, slice(0, half), jnp.where(col <= row, s_top, -1e30), v_blk[0:half])
            s_bot = s_ref[c, half:blk, :]
            row = lax.broadcasted_iota(jnp.int32, s_bot.shape, 0) + half
            col = lax.broadcasted_iota(jnp.int32, s_bot.shape, 1)
            update(c, slice(half, blk), jnp.where(col <= row, s_bot, -1e30), v_blk)

    scores(0, sa_ref)

    def pair(j, carry):
        k0 = 2 * j
        scores(k0 + 1, sb_ref)
        softmax_pv(sa_ref, k0, False)
        scores(k0 + 2, sa_ref)
        softmax_pv(sb_ref, k0 + 1, False)
        return carry

    lax.fori_loop(0, qi // 2, pair, 0)

    @pl.when(qi % 2 == 0)
    def _():
        softmax_pv(sa_ref, qi, True)

    @pl.when(qi % 2 == 1)
    def _():
        scores(qi, sb_ref)
        softmax_pv(sa_ref, qi - 1, False)
        softmax_pv(sb_ref, qi, True)

    a1 = acc_ref[0, :, 0:2 * dh] * (1.0 / acc_ref[0, :, 2 * dh:4 * dh])
    a2 = acc_ref[1, :, 0:2 * dh] * (1.0 / acc_ref[1, :, 2 * dh:4 * dh])
    lv = lam_ref[...]
    lam = (jnp.exp(jnp.sum(lv[0:1] * lv[1:2], axis=-1, keepdims=True))
           - jnp.exp(jnp.sum(lv[2:3] * lv[3:4], axis=-1, keepdims=True)) + lambda_init)
    o = a1 - lam * a2
    msq = jnp.mean(o * o, axis=-1, keepdims=True)
    o_ref[...] = (o * lax.rsqrt(msq + LN_EPS) * g_ref[...] * (1.0 - lambda_init)).astype(BF16)


def _diff_attn(q, kt, v_ext, lam_q1, lam_k1, lam_q2, lam_k2, subln_g, lambda_init, *, blk):
    bn, s, hd = q.shape
    dh = lam_q1.shape[0]
    heads = hd // (2 * dh)
    nb = s // blk
    lamv = jnp.zeros((8, LANES), F32).at[0:4, 0:dh].set(jnp.stack([lam_q1, lam_k1, lam_q2, lam_k2]))
    return pl.pallas_call(
        functools.partial(_diff_attn_kernel, blk=blk, dh=dh, lambda_init=lambda_init),
        grid=(bn, heads, nb),
        in_specs=[
            pl.BlockSpec((None, blk, 2 * dh), lambda b, h, i: (b, i, h)),
            pl.BlockSpec((None, None, nb, 2 * dh, blk), lambda b, h, i: (b, h, 0, 0, 0)),
            pl.BlockSpec((None, s, 4 * dh), lambda b, h, i: (b, 0, h)),
            pl.BlockSpec((8, LANES), lambda b, h, i: (0, 0)),
            pl.BlockSpec((1, 2 * dh), lambda b, h, i: (0, 0)),
        ],
        out_specs=pl.BlockSpec((None, blk, 2 * dh), lambda b, h, i: (b, i, h)),
        out_shape=jax.ShapeDtypeStruct((bn, s, hd), BF16),
        scratch_shapes=[pltpu.VMEM((2, blk, LANES), F32),
                        pltpu.VMEM((2, blk, 4 * dh), F32), pltpu.VMEM((2, blk, 2 * dh), BF16),
                        pltpu.VMEM((2, blk, blk), F32), pltpu.VMEM((2, blk, blk), F32)],
        compiler_params=_params(3),
        name="diff_attn",
    )(q, kt, v_ext, lamv, subln_g[None, :])


def _attn_out_kernel(o_ref, x_ref, w_ref, ln_g_ref, ln_b_ref, wr_ref, lstrict_ref,
                     y_ref, route_ref, cnt_ref, base_ref):
    @pl.when(pl.program_id(0) == 0)
    def _():
        base_ref[...] = jnp.zeros_like(base_ref)

    m = _dot(o_ref[...], w_ref[...])
    y = _layer_norm(DEEPNORM_ALPHA * x_ref[...] + m, ln_g_ref[...], ln_b_ref[...])
    y_ref[...] = y
    y_hi = y.astype(BF16)
    y_lo = (y - y_hi.astype(F32)).astype(BF16)
    hi_terms = _dot(y_hi, wr_ref[...])
    logits = hi_terms[:, 0:LANES] + hi_terms[:, LANES:2 * LANES] + _dot(y_lo, wr_ref[:, 0:LANES])

    lane = lax.broadcasted_iota(jnp.int32, logits.shape, 1).astype(F32)
    neg = -jnp.inf
    lg = jnp.where(lane < N_EXPERTS, logits, neg)
    v0 = jnp.max(lg, axis=-1, keepdims=True)
    i0 = jnp.min(jnp.where(lg == v0, lane, float(LANES)), axis=-1, keepdims=True)
    lg2 = jnp.where(lane == i0, neg, lg)
    v1 = jnp.max(lg2, axis=-1, keepdims=True)
    i1 = jnp.min(jnp.where(lg2 == v1, lane, float(LANES)), axis=-1, keepdims=True)
    e = jnp.exp(v1 - v0)
    g0 = 1.0 / (1.0 + e)
    g1 = e / (1.0 + e)
    oh0 = lane == i0
    oh1 = lane == i1
    c = jnp.where(oh0, 1.0, 0.0) + jnp.where(oh1, 1.0, 0.0)
    pre = _dot(lstrict_ref[...], c.astype(BF16)) + base_ref[...]
    r0 = jnp.sum(jnp.where(oh0, pre, 0.0), axis=-1, keepdims=True)
    r1 = jnp.sum(jnp.where(oh1, pre, 0.0), axis=-1, keepdims=True)
    base_ref[...] = base_ref[...] + jnp.sum(c, axis=0, keepdims=True)
    cnt_ref[...] = jnp.broadcast_to(base_ref[...], cnt_ref.shape)
    fields = (i0, i1, r0, r1, g0, g1)
    route = jnp.zeros_like(logits)
    for idx, val in enumerate(fields):
        route = jnp.where(lane == idx, val, route)
    route_ref[...] = route


def _attn_out(o2d, x2d, w_out, ln_g, ln_b, w_router, *, tile=1024):
    n, d = x2d.shape
    wr = jnp.pad(w_router, ((0, 0), (0, LANES - w_router.shape[1])))
    wr_hi = wr.astype(BF16)
    wr_lo = (wr - wr_hi.astype(F32)).astype(BF16)
    wr_split = jnp.concatenate([wr_hi, wr_lo], axis=1)
    lstrict = (jnp.arange(tile)[:, None] > jnp.arange(tile)[None, :]).astype(BF16)
    row = lambda t: t[None, :]
    tspec = pl.BlockSpec((tile, d), lambda i: (i, 0))
    return pl.pallas_call(
        _attn_out_kernel,
        grid=(n // tile,),
        in_specs=[tspec, tspec, _const_spec((d, d)), _const_spec((1, d)), _const_spec((1, d)),
                  _const_spec((d, 2 * LANES)), _const_spec((tile, tile))],
        out_specs=[tspec, pl.BlockSpec((tile, LANES), lambda i: (i, 0)),
                   pl.BlockSpec((8, LANES), lambda i: (0, 0))],
        out_shape=[jax.ShapeDtypeStruct((n, d), F32), jax.ShapeDtypeStruct((n, LANES), F32),
                   jax.ShapeDtypeStruct((8, LANES), F32)],
        scratch_shapes=[pltpu.VMEM((1, LANES), F32)],
        compiler_params=_params(1),
        name="attn_out_router",
    )(o2d, x2d, w_out.astype(BF16), row(ln_g), row(ln_b), wr_split, lstrict)


def _route_tables(route, cnt, *, tm):
    n = route.shape[0]
    counts = cnt[0, :N_EXPERTS].astype(jnp.int32)
    padded = (counts + tm - 1) // tm * tm
    pend = jnp.cumsum(padded)
    pstart = pend - padded

    def dest_of(k):
        e = route[:, k].astype(jnp.int32)
        start = jnp.sum(jnp.where(e[:, None] == jnp.arange(N_EXPERTS)[None, :], pstart[None, :], 0), axis=-1)
        return start + route[:, TOP_K + k].astype(jnp.int32)

    dest = (dest_of(0), dest_of(1))
    n_rows = n * TOP_K + N_EXPERTS * tm
    n_blocks = n_rows // tm
    n_used = pend[-1] // tm
    blk_idx = jnp.arange(n_blocks, dtype=jnp.int32)
    blk_start = jnp.minimum(blk_idx, n_used - 1) * tm
    blk_e = jnp.minimum(jnp.sum(blk_start[:, None] >= pend[None, :], axis=1), N_EXPERTS - 1)
    valid = jnp.clip(pstart[blk_e] + counts[blk_e] - blk_idx * tm, 0, tm)
    return (dest, blk_e.astype(jnp.int32), n_used.reshape(1).astype(jnp.int32),
            valid.astype(jnp.int32), n_rows)


SC_CORES = 2
SC_SUBCORES = 16
SC_WORKERS = SC_CORES * SC_SUBCORES
SC_ROWS = 32


def _sc_mesh():
    return plsc.VectorSubcoreMesh(core_axis_name="c", subcore_axis_name="s",
                                  num_cores=SC_CORES, num_subcores=SC_SUBCORES)


def _sc_worker_id():
    return lax.axis_index("s") * SC_CORES + lax.axis_index("c")


def _sc_dispatch(x2d, dest, n_rows):
    n, d = x2d.shape
    per_w = n // SC_WORKERS
    n_chunks = per_w // SC_ROWS
    assert n == SC_WORKERS * n_chunks * SC_ROWS and n_chunks % 2 == 0
    d0 = dest[0].reshape(SC_WORKERS, n_chunks, SC_ROWS)
    d1 = dest[1].reshape(SC_WORKERS, n_chunks, SC_ROWS)

    def body(x_hbm, d0_hbm, d1_hbm, xs_hbm, d0_v, d1_v, rows_v, rsem, s0sem, s1sem):
        wid = _sc_worker_id()
        base = wid * per_w
        pltpu.sync_copy(d0_hbm.at[wid], d0_v)
        pltpu.sync_copy(d1_hbm.at[wid], d1_v)

        def read(c, slot):
            return pltpu.make_async_copy(x_hbm.at[pl.ds(base + c * SC_ROWS, SC_ROWS)],
                                         rows_v.at[slot], rsem.at[slot])

        def scat(idx_v, sem, c, slot):
            return pltpu.make_async_copy(rows_v.at[slot], xs_hbm.at[idx_v.at[c]], sem.at[slot])

        def start_scatters(c, slot):
            scat(d0_v, s0sem, c, slot).start()
            scat(d1_v, s1sem, c, slot).start()

        def wait_scatters(c, slot):
            scat(d0_v, s0sem, c, slot).wait()
            scat(d1_v, s1sem, c, slot).wait()

        read(0, 0).start()

        def pair(j, carry):
            c0 = 2 * j
            read(c0, 0).wait()

            @pl.when(j > 0)
            def _():
                wait_scatters(c0 - 1, 1)

            read(c0 + 1, 1).start()
            start_scatters(c0, 0)
            read(c0 + 1, 1).wait()
            wait_scatters(c0, 0)

            @pl.when(c0 + 2 < n_chunks)
            def _():
                read(c0 + 2, 0).start()

            start_scatters(c0 + 1, 1)
            return carry

        lax.fori_loop(0, n_chunks // 2, pair, 0)
        wait_scatters(n_chunks - 1, 1)

    return pl.kernel(
        body,
        out_type=jax.ShapeDtypeStruct((n_rows, d), x2d.dtype),
        mesh=_sc_mesh(),
        scratch_types=[pltpu.VMEM((n_chunks, SC_ROWS), jnp.int32), pltpu.VMEM((n_chunks, SC_ROWS), jnp.int32),
                       pltpu.VMEM((2, SC_ROWS, d), x2d.dtype), pltpu.SemaphoreType.DMA((2,)),
                       pltpu.SemaphoreType.DMA((2,)), pltpu.SemaphoreType.DMA((2,))],
        name="sc_dispatch",
    )(x2d, d0, d1)


def _sc_gather(table, idx):
    _, d = table.shape
    b = idx.shape[0]
    per_w = b // SC_WORKERS
    n_chunks = per_w // SC_ROWS
    assert b == SC_WORKERS * n_chunks * SC_ROWS and n_chunks % 2 == 0
    idx3 = idx.reshape(SC_WORKERS, n_chunks, SC_ROWS)

    def body(table_hbm, idx_hbm, out_hbm, idx_v, rows_v, gsem, wsem):
        wid = _sc_worker_id()
        base = wid * per_w
        pltpu.sync_copy(idx_hbm.at[wid], idx_v)

        def gather(c, slot):
            return pltpu.make_async_copy(table_hbm.at[idx_v.at[c]], rows_v.at[slot], gsem.at[slot])

        def write(c, slot):
            return pltpu.make_async_copy(rows_v.at[slot],
                                         out_hbm.at[pl.ds(base + c * SC_ROWS, SC_ROWS)], wsem.at[slot])

        gather(0, 0).start()

        def pair(j, carry):
            c0 = 2 * j
            gather(c0, 0).wait()

            @pl.when(j > 0)
            def _():
                write(c0 - 1, 1).wait()

            gather(c0 + 1, 1).start()
            write(c0, 0).start()
            gather(c0 + 1, 1).wait()
            write(c0, 0).wait()

            @pl.when(c0 + 2 < n_chunks)
            def _():
                gather(c0 + 2, 0).start()

            write(c0 + 1, 1).start()
            return carry

        lax.fori_loop(0, n_chunks // 2, pair, 0)
        write(n_chunks - 1, 1).wait()

    return pl.kernel(
        body,
        out_type=jax.ShapeDtypeStruct((b, d), table.dtype),
        mesh=_sc_mesh(),
        scratch_types=[pltpu.VMEM((n_chunks, SC_ROWS), jnp.int32), pltpu.VMEM((2, SC_ROWS, d), table.dtype),
                       pltpu.SemaphoreType.DMA((2,)), pltpu.SemaphoreType.DMA((2,))],
        name="sc_combine_gather",
    )(table, idx3)


def _moe_kernel(blk_e, n_used, valid, xs_ref, wg_ref, wu_ref, wd_ref, y_ref, *, chunks):
    m = pl.program_id(0)
    f = pl.program_id(1)

    @pl.when(f == 0)
    def _():
        y_ref[...] = jnp.zeros_like(y_ref)

    @pl.when(m < n_used[0])
    def _():
        row = lax.broadcasted_iota(jnp.int32, xs_ref.shape, 0)
        x = jnp.where(row < valid[m], xs_ref[...], 0.0).astype(BF16)
        for lo, hi in chunks:
            g = _dot(x, wg_ref[:, lo:hi])
            u = _dot(x, wu_ref[:, lo:hi])
            h = (g * _sigmoid(g) * u).astype(BF16)
            y_ref[...] += _dot(h, wd_ref[lo:hi, :])


def _moe_grouped(xs, blk_e, n_used, valid, wg, wu, wd, *, tm, tf, fchunk=1024):
    n_rows, d = xs.shape
    fdim = wg.shape[2]
    nf = fdim // tf
    assert nf * tf == fdim
    n_blocks = n_rows // tm
    chunks = tuple((lo, min(lo + fchunk, tf)) for lo in range(0, tf, fchunk))

    def f_eff(m, f, nu):
        mm = jnp.minimum(m, nu[0] - 1)
        snake = jnp.where(mm % 2 == 0, f, nf - 1 - f)
        return jnp.where(m < nu[0], snake, jnp.where(mm % 2 == 0, nf - 1, 0))

    grid_spec = pltpu.PrefetchScalarGridSpec(
        num_scalar_prefetch=3,
        grid=(n_blocks, nf),
        in_specs=[
            pl.BlockSpec((tm, d), lambda m, f, be, nu, va: (jnp.minimum(m, nu[0] - 1), 0)),
            pl.BlockSpec((None, d, tf), lambda m, f, be, nu, va: (be[m], 0, f_eff(m, f, nu))),
            pl.BlockSpec((None, d, tf), lambda m, f, be, nu, va: (be[m], 0, f_eff(m, f, nu))),
            pl.BlockSpec((None, tf, d), lambda m, f, be, nu, va: (be[m], f_eff(m, f, nu), 0)),
        ],
        out_specs=pl.BlockSpec((tm, d), lambda m, f, be, nu, va: (m, 0)),
    )
    return pl.pallas_call(
        functools.partial(_moe_kernel, chunks=chunks),
        grid_spec=grid_spec,
        out_shape=jax.ShapeDtypeStruct((n_rows, d), F32),
        compiler_params=_params(2),
        name="moe_grouped",
    )(blk_e, n_used, valid, xs, wg, wu, wd)


def _final_kernel(x_ref, y0_ref, y1_ref, route_ref, p_ref, ln_g_ref, ln_b_ref, wpg_ref, bpg_ref,
                  wpp_ref, *rest):
    o_ref = rest[-1]
    route = route_ref[...]
    f = route[:, 4:5] * y0_ref[...] + route[:, 5:6] * y1_ref[...]
    y = _layer_norm(DEEPNORM_ALPHA * x_ref[...] + f, ln_g_ref[...], ln_b_ref[...])
    o_ref[...] = _ple(y, p_ref[...], wpg_ref, bpg_ref, wpp_ref)


def _final_part(x2d, y01, route, p_all, layer, ln_g, ln_b, wpg, bpg, wpp, *, part, n_parts, prev, tile):
    n, d = x2d.shape
    pd = p_all.shape[2]
    nblk = n // n_parts // tile
    off = part * nblk
    row = lambda t: t[None, :]
    tok = lambda width: pl.BlockSpec((tile, width), lambda i: (i + off, 0))
    in_specs = [tok(d), pl.BlockSpec((tile, d), lambda i: (i, 0)),
                pl.BlockSpec((tile, d), lambda i: (i + nblk, 0)), tok(LANES),
                pl.BlockSpec((None, tile, pd), lambda i: (layer, i + off, 0)),
                _const_spec((1, d)), _const_spec((1, d)),
                _const_spec((d, d)), _const_spec((1, d)), _const_spec((pd, d))]
    args = [x2d, y01, y01, route, p_all, row(ln_g), row(ln_b), wpg.astype(BF16), row(bpg), wpp.astype(BF16)]
    aliases = {}
    if prev is not None:
        in_specs.append(pl.BlockSpec(memory_space=pl.ANY))
        args.append(prev)
        aliases = {len(args) - 1: 0}
    return pl.pallas_call(
        _final_kernel,
        grid=(nblk,),
        in_specs=in_specs,
        out_specs=tok(d),
        out_shape=jax.ShapeDtypeStruct((n, d), F32),
        input_output_aliases=aliases,
        compiler_params=_params(1),
        name="moe_combine_ple",
    )(*args)


def _combine(x2d, y, dest, route, p_all, layer, ln_g, ln_b, wpg, bpg, wpp, *, n_parts=2, tile=1024):
    n = x2d.shape[0]
    step = n // n_parts
    tile = min(tile, step)
    out = None
    for part in range(n_parts):
        rows = slice(part * step, (part + 1) * step)
        y01 = _sc_gather(y, jnp.concatenate([dest[0][rows], dest[1][rows]]))
        out = _final_part(x2d, y01, route, p_all, layer, ln_g, ln_b, wpg, bpg, wpp,
                          part=part, n_parts=n_parts, prev=out, tile=tile)
    return out


def _even_layer(x, p_all, layer_idx, w_in, a_ln_g, a_ln_b, a_ws, a_bs, w_gate_up, b_gate, norm_g, w_out,
                ln1_g, ln1_b, wg, wu, wd, ln2_g, ln2_b, wpp, wpg, bpg):
    bn, s, d = x.shape
    x1 = _even_mixer(x, w_in, a_ln_g, a_ln_b, a_ws, a_bs, w_gate_up, b_gate, norm_g, w_out, ln1_g, ln1_b)
    x2 = _ffn_ple(x1.reshape(bn * s, d), p_all, layer_idx, wg, wu, wd, ln2_g, ln2_b, wpg, bpg, wpp)
    return x2.reshape(bn, s, d)


def _odd_layer(x, p_all, layer_idx, w_qkv, lam_q1, lam_k1, lam_q2, lam_k2, subln_g, w_out, ln1_g, ln1_b,
               w_router, ewg, ewu, ewd, ln2_g, ln2_b, wpp, wpg, bpg, *, attn_blk=1024, moe_tm=512,
               moe_tf=1792):
    bn, s, d = x.shape
    n = bn * s
    lambda_init = 0.8 - 0.6 * math.exp(-0.3 * layer_idx)
    q, kt, v_ext = _qkv(x, w_qkv, dh=lam_q1.shape[0], tile=attn_blk)
    o = _diff_attn(q, kt, v_ext, lam_q1, lam_k1, lam_q2, lam_k2, subln_g, lambda_init, blk=attn_blk)
    x3, route, cnt = _attn_out(o.reshape(n, d), x.reshape(n, d), w_out, ln1_g, ln1_b, w_router)
    dest, blk_e, n_used, valid, n_rows = _route_tables(route, cnt, tm=moe_tm)
    xs = _sc_dispatch(x3, dest, n_rows)
    y = _moe_grouped(xs, blk_e, n_used, valid, ewg.astype(BF16), ewu.astype(BF16), ewd.astype(BF16),
                     tm=moe_tm, tf=moe_tf)
    out = _combine(x3, y, dest, route, p_all, layer_idx, ln2_g, ln2_b, wpg, bpg, wpp)
    return out.reshape(bn, s, d)


def kernel(x, p, e_w_in, e_a_ln_g, e_a_ln_b, e_a_ws, e_a_bs, e_b_w_gate_up, e_b_b_gate, e_b_norm_g, e_w_out, e_ln1_g, e_ln1_b, e_ffn_wg, e_ffn_wu, e_ffn_wd, e_ln2_g, e_ln2_b, o_w_qkv, o_lam_q1, o_lam_k1, o_lam_q2, o_lam_k2, o_subln_g, o_w_out, o_ln1_g, o_ln1_b, o_router, o_exp_wg, o_exp_wu, o_exp_wd, o_ln2_g, o_ln2_b, ple_w_proj, ple_w_gate, ple_b_gate):
    p_all = p.reshape(p.shape[0], -1, p.shape[-1])
    for i in range(DEPTH):
        j = i // 2
        if i % 2 == 0:
            x = _even_layer(x, p_all, i, e_w_in[j], e_a_ln_g[j], e_a_ln_b[j], e_a_ws[j], e_a_bs[j],
                            e_b_w_gate_up[j], e_b_b_gate[j], e_b_norm_g[j], e_w_out[j],
                            e_ln1_g[j], e_ln1_b[j], e_ffn_wg[j], e_ffn_wu[j], e_ffn_wd[j],
                            e_ln2_g[j], e_ln2_b[j], ple_w_proj[i], ple_w_gate[i], ple_b_gate[i])
        else:
            x = _odd_layer(x, p_all, i, o_w_qkv[j], o_lam_q1[j], o_lam_k1[j], o_lam_q2[j], o_lam_k2[j],
                           o_subln_g[j], o_w_out[j], o_ln1_g[j], o_ln1_b[j], o_router[j],
                           o_exp_wg[j], o_exp_wu[j], o_exp_wd[j], o_ln2_g[j], o_ln2_b[j],
                           ple_w_proj[i], ple_w_gate[i], ple_b_gate[i])
    return x
```

```python
import functools
import math

import jax
import jax.numpy as jnp
from jax import lax
from jax.experimental import pallas as pl
from jax.experimental.pallas import tpu as pltpu
from jax.experimental.pallas import tpu_sc as plsc

F32 = jnp.float32
BF16 = jnp.bfloat16

DEPTH = 2
DEEPNORM_ALPHA = (2.0 * DEPTH) ** 0.25
LN_EPS = 1e-5
A_CHUNK = 128
A_GROUPS = 8
B_HEADS = 4
B_CHUNK = 64
B_TAU = 16.0
CS_ROWS = 256
C_HEADS = 8
N_EXPERTS = 8
TOP_K = 2
LANES = 128
LOG2_E = 1.4426950408889634
VMEM_LIMIT = 56 * 1024 * 1024

NT_DIMS = (((1,), (1,)), ((), ()))
TN_DIMS = (((0,), (0,)), ((), ()))


def _dot(a, b):
    return jnp.dot(a, b, preferred_element_type=F32)


def _layer_norm(x, g, b):
    mu = jnp.mean(x, axis=-1, keepdims=True)
    xc = x - mu
    var = jnp.mean(xc * xc, axis=-1, keepdims=True)
    return xc * lax.rsqrt(var + LN_EPS) * g + b


def _sigmoid(x):
    return 1.0 / (1.0 + jnp.exp(-x))


def _split3(a):
    hi = a.astype(BF16)
    r1 = a - hi.astype(F32)
    mid = r1.astype(BF16)
    lo = (r1 - mid.astype(F32)).astype(BF16)
    return hi, mid, lo


def _const_spec(shape):
    zeros = (0,) * len(shape)
    return pl.BlockSpec(shape, lambda *_: zeros, pipeline_mode=pl.Buffered(1))


def _params(n_axes):
    return pltpu.CompilerParams(dimension_semantics=("arbitrary",) * n_axes,
                                vmem_limit_bytes=VMEM_LIMIT)


def _even_mixer_kernel(x_ref, w_in_ref, a_g_ref, a_b_ref, wcat_ref, abias_ref, mstack_ref,
                       w_up_ref, b_gate_ref, ng_ref, w_out_ref, ln_g_ref, ln_b_ref,
                       o_ref, state_ref, *, tile, aw, dkh, dvh):
    @pl.when(pl.program_id(1) == 0)
    def _():
        state_ref[...] = jnp.zeros_like(state_ref)

    hk = B_HEADS * dkh
    hv = B_HEADS * dvh
    x = x_ref[...]
    z = _dot(x.astype(BF16), w_in_ref[...])

    u = jax.nn.gelu(z[:, 0:aw])
    v = _layer_norm(jax.nn.gelu(z[:, aw:2 * aw]), a_g_ref[...], a_b_ref[...])
    gd = aw // A_GROUPS
    rows = lax.broadcasted_iota(jnp.int32, (A_CHUNK, A_GROUPS * A_CHUNK), 0)
    cols = lax.broadcasted_iota(jnp.int32, (A_CHUNK, A_GROUPS * A_CHUNK), 1)
    wcat = jnp.where((cols % A_CHUNK) <= rows, wcat_ref[...], 0.0).astype(BF16)
    r_bd = lax.broadcasted_iota(jnp.int32, (A_GROUPS * A_CHUNK, aw), 0) // A_CHUNK
    c_bd = lax.broadcasted_iota(jnp.int32, (A_GROUPS * A_CHUNK, aw), 1) // gd
    mask_bd = r_bd == c_bd
    ya_parts = []
    for c in range(tile // A_CHUNK):
        sl = slice(c * A_CHUNK, (c + 1) * A_CHUNK)
        v_rep = jnp.concatenate([v[sl]] * A_GROUPS, axis=0)
        v_bd = jnp.where(mask_bd, v_rep, 0.0).astype(BF16)
        sg = _dot(wcat, v_bd) + abias_ref[...]
        ya_parts.append(u[sl] * sg)
    ya = jnp.concatenate(ya_parts, axis=0)

    o0 = 2 * aw
    q = z[:, o0:o0 + hk] * (dkh ** -0.5)
    k = z[:, o0 + hk:o0 + 2 * hk]
    vv = z[:, o0 + 2 * hk:o0 + 2 * hk + hv]
    r = z[:, o0 + 2 * hk + hv:o0 + 2 * hk + 2 * hv]
    g_low = z[:, o0 + 2 * hk + 2 * hv:]
    pre = _dot(g_low.astype(BF16), w_up_ref[...]) + b_gate_ref[...]
    log_a = (jnp.minimum(pre, 0.0) - jnp.log1p(jnp.exp(-jnp.abs(pre)))) * (1.0 / B_TAU)
    la_hi, la_mid, la_lo = _split3(log_a)
    ms = mstack_ref[...]
    b_parts, mid_parts, last_parts = [], [], []
    for t in range(tile // CS_ROWS):
        rs = slice(t * CS_ROWS, (t + 1) * CS_ROWS)
        cs = _dot(ms, la_hi[rs]) + _dot(ms, la_mid[rs]) + _dot(ms, la_lo[rs])
        b_parts.append(cs[0:CS_ROWS])
        mid_parts.append(cs[CS_ROWS:2 * CS_ROWS])
        last_parts.append(cs[2 * CS_ROWS:3 * CS_ROWS])
    b_cum = jnp.concatenate(b_parts, axis=0)
    d_mid = jnp.concatenate(mid_parts, axis=0)
    d_last = jnp.concatenate(last_parts, axis=0)
    qe = (q * jnp.exp(d_mid)).astype(BF16)
    ke = k * jnp.exp(-d_mid)
    kd = (k * jnp.exp(d_last)).astype(BF16)
    qb = (q * jnp.exp(b_cum)).astype(BF16)
    dec = jnp.exp(b_cum + d_last)
    vvb = vv.astype(BF16)

    mask_kk = (lax.broadcasted_iota(jnp.int32, (hk, hk), 0) // dkh
               == lax.broadcasted_iota(jnp.int32, (hk, hk), 1) // dkh)
    mask_vbd = (lax.broadcasted_iota(jnp.int32, (hk, hv), 0) // dkh
                == lax.broadcasted_iota(jnp.int32, (hk, hv), 1) // dvh)
    mask_st = (lax.broadcasted_iota(jnp.int32, (hv, hk), 0) // dvh
               == lax.broadcasted_iota(jnp.int32, (hv, hk), 1) // dkh)
    causal = ((lax.broadcasted_iota(jnp.int32, (B_CHUNK, hk), 1) % B_CHUNK)
              <= lax.broadcasted_iota(jnp.int32, (B_CHUNK, hk), 0))

    st = state_ref[...]
    o_parts = []
    for c in range(tile // B_CHUNK):
        sl = slice(c * B_CHUNK, (c + 1) * B_CHUNK)
        ke_bd = jnp.where(mask_kk, jnp.concatenate([ke[sl]] * B_HEADS, axis=0), 0.0).astype(BF16)
        s_cat = lax.dot_general(qe[sl], ke_bd, NT_DIMS, preferred_element_type=F32)
        s_cat = jnp.where(causal, s_cat, 0.0).astype(BF16)
        v_bd = jnp.where(mask_vbd, jnp.concatenate([vv[sl]] * B_HEADS, axis=0), 0.0).astype(BF16)
        o_c = _dot(s_cat, v_bd) + lax.dot_general(qb[sl], st.astype(BF16), NT_DIMS,
                                                   preferred_element_type=F32)
        kv_t = lax.dot_general(vvb[sl], kd[sl], TN_DIMS, preferred_element_type=F32)
        st = dec[c * B_CHUNK:c * B_CHUNK + 1] * st + jnp.where(mask_st, kv_t, 0.0)
        o_parts.append(o_c)
    state_ref[...] = st
    o = jnp.concatenate(o_parts, axis=0)

    yb_parts = []
    for h in range(B_HEADS):
        oh = o[:, h * dvh:(h + 1) * dvh]
        msq = jnp.mean(oh * oh, axis=-1, keepdims=True)
        yb_parts.append(oh * lax.rsqrt(msq + LN_EPS))
    yb = jnp.concatenate(yb_parts, axis=1) * ng_ref[...] * (r * _sigmoid(r))

    y_cat = jnp.concatenate([ya, yb], axis=1).astype(BF16)
    m = _dot(y_cat, w_out_ref[...])
    o_ref[...] = _layer_norm(DEEPNORM_ALPHA * x + m, ln_g_ref[...], ln_b_ref[...])


def _gla_cumsum_matrices(tile):
    i = jnp.arange(tile)[:, None]
    j = jnp.arange(tile)[None, :]
    same = (i // B_CHUNK) == (j // B_CHUNK)
    m_cum = same & (j <= i)
    m_mid = same & (j <= (i // B_CHUNK) * B_CHUNK + B_CHUNK // 2 - 1)
    m_last = same
    f = lambda t: t.astype(F32)
    return jnp.concatenate([f(m_cum), f(m_cum) - f(m_mid), f(m_last) - f(m_cum)], axis=0).astype(BF16)


def _even_mixer(x, w_in, a_ln_g, a_ln_b, a_ws, a_bs, w_gate_up, b_gate, norm_g, w_out, ln_g, ln_b,
                *, tile=1024):
    bn, s, d = x.shape
    aw = a_ln_g.shape[0]
    hk = w_gate_up.shape[1]
    dkh = hk // B_HEADS
    dvh = norm_g.shape[0]
    hv = B_HEADS * dvh
    rank = w_gate_up.shape[0]
    main = 2 * aw + 2 * hk + 2 * hv
    gd = aw // A_GROUPS
    w_in_p = jnp.concatenate([w_in[:, :main], jnp.pad(w_in[:, main:], ((0, 0), (0, LANES - rank)))],
                             axis=1).astype(BF16)
    w_up_p = jnp.pad(w_gate_up, ((0, LANES - rank), (0, 0))).astype(BF16)
    wcat = jnp.transpose(a_ws, (1, 0, 2)).reshape(A_CHUNK, A_GROUPS * A_CHUNK)
    abias = jnp.repeat(a_bs.T, gd, axis=1)
    mstack = _gla_cumsum_matrices(CS_ROWS)
    ng = jnp.tile(norm_g, B_HEADS)[None, :]
    row = lambda t: t[None, :]
    kern = functools.partial(_even_mixer_kernel, tile=tile, aw=aw, dkh=dkh, dvh=dvh)
    tile_spec = pl.BlockSpec((None, tile, d), lambda b, i: (b, i, 0))
    return pl.pallas_call(
        kern,
        grid=(bn, s // tile),
        in_specs=[tile_spec, _const_spec(w_in_p.shape), _const_spec((1, aw)), _const_spec((1, aw)),
                  _const_spec(wcat.shape), _const_spec(abias.shape), _const_spec(mstack.shape),
                  _const_spec(w_up_p.shape), _const_spec((1, hk)), _const_spec((1, hv)),
                  _const_spec(w_out.shape), _const_spec((1, d)), _const_spec((1, d))],
        out_specs=tile_spec,
        out_shape=jax.ShapeDtypeStruct(x.shape, F32),
        scratch_shapes=[pltpu.VMEM((hv, hk), F32)],
        compiler_params=_params(2),
        name="even_mixer",
    )(x, w_in_p, row(a_ln_g), row(a_ln_b), wcat, abias, mstack, w_up_p, row(b_gate), ng,
      w_out.astype(BF16), row(ln_g), row(ln_b))


def _ple(y, p, wpg_ref, bpg_ref, wpp_ref):
    gate = _sigmoid(_dot(y.astype(BF16), wpg_ref[...]) + bpg_ref[...])
    return y + gate * _dot(p.astype(BF16), wpp_ref[...])


def _ffn_ple_kernel(x_ref, p_ref, wg_ref, wu_ref, wd_ref, ln_g_ref, ln_b_ref, wpg_ref, bpg_ref,
                    wpp_ref, o_ref, *, chunks):
    x = x_ref[...]
    xb = x.astype(BF16)
    acc = None
    for lo, hi in chunks:
        g = _dot(xb, wg_ref[:, lo:hi])
        u = _dot(xb, wu_ref[:, lo:hi])
        h = (g * _sigmoid(g) * u).astype(BF16)
        part = _dot(h, wd_ref[lo:hi, :])
        acc = part if acc is None else acc + part
    y = _layer_norm(DEEPNORM_ALPHA * x + acc, ln_g_ref[...], ln_b_ref[...])
    o_ref[...] = _ple(y, p_ref[...], wpg_ref, bpg_ref, wpp_ref)


def _ffn_ple(x2d, p_all, layer, wg, wu, wd, ln_g, ln_b, wpg, bpg, wpp, *, tile=512, fchunk=1024):
    n, d = x2d.shape
    f = wg.shape[1]
    pd = p_all.shape[2]
    chunks = tuple((lo, min(lo + fchunk, f)) for lo in range(0, f, fchunk))
    row = lambda t: t[None, :]
    return pl.pallas_call(
        functools.partial(_ffn_ple_kernel, chunks=chunks),
        grid=(n // tile,),
        in_specs=[pl.BlockSpec((tile, d), lambda i: (i, 0)),
                  pl.BlockSpec((None, tile, pd), lambda i: (layer, i, 0)),
                  _const_spec((d, f)), _const_spec((d, f)), _const_spec((f, d)),
                  _const_spec((1, d)), _const_spec((1, d)),
                  _const_spec((d, d)), _const_spec((1, d)), _const_spec((pd, d))],
        out_specs=pl.BlockSpec((tile, d), lambda i: (i, 0)),
        out_shape=jax.ShapeDtypeStruct((n, d), F32),
        compiler_params=_params(1),
        name="ffn_ple",
    )(x2d, p_all, wg.astype(BF16), wu.astype(BF16), wd.astype(BF16), row(ln_g), row(ln_b),
      wpg.astype(BF16), row(bpg), wpp.astype(BF16))


def _qkv_kernel(x_ref, w_ref, q_ref, kt_ref, v_ref, *, hd, scale):
    z = _dot(x_ref[...].astype(BF16), w_ref[...])
    q_ref[...] = (z[:, 0:hd] * scale).astype(BF16)
    k_t = z[:, hd:2 * hd].T.astype(BF16)
    heads, dk2, tile = kt_ref.shape
    ones = jnp.ones((tile, dk2), BF16)
    for h in range(heads):
        kt_ref[h] = k_t[h * dk2:(h + 1) * dk2, :]
        v_ref[:, 2 * h * dk2:(2 * h + 1) * dk2] = z[:, 2 * hd + h * dk2:2 * hd + (h + 1) * dk2].astype(BF16)
        v_ref[:, (2 * h + 1) * dk2:(2 * h + 2) * dk2] = ones


def _qkv(x, w_qkv, *, dh, tile):
    bn, s, d = x.shape
    hd = w_qkv.shape[1] // 3
    heads = hd // (2 * dh)
    return pl.pallas_call(
        functools.partial(_qkv_kernel, hd=hd, scale=dh ** -0.5 * LOG2_E),
        grid=(bn, s // tile),
        in_specs=[pl.BlockSpec((None, tile, d), lambda b, i: (b, i, 0)), _const_spec(w_qkv.shape)],
        out_specs=[pl.BlockSpec((None, tile, hd), lambda b, i: (b, i, 0)),
                   pl.BlockSpec((None, heads, None, 2 * dh, tile), lambda b, i: (b, 0, i, 0, 0)),
                   pl.BlockSpec((None, tile, 2 * hd), lambda b, i: (b, i, 0))],
        out_shape=[jax.ShapeDtypeStruct((bn, s, hd), BF16),
                   jax.ShapeDtypeStruct((bn, heads, s // tile, 2 * dh, tile), BF16),
                   jax.ShapeDtypeStruct((bn, s, 2 * hd), BF16)],
        compiler_params=_params(2),
        name="qkv_proj",
    )(x, w_qkv.astype(BF16))


def _diff_attn_kernel(q_ref, kt_ref, v_ref, lam_ref, g_ref, o_ref, m_ref, acc_ref,
                      qm_ref, sa_ref, sb_ref, *, blk, nb, dh, lambda_init):
    pairs = [(qi, ki) for qi in range(nb) for ki in range(qi + 1)]
    bufs = (sa_ref, sb_ref)
    half = blk // 2

    def load_queries(qi):
        q = q_ref[qi * blk:(qi + 1) * blk, :]
        lane = lax.broadcasted_iota(jnp.int32, q.shape, 1)
        zero = jnp.zeros_like(q)
        qm_ref[0] = jnp.where(lane < dh, q, zero)
        qm_ref[1] = jnp.where(lane >= dh, q, zero)

    def scores(ki, s_ref):
        kt_blk = kt_ref[ki]
        for c in range(2):
            s_ref[c] = _dot(qm_ref[c], kt_blk)

    def update(c, rows, s, v_part):
        m_prev = m_ref[c, rows]
        m_new = jnp.maximum(m_prev, jnp.max(s, axis=-1, keepdims=True))
        corr = jnp.exp2(m_prev - m_new)
        p = jnp.exp2(s - jnp.concatenate([m_new] * (s.shape[1] // LANES), axis=1))
        acc_ref[c, rows] = (jnp.concatenate([corr, corr], axis=1) * acc_ref[c, rows]
                            + _dot(p.astype(BF16), v_part))
        m_ref[c, rows] = m_new

    def softmax_pv(s_ref, ki, masked):
        v_blk = v_ref[ki * blk:(ki + 1) * blk, :]
        for c in range(2):
            if not masked:
                update(c, slice(0, blk), s_ref[c], v_blk)
                continue
            s_top = s_ref[c, 0:half, 0:half]
            row = lax.broadcasted_iota(jnp.int32, s_top.shape, 0)
            col = lax.broadcasted_iota(jnp.int32, s_top.shape, 1)
            update(c, slice(0, half), jnp.where(col <= row, s_top, -1e30), v_blk[0:half])
            s_bot = s_ref[c, half:blk, :]
            row = lax.broadcasted_iota(jnp.int32, s_bot.shape, 0) + half
            col = lax.broadcasted_iota(jnp.int32, s_bot.shape, 1)
            update(c, slice(half, blk), jnp.where(col <= row, s_bot, -1e30), v_blk)

    def finalize(qi):
        a1 = acc_ref[0, :, 0:2 * dh] * (1.0 / acc_ref[0, :, 2 * dh:4 * dh])
        a2 = acc_ref[1, :, 0:2 * dh] * (1.0 / acc_ref[1, :, 2 * dh:4 * dh])
        lv = lam_ref[...]
        lam = (jnp.exp(jnp.sum(lv[0:1] * lv[1:2], axis=-1, keepdims=True))
               - jnp.exp(jnp.sum(lv[2:3] * lv[3:4], axis=-1, keepdims=True)) + lambda_init)
        o = a1 - lam * a2
        msq = jnp.mean(o * o, axis=-1, keepdims=True)
        o_ref[qi * blk:(qi + 1) * blk, :] = (o * lax.rsqrt(msq + LN_EPS) * g_ref[...]
                                              * (1.0 - lambda_init)).astype(BF16)

    load_queries(0)
    scores(0, bufs[0])
    for t, (qi, ki) in enumerate(pairs):
        if t + 1 < len(pairs):
            nqi, nki = pairs[t + 1]
            if nki == 0:
                load_queries(nqi)
            scores(nki, bufs[(t + 1) % 2])
        if ki == 0:
            m_ref[...] = jnp.full_like(m_ref, -1e30)
            acc_ref[...] = jnp.zeros_like(acc_ref)
        softmax_pv(bufs[t % 2], ki, masked=(ki == qi))
        if ki == qi:
            finalize(qi)


def _diff_attn(q, kt, v_ext, lam_q1, lam_k1, lam_q2, lam_k2, subln_g, lambda_init, *, blk):
    bn, s, hd = q.shape
    dh = lam_q1.shape[0]
    heads = hd // (2 * dh)
    nb = s // blk
    lamv = jnp.zeros((8, LANES), F32).at[0:4, 0:dh].set(jnp.stack([lam_q1, lam_k1, lam_q2, lam_k2]))
    return pl.pallas_call(
        functools.partial(_diff_attn_kernel, blk=blk, nb=nb, dh=dh, lambda_init=lambda_init),
        grid=(bn, heads),
        in_specs=[
            pl.BlockSpec((None, s, 2 * dh), lambda b, h: (b, 0, h)),
            pl.BlockSpec((None, None, nb, 2 * dh, blk), lambda b, h: (b, h, 0, 0, 0)),
            pl.BlockSpec((None, s, 4 * dh), lambda b, h: (b, 0, h)),
            pl.BlockSpec((8, LANES), lambda b, h: (0, 0)),
            pl.BlockSpec((1, 2 * dh), lambda b, h: (0, 0)),
        ],
        out_specs=pl.BlockSpec((None, s, 2 * dh), lambda b, h: (b, 0, h)),
        out_shape=jax.ShapeDtypeStruct((bn, s, hd), BF16),
        scratch_shapes=[pltpu.VMEM((2, blk, LANES), F32),
                        pltpu.VMEM((2, blk, 4 * dh), F32), pltpu.VMEM((2, blk, 2 * dh), BF16),
                        pltpu.VMEM((2, blk, blk), F32), pltpu.VMEM((2, blk, blk), F32)],
        compiler_params=_params(2),
        name="diff_attn",
    )(q, kt, v_ext, lamv, subln_g[None, :])


def _attn_out_kernel(o_ref, x_ref, w_ref, ln_g_ref, ln_b_ref, wr_ref, lstrict_ref,
                     y_ref, route_ref, cnt_ref, base_ref):
    @pl.when(pl.program_id(0) == 0)
    def _():
        base_ref[...] = jnp.zeros_like(base_ref)

    m = _dot(o_ref[...], w_ref[...])
    y = _layer_norm(DEEPNORM_ALPHA * x_ref[...] + m, ln_g_ref[...], ln_b_ref[...])
    y_ref[...] = y
    y_hi = y.astype(BF16)
    y_lo = (y - y_hi.astype(F32)).astype(BF16)
    hi_terms = _dot(y_hi, wr_ref[...])
    logits = hi_terms[:, 0:LANES] + hi_terms[:, LANES:2 * LANES] + _dot(y_lo, wr_ref[:, 0:LANES])

    lane = lax.broadcasted_iota(jnp.int32, logits.shape, 1).astype(F32)
    neg = -jnp.inf
    lg = jnp.where(lane < N_EXPERTS, logits, neg)
    v0 = jnp.max(lg, axis=-1, keepdims=True)
    i0 = jnp.min(jnp.where(lg == v0, lane, float(LANES)), axis=-1, keepdims=True)
    lg2 = jnp.where(lane == i0, neg, lg)
    v1 = jnp.max(lg2, axis=-1, keepdims=True)
    i1 = jnp.min(jnp.where(lg2 == v1, lane, float(LANES)), axis=-1, keepdims=True)
    e = jnp.exp(v1 - v0)
    g0 = 1.0 / (1.0 + e)
    g1 = e / (1.0 + e)
    oh0 = lane == i0
    oh1 = lane == i1
    c = jnp.where(oh0, 1.0, 0.0) + jnp.where(oh1, 1.0, 0.0)
    pre = _dot(lstrict_ref[...], c.astype(BF16)) + base_ref[...]
    r0 = jnp.sum(jnp.where(oh0, pre, 0.0), axis=-1, keepdims=True)
    r1 = jnp.sum(jnp.where(oh1, pre, 0.0), axis=-1, keepdims=True)
    base_ref[...] = base_ref[...] + jnp.sum(c, axis=0, keepdims=True)
    cnt_ref[...] = jnp.broadcast_to(base_ref[...], cnt_ref.shape)
    fields = (i0, i1, r0, r1, g0, g1)
    route = jnp.zeros_like(logits)
    for idx, val in enumerate(fields):
        route = jnp.where(lane == idx, val, route)
    route_ref[...] = route


def _attn_out(o2d, x2d, w_out, ln_g, ln_b, w_router, *, tile=1024):
    n, d = x2d.shape
    wr = jnp.pad(w_router, ((0, 0), (0, LANES - w_router.shape[1])))
    wr_hi = wr.astype(BF16)
    wr_lo = (wr - wr_hi.astype(F32)).astype(BF16)
    wr_split = jnp.concatenate([wr_hi, wr_lo], axis=1)
    lstrict = (jnp.arange(tile)[:, None] > jnp.arange(tile)[None, :]).astype(BF16)
    row = lambda t: t[None, :]
    tspec = pl.BlockSpec((tile, d), lambda i: (i, 0))
    return pl.pallas_call(
        _attn_out_kernel,
        grid=(n // tile,),
        in_specs=[tspec, tspec, _const_spec((d, d)), _const_spec((1, d)), _const_spec((1, d)),
                  _const_spec((d, 2 * LANES)), _const_spec((tile, tile))],
        out_specs=[tspec, pl.BlockSpec((tile, LANES), lambda i: (i, 0)),
                   pl.BlockSpec((8, LANES), lambda i: (0, 0))],
        out_shape=[jax.ShapeDtypeStruct((n, d), F32), jax.ShapeDtypeStruct((n, LANES), F32),
                   jax.ShapeDtypeStruct((8, LANES), F32)],
        scratch_shapes=[pltpu.VMEM((1, LANES), F32)],
        compiler_params=_params(1),
        name="attn_out_router",
    )(o2d, x2d, w_out.astype(BF16), row(ln_g), row(ln_b), wr_split, lstrict)


def _route_tables(route, cnt, *, tm):
    n = route.shape[0]
    counts = cnt[0, :N_EXPERTS].astype(jnp.int32)
    padded = (counts + tm - 1) // tm * tm
    pend = jnp.cumsum(padded)
    pstart = pend - padded

    def dest_of(k):
        e = route[:, k].astype(jnp.int32)
        start = jnp.sum(jnp.where(e[:, None] == jnp.arange(N_EXPERTS)[None, :], pstart[None, :], 0), axis=-1)
        return start + route[:, TOP_K + k].astype(jnp.int32)

    dest = (dest_of(0), dest_of(1))
    n_rows = n * TOP_K + N_EXPERTS * tm
    n_blocks = n_rows // tm
    n_used = pend[-1] // tm
    blk_idx = jnp.arange(n_blocks, dtype=jnp.int32)
    blk_start = jnp.minimum(blk_idx, n_used - 1) * tm
    blk_e = jnp.minimum(jnp.sum(blk_start[:, None] >= pend[None, :], axis=1), N_EXPERTS - 1)
    valid = jnp.clip(pstart[blk_e] + counts[blk_e] - blk_idx * tm, 0, tm)
    return (dest, blk_e.astype(jnp.int32), n_used.reshape(1).astype(jnp.int32),
            valid.astype(jnp.int32), n_rows)


SC_CORES = 2
SC_SUBCORES = 16
SC_WORKERS = SC_CORES * SC_SUBCORES
SC_ROWS = 32


def _sc_mesh():
    return plsc.VectorSubcoreMesh(core_axis_name="c", subcore_axis_name="s",
                                  num_cores=SC_CORES, num_subcores=SC_SUBCORES)


def _sc_worker_id():
    return lax.axis_index("s") * SC_CORES + lax.axis_index("c")


def _sc_dispatch(x2d, dest, n_rows):
    n, d = x2d.shape
    per_w = n // SC_WORKERS
    n_chunks = per_w // SC_ROWS
    assert n == SC_WORKERS * n_chunks * SC_ROWS and n_chunks % 2 == 0
    d0 = dest[0].reshape(SC_WORKERS, n_chunks, SC_ROWS)
    d1 = dest[1].reshape(SC_WORKERS, n_chunks, SC_ROWS)

    def body(x_hbm, d0_hbm, d1_hbm, xs_hbm, d0_v, d1_v, rows_v, rsem, s0sem, s1sem):
        wid = _sc_worker_id()
        base = wid * per_w
        pltpu.sync_copy(d0_hbm.at[wid], d0_v)
        pltpu.sync_copy(d1_hbm.at[wid], d1_v)

        def read(c, slot):
            return pltpu.make_async_copy(x_hbm.at[pl.ds(base + c * SC_ROWS, SC_ROWS)],
                                         rows_v.at[slot], rsem.at[slot])

        def scat(idx_v, sem, c, slot):
            return pltpu.make_async_copy(rows_v.at[slot], xs_hbm.at[idx_v.at[c]], sem.at[slot])

        def start_scatters(c, slot):
            scat(d0_v, s0sem, c, slot).start()
            scat(d1_v, s1sem, c, slot).start()

        def wait_scatters(c, slot):
            scat(d0_v, s0sem, c, slot).wait()
            scat(d1_v, s1sem, c, slot).wait()

        read(0, 0).start()

        def pair(j, carry):
            c0 = 2 * j
            read(c0, 0).wait()

            @pl.when(j > 0)
            def _():
                wait_scatters(c0 - 1, 1)

            read(c0 + 1, 1).start()
            start_scatters(c0, 0)
            read(c0 + 1, 1).wait()
            wait_scatters(c0, 0)

            @pl.when(c0 + 2 < n_chunks)
            def _():
                read(c0 + 2, 0).start()

            start_scatters(c0 + 1, 1)
            return carry

        lax.fori_loop(0, n_chunks // 2, pair, 0)
        wait_scatters(n_chunks - 1, 1)

    return pl.kernel(
        body,
        out_type=jax.ShapeDtypeStruct((n_rows, d), x2d.dtype),
        mesh=_sc_mesh(),
        scratch_types=[pltpu.VMEM((n_chunks, SC_ROWS), jnp.int32), pltpu.VMEM((n_chunks, SC_ROWS), jnp.int32),
                       pltpu.VMEM((2, SC_ROWS, d), x2d.dtype), pltpu.SemaphoreType.DMA((2,)),
                       pltpu.SemaphoreType.DMA((2,)), pltpu.SemaphoreType.DMA((2,))],
        name="sc_dispatch",
    )(x2d, d0, d1)


def _sc_gather(table, idx):
    _, d = table.shape
    b = idx.shape[0]
    per_w = b // SC_WORKERS
    n_chunks = per_w // SC_ROWS
    assert b == SC_WORKERS * n_chunks * SC_ROWS and n_chunks % 2 == 0
    idx3 = idx.reshape(SC_WORKERS, n_chunks, SC_ROWS)

    def body(table_hbm, idx_hbm, out_hbm, idx_v, rows_v, gsem, wsem):
        wid = _sc_worker_id()
        base = wid * per_w
        pltpu.sync_copy(idx_hbm.at[wid], idx_v)

        def gather(c, slot):
            return pltpu.make_async_copy(table_hbm.at[idx_v.at[c]], rows_v.at[slot], gsem.at[slot])

        def write(c, slot):
            return pltpu.make_async_copy(rows_v.at[slot],
                                         out_hbm.at[pl.ds(base + c * SC_ROWS, SC_ROWS)], wsem.at[slot])

        gather(0, 0).start()

        def pair(j, carry):
            c0 = 2 * j
            gather(c0, 0).wait()

            @pl.when(j > 0)
            def _():
                write(c0 - 1, 1).wait()

            gather(c0 + 1, 1).start()
            write(c0, 0).start()
            gather(c0 + 1, 1).wait()
            write(c0, 0).wait()

            @pl.when(c0 + 2 < n_chunks)
            def _():
                gather(c0 + 2, 0).start()

            write(c0 + 1, 1).start()
            return carry

        lax.fori_loop(0, n_chunks // 2, pair, 0)
        write(n_chunks - 1, 1).wait()

    return pl.kernel(
        body,
        out_type=jax.ShapeDtypeStruct((b, d), table.dtype),
        mesh=_sc_mesh(),
        scratch_types=[pltpu.VMEM((n_chunks, SC_ROWS), jnp.int32), pltpu.VMEM((2, SC_ROWS, d), table.dtype),
                       pltpu.SemaphoreType.DMA((2,)), pltpu.SemaphoreType.DMA((2,))],
        name="sc_combine_gather",
    )(table, idx3)


def _moe_kernel(blk_e, n_used, valid, xs_ref, wg_ref, wu_ref, wd_ref, y_ref, *, chunks):
    m = pl.program_id(0)
    f = pl.program_id(1)

    @pl.when(f == 0)
    def _():
        y_ref[...] = jnp.zeros_like(y_ref)

    @pl.when(m < n_used[0])
    def _():
        row = lax.broadcasted_iota(jnp.int32, xs_ref.shape, 0)
        x = jnp.where(row < valid[m], xs_ref[...], 0.0).astype(BF16)
        for lo, hi in chunks:
            g = _dot(x, wg_ref[:, lo:hi])
            u = _dot(x, wu_ref[:, lo:hi])
            h = (g * _sigmoid(g) * u).astype(BF16)
            y_ref[...] += _dot(h, wd_ref[lo:hi, :])


def _moe_grouped(xs, blk_e, n_used, valid, wg, wu, wd, *, tm, tf, fchunk=1024):
    n_rows, d = xs.shape
    fdim = wg.shape[2]
    nf = fdim // tf
    assert nf * tf == fdim
    n_blocks = n_rows // tm
    chunks = tuple((lo, min(lo + fchunk, tf)) for lo in range(0, tf, fchunk))

    def f_eff(m, f, nu):
        mm = jnp.minimum(m, nu[0] - 1)
        snake = jnp.where(mm % 2 == 0, f, nf - 1 - f)
        return jnp.where(m < nu[0], snake, jnp.where(mm % 2 == 0, nf - 1, 0))

    grid_spec = pltpu.PrefetchScalarGridSpec(
        num_scalar_prefetch=3,
        grid=(n_blocks, nf),
        in_specs=[
            pl.BlockSpec((tm, d), lambda m, f, be, nu, va: (jnp.minimum(m, nu[0] - 1), 0)),
            pl.BlockSpec((None, d, tf), lambda m, f, be, nu, va: (be[m], 0, f_eff(m, f, nu))),
            pl.BlockSpec((None, d, tf), lambda m, f, be, nu, va: (be[m], 0, f_eff(m, f, nu))),
            pl.BlockSpec((None, tf, d), lambda m, f, be, nu, va: (be[m], f_eff(m, f, nu), 0)),
        ],
        out_specs=pl.BlockSpec((tm, d), lambda m, f, be, nu, va: (m, 0)),
    )
    return pl.pallas_call(
        functools.partial(_moe_kernel, chunks=chunks),
        grid_spec=grid_spec,
        out_shape=jax.ShapeDtypeStruct((n_rows, d), F32),
        compiler_params=_params(2),
        name="moe_grouped",
    )(blk_e, n_used, valid, xs, wg, wu, wd)


def _final_kernel(x_ref, y0_ref, y1_ref, route_ref, p_ref, ln_g_ref, ln_b_ref, wpg_ref, bpg_ref,
                  wpp_ref, *rest):
    o_ref = rest[-1]
    route = route_ref[...]
    f = route[:, 4:5] * y0_ref[...] + route[:, 5:6] * y1_ref[...]
    y = _layer_norm(DEEPNORM_ALPHA * x_ref[...] + f, ln_g_ref[...], ln_b_ref[...])
    o_ref[...] = _ple(y, p_ref[...], wpg_ref, bpg_ref, wpp_ref)


def _final_part(x2d, y01, route, p_all, layer, ln_g, ln_b, wpg, bpg, wpp, *, part, n_parts, prev, tile):
    n, d = x2d.shape
    pd = p_all.shape[2]
    nblk = n // n_parts // tile
    off = part * nblk
    row = lambda t: t[None, :]
    tok = lambda width: pl.BlockSpec((tile, width), lambda i: (i + off, 0))
    in_specs = [tok(d), pl.BlockSpec((tile, d), lambda i: (i, 0)),
                pl.BlockSpec((tile, d), lambda i: (i + nblk, 0)), tok(LANES),
                pl.BlockSpec((None, tile, pd), lambda i: (layer, i + off, 0)),
                _const_spec((1, d)), _const_spec((1, d)),
                _const_spec((d, d)), _const_spec((1, d)), _const_spec((pd, d))]
    args = [x2d, y01, y01, route, p_all, row(ln_g), row(ln_b), wpg.astype(BF16), row(bpg), wpp.astype(BF16)]
    aliases = {}
    if prev is not None:
        in_specs.append(pl.BlockSpec(memory_space=pl.ANY))
        args.append(prev)
        aliases = {len(args) - 1: 0}
    return pl.pallas_call(
        _final_kernel,
        grid=(nblk,),
        in_specs=in_specs,
        out_specs=tok(d),
        out_shape=jax.ShapeDtypeStruct((n, d), F32),
        input_output_aliases=aliases,
        compiler_params=_params(1),
        name="moe_combine_ple",
    )(*args)


def _combine(x2d, y, dest, route, p_all, layer, ln_g, ln_b, wpg, bpg, wpp, *, n_parts=2, tile=1024):
    n = x2d.shape[0]
    step = n // n_parts
    tile = min(tile, step)
    out = None
    for part in range(n_parts):
        rows = slice(part * step, (part + 1) * step)
        y01 = _sc_gather(y, jnp.concatenate([dest[0][rows], dest[1][rows]]))
        out = _final_part(x2d, y01, route, p_all, layer, ln_g, ln_b, wpg, bpg, wpp,
                          part=part, n_parts=n_parts, prev=out, tile=tile)
    return out


def _even_layer(x, p_all, layer_idx, w_in, a_ln_g, a_ln_b, a_ws, a_bs, w_gate_up, b_gate, norm_g, w_out,
                ln1_g, ln1_b, wg, wu, wd, ln2_g, ln2_b, wpp, wpg, bpg):
    bn, s, d = x.shape
    x1 = _even_mixer(x, w_in, a_ln_g, a_ln_b, a_ws, a_bs, w_gate_up, b_gate, norm_g, w_out, ln1_g, ln1_b)
    x2 = _ffn_ple(x1.reshape(bn * s, d), p_all, layer_idx, wg, wu, wd, ln2_g, ln2_b, wpg, bpg, wpp)
    return x2.reshape(bn, s, d)


def _odd_layer(x, p_all, layer_idx, w_qkv, lam_q1, lam_k1, lam_q2, lam_k2, subln_g, w_out, ln1_g, ln1_b,
               w_router, ewg, ewu, ewd, ln2_g, ln2_b, wpp, wpg, bpg, *, attn_blk=1024, moe_tm=512,
               moe_tf=1792):
    bn, s, d = x.shape
    n = bn * s
    lambda_init = 0.8 - 0.6 * math.exp(-0.3 * layer_idx)
    q, kt, v_ext = _qkv(x, w_qkv, dh=lam_q1.shape[0], tile=attn_blk)
    o = _diff_attn(q, kt, v_ext, lam_q1, lam_k1, lam_q2, lam_k2, subln_g, lambda_init, blk=attn_blk)
    x3, route, cnt = _attn_out(o.reshape(n, d), x.reshape(n, d), w_out, ln1_g, ln1_b, w_router)
    dest, blk_e, n_used, valid, n_rows = _route_tables(route, cnt, tm=moe_tm)
    xs = _sc_dispatch(x3, dest, n_rows)
    y = _moe_grouped(xs, blk_e, n_used, valid, ewg.astype(BF16), ewu.astype(BF16), ewd.astype(BF16),
                     tm=moe_tm, tf=moe_tf)
    out = _combine(x3, y, dest, route, p_all, layer_idx, ln2_g, ln2_b, wpg, bpg, wpp)
    return out.reshape(bn, s, d)


def kernel(x, p, e_w_in, e_a_ln_g, e_a_ln_b, e_a_ws, e_a_bs, e_b_w_gate_up, e_b_b_gate, e_b_norm_g, e_w_out, e_ln1_g, e_ln1_b, e_ffn_wg, e_ffn_wu, e_ffn_wd, e_ln2_g, e_ln2_b, o_w_qkv, o_lam_q1, o_lam_k1, o_lam_q2, o_lam_k2, o_subln_g, o_w_out, o_ln1_g, o_ln1_b, o_router, o_exp_wg, o_exp_wu, o_exp_wd, o_ln2_g, o_ln2_b, ple_w_proj, ple_w_gate, ple_b_gate):
    p_all = p.reshape(p.shape[0], -1, p.shape[-1])
    for i in range(DEPTH):
        j = i // 2
        if i % 2 == 0:
            x = _even_layer(x, p_all, i, e_w_in[j], e_a_ln_g[j], e_a_ln_b[j], e_a_ws[j], e_a_bs[j],
                            e_b_w_gate_up[j], e_b_b_gate[j], e_b_norm_g[j], e_w_out[j],
                            e_ln1_g[j], e_ln1_b[j], e_ffn_wg[j], e_ffn_wu[j], e_ffn_wd[j],
                            e_ln2_g[j], e_ln2_b[j], ple_w_proj[i], ple_w_gate[i], ple_b_gate[i])
        else:
            x = _odd_layer(x, p_all, i, o_w_qkv[j], o_lam_q1[j], o_lam_k1[j], o_lam_q2[j], o_lam_k2[j],
                           o_subln_g[j], o_w_out[j], o_ln1_g[j], o_ln1_b[j], o_router[j],
                           o_exp_wg[j], o_exp_wu[j], o_exp_wd[j], o_ln2_g[j], o_ln2_b[j],
                           ple_w_proj[i], ple_w_gate[i], ple_b_gate[i])
    return x
```

```python
import functools
import math

import jax
import jax.numpy as jnp
from jax import lax
from jax.experimental import pallas as pl
from jax.experimental.pallas import tpu as pltpu
from jax.experimental.pallas import tpu_sc as plsc

F32 = jnp.float32
BF16 = jnp.bfloat16

DEPTH = 2
DEEPNORM_ALPHA = (2.0 * DEPTH) ** 0.25
LN_EPS = 1e-5
A_CHUNK = 128
A_GROUPS = 8
B_HEADS = 4
B_CHUNK = 64
B_TAU = 16.0
CS_ROWS = 256
C_HEADS = 8
N_EXPERTS = 8
TOP_K = 2
LANES = 128
LOG2_E = 1.4426950408889634
VMEM_LIMIT = 56 * 1024 * 1024

NT_DIMS = (((1,), (1,)), ((), ()))
TN_DIMS = (((0,), (0,)), ((), ()))


def _dot(a, b):
    return jnp.dot(a, b, preferred_element_type=F32)


def _layer_norm(x, g, b):
    mu = jnp.mean(x, axis=-1, keepdims=True)
    xc = x - mu
    var = jnp.mean(xc * xc, axis=-1, keepdims=True)
    return xc * lax.rsqrt(var + LN_EPS) * g + b


def _sigmoid(x):
    return 1.0 / (1.0 + jnp.exp(-x))


def _split3(a):
    hi = a.astype(BF16)
    r1 = a - hi.astype(F32)
    mid = r1.astype(BF16)
    lo = (r1 - mid.astype(F32)).astype(BF16)
    return hi, mid, lo


def _const_spec(shape):
    zeros = (0,) * len(shape)
    return pl.BlockSpec(shape, lambda *_: zeros, pipeline_mode=pl.Buffered(1))


def _params(n_axes):
    return pltpu.CompilerParams(dimension_semantics=("arbitrary",) * n_axes,
                                vmem_limit_bytes=VMEM_LIMIT)


def _even_mixer_kernel(x_ref, w_in_ref, a_g_ref, a_b_ref, wcat_ref, abias_ref, mstack_ref,
                       w_up_ref, b_gate_ref, ng_ref, w_out_ref, ln_g_ref, ln_b_ref,
                       o_ref, state_ref, *, tile, aw, dkh, dvh):
    @pl.when(pl.program_id(1) == 0)
    def _():
        state_ref[...] = jnp.zeros_like(state_ref)

    hk = B_HEADS * dkh
    hv = B_HEADS * dvh
    x = x_ref[...]
    z = _dot(x.astype(BF16), w_in_ref[...])

    u = jax.nn.gelu(z[:, 0:aw])
    v = _layer_norm(jax.nn.gelu(z[:, aw:2 * aw]), a_g_ref[...], a_b_ref[...])
    gd = aw // A_GROUPS
    rows = lax.broadcasted_iota(jnp.int32, (A_CHUNK, A_GROUPS * A_CHUNK), 0)
    cols = lax.broadcasted_iota(jnp.int32, (A_CHUNK, A_GROUPS * A_CHUNK), 1)
    wcat = jnp.where((cols % A_CHUNK) <= rows, wcat_ref[...], 0.0).astype(BF16)
    r_bd = lax.broadcasted_iota(jnp.int32, (A_GROUPS * A_CHUNK, aw), 0) // A_CHUNK
    c_bd = lax.broadcasted_iota(jnp.int32, (A_GROUPS * A_CHUNK, aw), 1) // gd
    mask_bd = r_bd == c_bd
    ya_parts = []
    for c in range(tile // A_CHUNK):
        sl = slice(c * A_CHUNK, (c + 1) * A_CHUNK)
        v_rep = jnp.concatenate([v[sl]] * A_GROUPS, axis=0)
        v_bd = jnp.where(mask_bd, v_rep, 0.0).astype(BF16)
        sg = _dot(wcat, v_bd) + abias_ref[...]
        ya_parts.append(u[sl] * sg)
    ya = jnp.concatenate(ya_parts, axis=0)

    o0 = 2 * aw
    q = z[:, o0:o0 + hk] * (dkh ** -0.5)
    k = z[:, o0 + hk:o0 + 2 * hk]
    vv = z[:, o0 + 2 * hk:o0 + 2 * hk + hv]
    r = z[:, o0 + 2 * hk + hv:o0 + 2 * hk + 2 * hv]
    g_low = z[:, o0 + 2 * hk + 2 * hv:]
    pre = _dot(g_low.astype(BF16), w_up_ref[...]) + b_gate_ref[...]
    log_a = (jnp.minimum(pre, 0.0) - jnp.log1p(jnp.exp(-jnp.abs(pre)))) * (1.0 / B_TAU)
    la_hi, la_mid, la_lo = _split3(log_a)
    ms = mstack_ref[...]
    b_parts, mid_parts, last_parts = [], [], []
    for t in range(tile // CS_ROWS):
        rs = slice(t * CS_ROWS, (t + 1) * CS_ROWS)
        cs = _dot(ms, la_hi[rs]) + _dot(ms, la_mid[rs]) + _dot(ms, la_lo[rs])
        b_parts.append(cs[0:CS_ROWS])
        mid_parts.append(cs[CS_ROWS:2 * CS_ROWS])
        last_parts.append(cs[2 * CS_ROWS:3 * CS_ROWS])
    b_cum = jnp.concatenate(b_parts, axis=0)
    d_mid = jnp.concatenate(mid_parts, axis=0)
    d_last = jnp.concatenate(last_parts, axis=0)
    qe = (q * jnp.exp(d_mid)).astype(BF16)
    ke = k * jnp.exp(-d_mid)
    kd = (k * jnp.exp(d_last)).astype(BF16)
    qb = (q * jnp.exp(b_cum)).astype(BF16)
    dec = jnp.exp(b_cum + d_last)
    vvb = vv.astype(BF16)

    mask_kk = (lax.broadcasted_iota(jnp.int32, (hk, hk), 0) // dkh
               == lax.broadcasted_iota(jnp.int32, (hk, hk), 1) // dkh)
    mask_vbd = (lax.broadcasted_iota(jnp.int32, (hk, hv), 0) // dkh
                == lax.broadcasted_iota(jnp.int32, (hk, hv), 1) // dvh)
    mask_st = (lax.broadcasted_iota(jnp.int32, (hv, hk), 0) // dvh
               == lax.broadcasted_iota(jnp.int32, (hv, hk), 1) // dkh)
    causal = ((lax.broadcasted_iota(jnp.int32, (B_CHUNK, hk), 1) % B_CHUNK)
              <= lax.broadcasted_iota(jnp.int32, (B_CHUNK, hk), 0))

    st = state_ref[...]
    o_parts = []
    for c in range(tile // B_CHUNK):
        sl = slice(c * B_CHUNK, (c + 1) * B_CHUNK)
        ke_bd = jnp.where(mask_kk, jnp.concatenate([ke[sl]] * B_HEADS, axis=0), 0.0).astype(BF16)
        s_cat = lax.dot_general(qe[sl], ke_bd, NT_DIMS, preferred_element_type=F32)
        s_cat = jnp.where(causal, s_cat, 0.0).astype(BF16)
        v_bd = jnp.where(mask_vbd, jnp.concatenate([vv[sl]] * B_HEADS, axis=0), 0.0).astype(BF16)
        o_c = _dot(s_cat, v_bd) + lax.dot_general(qb[sl], st.astype(BF16), NT_DIMS,
                                                   preferred_element_type=F32)
        kv_t = lax.dot_general(vvb[sl], kd[sl], TN_DIMS, preferred_element_type=F32)
        st = dec[c * B_CHUNK:c * B_CHUNK + 1] * st + jnp.where(mask_st, kv_t, 0.0)
        o_parts.append(o_c)
    state_ref[...] = st
    o = jnp.concatenate(o_parts, axis=0)

    yb_parts = []
    for h in range(B_HEADS):
        oh = o[:, h * dvh:(h + 1) * dvh]
        msq = jnp.mean(oh * oh, axis=-1, keepdims=True)
        yb_parts.append(oh * lax.rsqrt(msq + LN_EPS))
    yb = jnp.concatenate(yb_parts, axis=1) * ng_ref[...] * (r * _sigmoid(r))

    y_cat = jnp.concatenate([ya, yb], axis=1).astype(BF16)
    m = _dot(y_cat, w_out_ref[...])
    o_ref[...] = _layer_norm(DEEPNORM_ALPHA * x + m, ln_g_ref[...], ln_b_ref[...])


def _gla_cumsum_matrices(tile):
    i = jnp.arange(tile)[:, None]
    j = jnp.arange(tile)[None, :]
    same = (i // B_CHUNK) == (j // B_CHUNK)
    m_cum = same & (j <= i)
    m_mid = same & (j <= (i // B_CHUNK) * B_CHUNK + B_CHUNK // 2 - 1)
    m_last = same
    f = lambda t: t.astype(F32)
    return jnp.concatenate([f(m_cum), f(m_cum) - f(m_mid), f(m_last) - f(m_cum)], axis=0).astype(BF16)


def _even_mixer(x, w_in, a_ln_g, a_ln_b, a_ws, a_bs, w_gate_up, b_gate, norm_g, w_out, ln_g, ln_b,
                *, tile=1024):
    bn, s, d = x.shape
    aw = a_ln_g.shape[0]
    hk = w_gate_up.shape[1]
    dkh = hk // B_HEADS
    dvh = norm_g.shape[0]
    hv = B_HEADS * dvh
    rank = w_gate_up.shape[0]
    main = 2 * aw + 2 * hk + 2 * hv
    gd = aw // A_GROUPS
    w_in_p = jnp.concatenate([w_in[:, :main], jnp.pad(w_in[:, main:], ((0, 0), (0, LANES - rank)))],
                             axis=1).astype(BF16)
    w_up_p = jnp.pad(w_gate_up, ((0, LANES - rank), (0, 0))).astype(BF16)
    wcat = jnp.transpose(a_ws, (1, 0, 2)).reshape(A_CHUNK, A_GROUPS * A_CHUNK)
    abias = jnp.repeat(a_bs.T, gd, axis=1)
    mstack = _gla_cumsum_matrices(CS_ROWS)
    ng = jnp.tile(norm_g, B_HEADS)[None, :]
    row = lambda t: t[None, :]
    kern = functools.partial(_even_mixer_kernel, tile=tile, aw=aw, dkh=dkh, dvh=dvh)
    tile_spec = pl.BlockSpec((None, tile, d), lambda b, i: (b, i, 0))
    return pl.pallas_call(
        kern,
        grid=(bn, s // tile),
        in_specs=[tile_spec, _const_spec(w_in_p.shape), _const_spec((1, aw)), _const_spec((1, aw)),
                  _const_spec(wcat.shape), _const_spec(abias.shape), _const_spec(mstack.shape),
                  _const_spec(w_up_p.shape), _const_spec((1, hk)), _const_spec((1, hv)),
                  _const_spec(w_out.shape), _const_spec((1, d)), _const_spec((1, d))],
        out_specs=tile_spec,
        out_shape=jax.ShapeDtypeStruct(x.shape, F32),
        scratch_shapes=[pltpu.VMEM((hv, hk), F32)],
        compiler_params=_params(2),
        name="even_mixer",
    )(x, w_in_p, row(a_ln_g), row(a_ln_b), wcat, abias, mstack, w_up_p, row(b_gate), ng,
      w_out.astype(BF16), row(ln_g), row(ln_b))


def _ple(y, p, wpg_ref, bpg_ref, wpp_ref):
    gate = _sigmoid(_dot(y.astype(BF16), wpg_ref[...]) + bpg_ref[...])
    return y + gate * _dot(p.astype(BF16), wpp_ref[...])


def _ffn_ple_kernel(x_ref, p_ref, wg_ref, wu_ref, wd_ref, ln_g_ref, ln_b_ref, wpg_ref, bpg_ref,
                    wpp_ref, o_ref, *, chunks):
    x = x_ref[...]
    xb = x.astype(BF16)
    acc = None
    for lo, hi in chunks:
        g = _dot(xb, wg_ref[:, lo:hi])
        u = _dot(xb, wu_ref[:, lo:hi])
        h = (g * _sigmoid(g) * u).astype(BF16)
        part = _dot(h, wd_ref[lo:hi, :])
        acc = part if acc is None else acc + part
    y = _layer_norm(DEEPNORM_ALPHA * x + acc, ln_g_ref[...], ln_b_ref[...])
    o_ref[...] = _ple(y, p_ref[...], wpg_ref, bpg_ref, wpp_ref)


def _ffn_ple(x2d, p_all, layer, wg, wu, wd, ln_g, ln_b, wpg, bpg, wpp, *, tile=512, fchunk=1024):
    n, d = x2d.shape
    f = wg.shape[1]
    pd = p_all.shape[2]
    chunks = tuple((lo, min(lo + fchunk, f)) for lo in range(0, f, fchunk))
    row = lambda t: t[None, :]
    return pl.pallas_call(
        functools.partial(_ffn_ple_kernel, chunks=chunks),
        grid=(n // tile,),
        in_specs=[pl.BlockSpec((tile, d), lambda i: (i, 0)),
                  pl.BlockSpec((None, tile, pd), lambda i: (layer, i, 0)),
                  _const_spec((d, f)), _const_spec((d, f)), _const_spec((f, d)),
                  _const_spec((1, d)), _const_spec((1, d)),
                  _const_spec((d, d)), _const_spec((1, d)), _const_spec((pd, d))],
        out_specs=pl.BlockSpec((tile, d), lambda i: (i, 0)),
        out_shape=jax.ShapeDtypeStruct((n, d), F32),
        compiler_params=_params(1),
        name="ffn_ple",
    )(x2d, p_all, wg.astype(BF16), wu.astype(BF16), wd.astype(BF16), row(ln_g), row(ln_b),
      wpg.astype(BF16), row(bpg), wpp.astype(BF16))


def _qkv_kernel(x_ref, w_ref, q_ref, kt_ref, v_ref, *, hd, scale):
    z = _dot(x_ref[...].astype(BF16), w_ref[...])
    q_ref[...] = (z[:, 0:hd] * scale).astype(BF16)
    k_t = z[:, hd:2 * hd].T.astype(BF16)
    heads, dk2, tile = kt_ref.shape
    ones = jnp.ones((tile, dk2), BF16)
    for h in range(heads):
        kt_ref[h] = k_t[h * dk2:(h + 1) * dk2, :]
        v_ref[:, 2 * h * dk2:(2 * h + 1) * dk2] = z[:, 2 * hd + h * dk2:2 * hd + (h + 1) * dk2].astype(BF16)
        v_ref[:, (2 * h + 1) * dk2:(2 * h + 2) * dk2] = ones


def _qkv(x, w_qkv, *, dh, tile):
    bn, s, d = x.shape
    hd = w_qkv.shape[1] // 3
    heads = hd // (2 * dh)
    return pl.pallas_call(
        functools.partial(_qkv_kernel, hd=hd, scale=dh ** -0.5 * LOG2_E),
        grid=(bn, s // tile),
        in_specs=[pl.BlockSpec((None, tile, d), lambda b, i: (b, i, 0)), _const_spec(w_qkv.shape)],
        out_specs=[pl.BlockSpec((None, tile, hd), lambda b, i: (b, i, 0)),
                   pl.BlockSpec((None, heads, None, 2 * dh, tile), lambda b, i: (b, 0, i, 0, 0)),
                   pl.BlockSpec((None, tile, 2 * hd), lambda b, i: (b, i, 0))],
        out_shape=[jax.ShapeDtypeStruct((bn, s, hd), BF16),
                   jax.ShapeDtypeStruct((bn, heads, s // tile, 2 * dh, tile), BF16),
                   jax.ShapeDtypeStruct((bn, s, 2 * hd), BF16)],
        compiler_params=_params(2),
        name="qkv_proj",
    )(x, w_qkv.astype(BF16))


def _diff_attn_kernel(q_ref, kt_ref, v_ref, lam_ref, g_ref, o_ref, m_ref, acc_ref,
                      qm_ref, sa_ref, sb_ref, *, blk, nb, dh, lambda_init):
    pairs = [(qi, ki) for qi in range(nb) for ki in range(qi + 1)]
    bufs = (sa_ref, sb_ref)
    half = blk // 2

    def load_queries(qi):
        q = q_ref[qi * blk:(qi + 1) * blk, :]
        lane = lax.broadcasted_iota(jnp.int32, q.shape, 1)
        zero = jnp.zeros_like(q)
        qm_ref[0] = jnp.where(lane < dh, q, zero)
        qm_ref[1] = jnp.where(lane >= dh, q, zero)

    def scores(ki, s_ref, diagonal):
        kt_blk = kt_ref[ki]
        for c in range(2):
            if diagonal:
                s_ref[c, 0:half, 0:half] = _dot(qm_ref[c, 0:half], kt_blk[:, 0:half])
                s_ref[c, half:blk, :] = _dot(qm_ref[c, half:blk], kt_blk)
            else:
                s_ref[c] = _dot(qm_ref[c], kt_blk)

    def update(c, rows, s, v_part):
        m_prev = m_ref[c, rows]
        m_new = jnp.maximum(m_prev, jnp.max(s, axis=-1, keepdims=True))
        corr = jnp.exp2(m_prev - m_new)
        p = jnp.exp2(s - jnp.concatenate([m_new] * (s.shape[1] // LANES), axis=1))
        acc_ref[c, rows] = (jnp.concatenate([corr, corr], axis=1) * acc_ref[c, rows]
                            + _dot(p.astype(BF16), v_part))
        m_ref[c, rows] = m_new

    def softmax_pv(s_ref, ki, masked):
        v_blk = v_ref[ki * blk:(ki + 1) * blk, :]
        for c in range(2):
            if not masked:
                update(c, slice(0, blk), s_ref[c], v_blk)
                continue
            s_top = s_ref[c, 0:half, 0:half]
            row = lax.broadcasted_iota(jnp.int32, s_top.shape, 0)
            col = lax.broadcasted_iota(jnp.int32, s_top.shape, 1)
            update(c, slice(0, half), jnp.where(col <= row, s_top, -1e30), v_blk[0:half])
            s_bot = s_ref[c, half:blk, :]
            row = lax.broadcasted_iota(jnp.int32, s_bot.shape, 0) + half
            col = lax.broadcasted_iota(jnp.int32, s_bot.shape, 1)
            update(c, slice(half, blk), jnp.where(col <= row, s_bot, -1e30), v_blk)

    def finalize(qi):
        a1 = acc_ref[0, :, 0:2 * dh] * (1.0 / acc_ref[0, :, 2 * dh:4 * dh])
        a2 = acc_ref[1, :, 0:2 * dh] * (1.0 / acc_ref[1, :, 2 * dh:4 * dh])
        lv = lam_ref[...]
        lam = (jnp.exp(jnp.sum(lv[0:1] * lv[1:2], axis=-1, keepdims=True))
               - jnp.exp(jnp.sum(lv[2:3] * lv[3:4], axis=-1, keepdims=True)) + lambda_init)
        o = a1 - lam * a2
        msq = jnp.mean(o * o, axis=-1, keepdims=True)
        o_ref[qi * blk:(qi + 1) * blk, :] = (o * lax.rsqrt(msq + LN_EPS) * g_ref[...]
                                              * (1.0 - lambda_init)).astype(BF16)

    load_queries(0)
    scores(0, bufs[0], True)
    for t, (qi, ki) in enumerate(pairs):
        if t + 1 < len(pairs):
            nqi, nki = pairs[t + 1]
            if nki == 0:
                load_queries(nqi)
            scores(nki, bufs[(t + 1) % 2], nki == nqi)
        if ki == 0:
            m_ref[...] = jnp.full_like(m_ref, -1e30)
            acc_ref[...] = jnp.zeros_like(acc_ref)
        softmax_pv(bufs[t % 2], ki, masked=(ki == qi))
        if ki == qi:
            finalize(qi)


def _diff_attn(q, kt, v_ext, lam_q1, lam_k1, lam_q2, lam_k2, subln_g, lambda_init, *, blk):
    bn, s, hd = q.shape
    dh = lam_q1.shape[0]
    heads = hd // (2 * dh)
    nb = s // blk
    lamv = jnp.zeros((8, LANES), F32).at[0:4, 0:dh].set(jnp.stack([lam_q1, lam_k1, lam_q2, lam_k2]))
    return pl.pallas_call(
        functools.partial(_diff_attn_kernel, blk=blk, nb=nb, dh=dh, lambda_init=lambda_init),
        grid=(bn, heads),
        in_specs=[
            pl.BlockSpec((None, s, 2 * dh), lambda b, h: (b, 0, h)),
            pl.BlockSpec((None, None, nb, 2 * dh, blk), lambda b, h: (b, h, 0, 0, 0)),
            pl.BlockSpec((None, s, 4 * dh), lambda b, h: (b, 0, h)),
            pl.BlockSpec((8, LANES), lambda b, h: (0, 0)),
            pl.BlockSpec((1, 2 * dh), lambda b, h: (0, 0)),
        ],
        out_specs=pl.BlockSpec((None, s, 2 * dh), lambda b, h: (b, 0, h)),
        out_shape=jax.ShapeDtypeStruct((bn, s, hd), BF16),
        scratch_shapes=[pltpu.VMEM((2, blk, LANES), F32),
                        pltpu.VMEM((2, blk, 4 * dh), F32), pltpu.VMEM((2, blk, 2 * dh), BF16),
                        pltpu.VMEM((2, blk, blk), F32), pltpu.VMEM((2, blk, blk), F32)],
        compiler_params=_params(2),
        name="diff_attn",
    )(q, kt, v_ext, lamv, subln_g[None, :])


def _attn_out_kernel(o_ref, x_ref, w_ref, ln_g_ref, ln_b_ref, wr_ref, lstrict_ref,
                     y_ref, route_ref, cnt_ref, base_ref):
    @pl.when(pl.program_id(0) == 0)
    def _():
        base_ref[...] = jnp.zeros_like(base_ref)

    m = _dot(o_ref[...], w_ref[...])
    y = _layer_norm(DEEPNORM_ALPHA * x_ref[...] + m, ln_g_ref[...], ln_b_ref[...])
    y_ref[...] = y
    y_hi = y.astype(BF16)
    y_lo = (y - y_hi.astype(F32)).astype(BF16)
    hi_terms = _dot(y_hi, wr_ref[...])
    logits = hi_terms[:, 0:LANES] + hi_terms[:, LANES:2 * LANES] + _dot(y_lo, wr_ref[:, 0:LANES])

    lane = lax.broadcasted_iota(jnp.int32, logits.shape, 1).astype(F32)
    neg = -jnp.inf
    lg = jnp.where(lane < N_EXPERTS, logits, neg)
    v0 = jnp.max(lg, axis=-1, keepdims=True)
    i0 = jnp.min(jnp.where(lg == v0, lane, float(LANES)), axis=-1, keepdims=True)
    lg2 = jnp.where(lane == i0, neg, lg)
    v1 = jnp.max(lg2, axis=-1, keepdims=True)
    i1 = jnp.min(jnp.where(lg2 == v1, lane, float(LANES)), axis=-1, keepdims=True)
    e = jnp.exp(v1 - v0)
    g0 = 1.0 / (1.0 + e)
    g1 = e / (1.0 + e)
    oh0 = lane == i0
    oh1 = lane == i1
    c = jnp.where(oh0, 1.0, 0.0) + jnp.where(oh1, 1.0, 0.0)
    pre = _dot(lstrict_ref[...], c.astype(BF16)) + base_ref[...]
    r0 = jnp.sum(jnp.where(oh0, pre, 0.0), axis=-1, keepdims=True)
    r1 = jnp.sum(jnp.where(oh1, pre, 0.0), axis=-1, keepdims=True)
    base_ref[...] = base_ref[...] + jnp.sum(c, axis=0, keepdims=True)
    cnt_ref[...] = jnp.broadcast_to(base_ref[...], cnt_ref.shape)
    fields = (i0, i1, r0, r1, g0, g1)
    route = jnp.zeros_like(logits)
    for idx, val in enumerate(fields):
        route = jnp.where(lane == idx, val, route)
    route_ref[...] = route


def _attn_out(o2d, x2d, w_out, ln_g, ln_b, w_router, *, tile=1024):
    n, d = x2d.shape
    wr = jnp.pad(w_router, ((0, 0), (0, LANES - w_router.shape[1])))
    wr_hi = wr.astype(BF16)
    wr_lo = (wr - wr_hi.astype(F32)).astype(BF16)
    wr_split = jnp.concatenate([wr_hi, wr_lo], axis=1)
    lstrict = (jnp.arange(tile)[:, None] > jnp.arange(tile)[None, :]).astype(BF16)
    row = lambda t: t[None, :]
    tspec = pl.BlockSpec((tile, d), lambda i: (i, 0))
    return pl.pallas_call(
        _attn_out_kernel,
        grid=(n // tile,),
        in_specs=[tspec, tspec, _const_spec((d, d)), _const_spec((1, d)), _const_spec((1, d)),
                  _const_spec((d, 2 * LANES)), _const_spec((tile, tile))],
        out_specs=[tspec, pl.BlockSpec((tile, LANES), lambda i: (i, 0)),
                   pl.BlockSpec((8, LANES), lambda i: (0, 0))],
        out_shape=[jax.ShapeDtypeStruct((n, d), F32), jax.ShapeDtypeStruct((n, LANES), F32),
                   jax.ShapeDtypeStruct((8, LANES), F32)],
        scratch_shapes=[pltpu.VMEM((1, LANES), F32)],
        compiler_params=_params(1),
        name="attn_out_router",
    )(o2d, x2d, w_out.astype(BF16), row(ln_g), row(ln_b), wr_split, lstrict)


def _route_tables(route, cnt, *, tm):
    n = route.shape[0]
    counts = cnt[0, :N_EXPERTS].astype(jnp.int32)
    padded = (counts + tm - 1) // tm * tm
    pend = jnp.cumsum(padded)
    pstart = pend - padded

    def dest_of(k):
        e = route[:, k].astype(jnp.int32)
        start = jnp.sum(jnp.where(e[:, None] == jnp.arange(N_EXPERTS)[None, :], pstart[None, :], 0), axis=-1)
        return start + route[:, TOP_K + k].astype(jnp.int32)

    dest = (dest_of(0), dest_of(1))
    n_rows = n * TOP_K + N_EXPERTS * tm
    n_blocks = n_rows // tm
    n_used = pend[-1] // tm
    blk_idx = jnp.arange(n_blocks, dtype=jnp.int32)
    blk_start = jnp.minimum(blk_idx, n_used - 1) * tm
    blk_e = jnp.minimum(jnp.sum(blk_start[:, None] >= pend[None, :], axis=1), N_EXPERTS - 1)
    valid = jnp.clip(pstart[blk_e] + counts[blk_e] - blk_idx * tm, 0, tm)
    return (dest, blk_e.astype(jnp.int32), n_used.reshape(1).astype(jnp.int32),
            valid.astype(jnp.int32), n_rows)


SC_CORES = 2
SC_SUBCORES = 16
SC_WORKERS = SC_CORES * SC_SUBCORES
SC_ROWS = 32


def _sc_mesh():
    return plsc.VectorSubcoreMesh(core_axis_name="c", subcore_axis_name="s",
                                  num_cores=SC_CORES, num_subcores=SC_SUBCORES)


def _sc_worker_id():
    return lax.axis_index("s") * SC_CORES + lax.axis_index("c")


def _sc_dispatch(x2d, dest, n_rows):
    n, d = x2d.shape
    per_w = n // SC_WORKERS
    n_chunks = per_w // SC_ROWS
    assert n == SC_WORKERS * n_chunks * SC_ROWS and n_chunks % 2 == 0
    d0 = dest[0].reshape(SC_WORKERS, n_chunks, SC_ROWS)
    d1 = dest[1].reshape(SC_WORKERS, n_chunks, SC_ROWS)

    def body(x_hbm, d0_hbm, d1_hbm, xs_hbm, d0_v, d1_v, rows_v, rsem, s0sem, s1sem):
        wid = _sc_worker_id()
        base = wid * per_w
        pltpu.sync_copy(d0_hbm.at[wid], d0_v)
        pltpu.sync_copy(d1_hbm.at[wid], d1_v)

        def read(c, slot):
            return pltpu.make_async_copy(x_hbm.at[pl.ds(base + c * SC_ROWS, SC_ROWS)],
                                         rows_v.at[slot], rsem.at[slot])

        def scat(idx_v, sem, c, slot):
            return pltpu.make_async_copy(rows_v.at[slot], xs_hbm.at[idx_v.at[c]], sem.at[slot])

        def start_scatters(c, slot):
            scat(d0_v, s0sem, c, slot).start()
            scat(d1_v, s1sem, c, slot).start()

        def wait_scatters(c, slot):
            scat(d0_v, s0sem, c, slot).wait()
            scat(d1_v, s1sem, c, slot).wait()

        read(0, 0).start()

        def pair(j, carry):
            c0 = 2 * j
            read(c0, 0).wait()

            @pl.when(j > 0)
            def _():
                wait_scatters(c0 - 1, 1)

            read(c0 + 1, 1).start()
            start_scatters(c0, 0)
            read(c0 + 1, 1).wait()
            wait_scatters(c0, 0)

            @pl.when(c0 + 2 < n_chunks)
            def _():
                read(c0 + 2, 0).start()

            start_scatters(c0 + 1, 1)
            return carry

        lax.fori_loop(0, n_chunks // 2, pair, 0)
        wait_scatters(n_chunks - 1, 1)

    return pl.kernel(
        body,
        out_type=jax.ShapeDtypeStruct((n_rows, d), x2d.dtype),
        mesh=_sc_mesh(),
        scratch_types=[pltpu.VMEM((n_chunks, SC_ROWS), jnp.int32), pltpu.VMEM((n_chunks, SC_ROWS), jnp.int32),
                       pltpu.VMEM((2, SC_ROWS, d), x2d.dtype), pltpu.SemaphoreType.DMA((2,)),
                       pltpu.SemaphoreType.DMA((2,)), pltpu.SemaphoreType.DMA((2,))],
        name="sc_dispatch",
    )(x2d, d0, d1)


def _sc_gather(table, idx):
    _, d = table.shape
    b = idx.shape[0]
    per_w = b // SC_WORKERS
    n_chunks = per_w // SC_ROWS
    assert b == SC_WORKERS * n_chunks * SC_ROWS and n_chunks % 2 == 0
    idx3 = idx.reshape(SC_WORKERS, n_chunks, SC_ROWS)

    def body(table_hbm, idx_hbm, out_hbm, idx_v, rows_v, gsem, wsem):
        wid = _sc_worker_id()
        base = wid * per_w
        pltpu.sync_copy(idx_hbm.at[wid], idx_v)

        def gather(c, slot):
            return pltpu.make_async_copy(table_hbm.at[idx_v.at[c]], rows_v.at[slot], gsem.at[slot])

        def write(c, slot):
            return pltpu.make_async_copy(rows_v.at[slot],
                                         out_hbm.at[pl.ds(base + c * SC_ROWS, SC_ROWS)], wsem.at[slot])

        gather(0, 0).start()

        def pair(j, carry):
            c0 = 2 * j
            gather(c0, 0).wait()

            @pl.when(j > 0)
            def _():
                write(c0 - 1, 1).wait()

            gather(c0 + 1, 1).start()
            write(c0, 0).start()
            gather(c0 + 1, 1).wait()
            write(c0, 0).wait()

            @pl.when(c0 + 2 < n_chunks)
            def _():
                gather(c0 + 2, 0).start()

            write(c0 + 1, 1).start()
            return carry

        lax.fori_loop(0, n_chunks // 2, pair, 0)
        write(n_chunks - 1, 1).wait()

    return pl.kernel(
        body,
        out_type=jax.ShapeDtypeStruct((b, d), table.dtype),
        mesh=_sc_mesh(),
        scratch_types=[pltpu.VMEM((n_chunks, SC_ROWS), jnp.int32), pltpu.VMEM((2, SC_ROWS, d), table.dtype),
                       pltpu.SemaphoreType.DMA((2,)), pltpu.SemaphoreType.DMA((2,))],
        name="sc_combine_gather",
    )(table, idx3)


def _moe_kernel(blk_e, n_used, valid, xs_ref, wg_ref, wu_ref, wd_ref, y_ref, *, chunks):
    m = pl.program_id(0)
    f = pl.program_id(1)

    @pl.when(f == 0)
    def _():
        y_ref[...] = jnp.zeros_like(y_ref)

    @pl.when(m < n_used[0])
    def _():
        row = lax.broadcasted_iota(jnp.int32, xs_ref.shape, 0)
        x = jnp.where(row < valid[m], xs_ref[...], 0.0).astype(BF16)
        for lo, hi in chunks:
            g = _dot(x, wg_ref[:, lo:hi])
            u = _dot(x, wu_ref[:, lo:hi])
            h = (g * _sigmoid(g) * u).astype(BF16)
            y_ref[...] += _dot(h, wd_ref[lo:hi, :])


def _moe_grouped(xs, blk_e, n_used, valid, wg, wu, wd, *, tm, tf, fchunk=1024):
    n_rows, d = xs.shape
    fdim = wg.shape[2]
    nf = fdim // tf
    assert nf * tf == fdim
    n_blocks = n_rows // tm
    chunks = tuple((lo, min(lo + fchunk, tf)) for lo in range(0, tf, fchunk))

    def f_eff(m, f, nu):
        mm = jnp.minimum(m, nu[0] - 1)
        snake = jnp.where(mm % 2 == 0, f, nf - 1 - f)
        return jnp.where(m < nu[0], snake, jnp.where(mm % 2 == 0, nf - 1, 0))

    grid_spec = pltpu.PrefetchScalarGridSpec(
        num_scalar_prefetch=3,
        grid=(n_blocks, nf),
        in_specs=[
            pl.BlockSpec((tm, d), lambda m, f, be, nu, va: (jnp.minimum(m, nu[0] - 1), 0)),
            pl.BlockSpec((None, d, tf), lambda m, f, be, nu, va: (be[m], 0, f_eff(m, f, nu))),
            pl.BlockSpec((None, d, tf), lambda m, f, be, nu, va: (be[m], 0, f_eff(m, f, nu))),
            pl.BlockSpec((None, tf, d), lambda m, f, be, nu, va: (be[m], f_eff(m, f, nu), 0)),
        ],
        out_specs=pl.BlockSpec((tm, d), lambda m, f, be, nu, va: (m, 0)),
    )
    return pl.pallas_call(
        functools.partial(_moe_kernel, chunks=chunks),
        grid_spec=grid_spec,
        out_shape=jax.ShapeDtypeStruct((n_rows, d), F32),
        compiler_params=_params(2),
        name="moe_grouped",
    )(blk_e, n_used, valid, xs, wg, wu, wd)


def _final_kernel(x_ref, y0_ref, y1_ref, route_ref, p_ref, ln_g_ref, ln_b_ref, wpg_ref, bpg_ref,
                  wpp_ref, *rest):
    o_ref = rest[-1]
    route = route_ref[...]
    f = route[:, 4:5] * y0_ref[...] + route[:, 5:6] * y1_ref[...]
    y = _layer_norm(DEEPNORM_ALPHA * x_ref[...] + f, ln_g_ref[...], ln_b_ref[...])
    o_ref[...] = _ple(y, p_ref[...], wpg_ref, bpg_ref, wpp_ref)


def _final_part(x2d, y01, route, p_all, layer, ln_g, ln_b, wpg, bpg, wpp, *, part, n_parts, prev, tile):
    n, d = x2d.shape
    pd = p_all.shape[2]
    nblk = n // n_parts // tile
    off = part * nblk
    row = lambda t: t[None, :]
    tok = lambda width: pl.BlockSpec((tile, width), lambda i: (i + off, 0))
    in_specs = [tok(d), pl.BlockSpec((tile, d), lambda i: (i, 0)),
                pl.BlockSpec((tile, d), lambda i: (i + nblk, 0)), tok(LANES),
                pl.BlockSpec((None, tile, pd), lambda i: (layer, i + off, 0)),
                _const_spec((1, d)), _const_spec((1, d)),
                _const_spec((d, d)), _const_spec((1, d)), _const_spec((pd, d))]
    args = [x2d, y01, y01, route, p_all, row(ln_g), row(ln_b), wpg.astype(BF16), row(bpg), wpp.astype(BF16)]
    aliases = {}
    if prev is not None:
        in_specs.append(pl.BlockSpec(memory_space=pl.ANY))
        args.append(prev)
        aliases = {len(args) - 1: 0}
    return pl.pallas_call(
        _final_kernel,
        grid=(nblk,),
        in_specs=in_specs,
        out_specs=tok(d),
        out_shape=jax.ShapeDtypeStruct((n, d), F32),
        input_output_aliases=aliases,
        compiler_params=_params(1),
        name="moe_combine_ple",
    )(*args)


def _combine(x2d, y, dest, route, p_all, layer, ln_g, ln_b, wpg, bpg, wpp, *, n_parts=2, tile=1024):
    n = x2d.shape[0]
    step = n // n_parts
    tile = min(tile, step)
    out = None
    for part in range(n_parts):
        rows = slice(part * step, (part + 1) * step)
        y01 = _sc_gather(y, jnp.concatenate([dest[0][rows], dest[1][rows]]))
        out = _final_part(x2d, y01, route, p_all, layer, ln_g, ln_b, wpg, bpg, wpp,
                          part=part, n_parts=n_parts, prev=out, tile=tile)
    return out


def _even_layer(x, p_all, layer_idx, w_in, a_ln_g, a_ln_b, a_ws, a_bs, w_gate_up, b_gate, norm_g, w_out,
                ln1_g, ln1_b, wg, wu, wd, ln2_g, ln2_b, wpp, wpg, bpg):
    bn, s, d = x.shape
    x1 = _even_mixer(x, w_in, a_ln_g, a_ln_b, a_ws, a_bs, w_gate_up, b_gate, norm_g, w_out, ln1_g, ln1_b)
    x2 = _ffn_ple(x1.reshape(bn * s, d), p_all, layer_idx, wg, wu, wd, ln2_g, ln2_b, wpg, bpg, wpp)
    return x2.reshape(bn, s, d)


def _odd_layer(x, p_all, layer_idx, w_qkv, lam_q1, lam_k1, lam_q2, lam_k2, subln_g, w_out, ln1_g, ln1_b,
               w_router, ewg, ewu, ewd, ln2_g, ln2_b, wpp, wpg, bpg, *, attn_blk=1024, moe_tm=512,
               moe_tf=1792):
    bn, s, d = x.shape
    n = bn * s
    lambda_init = 0.8 - 0.6 * math.exp(-0.3 * layer_idx)
    q, kt, v_ext = _qkv(x, w_qkv, dh=lam_q1.shape[0], tile=attn_blk)
    o = _diff_attn(q, kt, v_ext, lam_q1, lam_k1, lam_q2, lam_k2, subln_g, lambda_init, blk=attn_blk)
    x3, route, cnt = _attn_out(o.reshape(n, d), x.reshape(n, d), w_out, ln1_g, ln1_b, w_router)
    dest, blk_e, n_used, valid, n_rows = _route_tables(route, cnt, tm=moe_tm)
    xs = _sc_dispatch(x3, dest, n_rows)
    y = _moe_grouped(xs, blk_e, n_used, valid, ewg.astype(BF16), ewu.astype(BF16), ewd.astype(BF16),
                     tm=moe_tm, tf=moe_tf)
    out = _combine(x3, y, dest, route, p_all, layer_idx, ln2_g, ln2_b, wpg, bpg, wpp)
    return out.reshape(bn, s, d)


def kernel(x, p, e_w_in, e_a_ln_g, e_a_ln_b, e_a_ws, e_a_bs, e_b_w_gate_up, e_b_b_gate, e_b_norm_g, e_w_out, e_ln1_g, e_ln1_b, e_ffn_wg, e_ffn_wu, e_ffn_wd, e_ln2_g, e_ln2_b, o_w_qkv, o_lam_q1, o_lam_k1, o_lam_q2, o_lam_k2, o_subln_g, o_w_out, o_ln1_g, o_ln1_b, o_router, o_exp_wg, o_exp_wu, o_exp_wd, o_ln2_g, o_ln2_b, ple_w_proj, ple_w_gate, ple_b_gate):
    p_all = p.reshape(p.shape[0], -1, p.shape[-1])
    for i in range(DEPTH):
        j = i // 2
        if i % 2 == 0:
            x = _even_layer(x, p_all, i, e_w_in[j], e_a_ln_g[j], e_a_ln_b[j], e_a_ws[j], e_a_bs[j],
                            e_b_w_gate_up[j], e_b_b_gate[j], e_b_norm_g[j], e_w_out[j],
                            e_ln1_g[j], e_ln1_b[j], e_ffn_wg[j], e_ffn_wu[j], e_ffn_wd[j],
                            e_ln2_g[j], e_ln2_b[j], ple_w_proj[i], ple_w_gate[i], ple_b_gate[i])
        else:
            x = _odd_layer(x, p_all, i, o_w_qkv[j], o_lam_q1[j], o_lam_k1[j], o_lam_q2[j], o_lam_k2[j],
                           o_subln_g[j], o_w_out[j], o_ln1_g[j], o_ln1_b[j], o_router[j],
                           o_exp_wg[j], o_exp_wu[j], o_exp_wd[j], o_ln2_g[j], o_ln2_b[j],
                           ple_w_proj[i], ple_w_gate[i], ple_b_gate[i])
    return x
```

```python
import functools
import math

import jax
import jax.numpy as jnp
from jax import lax
from jax.experimental import pallas as pl
from jax.experimental.pallas import tpu as pltpu
from jax.experimental.pallas import tpu_sc as plsc

F32 = jnp.float32
BF16 = jnp.bfloat16

DEPTH = 2
DEEPNORM_ALPHA = (2.0 * DEPTH) ** 0.25
LN_EPS = 1e-5
A_CHUNK = 128
A_GROUPS = 8
B_HEADS = 4
B_CHUNK = 64
B_TAU = 16.0
CS_ROWS = 256
C_HEADS = 8
N_EXPERTS = 8
TOP_K = 2
LANES = 128
LOG2_E = 1.4426950408889634
VMEM_LIMIT = 56 * 1024 * 1024

NT_DIMS = (((1,), (1,)), ((), ()))
TN_DIMS = (((0,), (0,)), ((), ()))


def _dot(a, b):
    return jnp.dot(a, b, preferred_element_type=F32)


def _layer_norm(x, g, b):
    mu = jnp.mean(x, axis=-1, keepdims=True)
    xc = x - mu
    var = jnp.mean(xc * xc, axis=-1, keepdims=True)
    return xc * lax.rsqrt(var + LN_EPS) * g + b


def _sigmoid(x):
    return 1.0 / (1.0 + jnp.exp(-x))


def _split3(a):
    hi = a.astype(BF16)
    r1 = a - hi.astype(F32)
    mid = r1.astype(BF16)
    lo = (r1 - mid.astype(F32)).astype(BF16)
    return hi, mid, lo


def _const_spec(shape):
    zeros = (0,) * len(shape)
    return pl.BlockSpec(shape, lambda *_: zeros, pipeline_mode=pl.Buffered(1))


def _params(n_axes):
    return pltpu.CompilerParams(dimension_semantics=("arbitrary",) * n_axes,
                                vmem_limit_bytes=VMEM_LIMIT)


def _even_mixer_kernel(x_ref, w_in_ref, a_g_ref, a_b_ref, wcat_ref, abias_ref, mstack_ref,
                       w_up_ref, b_gate_ref, ng_ref, w_out_ref, ln_g_ref, ln_b_ref,
                       o_ref, state_ref, *, tile, aw, dkh, dvh):
    @pl.when(pl.program_id(1) == 0)
    def _():
        state_ref[...] = jnp.zeros_like(state_ref)

    hk = B_HEADS * dkh
    hv = B_HEADS * dvh
    x = x_ref[...]
    z = _dot(x.astype(BF16), w_in_ref[...])

    u = jax.nn.gelu(z[:, 0:aw])
    v = _layer_norm(jax.nn.gelu(z[:, aw:2 * aw]), a_g_ref[...], a_b_ref[...])
    gd = aw // A_GROUPS
    rows = lax.broadcasted_iota(jnp.int32, (A_CHUNK, A_GROUPS * A_CHUNK), 0)
    cols = lax.broadcasted_iota(jnp.int32, (A_CHUNK, A_GROUPS * A_CHUNK), 1)
    wcat = jnp.where((cols % A_CHUNK) <= rows, wcat_ref[...], 0.0).astype(BF16)
    r_bd = lax.broadcasted_iota(jnp.int32, (A_GROUPS * A_CHUNK, aw), 0) // A_CHUNK
    c_bd = lax.broadcasted_iota(jnp.int32, (A_GROUPS * A_CHUNK, aw), 1) // gd
    mask_bd = r_bd == c_bd
    ya_parts = []
    for c in range(tile // A_CHUNK):
        sl = slice(c * A_CHUNK, (c + 1) * A_CHUNK)
        v_rep = jnp.concatenate([v[sl]] * A_GROUPS, axis=0)
        v_bd = jnp.where(mask_bd, v_rep, 0.0).astype(BF16)
        sg = _dot(wcat, v_bd) + abias_ref[...]
        ya_parts.append(u[sl] * sg)
    ya = jnp.concatenate(ya_parts, axis=0)

    o0 = 2 * aw
    q = z[:, o0:o0 + hk] * (dkh ** -0.5)
    k = z[:, o0 + hk:o0 + 2 * hk]
    vv = z[:, o0 + 2 * hk:o0 + 2 * hk + hv]
    r = z[:, o0 + 2 * hk + hv:o0 + 2 * hk + 2 * hv]
    g_low = z[:, o0 + 2 * hk + 2 * hv:]
    pre = _dot(g_low.astype(BF16), w_up_ref[...]) + b_gate_ref[...]
    log_a = (jnp.minimum(pre, 0.0) - jnp.log1p(jnp.exp(-jnp.abs(pre)))) * (1.0 / B_TAU)
    la_hi, la_mid, la_lo = _split3(log_a)
    ms = mstack_ref[...]
    b_parts, mid_parts, last_parts = [], [], []
    for t in range(tile // CS_ROWS):
        rs = slice(t * CS_ROWS, (t + 1) * CS_ROWS)
        cs = _dot(ms, la_hi[rs]) + _dot(ms, la_mid[rs]) + _dot(ms, la_lo[rs])
        b_parts.append(cs[0:CS_ROWS])
        mid_parts.append(cs[CS_ROWS:2 * CS_ROWS])
        last_parts.append(cs[2 * CS_ROWS:3 * CS_ROWS])
    b_cum = jnp.concatenate(b_parts, axis=0)
    d_mid = jnp.concatenate(mid_parts, axis=0)
    d_last = jnp.concatenate(last_parts, axis=0)
    qe = (q * jnp.exp(d_mid)).astype(BF16)
    ke = k * jnp.exp(-d_mid)
    kd = (k * jnp.exp(d_last)).astype(BF16)
    qb = (q * jnp.exp(b_cum)).astype(BF16)
    dec = jnp.exp(b_cum + d_last)
    vvb = vv.astype(BF16)

    mask_kk = (lax.broadcasted_iota(jnp.int32, (hk, hk), 0) // dkh
               == lax.broadcasted_iota(jnp.int32, (hk, hk), 1) // dkh)
    mask_vbd = (lax.broadcasted_iota(jnp.int32, (hk, hv), 0) // dkh
                == lax.broadcasted_iota(jnp.int32, (hk, hv), 1) // dvh)
    mask_st = (lax.broadcasted_iota(jnp.int32, (hv, hk), 0) // dvh
               == lax.broadcasted_iota(jnp.int32, (hv, hk), 1) // dkh)
    causal = ((lax.broadcasted_iota(jnp.int32, (B_CHUNK, hk), 1) % B_CHUNK)
              <= lax.broadcasted_iota(jnp.int32, (B_CHUNK, hk), 0))

    st = state_ref[...]
    o_parts = []
    for c in range(tile // B_CHUNK):
        sl = slice(c * B_CHUNK, (c + 1) * B_CHUNK)
        ke_bd = jnp.where(mask_kk, jnp.concatenate([ke[sl]] * B_HEADS, axis=0), 0.0).astype(BF16)
        s_cat = lax.dot_general(qe[sl], ke_bd, NT_DIMS, preferred_element_type=F32)
        s_cat = jnp.where(causal, s_cat, 0.0).astype(BF16)
        v_bd = jnp.where(mask_vbd, jnp.concatenate([vv[sl]] * B_HEADS, axis=0), 0.0).astype(BF16)
        o_c = _dot(s_cat, v_bd) + lax.dot_general(qb[sl], st.astype(BF16), NT_DIMS,
                                                   preferred_element_type=F32)
        kv_t = lax.dot_general(vvb[sl], kd[sl], TN_DIMS, preferred_element_type=F32)
        st = dec[c * B_CHUNK:c * B_CHUNK + 1] * st + jnp.where(mask_st, kv_t, 0.0)
        o_parts.append(o_c)
    state_ref[...] = st
    o = jnp.concatenate(o_parts, axis=0)

    yb_parts = []
    for h in range(B_HEADS):
        oh = o[:, h * dvh:(h + 1) * dvh]
        msq = jnp.mean(oh * oh, axis=-1, keepdims=True)
        yb_parts.append(oh * lax.rsqrt(msq + LN_EPS))
    yb = jnp.concatenate(yb_parts, axis=1) * ng_ref[...] * (r * _sigmoid(r))

    y_cat = jnp.concatenate([ya, yb], axis=1).astype(BF16)
    m = _dot(y_cat, w_out_ref[...])
    o_ref[...] = _layer_norm(DEEPNORM_ALPHA * x + m, ln_g_ref[...], ln_b_ref[...])


def _gla_cumsum_matrices(tile):
    i = jnp.arange(tile)[:, None]
    j = jnp.arange(tile)[None, :]
    same = (i // B_CHUNK) == (j // B_CHUNK)
    m_cum = same & (j <= i)
    m_mid = same & (j <= (i // B_CHUNK) * B_CHUNK + B_CHUNK // 2 - 1)
    m_last = same
    f = lambda t: t.astype(F32)
    return jnp.concatenate([f(m_cum), f(m_cum) - f(m_mid), f(m_last) - f(m_cum)], axis=0).astype(BF16)


def _even_mixer(x, w_in, a_ln_g, a_ln_b, a_ws, a_bs, w_gate_up, b_gate, norm_g, w_out, ln_g, ln_b,
                *, tile=1024):
    bn, s, d = x.shape
    aw = a_ln_g.shape[0]
    hk = w_gate_up.shape[1]
    dkh = hk // B_HEADS
    dvh = norm_g.shape[0]
    hv = B_HEADS * dvh
    rank = w_gate_up.shape[0]
    main = 2 * aw + 2 * hk + 2 * hv
    gd = aw // A_GROUPS
    w_in_p = jnp.concatenate([w_in[:, :main], jnp.pad(w_in[:, main:], ((0, 0), (0, LANES - rank)))],
                             axis=1).astype(BF16)
    w_up_p = jnp.pad(w_gate_up, ((0, LANES - rank), (0, 0))).astype(BF16)
    wcat = jnp.transpose(a_ws, (1, 0, 2)).reshape(A_CHUNK, A_GROUPS * A_CHUNK)
    abias = jnp.repeat(a_bs.T, gd, axis=1)
    mstack = _gla_cumsum_matrices(CS_ROWS)
    ng = jnp.tile(norm_g, B_HEADS)[None, :]
    row = lambda t: t[None, :]
    kern = functools.partial(_even_mixer_kernel, tile=tile, aw=aw, dkh=dkh, dvh=dvh)
    tile_spec = pl.BlockSpec((None, tile, d), lambda b, i: (b, i, 0))
    return pl.pallas_call(
        kern,
        grid=(bn, s // tile),
        in_specs=[tile_spec, _const_spec(w_in_p.shape), _const_spec((1, aw)), _const_spec((1, aw)),
                  _const_spec(wcat.shape), _const_spec(abias.shape), _const_spec(mstack.shape),
                  _const_spec(w_up_p.shape), _const_spec((1, hk)), _const_spec((1, hv)),
                  _const_spec(w_out.shape), _const_spec((1, d)), _const_spec((1, d))],
        out_specs=tile_spec,
        out_shape=jax.ShapeDtypeStruct(x.shape, F32),
        scratch_shapes=[pltpu.VMEM((hv, hk), F32)],
        compiler_params=_params(2),
        name="even_mixer",
    )(x, w_in_p, row(a_ln_g), row(a_ln_b), wcat, abias, mstack, w_up_p, row(b_gate), ng,
      w_out.astype(BF16), row(ln_g), row(ln_b))


def _ple(y, p, wpg_ref, bpg_ref, wpp_ref):
    gate = _sigmoid(_dot(y.astype(BF16), wpg_ref[...]) + bpg_ref[...])
    return y + gate * _dot(p.astype(BF16), wpp_ref[...])


def _ffn_ple_kernel(x_ref, p_ref, wg_ref, wu_ref, wd_ref, ln_g_ref, ln_b_ref, wpg_ref, bpg_ref,
                    wpp_ref, o_ref, *, chunks):
    x = x_ref[...]
    xb = x.astype(BF16)
    acc = None
    for lo, hi in chunks:
        g = _dot(xb, wg_ref[:, lo:hi])
        u = _dot(xb, wu_ref[:, lo:hi])
        h = (g * _sigmoid(g) * u).astype(BF16)
        part = _dot(h, wd_ref[lo:hi, :])
        acc = part if acc is None else acc + part
    y = _layer_norm(DEEPNORM_ALPHA * x + acc, ln_g_ref[...], ln_b_ref[...])
    o_ref[...] = _ple(y, p_ref[...], wpg_ref, bpg_ref, wpp_ref)


def _ffn_ple(x2d, p_all, layer, wg, wu, wd, ln_g, ln_b, wpg, bpg, wpp, *, tile=512, fchunk=1024):
    n, d = x2d.shape
    f = wg.shape[1]
    pd = p_all.shape[2]
    chunks = tuple((lo, min(lo + fchunk, f)) for lo in range(0, f, fchunk))
    row = lambda t: t[None, :]
    return pl.pallas_call(
        functools.partial(_ffn_ple_kernel, chunks=chunks),
        grid=(n // tile,),
        in_specs=[pl.BlockSpec((tile, d), lambda i: (i, 0)),
                  pl.BlockSpec((None, tile, pd), lambda i: (layer, i, 0)),
                  _const_spec((d, f)), _const_spec((d, f)), _const_spec((f, d)),
                  _const_spec((1, d)), _const_spec((1, d)),
                  _const_spec((d, d)), _const_spec((1, d)), _const_spec((pd, d))],
        out_specs=pl.BlockSpec((tile, d), lambda i: (i, 0)),
        out_shape=jax.ShapeDtypeStruct((n, d), F32),
        compiler_params=_params(1),
        name="ffn_ple",
    )(x2d, p_all, wg.astype(BF16), wu.astype(BF16), wd.astype(BF16), row(ln_g), row(ln_b),
      wpg.astype(BF16), row(bpg), wpp.astype(BF16))


def _qkv_kernel(x_ref, w_ref, q_ref, kt_ref, v_ref, *, hd, scale):
    z = _dot(x_ref[...].astype(BF16), w_ref[...])
    q_ref[...] = (z[:, 0:hd] * scale).astype(BF16)
    k_t = z[:, hd:2 * hd].T.astype(BF16)
    heads, dk2, tile = kt_ref.shape
    ones = jnp.ones((tile, dk2), BF16)
    for h in range(heads):
        kt_ref[h] = k_t[h * dk2:(h + 1) * dk2, :]
        v_ref[:, 2 * h * dk2:(2 * h + 1) * dk2] = z[:, 2 * hd + h * dk2:2 * hd + (h + 1) * dk2].astype(BF16)
        v_ref[:, (2 * h + 1) * dk2:(2 * h + 2) * dk2] = ones


def _qkv(x, w_qkv, *, dh, tile):
    bn, s, d = x.shape
    hd = w_qkv.shape[1] // 3
    heads = hd // (2 * dh)
    return pl.pallas_call(
        functools.partial(_qkv_kernel, hd=hd, scale=dh ** -0.5 * LOG2_E),
        grid=(bn, s // tile),
        in_specs=[pl.BlockSpec((None, tile, d), lambda b, i: (b, i, 0)), _const_spec(w_qkv.shape)],
        out_specs=[pl.BlockSpec((None, tile, hd), lambda b, i: (b, i, 0)),
                   pl.BlockSpec((None, heads, None, 2 * dh, tile), lambda b, i: (b, 0, i, 0, 0)),
                   pl.BlockSpec((None, tile, 2 * hd), lambda b, i: (b, i, 0))],
        out_shape=[jax.ShapeDtypeStruct((bn, s, hd), BF16),
                   jax.ShapeDtypeStruct((bn, heads, s // tile, 2 * dh, tile), BF16),
                   jax.ShapeDtypeStruct((bn, s, 2 * hd), BF16)],
        compiler_params=_params(2),
        name="qkv_proj",
    )(x, w_qkv.astype(BF16))


def _diff_attn_kernel(q_ref, kt_ref, v_ref, lam_ref, g_ref, o_ref, m_ref, acc_ref,
                      qm_ref, sa_ref, sb_ref, *, blk, nb, dh, lambda_init):
    pairs = [(qi, ki) for qi in range(nb) for ki in range(qi + 1)]
    bufs = (sa_ref, sb_ref)
    half = blk // 2

    def load_queries(qi):
        q = q_ref[qi * blk:(qi + 1) * blk, :]
        lane = lax.broadcasted_iota(jnp.int32, q.shape, 1)
        zero = jnp.zeros_like(q)
        qm_ref[0] = jnp.where(lane < dh, q, zero)
        qm_ref[1] = jnp.where(lane >= dh, q, zero)

    def scores(ki, s_ref, diagonal):
        kt_blk = kt_ref[ki]
        for c in range(2):
            if diagonal:
                s_ref[c, 0:half, 0:half] = _dot(qm_ref[c, 0:half], kt_blk[:, 0:half])
                s_ref[c, half:blk, :] = _dot(qm_ref[c, half:blk], kt_blk)
            else:
                s_ref[c] = _dot(qm_ref[c], kt_blk)

    def update(c, rows, s, v_part):
        m_prev = m_ref[c, rows]
        m_new = jnp.maximum(m_prev, jnp.max(s, axis=-1, keepdims=True))
        corr = jnp.exp2(m_prev - m_new)
        p = jnp.exp2(s - jnp.concatenate([m_new] * (s.shape[1] // LANES), axis=1))
        acc_ref[c, rows] = (jnp.concatenate([corr, corr], axis=1) * acc_ref[c, rows]
                            + _dot(p.astype(BF16), v_part))
        m_ref[c, rows] = m_new

    def softmax_pv(s_ref, ki, masked):
        v_blk = v_ref[ki * blk:(ki + 1) * blk, :]
        for c in range(2):
            if not masked:
                update(c, slice(0, blk), s_ref[c], v_blk)
                continue
            s_top = s_ref[c, 0:half, 0:half]
            row = lax.broadcasted_iota(jnp.int32, s_top.shape, 0)
            col = lax.broadcasted_iota(jnp.int32, s_top.shape, 1)
            update(c, slice(0, half), jnp.where(col <= row, s_top, -1e30), v_blk[0:half])
            s_bot = s_ref[c, half:blk, :]
            row = lax.broadcasted_iota(jnp.int32, s_bot.shape, 0) + half
            col = lax.broadcasted_iota(jnp.int32, s_bot.shape, 1)
            update(c, slice(half, blk), jnp.where(col <= row, s_bot, -1e30), v_blk)

    def finalize(qi):
        a1 = acc_ref[0, :, 0:2 * dh] * (1.0 / acc_ref[0, :, 2 * dh:4 * dh])
        a2 = acc_ref[1, :, 0:2 * dh] * (1.0 / acc_ref[1, :, 2 * dh:4 * dh])
        lv = lam_ref[...]
        lam = (jnp.exp(jnp.sum(lv[0:1] * lv[1:2], axis=-1, keepdims=True))
               - jnp.exp(jnp.sum(lv[2:3] * lv[3:4], axis=-1, keepdims=True)) + lambda_init)
        o = a1 - lam * a2
        msq = jnp.mean(o * o, axis=-1, keepdims=True)
        o_ref[qi * blk:(qi + 1) * blk, :] = (o * lax.rsqrt(msq + LN_EPS) * g_ref[...]
                                              * (1.0 - lambda_init)).astype(BF16)

    load_queries(0)
    scores(0, bufs[0], True)
    for t, (qi, ki) in enumerate(pairs):
        if t + 1 < len(pairs):
            nqi, nki = pairs[t + 1]
            if nki == 0:
                load_queries(nqi)
            scores(nki, bufs[(t + 1) % 2], nki == nqi)
        if ki == 0:
            m_ref[...] = jnp.full_like(m_ref, -1e30)
            acc_ref[...] = jnp.zeros_like(acc_ref)
        softmax_pv(bufs[t % 2], ki, masked=(ki == qi))
        if ki == qi:
            finalize(qi)


def _diff_attn(q, kt, v_ext, lam_q1, lam_k1, lam_q2, lam_k2, subln_g, lambda_init, *, blk):
    bn, s, hd = q.shape
    dh = lam_q1.shape[0]
    heads = hd // (2 * dh)
    nb = s // blk
    lamv = jnp.zeros((8, LANES), F32).at[0:4, 0:dh].set(jnp.stack([lam_q1, lam_k1, lam_q2, lam_k2]))
    return pl.pallas_call(
        functools.partial(_diff_attn_kernel, blk=blk, nb=nb, dh=dh, lambda_init=lambda_init),
        grid=(bn, heads),
        in_specs=[
            pl.BlockSpec((None, s, 2 * dh), lambda b, h: (b, 0, h)),
            pl.BlockSpec((None, None, nb, 2 * dh, blk), lambda b, h: (b, h, 0, 0, 0)),
            pl.BlockSpec((None, s, 4 * dh), lambda b, h: (b, 0, h)),
            pl.BlockSpec((8, LANES), lambda b, h: (0, 0)),
            pl.BlockSpec((1, 2 * dh), lambda b, h: (0, 0)),
        ],
        out_specs=pl.BlockSpec((None, s, 2 * dh), lambda b, h: (b, 0, h)),
        out_shape=jax.ShapeDtypeStruct((bn, s, hd), BF16),
        scratch_shapes=[pltpu.VMEM((2, blk, LANES), F32),
                        pltpu.VMEM((2, blk, 4 * dh), F32), pltpu.VMEM((2, blk, 2 * dh), BF16),
                        pltpu.VMEM((2, blk, blk), F32), pltpu.VMEM((2, blk, blk), F32)],
        compiler_params=_params(2),
        name="diff_attn",
    )(q, kt, v_ext, lamv, subln_g[None, :])


def _attn_out_kernel(o_ref, x_ref, w_ref, ln_g_ref, ln_b_ref, wr_ref, lstrict_ref,
                     y_ref, route_ref, cnt_ref, base_ref):
    @pl.when(pl.program_id(0) == 0)
    def _():
        base_ref[...] = jnp.zeros_like(base_ref)

    m = _dot(o_ref[...], w_ref[...])
    y = _layer_norm(DEEPNORM_ALPHA * x_ref[...] + m, ln_g_ref[...], ln_b_ref[...])
    y_ref[...] = y
    y_hi = y.astype(BF16)
    y_lo = (y - y_hi.astype(F32)).astype(BF16)
    hi_terms = _dot(y_hi, wr_ref[...])
    logits = hi_terms[:, 0:LANES] + hi_terms[:, LANES:2 * LANES] + _dot(y_lo, wr_ref[:, 0:LANES])

    lane = lax.broadcasted_iota(jnp.int32, logits.shape, 1).astype(F32)
    neg = -jnp.inf
    lg = jnp.where(lane < N_EXPERTS, logits, neg)
    v0 = jnp.max(lg, axis=-1, keepdims=True)
    i0 = jnp.min(jnp.where(lg == v0, lane, float(LANES)), axis=-1, keepdims=True)
    lg2 = jnp.where(lane == i0, neg, lg)
    v1 = jnp.max(lg2, axis=-1, keepdims=True)
    i1 = jnp.min(jnp.where(lg2 == v1, lane, float(LANES)), axis=-1, keepdims=True)
    e = jnp.exp(v1 - v0)
    g0 = 1.0 / (1.0 + e)
    g1 = e / (1.0 + e)
    oh0 = lane == i0
    oh1 = lane == i1
    c = jnp.where(oh0, 1.0, 0.0) + jnp.where(oh1, 1.0, 0.0)
    pre = _dot(lstrict_ref[...], c.astype(BF16)) + base_ref[...]
    r0 = jnp.sum(jnp.where(oh0, pre, 0.0), axis=-1, keepdims=True)
    r1 = jnp.sum(jnp.where(oh1, pre, 0.0), axis=-1, keepdims=True)
    base_ref[...] = base_ref[...] + jnp.sum(c, axis=0, keepdims=True)
    cnt_ref[...] = jnp.broadcast_to(base_ref[...], cnt_ref.shape)
    fields = (i0, i1, r0, r1, g0, g1)
    route = jnp.zeros_like(logits)
    for idx, val in enumerate(fields):
        route = jnp.where(lane == idx, val, route)
    route_ref[...] = route


def _attn_out(o2d, x2d, w_out, ln_g, ln_b, w_router, *, tile=1024):
    n, d = x2d.shape
    wr = jnp.pad(w_router, ((0, 0), (0, LANES - w_router.shape[1])))
    wr_hi = wr.astype(BF16)
    wr_lo = (wr - wr_hi.astype(F32)).astype(BF16)
    wr_split = jnp.concatenate([wr_hi, wr_lo], axis=1)
    lstrict = (jnp.arange(tile)[:, None] > jnp.arange(tile)[None, :]).astype(BF16)
    row = lambda t: t[None, :]
    tspec = pl.BlockSpec((tile, d), lambda i: (i, 0))
    return pl.pallas_call(
        _attn_out_kernel,
        grid=(n // tile,),
        in_specs=[tspec, tspec, _const_spec((d, d)), _const_spec((1, d)), _const_spec((1, d)),
                  _const_spec((d, 2 * LANES)), _const_spec((tile, tile))],
        out_specs=[tspec, pl.BlockSpec((tile, LANES), lambda i: (i, 0)),
                   pl.BlockSpec((8, LANES), lambda i: (0, 0))],
        out_shape=[jax.ShapeDtypeStruct((n, d), F32), jax.ShapeDtypeStruct((n, LANES), F32),
                   jax.ShapeDtypeStruct((8, LANES), F32)],
        scratch_shapes=[pltpu.VMEM((1, LANES), F32)],
        compiler_params=_params(1),
        name="attn_out_router",
    )(o2d, x2d, w_out.astype(BF16), row(ln_g), row(ln_b), wr_split, lstrict)


def _route_tables(route, cnt, *, tm):
    n = route.shape[0]
    counts = cnt[0, :N_EXPERTS].astype(jnp.int32)
    padded = (counts + tm - 1) // tm * tm
    pend = jnp.cumsum(padded)
    pstart = pend - padded

    def dest_of(k):
        e = route[:, k].astype(jnp.int32)
        start = jnp.sum(jnp.where(e[:, None] == jnp.arange(N_EXPERTS)[None, :], pstart[None, :], 0), axis=-1)
        return start + route[:, TOP_K + k].astype(jnp.int32)

    dest = (dest_of(0), dest_of(1))
    n_rows = n * TOP_K + N_EXPERTS * tm
    n_blocks = n_rows // tm
    n_used = pend[-1] // tm
    blk_idx = jnp.arange(n_blocks, dtype=jnp.int32)
    blk_start = jnp.minimum(blk_idx, n_used - 1) * tm
    blk_e = jnp.minimum(jnp.sum(blk_start[:, None] >= pend[None, :], axis=1), N_EXPERTS - 1)
    valid = jnp.clip(pstart[blk_e] + counts[blk_e] - blk_idx * tm, 0, tm)
    return (dest, blk_e.astype(jnp.int32), n_used.reshape(1).astype(jnp.int32),
            valid.astype(jnp.int32), n_rows)


SC_CORES = 2
SC_SUBCORES = 16
SC_WORKERS = SC_CORES * SC_SUBCORES
SC_ROWS = 32


def _sc_mesh():
    return plsc.VectorSubcoreMesh(core_axis_name="c", subcore_axis_name="s",
                                  num_cores=SC_CORES, num_subcores=SC_SUBCORES)


def _sc_worker_id():
    return lax.axis_index("s") * SC_CORES + lax.axis_index("c")


def _sc_dispatch(x2d, dest, n_rows):
    n, d = x2d.shape
    per_w = n // SC_WORKERS
    n_chunks = per_w // SC_ROWS
    assert n == SC_WORKERS * n_chunks * SC_ROWS and n_chunks % 2 == 0
    d0 = dest[0].reshape(SC_WORKERS, n_chunks, SC_ROWS)
    d1 = dest[1].reshape(SC_WORKERS, n_chunks, SC_ROWS)

    def body(x_hbm, d0_hbm, d1_hbm, xs_hbm, d0_v, d1_v, rows_v, rsem, s0sem, s1sem):
        wid = _sc_worker_id()
        base = wid * per_w
        pltpu.sync_copy(d0_hbm.at[wid], d0_v)
        pltpu.sync_copy(d1_hbm.at[wid], d1_v)

        def read(c, slot):
            return pltpu.make_async_copy(x_hbm.at[pl.ds(base + c * SC_ROWS, SC_ROWS)],
                                         rows_v.at[slot], rsem.at[slot])

        def scat(idx_v, sem, c, slot):
            return pltpu.make_async_copy(rows_v.at[slot], xs_hbm.at[idx_v.at[c]], sem.at[slot])

        def start_scatters(c, slot):
            scat(d0_v, s0sem, c, slot).start()
            scat(d1_v, s1sem, c, slot).start()

        def wait_scatters(c, slot):
            scat(d0_v, s0sem, c, slot).wait()
            scat(d1_v, s1sem, c, slot).wait()

        read(0, 0).start()

        def pair(j, carry):
            c0 = 2 * j
            read(c0, 0).wait()

            @pl.when(j > 0)
            def _():
                wait_scatters(c0 - 1, 1)

            read(c0 + 1, 1).start()
            start_scatters(c0, 0)
            read(c0 + 1, 1).wait()
            wait_scatters(c0, 0)

            @pl.when(c0 + 2 < n_chunks)
            def _():
                read(c0 + 2, 0).start()

            start_scatters(c0 + 1, 1)
            return carry

        lax.fori_loop(0, n_chunks // 2, pair, 0)
        wait_scatters(n_chunks - 1, 1)

    return pl.kernel(
        body,
        out_type=jax.ShapeDtypeStruct((n_rows, d), x2d.dtype),
        mesh=_sc_mesh(),
        scratch_types=[pltpu.VMEM((n_chunks, SC_ROWS), jnp.int32), pltpu.VMEM((n_chunks, SC_ROWS), jnp.int32),
                       pltpu.VMEM((2, SC_ROWS, d), x2d.dtype), pltpu.SemaphoreType.DMA((2,)),
                       pltpu.SemaphoreType.DMA((2,)), pltpu.SemaphoreType.DMA((2,))],
        name="sc_dispatch",
    )(x2d, d0, d1)


def _sc_gather(table, idx):
    _, d = table.shape
    b = idx.shape[0]
    per_w = b // SC_WORKERS
    n_chunks = per_w // SC_ROWS
    assert b == SC_WORKERS * n_chunks * SC_ROWS and n_chunks % 2 == 0
    idx3 = idx.reshape(SC_WORKERS, n_chunks, SC_ROWS)

    def body(table_hbm, idx_hbm, out_hbm, idx_v, rows_v, gsem, wsem):
        wid = _sc_worker_id()
        base = wid * per_w
        pltpu.sync_copy(idx_hbm.at[wid], idx_v)

        def gather(c, slot):
            return pltpu.make_async_copy(table_hbm.at[idx_v.at[c]], rows_v.at[slot], gsem.at[slot])

        def write(c, slot):
            return pltpu.make_async_copy(rows_v.at[slot],
                                         out_hbm.at[pl.ds(base + c * SC_ROWS, SC_ROWS)], wsem.at[slot])

        gather(0, 0).start()

        def pair(j, carry):
            c0 = 2 * j
            gather(c0, 0).wait()

            @pl.when(j > 0)
            def _():
                write(c0 - 1, 1).wait()

            gather(c0 + 1, 1).start()
            write(c0, 0).start()
            gather(c0 + 1, 1).wait()
            write(c0, 0).wait()

            @pl.when(c0 + 2 < n_chunks)
            def _():
                gather(c0 + 2, 0).start()

            write(c0 + 1, 1).start()
            return carry

        lax.fori_loop(0, n_chunks // 2, pair, 0)
        write(n_chunks - 1, 1).wait()

    return pl.kernel(
        body,
        out_type=jax.ShapeDtypeStruct((b, d), table.dtype),
        mesh=_sc_mesh(),
        scratch_types=[pltpu.VMEM((n_chunks, SC_ROWS), jnp.int32), pltpu.VMEM((2, SC_ROWS, d), table.dtype),
                       pltpu.SemaphoreType.DMA((2,)), pltpu.SemaphoreType.DMA((2,))],
        name="sc_combine_gather",
    )(table, idx3)


def _moe_kernel(blk_e, n_used, valid, xs_ref, wg_ref, wu_ref, wd_ref, y_ref, *, chunks):
    m = pl.program_id(0)
    f = pl.program_id(1)

    @pl.when(f == 0)
    def _():
        y_ref[...] = jnp.zeros_like(y_ref)

    def expert_rows(n_rows):
        row = lax.broadcasted_iota(jnp.int32, (n_rows, xs_ref.shape[1]), 0)
        x = jnp.where(row < valid[m], xs_ref[0:n_rows, :], 0.0).astype(BF16)
        for lo, hi in chunks:
            g = _dot(x, wg_ref[:, lo:hi])
            u = _dot(x, wu_ref[:, lo:hi])
            h = (g * _sigmoid(g) * u).astype(BF16)
            y_ref[0:n_rows, :] += _dot(h, wd_ref[lo:hi, :])

    half = xs_ref.shape[0] // 2

    @pl.when((m < n_used[0]) & (valid[m] > half))
    def _():
        expert_rows(xs_ref.shape[0])

    @pl.when((m < n_used[0]) & (valid[m] <= half))
    def _():
        expert_rows(half)


def _cast_kernel(*refs):
    n = len(refs) // 2
    for src_ref, dst_ref in zip(refs[:n], refs[n:]):
        dst_ref[...] = src_ref[...].astype(dst_ref.dtype)


def _to_bf16(*ws):
    e, r, c = ws[0].shape
    rb = r // 4
    spec = pl.BlockSpec((None, rb, c), lambda i, j: (i, j, 0))
    return pl.pallas_call(
        _cast_kernel,
        grid=(e, r // rb),
        in_specs=[spec] * len(ws),
        out_specs=[spec] * len(ws),
        out_shape=[jax.ShapeDtypeStruct(w.shape, BF16) for w in ws],
        compiler_params=_params(2),
        name="expert_weight_cast",
    )(*ws)


def _moe_grouped(xs, blk_e, n_used, valid, wg, wu, wd, *, tm, tf, fchunk=1024):
    n_rows, d = xs.shape
    fdim = wg.shape[2]
    nf = fdim // tf
    assert nf * tf == fdim
    n_blocks = n_rows // tm
    chunks = tuple((lo, min(lo + fchunk, tf)) for lo in range(0, tf, fchunk))

    def f_eff(m, f, nu):
        mm = jnp.minimum(m, nu[0] - 1)
        snake = jnp.where(mm % 2 == 0, f, nf - 1 - f)
        return jnp.where(m < nu[0], snake, jnp.where(mm % 2 == 0, nf - 1, 0))

    grid_spec = pltpu.PrefetchScalarGridSpec(
        num_scalar_prefetch=3,
        grid=(n_blocks, nf),
        in_specs=[
            pl.BlockSpec((tm, d), lambda m, f, be, nu, va: (jnp.minimum(m, nu[0] - 1), 0)),
            pl.BlockSpec((None, d, tf), lambda m, f, be, nu, va: (be[m], 0, f_eff(m, f, nu))),
            pl.BlockSpec((None, d, tf), lambda m, f, be, nu, va: (be[m], 0, f_eff(m, f, nu))),
            pl.BlockSpec((None, tf, d), lambda m, f, be, nu, va: (be[m], f_eff(m, f, nu), 0)),
        ],
        out_specs=pl.BlockSpec((tm, d), lambda m, f, be, nu, va: (m, 0)),
    )
    return pl.pallas_call(
        functools.partial(_moe_kernel, chunks=chunks),
        grid_spec=grid_spec,
        out_shape=jax.ShapeDtypeStruct((n_rows, d), F32),
        compiler_params=_params(2),
        name="moe_grouped",
    )(blk_e, n_used, valid, xs, wg, wu, wd)


def _final_kernel(x_ref, y0_ref, y1_ref, route_ref, p_ref, ln_g_ref, ln_b_ref, wpg_ref, bpg_ref,
                  wpp_ref, *rest):
    o_ref = rest[-1]
    route = route_ref[...]
    f = route[:, 4:5] * y0_ref[...] + route[:, 5:6] * y1_ref[...]
    y = _layer_norm(DEEPNORM_ALPHA * x_ref[...] + f, ln_g_ref[...], ln_b_ref[...])
    o_ref[...] = _ple(y, p_ref[...], wpg_ref, bpg_ref, wpp_ref)


def _final_part(x2d, y01, route, p_all, layer, ln_g, ln_b, wpg, bpg, wpp, *, part, n_parts, prev, tile):
    n, d = x2d.shape
    pd = p_all.shape[2]
    nblk = n // n_parts // tile
    off = part * nblk
    row = lambda t: t[None, :]
    tok = lambda width: pl.BlockSpec((tile, width), lambda i: (i + off, 0))
    in_specs = [tok(d), pl.BlockSpec((tile, d), lambda i: (i, 0)),
                pl.BlockSpec((tile, d), lambda i: (i + nblk, 0)), tok(LANES),
                pl.BlockSpec((None, tile, pd), lambda i: (layer, i + off, 0)),
                _const_spec((1, d)), _const_spec((1, d)),
                _const_spec((d, d)), _const_spec((1, d)), _const_spec((pd, d))]
    args = [x2d, y01, y01, route, p_all, row(ln_g), row(ln_b), wpg.astype(BF16), row(bpg), wpp.astype(BF16)]
    aliases = {}
    if prev is not None:
        in_specs.append(pl.BlockSpec(memory_space=pl.ANY))
        args.append(prev)
        aliases = {len(args) - 1: 0}
    return pl.pallas_call(
        _final_kernel,
        grid=(nblk,),
        in_specs=in_specs,
        out_specs=tok(d),
        out_shape=jax.ShapeDtypeStruct((n, d), F32),
        input_output_aliases=aliases,
        compiler_params=_params(1),
        name="moe_combine_ple",
    )(*args)


def _combine(x2d, y, dest, route, p_all, layer, ln_g, ln_b, wpg, bpg, wpp, *, n_parts=2, tile=1024):
    n = x2d.shape[0]
    step = n // n_parts
    tile = min(tile, step)
    out = None
    for part in range(n_parts):
        rows = slice(part * step, (part + 1) * step)
        y01 = _sc_gather(y, jnp.concatenate([dest[0][rows], dest[1][rows]]))
        out = _final_part(x2d, y01, route, p_all, layer, ln_g, ln_b, wpg, bpg, wpp,
                          part=part, n_parts=n_parts, prev=out, tile=tile)
    return out


def _even_layer(x, p_all, layer_idx, w_in, a_ln_g, a_ln_b, a_ws, a_bs, w_gate_up, b_gate, norm_g, w_out,
                ln1_g, ln1_b, wg, wu, wd, ln2_g, ln2_b, wpp, wpg, bpg):
    bn, s, d = x.shape
    x1 = _even_mixer(x, w_in, a_ln_g, a_ln_b, a_ws, a_bs, w_gate_up, b_gate, norm_g, w_out, ln1_g, ln1_b)
    x2 = _ffn_ple(x1.reshape(bn * s, d), p_all, layer_idx, wg, wu, wd, ln2_g, ln2_b, wpg, bpg, wpp)
    return x2.reshape(bn, s, d)


def _odd_layer(x, p_all, layer_idx, w_qkv, lam_q1, lam_k1, lam_q2, lam_k2, subln_g, w_out, ln1_g, ln1_b,
               w_router, ewg, ewu, ewd, ln2_g, ln2_b, wpp, wpg, bpg, *, attn_blk=1024, moe_tm=512,
               moe_tf=1792):
    bn, s, d = x.shape
    n = bn * s
    lambda_init = 0.8 - 0.6 * math.exp(-0.3 * layer_idx)
    q, kt, v_ext = _qkv(x, w_qkv, dh=lam_q1.shape[0], tile=attn_blk)
    o = _diff_attn(q, kt, v_ext, lam_q1, lam_k1, lam_q2, lam_k2, subln_g, lambda_init, blk=attn_blk)
    x3, route, cnt = _attn_out(o.reshape(n, d), x.reshape(n, d), w_out, ln1_g, ln1_b, w_router)
    dest, blk_e, n_used, valid, n_rows = _route_tables(route, cnt, tm=moe_tm)
    xs = _sc_dispatch(x3, dest, n_rows)
    wg_b, wu_b = _to_bf16(ewg, ewu)
    (wd_b,) = _to_bf16(ewd)
    y = _moe_grouped(xs, blk_e, n_used, valid, wg_b, wu_b, wd_b,
                     tm=moe_tm, tf=moe_tf)
    out = _combine(x3, y, dest, route, p_all, layer_idx, ln2_g, ln2_b, wpg, bpg, wpp)
    return out.reshape(bn, s, d)


def kernel(x, p, e_w_in, e_a_ln_g, e_a_ln_b, e_a_ws, e_a_bs, e_b_w_gate_up, e_b_b_gate, e_b_norm_g, e_w_out, e_ln1_g, e_ln1_b, e_ffn_wg, e_ffn_wu, e_ffn_wd, e_ln2_g, e_ln2_b, o_w_qkv, o_lam_q1, o_lam_k1, o_lam_q2, o_lam_k2, o_subln_g, o_w_out, o_ln1_g, o_ln1_b, o_router, o_exp_wg, o_exp_wu, o_exp_wd, o_ln2_g, o_ln2_b, ple_w_proj, ple_w_gate, ple_b_gate):
    p_all = p.reshape(p.shape[0], -1, p.shape[-1])
    for i in range(DEPTH):
        j = i // 2
        if i % 2 == 0:
            x = _even_layer(x, p_all, i, e_w_in[j], e_a_ln_g[j], e_a_ln_b[j], e_a_ws[j], e_a_bs[j],
                            e_b_w_gate_up[j], e_b_b_gate[j], e_b_norm_g[j], e_w_out[j],
                            e_ln1_g[j], e_ln1_b[j], e_ffn_wg[j], e_ffn_wu[j], e_ffn_wd[j],
                            e_ln2_g[j], e_ln2_b[j], ple_w_proj[i], ple_w_gate[i], ple_b_gate[i])
        else:
            x = _odd_layer(x, p_all, i, o_w_qkv[j], o_lam_q1[j], o_lam_k1[j], o_lam_q2[j], o_lam_k2[j],
                           o_subln_g[j], o_w_out[j], o_ln1_g[j], o_ln1_b[j], o_router[j],
                           o_exp_wg[j], o_exp_wu[j], o_exp_wd[j], o_ln2_g[j], o_ln2_b[j],
                           ple_w_proj[i], ple_w_gate[i], ple_b_gate[i])
    return x
```

```python
import functools
import math

import jax
import jax.numpy as jnp
from jax import lax
from jax.experimental import pallas as pl
from jax.experimental.pallas import tpu as pltpu
from jax.experimental.pallas import tpu_sc as plsc

F32 = jnp.float32
BF16 = jnp.bfloat16

DEPTH = 2
DEEPNORM_ALPHA = (2.0 * DEPTH) ** 0.25
LN_EPS = 1e-5
A_CHUNK = 128
A_GROUPS = 8
B_HEADS = 4
B_CHUNK = 64
B_TAU = 16.0
CS_ROWS = 256
C_HEADS = 8
N_EXPERTS = 8
TOP_K = 2
LANES = 128
LOG2_E = 1.4426950408889634
VMEM_LIMIT = 56 * 1024 * 1024
MOE_VMEM_LIMIT = 60 * 1024 * 1024

NT_DIMS = (((1,), (1,)), ((), ()))
TN_DIMS = (((0,), (0,)), ((), ()))


def _dot(a, b):
    return jnp.dot(a, b, preferred_element_type=F32)


def _layer_norm(x, g, b):
    mu = jnp.mean(x, axis=-1, keepdims=True)
    xc = x - mu
    var = jnp.mean(xc * xc, axis=-1, keepdims=True)
    return xc * lax.rsqrt(var + LN_EPS) * g + b


def _sigmoid(x):
    return 1.0 / (1.0 + jnp.exp(-x))


def _split3(a):
    hi = a.astype(BF16)
    r1 = a - hi.astype(F32)
    mid = r1.astype(BF16)
    lo = (r1 - mid.astype(F32)).astype(BF16)
    return hi, mid, lo


def _const_spec(shape):
    zeros = (0,) * len(shape)
    return pl.BlockSpec(shape, lambda *_: zeros, pipeline_mode=pl.Buffered(1))


def _params(n_axes):
    return pltpu.CompilerParams(dimension_semantics=("arbitrary",) * n_axes,
                                vmem_limit_bytes=VMEM_LIMIT)


def _even_mixer_kernel(x_ref, w_in_ref, a_g_ref, a_b_ref, wcat_ref, abias_ref, mstack_ref,
                       w_up_ref, b_gate_ref, ng_ref, w_out_ref, ln_g_ref, ln_b_ref,
                       o_ref, state_ref, *, tile, aw, dkh, dvh):
    @pl.when(pl.program_id(1) == 0)
    def _():
        state_ref[...] = jnp.zeros_like(state_ref)

    hk = B_HEADS * dkh
    hv = B_HEADS * dvh
    x = x_ref[...]
    z = _dot(x.astype(BF16), w_in_ref[...])

    u = jax.nn.gelu(z[:, 0:aw])
    v = _layer_norm(jax.nn.gelu(z[:, aw:2 * aw]), a_g_ref[...], a_b_ref[...])
    gd = aw // A_GROUPS
    rows = lax.broadcasted_iota(jnp.int32, (A_CHUNK, A_GROUPS * A_CHUNK), 0)
    cols = lax.broadcasted_iota(jnp.int32, (A_CHUNK, A_GROUPS * A_CHUNK), 1)
    wcat = jnp.where((cols % A_CHUNK) <= rows, wcat_ref[...], 0.0).astype(BF16)
    r_bd = lax.broadcasted_iota(jnp.int32, (A_GROUPS * A_CHUNK, aw), 0) // A_CHUNK
    c_bd = lax.broadcasted_iota(jnp.int32, (A_GROUPS * A_CHUNK, aw), 1) // gd
    mask_bd = r_bd == c_bd
    ya_parts = []
    for c in range(tile // A_CHUNK):
        sl = slice(c * A_CHUNK, (c + 1) * A_CHUNK)
        v_rep = jnp.concatenate([v[sl]] * A_GROUPS, axis=0)
        v_bd = jnp.where(mask_bd, v_rep, 0.0).astype(BF16)
        sg = _dot(wcat, v_bd) + abias_ref[...]
        ya_parts.append(u[sl] * sg)
    ya = jnp.concatenate(ya_parts, axis=0)

    o0 = 2 * aw
    q = z[:, o0:o0 + hk] * (dkh ** -0.5)
    k = z[:, o0 + hk:o0 + 2 * hk]
    vv = z[:, o0 + 2 * hk:o0 + 2 * hk + hv]
    r = z[:, o0 + 2 * hk + hv:o0 + 2 * hk + 2 * hv]
    g_low = z[:, o0 + 2 * hk + 2 * hv:]
    pre = _dot(g_low.astype(BF16), w_up_ref[...]) + b_gate_ref[...]
    log_a = (jnp.minimum(pre, 0.0) - jnp.log1p(jnp.exp(-jnp.abs(pre)))) * (1.0 / B_TAU)
    la_hi, la_mid, la_lo = _split3(log_a)
    ms = mstack_ref[...]
    b_parts, mid_parts, last_parts = [], [], []
    for t in range(tile // CS_ROWS):
        rs = slice(t * CS_ROWS, (t + 1) * CS_ROWS)
        cs = _dot(ms, la_hi[rs]) + _dot(ms, la_mid[rs]) + _dot(ms, la_lo[rs])
        b_parts.append(cs[0:CS_ROWS])
        mid_parts.append(cs[CS_ROWS:2 * CS_ROWS])
        last_parts.append(cs[2 * CS_ROWS:3 * CS_ROWS])
    b_cum = jnp.concatenate(b_parts, axis=0)
    d_mid = jnp.concatenate(mid_parts, axis=0)
    d_last = jnp.concatenate(last_parts, axis=0)
    qe = (q * jnp.exp(d_mid)).astype(BF16)
    ke = k * jnp.exp(-d_mid)
    kd = (k * jnp.exp(d_last)).astype(BF16)
    qb = (q * jnp.exp(b_cum)).astype(BF16)
    dec = jnp.exp(b_cum + d_last)
    vvb = vv.astype(BF16)

    mask_kk = (lax.broadcasted_iota(jnp.int32, (hk, hk), 0) // dkh
               == lax.broadcasted_iota(jnp.int32, (hk, hk), 1) // dkh)
    mask_vbd = (lax.broadcasted_iota(jnp.int32, (hk, hv), 0) // dkh
                == lax.broadcasted_iota(jnp.int32, (hk, hv), 1) // dvh)
    mask_st = (lax.broadcasted_iota(jnp.int32, (hv, hk), 0) // dvh
               == lax.broadcasted_iota(jnp.int32, (hv, hk), 1) // dkh)
    causal = ((lax.broadcasted_iota(jnp.int32, (B_CHUNK, hk), 1) % B_CHUNK)
              <= lax.broadcasted_iota(jnp.int32, (B_CHUNK, hk), 0))

    st = state_ref[...]
    o_parts = []
    for c in range(tile // B_CHUNK):
        sl = slice(c * B_CHUNK, (c + 1) * B_CHUNK)
        ke_bd = jnp.where(mask_kk, jnp.concatenate([ke[sl]] * B_HEADS, axis=0), 0.0).astype(BF16)
        s_cat = lax.dot_general(qe[sl], ke_bd, NT_DIMS, preferred_element_type=F32)
        s_cat = jnp.where(causal, s_cat, 0.0).astype(BF16)
        v_bd = jnp.where(mask_vbd, jnp.concatenate([vv[sl]] * B_HEADS, axis=0), 0.0).astype(BF16)
        o_c = _dot(s_cat, v_bd) + lax.dot_general(qb[sl], st.astype(BF16), NT_DIMS,
                                                   preferred_element_type=F32)
        kv_t = lax.dot_general(vvb[sl], kd[sl], TN_DIMS, preferred_element_type=F32)
        st = dec[c * B_CHUNK:c * B_CHUNK + 1] * st + jnp.where(mask_st, kv_t, 0.0)
        o_parts.append(o_c)
    state_ref[...] = st
    o = jnp.concatenate(o_parts, axis=0)

    yb_parts = []
    for h in range(B_HEADS):
        oh = o[:, h * dvh:(h + 1) * dvh]
        msq = jnp.mean(oh * oh, axis=-1, keepdims=True)
        yb_parts.append(oh * lax.rsqrt(msq + LN_EPS))
    yb = jnp.concatenate(yb_parts, axis=1) * ng_ref[...] * (r * _sigmoid(r))

    y_cat = jnp.concatenate([ya, yb], axis=1).astype(BF16)
    m = _dot(y_cat, w_out_ref[...])
    o_ref[...] = _layer_norm(DEEPNORM_ALPHA * x + m, ln_g_ref[...], ln_b_ref[...])


def _gla_cumsum_matrices(tile):
    i = jnp.arange(tile)[:, None]
    j = jnp.arange(tile)[None, :]
    same = (i // B_CHUNK) == (j // B_CHUNK)
    m_cum = same & (j <= i)
    m_mid = same & (j <= (i // B_CHUNK) * B_CHUNK + B_CHUNK // 2 - 1)
    m_last = same
    f = lambda t: t.astype(F32)
    return jnp.concatenate([f(m_cum), f(m_cum) - f(m_mid), f(m_last) - f(m_cum)], axis=0).astype(BF16)


def _even_mixer(x, w_in, a_ln_g, a_ln_b, a_ws, a_bs, w_gate_up, b_gate, norm_g, w_out, ln_g, ln_b,
                *, tile=1024):
    bn, s, d = x.shape
    aw = a_ln_g.shape[0]
    hk = w_gate_up.shape[1]
    dkh = hk // B_HEADS
    dvh = norm_g.shape[0]
    hv = B_HEADS * dvh
    rank = w_gate_up.shape[0]
    main = 2 * aw + 2 * hk + 2 * hv
    gd = aw // A_GROUPS
    w_in_p = jnp.concatenate([w_in[:, :main], jnp.pad(w_in[:, main:], ((0, 0), (0, LANES - rank)))],
                             axis=1).astype(BF16)
    w_up_p = jnp.pad(w_gate_up, ((0, LANES - rank), (0, 0))).astype(BF16)
    wcat = jnp.transpose(a_ws, (1, 0, 2)).reshape(A_CHUNK, A_GROUPS * A_CHUNK)
    abias = jnp.repeat(a_bs.T, gd, axis=1)
    mstack = _gla_cumsum_matrices(CS_ROWS)
    ng = jnp.tile(norm_g, B_HEADS)[None, :]
    row = lambda t: t[None, :]
    kern = functools.partial(_even_mixer_kernel, tile=tile, aw=aw, dkh=dkh, dvh=dvh)
    tile_spec = pl.BlockSpec((None, tile, d), lambda b, i: (b, i, 0))
    return pl.pallas_call(
        kern,
        grid=(bn, s // tile),
        in_specs=[tile_spec, _const_spec(w_in_p.shape), _const_spec((1, aw)), _const_spec((1, aw)),
                  _const_spec(wcat.shape), _const_spec(abias.shape), _const_spec(mstack.shape),
                  _const_spec(w_up_p.shape), _const_spec((1, hk)), _const_spec((1, hv)),
                  _const_spec(w_out.shape), _const_spec((1, d)), _const_spec((1, d))],
        out_specs=tile_spec,
        out_shape=jax.ShapeDtypeStruct(x.shape, F32),
        scratch_shapes=[pltpu.VMEM((hv, hk), F32)],
        compiler_params=_params(2),
        name="even_mixer",
    )(x, w_in_p, row(a_ln_g), row(a_ln_b), wcat, abias, mstack, w_up_p, row(b_gate), ng,
      w_out.astype(BF16), row(ln_g), row(ln_b))


def _ple(y, p, wpg_ref, bpg_ref, wpp_ref):
    gate = _sigmoid(_dot(y.astype(BF16), wpg_ref[...]) + bpg_ref[...])
    return y + gate * _dot(p.astype(BF16), wpp_ref[...])


def _ffn_ple_kernel(x_ref, p_ref, wg_ref, wu_ref, wd_ref, ln_g_ref, ln_b_ref, wpg_ref, bpg_ref,
                    wpp_ref, o_ref, *, chunks):
    x = x_ref[...]
    xb = x.astype(BF16)
    acc = None
    for lo, hi in chunks:
        g = _dot(xb, wg_ref[:, lo:hi])
        u = _dot(xb, wu_ref[:, lo:hi])
        h = (g * _sigmoid(g) * u).astype(BF16)
        part = _dot(h, wd_ref[lo:hi, :])
        acc = part if acc is None else acc + part
    y = _layer_norm(DEEPNORM_ALPHA * x + acc, ln_g_ref[...], ln_b_ref[...])
    o_ref[...] = _ple(y, p_ref[...], wpg_ref, bpg_ref, wpp_ref)


def _ffn_ple(x2d, p_all, layer, wg, wu, wd, ln_g, ln_b, wpg, bpg, wpp, *, tile=512, fchunk=1024):
    n, d = x2d.shape
    f = wg.shape[1]
    pd = p_all.shape[2]
    chunks = tuple((lo, min(lo + fchunk, f)) for lo in range(0, f, fchunk))
    row = lambda t: t[None, :]
    return pl.pallas_call(
        functools.partial(_ffn_ple_kernel, chunks=chunks),
        grid=(n // tile,),
        in_specs=[pl.BlockSpec((tile, d), lambda i: (i, 0)),
                  pl.BlockSpec((None, tile, pd), lambda i: (layer, i, 0)),
                  _const_spec((d, f)), _const_spec((d, f)), _const_spec((f, d)),
                  _const_spec((1, d)), _const_spec((1, d)),
                  _const_spec((d, d)), _const_spec((1, d)), _const_spec((pd, d))],
        out_specs=pl.BlockSpec((tile, d), lambda i: (i, 0)),
        out_shape=jax.ShapeDtypeStruct((n, d), F32),
        compiler_params=_params(1),
        name="ffn_ple",
    )(x2d, p_all, wg.astype(BF16), wu.astype(BF16), wd.astype(BF16), row(ln_g), row(ln_b),
      wpg.astype(BF16), row(bpg), wpp.astype(BF16))


def _qkv_kernel(x_ref, w_ref, q_ref, kt_ref, v_ref, *, hd, scale):
    z = _dot(x_ref[...].astype(BF16), w_ref[...])
    q_ref[...] = (z[:, 0:hd] * scale).astype(BF16)
    k_t = z[:, hd:2 * hd].T.astype(BF16)
    heads, dk2, tile = kt_ref.shape
    ones = jnp.ones((tile, dk2), BF16)
    for h in range(heads):
        kt_ref[h] = k_t[h * dk2:(h + 1) * dk2, :]
        v_ref[:, 2 * h * dk2:(2 * h + 1) * dk2] = z[:, 2 * hd + h * dk2:2 * hd + (h + 1) * dk2].astype(BF16)
        v_ref[:, (2 * h + 1) * dk2:(2 * h + 2) * dk2] = ones


def _qkv(x, w_qkv, *, dh, tile):
    bn, s, d = x.shape
    hd = w_qkv.shape[1] // 3
    heads = hd // (2 * dh)
    return pl.pallas_call(
        functools.partial(_qkv_kernel, hd=hd, scale=dh ** -0.5 * LOG2_E),
        grid=(bn, s // tile),
        in_specs=[pl.BlockSpec((None, tile, d), lambda b, i: (b, i, 0)), _const_spec(w_qkv.shape)],
        out_specs=[pl.BlockSpec((None, tile, hd), lambda b, i: (b, i, 0)),
                   pl.BlockSpec((None, heads, None, 2 * dh, tile), lambda b, i: (b, 0, i, 0, 0)),
                   pl.BlockSpec((None, tile, 2 * hd), lambda b, i: (b, i, 0))],
        out_shape=[jax.ShapeDtypeStruct((bn, s, hd), BF16),
                   jax.ShapeDtypeStruct((bn, heads, s // tile, 2 * dh, tile), BF16),
                   jax.ShapeDtypeStruct((bn, s, 2 * hd), BF16)],
        compiler_params=_params(2),
        name="qkv_proj",
    )(x, w_qkv.astype(BF16))


def _diff_attn_kernel(q_ref, kt_ref, v_ref, lam_ref, g_ref, o_ref, m_ref, acc_ref,
                      qm_ref, sa_ref, sb_ref, *, blk, nb, dh, lambda_init):
    pairs = [(qi, ki) for qi in range(nb) for ki in range(qi + 1)]
    bufs = (sa_ref, sb_ref)
    half = blk // 2

    def load_queries(qi):
        q = q_ref[qi * blk:(qi + 1) * blk, :]
        lane = lax.broadcasted_iota(jnp.int32, q.shape, 1)
        zero = jnp.zeros_like(q)
        qm_ref[0] = jnp.where(lane < dh, q, zero)
        qm_ref[1] = jnp.where(lane >= dh, q, zero)

    def scores(ki, s_ref, diagonal):
        kt_blk = kt_ref[ki]
        for c in range(2):
            if diagonal:
                s_ref[c, 0:half, 0:half] = _dot(qm_ref[c, 0:half], kt_blk[:, 0:half])
                s_ref[c, half:blk, :] = _dot(qm_ref[c, half:blk], kt_blk)
            else:
                s_ref[c] = _dot(qm_ref[c], kt_blk)

    def update(c, rows, s, v_part):
        m_prev = m_ref[c, rows]
        m_new = jnp.maximum(m_prev, jnp.max(s, axis=-1, keepdims=True))
        corr = jnp.exp2(m_prev - m_new)
        p = jnp.exp2(s - jnp.concatenate([m_new] * (s.shape[1] // LANES), axis=1))
        acc_ref[c, rows] = (jnp.concatenate([corr, corr], axis=1) * acc_ref[c, rows]
                            + _dot(p.astype(BF16), v_part))
        m_ref[c, rows] = m_new

    def softmax_pv(s_ref, ki, masked):
        v_blk = v_ref[ki * blk:(ki + 1) * blk, :]
        for c in range(2):
            if not masked:
                update(c, slice(0, blk), s_ref[c], v_blk)
                continue
            s_top = s_ref[c, 0:half, 0:half]
            row = lax.broadcasted_iota(jnp.int32, s_top.shape, 0)
            col = lax.broadcasted_iota(jnp.int32, s_top.shape, 1)
            update(c, slice(0, half), jnp.where(col <= row, s_top, -1e30), v_blk[0:half])
            s_bot = s_ref[c, half:blk, :]
            row = lax.broadcasted_iota(jnp.int32, s_bot.shape, 0) + half
            col = lax.broadcasted_iota(jnp.int32, s_bot.shape, 1)
            update(c, slice(half, blk), jnp.where(col <= row, s_bot, -1e30), v_blk)

    def finalize(qi):
        a1 = acc_ref[0, :, 0:2 * dh] * (1.0 / acc_ref[0, :, 2 * dh:4 * dh])
        a2 = acc_ref[1, :, 0:2 * dh] * (1.0 / acc_ref[1, :, 2 * dh:4 * dh])
        lv = lam_ref[...]
        lam = (jnp.exp(jnp.sum(lv[0:1] * lv[1:2], axis=-1, keepdims=True))
               - jnp.exp(jnp.sum(lv[2:3] * lv[3:4], axis=-1, keepdims=True)) + lambda_init)
        o = a1 - lam * a2
        msq = jnp.mean(o * o, axis=-1, keepdims=True)
        o_ref[qi * blk:(qi + 1) * blk, :] = (o * lax.rsqrt(msq + LN_EPS) * g_ref[...]
                                              * (1.0 - lambda_init)).astype(BF16)

    load_queries(0)
    scores(0, bufs[0], True)
    for t, (qi, ki) in enumerate(pairs):
        if t + 1 < len(pairs):
            nqi, nki = pairs[t + 1]
            if nki == 0:
                load_queries(nqi)
            scores(nki, bufs[(t + 1) % 2], nki == nqi)
        if ki == 0:
            m_ref[...] = jnp.full_like(m_ref, -1e30)
            acc_ref[...] = jnp.zeros_like(acc_ref)
        softmax_pv(bufs[t % 2], ki, masked=(ki == qi))
        if ki == qi:
            finalize(qi)


def _diff_attn(q, kt, v_ext, lam_q1, lam_k1, lam_q2, lam_k2, subln_g, lambda_init, *, blk):
    bn, s, hd = q.shape
    dh = lam_q1.shape[0]
    heads = hd // (2 * dh)
    nb = s // blk
    lamv = jnp.zeros((8, LANES), F32).at[0:4, 0:dh].set(jnp.stack([lam_q1, lam_k1, lam_q2, lam_k2]))
    return pl.pallas_call(
        functools.partial(_diff_attn_kernel, blk=blk, nb=nb, dh=dh, lambda_init=lambda_init),
        grid=(bn, heads),
        in_specs=[
            pl.BlockSpec((None, s, 2 * dh), lambda b, h: (b, 0, h)),
            pl.BlockSpec((None, None, nb, 2 * dh, blk), lambda b, h: (b, h, 0, 0, 0)),
            pl.BlockSpec((None, s, 4 * dh), lambda b, h: (b, 0, h)),
            pl.BlockSpec((8, LANES), lambda b, h: (0, 0)),
            pl.BlockSpec((1, 2 * dh), lambda b, h: (0, 0)),
        ],
        out_specs=pl.BlockSpec((None, s, 2 * dh), lambda b, h: (b, 0, h)),
        out_shape=jax.ShapeDtypeStruct((bn, s, hd), BF16),
        scratch_shapes=[pltpu.VMEM((2, blk, LANES), F32),
                        pltpu.VMEM((2, blk, 4 * dh), F32), pltpu.VMEM((2, blk, 2 * dh), BF16),
                        pltpu.VMEM((2, blk, blk), F32), pltpu.VMEM((2, blk, blk), F32)],
        compiler_params=_params(2),
        name="diff_attn",
    )(q, kt, v_ext, lamv, subln_g[None, :])


def _attn_out_kernel(o_ref, x_ref, w_ref, ln_g_ref, ln_b_ref, wr_ref, lstrict_ref,
                     y_ref, route_ref, cnt_ref, base_ref):
    @pl.when(pl.program_id(0) == 0)
    def _():
        base_ref[...] = jnp.zeros_like(base_ref)

    m = _dot(o_ref[...], w_ref[...])
    y = _layer_norm(DEEPNORM_ALPHA * x_ref[...] + m, ln_g_ref[...], ln_b_ref[...])
    y_ref[...] = y
    y_hi = y.astype(BF16)
    y_lo = (y - y_hi.astype(F32)).astype(BF16)
    hi_terms = _dot(y_hi, wr_ref[...])
    logits = hi_terms[:, 0:LANES] + hi_terms[:, LANES:2 * LANES] + _dot(y_lo, wr_ref[:, 0:LANES])

    lane = lax.broadcasted_iota(jnp.int32, logits.shape, 1).astype(F32)
    neg = -jnp.inf
    lg = jnp.where(lane < N_EXPERTS, logits, neg)
    v0 = jnp.max(lg, axis=-1, keepdims=True)
    i0 = jnp.min(jnp.where(lg == v0, lane, float(LANES)), axis=-1, keepdims=True)
    lg2 = jnp.where(lane == i0, neg, lg)
    v1 = jnp.max(lg2, axis=-1, keepdims=True)
    i1 = jnp.min(jnp.where(lg2 == v1, lane, float(LANES)), axis=-1, keepdims=True)
    e = jnp.exp(v1 - v0)
    g0 = 1.0 / (1.0 + e)
    g1 = e / (1.0 + e)
    oh0 = lane == i0
    oh1 = lane == i1
    c = jnp.where(oh0, 1.0, 0.0) + jnp.where(oh1, 1.0, 0.0)
    pre = _dot(lstrict_ref[...], c.astype(BF16)) + base_ref[...]
    r0 = jnp.sum(jnp.where(oh0, pre, 0.0), axis=-1, keepdims=True)
    r1 = jnp.sum(jnp.where(oh1, pre, 0.0), axis=-1, keepdims=True)
    base_ref[...] = base_ref[...] + jnp.sum(c, axis=0, keepdims=True)
    cnt_ref[...] = jnp.broadcast_to(base_ref[...], cnt_ref.shape)
    fields = (i0, i1, r0, r1, g0, g1)
    route = jnp.zeros_like(logits)
    for idx, val in enumerate(fields):
        route = jnp.where(lane == idx, val, route)
    route_ref[...] = route


def _attn_out(o2d, x2d, w_out, ln_g, ln_b, w_router, *, tile=1024):
    n, d = x2d.shape
    wr = jnp.pad(w_router, ((0, 0), (0, LANES - w_router.shape[1])))
    wr_hi = wr.astype(BF16)
    wr_lo = (wr - wr_hi.astype(F32)).astype(BF16)
    wr_split = jnp.concatenate([wr_hi, wr_lo], axis=1)
    lstrict = (jnp.arange(tile)[:, None] > jnp.arange(tile)[None, :]).astype(BF16)
    row = lambda t: t[None, :]
    tspec = pl.BlockSpec((tile, d), lambda i: (i, 0))
    return pl.pallas_call(
        _attn_out_kernel,
        grid=(n // tile,),
        in_specs=[tspec, tspec, _const_spec((d, d)), _const_spec((1, d)), _const_spec((1, d)),
                  _const_spec((d, 2 * LANES)), _const_spec((tile, tile))],
        out_specs=[tspec, pl.BlockSpec((tile, LANES), lambda i: (i, 0)),
                   pl.BlockSpec((8, LANES), lambda i: (0, 0))],
        out_shape=[jax.ShapeDtypeStruct((n, d), F32), jax.ShapeDtypeStruct((n, LANES), F32),
                   jax.ShapeDtypeStruct((8, LANES), F32)],
        scratch_shapes=[pltpu.VMEM((1, LANES), F32)],
        compiler_params=_params(1),
        name="attn_out_router",
    )(o2d, x2d, w_out.astype(BF16), row(ln_g), row(ln_b), wr_split, lstrict)


def _route_tables(route, cnt, *, tm):
    n = route.shape[0]
    counts = cnt[0, :N_EXPERTS].astype(jnp.int32)
    padded = (counts + tm - 1) // tm * tm
    pend = jnp.cumsum(padded)
    pstart = pend - padded

    def dest_of(k):
        e = route[:, k].astype(jnp.int32)
        start = jnp.sum(jnp.where(e[:, None] == jnp.arange(N_EXPERTS)[None, :], pstart[None, :], 0), axis=-1)
        return start + route[:, TOP_K + k].astype(jnp.int32)

    dest = (dest_of(0), dest_of(1))
    n_rows = n * TOP_K + N_EXPERTS * tm
    n_blocks = n_rows // tm
    n_used = pend[-1] // tm
    blk_idx = jnp.arange(n_blocks, dtype=jnp.int32)
    blk_start = jnp.minimum(blk_idx, n_used - 1) * tm
    blk_e = jnp.minimum(jnp.sum(blk_start[:, None] >= pend[None, :], axis=1), N_EXPERTS - 1)
    valid = jnp.clip(pstart[blk_e] + counts[blk_e] - blk_idx * tm, 0, tm)
    return (dest, blk_e.astype(jnp.int32), n_used.reshape(1).astype(jnp.int32),
            valid.astype(jnp.int32), n_rows)


SC_CORES = 2
SC_SUBCORES = 16
SC_WORKERS = SC_CORES * SC_SUBCORES
SC_ROWS = 32


def _sc_mesh():
    return plsc.VectorSubcoreMesh(core_axis_name="c", subcore_axis_name="s",
                                  num_cores=SC_CORES, num_subcores=SC_SUBCORES)


def _sc_worker_id():
    return lax.axis_index("s") * SC_CORES + lax.axis_index("c")


def _sc_dispatch(x2d, dest, n_rows):
    n, d = x2d.shape
    per_w = n // SC_WORKERS
    n_chunks = per_w // SC_ROWS
    assert n == SC_WORKERS * n_chunks * SC_ROWS and n_chunks % 2 == 0
    d0 = dest[0].reshape(SC_WORKERS, n_chunks, SC_ROWS)
    d1 = dest[1].reshape(SC_WORKERS, n_chunks, SC_ROWS)

    def body(x_hbm, d0_hbm, d1_hbm, xs_hbm, d0_v, d1_v, rows_v, rsem, s0sem, s1sem):
        wid = _sc_worker_id()
        base = wid * per_w
        pltpu.sync_copy(d0_hbm.at[wid], d0_v)
        pltpu.sync_copy(d1_hbm.at[wid], d1_v)

        def read(c, slot):
            return pltpu.make_async_copy(x_hbm.at[pl.ds(base + c * SC_ROWS, SC_ROWS)],
                                         rows_v.at[slot], rsem.at[slot])

        def scat(idx_v, sem, c, slot):
            return pltpu.make_async_copy(rows_v.at[slot], xs_hbm.at[idx_v.at[c]], sem.at[slot])

        def start_scatters(c, slot):
            scat(d0_v, s0sem, c, slot).start()
            scat(d1_v, s1sem, c, slot).start()

        def wait_scatters(c, slot):
            scat(d0_v, s0sem, c, slot).wait()
            scat(d1_v, s1sem, c, slot).wait()

        read(0, 0).start()

        def pair(j, carry):
            c0 = 2 * j
            read(c0, 0).wait()

            @pl.when(j > 0)
            def _():
                wait_scatters(c0 - 1, 1)

            read(c0 + 1, 1).start()
            start_scatters(c0, 0)
            read(c0 + 1, 1).wait()
            wait_scatters(c0, 0)

            @pl.when(c0 + 2 < n_chunks)
            def _():
                read(c0 + 2, 0).start()

            start_scatters(c0 + 1, 1)
            return carry

        lax.fori_loop(0, n_chunks // 2, pair, 0)
        wait_scatters(n_chunks - 1, 1)

    return pl.kernel(
        body,
        out_type=jax.ShapeDtypeStruct((n_rows, d), x2d.dtype),
        mesh=_sc_mesh(),
        scratch_types=[pltpu.VMEM((n_chunks, SC_ROWS), jnp.int32), pltpu.VMEM((n_chunks, SC_ROWS), jnp.int32),
                       pltpu.VMEM((2, SC_ROWS, d), x2d.dtype), pltpu.SemaphoreType.DMA((2,)),
                       pltpu.SemaphoreType.DMA((2,)), pltpu.SemaphoreType.DMA((2,))],
        name="sc_dispatch",
    )(x2d, d0, d1)


def _sc_gather(table, idx):
    _, d = table.shape
    b = idx.shape[0]
    per_w = b // SC_WORKERS
    n_chunks = per_w // SC_ROWS
    assert b == SC_WORKERS * n_chunks * SC_ROWS and n_chunks % 2 == 0
    idx3 = idx.reshape(SC_WORKERS, n_chunks, SC_ROWS)

    def body(table_hbm, idx_hbm, out_hbm, idx_v, rows_v, gsem, wsem):
        wid = _sc_worker_id()
        base = wid * per_w
        pltpu.sync_copy(idx_hbm.at[wid], idx_v)

        def gather(c, slot):
            return pltpu.make_async_copy(table_hbm.at[idx_v.at[c]], rows_v.at[slot], gsem.at[slot])

        def write(c, slot):
            return pltpu.make_async_copy(rows_v.at[slot],
                                         out_hbm.at[pl.ds(base + c * SC_ROWS, SC_ROWS)], wsem.at[slot])

        gather(0, 0).start()

        def pair(j, carry):
            c0 = 2 * j
            gather(c0, 0).wait()

            @pl.when(j > 0)
            def _():
                write(c0 - 1, 1).wait()

            gather(c0 + 1, 1).start()
            write(c0, 0).start()
            gather(c0 + 1, 1).wait()
            write(c0, 0).wait()

            @pl.when(c0 + 2 < n_chunks)
            def _():
                gather(c0 + 2, 0).start()

            write(c0 + 1, 1).start()
            return carry

        lax.fori_loop(0, n_chunks // 2, pair, 0)
        write(n_chunks - 1, 1).wait()

    return pl.kernel(
        body,
        out_type=jax.ShapeDtypeStruct((b, d), table.dtype),
        mesh=_sc_mesh(),
        scratch_types=[pltpu.VMEM((n_chunks, SC_ROWS), jnp.int32), pltpu.VMEM((2, SC_ROWS, d), table.dtype),
                       pltpu.SemaphoreType.DMA((2,)), pltpu.SemaphoreType.DMA((2,))],
        name="sc_combine_gather",
    )(table, idx3)


def _moe_kernel(blk_e, n_used, valid, xs_ref, wg_ref, wu_ref, wd_ref, y_ref, *, chunks):
    m = pl.program_id(0)
    f = pl.program_id(1)

    @pl.when(f == 0)
    def _():
        y_ref[...] = jnp.zeros_like(y_ref)

    def expert_rows(n_rows):
        row = lax.broadcasted_iota(jnp.int32, (n_rows, xs_ref.shape[1]), 0)
        x = jnp.where(row < valid[m], xs_ref[0:n_rows, :], 0.0).astype(BF16)
        for lo, hi in chunks:
            g = _dot(x, wg_ref[:, lo:hi])
            u = _dot(x, wu_ref[:, lo:hi])
            h = (g * _sigmoid(g) * u).astype(BF16)
            y_ref[0:n_rows, :] += _dot(h, wd_ref[lo:hi, :])

    half = xs_ref.shape[0] // 2

    @pl.when((m < n_used[0]) & (valid[m] > half))
    def _():
        expert_rows(xs_ref.shape[0])

    @pl.when((m < n_used[0]) & (valid[m] <= half))
    def _():
        expert_rows(half)


def _cast_kernel(*refs):
    n = len(refs) // 2
    for src_ref, dst_ref in zip(refs[:n], refs[n:]):
        dst_ref[...] = src_ref[...].astype(dst_ref.dtype)


def _to_bf16(*ws):
    e = ws[0].shape[0]
    parts = 4
    specs = [pl.BlockSpec((None, w.shape[1] // parts, w.shape[2]), lambda i, j: (i, j, 0)) for w in ws]
    return pl.pallas_call(
        _cast_kernel,
        grid=(e, parts),
        in_specs=specs,
        out_specs=specs,
        out_shape=[jax.ShapeDtypeStruct(w.shape, BF16) for w in ws],
        compiler_params=_params(2),
        name="expert_weight_cast",
    )(*ws)


def _moe_grouped(xs, blk_e, n_used, valid, wg, wu, wd, *, tm, tf, fchunk=1024):
    n_rows, d = xs.shape
    fdim = wg.shape[2]
    nf = fdim // tf
    assert nf * tf == fdim
    n_blocks = n_rows // tm
    chunks = tuple((lo, min(lo + fchunk, tf)) for lo in range(0, tf, fchunk))

    def f_eff(m, f, nu):
        mm = jnp.minimum(m, nu[0] - 1)
        snake = jnp.where(mm % 2 == 0, f, nf - 1 - f)
        return jnp.where(m < nu[0], snake, jnp.where(mm % 2 == 0, nf - 1, 0))

    grid_spec = pltpu.PrefetchScalarGridSpec(
        num_scalar_prefetch=3,
        grid=(n_blocks, nf),
        in_specs=[
            pl.BlockSpec((tm, d), lambda m, f, be, nu, va: (jnp.minimum(m, nu[0] - 1), 0)),
            pl.BlockSpec((None, d, tf), lambda m, f, be, nu, va: (be[m], 0, f_eff(m, f, nu))),
            pl.BlockSpec((None, d, tf), lambda m, f, be, nu, va: (be[m], 0, f_eff(m, f, nu))),
            pl.BlockSpec((None, tf, d), lambda m, f, be, nu, va: (be[m], f_eff(m, f, nu), 0)),
        ],
        out_specs=pl.BlockSpec((tm, d), lambda m, f, be, nu, va: (m, 0)),
    )
    return pl.pallas_call(
        functools.partial(_moe_kernel, chunks=chunks),
        grid_spec=grid_spec,
        out_shape=jax.ShapeDtypeStruct((n_rows, d), F32),
        compiler_params=pltpu.CompilerParams(dimension_semantics=("arbitrary", "arbitrary"),
                                             vmem_limit_bytes=MOE_VMEM_LIMIT),
        name="moe_grouped",
    )(blk_e, n_used, valid, xs, wg, wu, wd)


def _final_kernel(x_ref, y0_ref, y1_ref, route_ref, p_ref, ln_g_ref, ln_b_ref, wpg_ref, bpg_ref,
                  wpp_ref, *rest):
    o_ref = rest[-1]
    route = route_ref[...]
    f = route[:, 4:5] * y0_ref[...] + route[:, 5:6] * y1_ref[...]
    y = _layer_norm(DEEPNORM_ALPHA * x_ref[...] + f, ln_g_ref[...], ln_b_ref[...])
    o_ref[...] = _ple(y, p_ref[...], wpg_ref, bpg_ref, wpp_ref)


def _final_part(x2d, y01, route, p_all, layer, ln_g, ln_b, wpg, bpg, wpp, *, part, n_parts, prev, tile):
    n, d = x2d.shape
    pd = p_all.shape[2]
    nblk = n // n_parts // tile
    off = part * nblk
    row = lambda t: t[None, :]
    tok = lambda width: pl.BlockSpec((tile, width), lambda i: (i + off, 0))
    in_specs = [tok(d), pl.BlockSpec((tile, d), lambda i: (i, 0)),
                pl.BlockSpec((tile, d), lambda i: (i + nblk, 0)), tok(LANES),
                pl.BlockSpec((None, tile, pd), lambda i: (layer, i + off, 0)),
                _const_spec((1, d)), _const_spec((1, d)),
                _const_spec((d, d)), _const_spec((1, d)), _const_spec((pd, d))]
    args = [x2d, y01, y01, route, p_all, row(ln_g), row(ln_b), wpg.astype(BF16), row(bpg), wpp.astype(BF16)]
    aliases = {}
    if prev is not None:
        in_specs.append(pl.BlockSpec(memory_space=pl.ANY))
        args.append(prev)
        aliases = {len(args) - 1: 0}
    return pl.pallas_call(
        _final_kernel,
        grid=(nblk,),
        in_specs=in_specs,
        out_specs=tok(d),
        out_shape=jax.ShapeDtypeStruct((n, d), F32),
        input_output_aliases=aliases,
        compiler_params=_params(1),
        name="moe_combine_ple",
    )(*args)


def _combine(x2d, y, dest, route, p_all, layer, ln_g, ln_b, wpg, bpg, wpp, *, n_parts=2, tile=1024):
    n = x2d.shape[0]
    step = n // n_parts
    tile = min(tile, step)
    out = None
    for part in range(n_parts):
        rows = slice(part * step, (part + 1) * step)
        y01 = _sc_gather(y, jnp.concatenate([dest[0][rows], dest[1][rows]]))
        out = _final_part(x2d, y01, route, p_all, layer, ln_g, ln_b, wpg, bpg, wpp,
                          part=part, n_parts=n_parts, prev=out, tile=tile)
    return out


def _even_layer(x, p_all, layer_idx, w_in, a_ln_g, a_ln_b, a_ws, a_bs, w_gate_up, b_gate, norm_g, w_out,
                ln1_g, ln1_b, wg, wu, wd, ln2_g, ln2_b, wpp, wpg, bpg):
    bn, s, d = x.shape
    x1 = _even_mixer(x, w_in, a_ln_g, a_ln_b, a_ws, a_bs, w_gate_up, b_gate, norm_g, w_out, ln1_g, ln1_b)
    x2 = _ffn_ple(x1.reshape(bn * s, d), p_all, layer_idx, wg, wu, wd, ln2_g, ln2_b, wpg, bpg, wpp)
    return x2.reshape(bn, s, d)


def _odd_layer(x, p_all, layer_idx, w_qkv, lam_q1, lam_k1, lam_q2, lam_k2, subln_g, w_out, ln1_g, ln1_b,
               w_router, ewg, ewu, ewd, ln2_g, ln2_b, wpp, wpg, bpg, *, attn_blk=1024, moe_tm=512,
               moe_tf=3584):
    bn, s, d = x.shape
    n = bn * s
    lambda_init = 0.8 - 0.6 * math.exp(-0.3 * layer_idx)
    q, kt, v_ext = _qkv(x, w_qkv, dh=lam_q1.shape[0], tile=attn_blk)
    o = _diff_attn(q, kt, v_ext, lam_q1, lam_k1, lam_q2, lam_k2, subln_g, lambda_init, blk=attn_blk)
    x3, route, cnt = _attn_out(o.reshape(n, d), x.reshape(n, d), w_out, ln1_g, ln1_b, w_router)
    dest, blk_e, n_used, valid, n_rows = _route_tables(route, cnt, tm=moe_tm)
    xs = _sc_dispatch(x3, dest, n_rows)
    wg_b, wu_b, wd_b = _to_bf16(ewg, ewu, ewd)
    y = _moe_grouped(xs, blk_e, n_used, valid, wg_b, wu_b, wd_b,
                     tm=moe_tm, tf=moe_tf)
    out = _combine(x3, y, dest, route, p_all, layer_idx, ln2_g, ln2_b, wpg, bpg, wpp)
    return out.reshape(bn, s, d)


def kernel(x, p, e_w_in, e_a_ln_g, e_a_ln_b, e_a_ws, e_a_bs, e_b_w_gate_up, e_b_b_gate, e_b_norm_g, e_w_out, e_ln1_g, e_ln1_b, e_ffn_wg, e_ffn_wu, e_ffn_wd, e_ln2_g, e_ln2_b, o_w_qkv, o_lam_q1, o_lam_k1, o_lam_q2, o_lam_k2, o_subln_g, o_w_out, o_ln1_g, o_ln1_b, o_router, o_exp_wg, o_exp_wu, o_exp_wd, o_ln2_g, o_ln2_b, ple_w_proj, ple_w_gate, ple_b_gate):
    p_all = p.reshape(p.shape[0], -1, p.shape[-1])
    for i in range(DEPTH):
        j = i // 2
        if i % 2 == 0:
            x = _even_layer(x, p_all, i, e_w_in[j], e_a_ln_g[j], e_a_ln_b[j], e_a_ws[j], e_a_bs[j],
                            e_b_w_gate_up[j], e_b_b_gate[j], e_b_norm_g[j], e_w_out[j],
                            e_ln1_g[j], e_ln1_b[j], e_ffn_wg[j], e_ffn_wu[j], e_ffn_wd[j],
                            e_ln2_g[j], e_ln2_b[j], ple_w_proj[i], ple_w_gate[i], ple_b_gate[i])
        else:
            x = _odd_layer(x, p_all, i, o_w_qkv[j], o_lam_q1[j], o_lam_k1[j], o_lam_q2[j], o_lam_k2[j],
                           o_subln_g[j], o_w_out[j], o_ln1_g[j], o_ln1_b[j], o_router[j],
                           o_exp_wg[j], o_exp_wu[j], o_exp_wd[j], o_ln2_g[j], o_ln2_b[j],
                           ple_w_proj[i], ple_w_gate[i], ple_b_gate[i])
    return x
```

```python
import functools
import math

import jax
import jax.numpy as jnp
from jax import lax
from jax.experimental import pallas as pl
from jax.experimental.pallas import tpu as pltpu
from jax.experimental.pallas import tpu_sc as plsc

F32 = jnp.float32
BF16 = jnp.bfloat16

DEPTH = 2
DEEPNORM_ALPHA = (2.0 * DEPTH) ** 0.25
LN_EPS = 1e-5
A_CHUNK = 128
A_GROUPS = 8
B_HEADS = 4
B_CHUNK = 64
B_TAU = 16.0
CS_ROWS = 256
C_HEADS = 8
N_EXPERTS = 8
TOP_K = 2
LANES = 128
LOG2_E = 1.4426950408889634
VMEM_LIMIT = 56 * 1024 * 1024
MOE_VMEM_LIMIT = 60 * 1024 * 1024

NT_DIMS = (((1,), (1,)), ((), ()))
TN_DIMS = (((0,), (0,)), ((), ()))


def _dot(a, b):
    return jnp.dot(a, b, preferred_element_type=F32)


def _layer_norm(x, g, b):
    mu = jnp.mean(x, axis=-1, keepdims=True)
    xc = x - mu
    var = jnp.mean(xc * xc, axis=-1, keepdims=True)
    return xc * lax.rsqrt(var + LN_EPS) * g + b


def _sigmoid(x):
    return 1.0 / (1.0 + jnp.exp(-x))


def _split3(a):
    hi = a.astype(BF16)
    r1 = a - hi.astype(F32)
    mid = r1.astype(BF16)
    lo = (r1 - mid.astype(F32)).astype(BF16)
    return hi, mid, lo


def _const_spec(shape):
    zeros = (0,) * len(shape)
    return pl.BlockSpec(shape, lambda *_: zeros, pipeline_mode=pl.Buffered(1))


def _params(n_axes):
    return pltpu.CompilerParams(dimension_semantics=("arbitrary",) * n_axes,
                                vmem_limit_bytes=VMEM_LIMIT)


def _even_mixer_kernel(x_ref, w_in_ref, a_g_ref, a_b_ref, wcat_ref, abias_ref, mstack_ref,
                       w_up_ref, b_gate_ref, ng_ref, w_out_ref, ln_g_ref, ln_b_ref,
                       o_ref, state_ref, *, tile, aw, dkh, dvh):
    @pl.when(pl.program_id(1) == 0)
    def _():
        state_ref[...] = jnp.zeros_like(state_ref)

    hk = B_HEADS * dkh
    hv = B_HEADS * dvh
    x = x_ref[...]
    z = _dot(x.astype(BF16), w_in_ref[...])

    u = jax.nn.gelu(z[:, 0:aw])
    v = _layer_norm(jax.nn.gelu(z[:, aw:2 * aw]), a_g_ref[...], a_b_ref[...])
    gd = aw // A_GROUPS
    rows = lax.broadcasted_iota(jnp.int32, (A_CHUNK, A_GROUPS * A_CHUNK), 0)
    cols = lax.broadcasted_iota(jnp.int32, (A_CHUNK, A_GROUPS * A_CHUNK), 1)
    wcat = jnp.where((cols % A_CHUNK) <= rows, wcat_ref[...], 0.0).astype(BF16)
    r_bd = lax.broadcasted_iota(jnp.int32, (A_GROUPS * A_CHUNK, aw), 0) // A_CHUNK
    c_bd = lax.broadcasted_iota(jnp.int32, (A_GROUPS * A_CHUNK, aw), 1) // gd
    mask_bd = r_bd == c_bd
    ya_parts = []
    for c in range(tile // A_CHUNK):
        sl = slice(c * A_CHUNK, (c + 1) * A_CHUNK)
        v_rep = jnp.concatenate([v[sl]] * A_GROUPS, axis=0)
        v_bd = jnp.where(mask_bd, v_rep, 0.0).astype(BF16)
        sg = _dot(wcat, v_bd) + abias_ref[...]
        ya_parts.append(u[sl] * sg)
    ya = jnp.concatenate(ya_parts, axis=0)

    o0 = 2 * aw
    q = z[:, o0:o0 + hk] * (dkh ** -0.5)
    k = z[:, o0 + hk:o0 + 2 * hk]
    vv = z[:, o0 + 2 * hk:o0 + 2 * hk + hv]
    r = z[:, o0 + 2 * hk + hv:o0 + 2 * hk + 2 * hv]
    g_low = z[:, o0 + 2 * hk + 2 * hv:]
    pre = _dot(g_low.astype(BF16), w_up_ref[...]) + b_gate_ref[...]
    log_a = (jnp.minimum(pre, 0.0) - jnp.log1p(jnp.exp(-jnp.abs(pre)))) * (1.0 / B_TAU)
    la_hi, la_mid, la_lo = _split3(log_a)
    ms = mstack_ref[...]
    b_parts, mid_parts, last_parts = [], [], []
    for t in range(tile // CS_ROWS):
        rs = slice(t * CS_ROWS, (t + 1) * CS_ROWS)
        cs = _dot(ms, la_hi[rs]) + _dot(ms, la_mid[rs]) + _dot(ms, la_lo[rs])
        b_parts.append(cs[0:CS_ROWS])
        mid_parts.append(cs[CS_ROWS:2 * CS_ROWS])
        last_parts.append(cs[2 * CS_ROWS:3 * CS_ROWS])
    b_cum = jnp.concatenate(b_parts, axis=0)
    d_mid = jnp.concatenate(mid_parts, axis=0)
    d_last = jnp.concatenate(last_parts, axis=0)
    qe = (q * jnp.exp(d_mid)).astype(BF16)
    ke = k * jnp.exp(-d_mid)
    kd = (k * jnp.exp(d_last)).astype(BF16)
    qb = (q * jnp.exp(b_cum)).astype(BF16)
    dec = jnp.exp(b_cum + d_last)
    vvb = vv.astype(BF16)

    mask_kk = (lax.broadcasted_iota(jnp.int32, (hk, hk), 0) // dkh
               == lax.broadcasted_iota(jnp.int32, (hk, hk), 1) // dkh)
    mask_vbd = (lax.broadcasted_iota(jnp.int32, (hk, hv), 0) // dkh
                == lax.broadcasted_iota(jnp.int32, (hk, hv), 1) // dvh)
    mask_st = (lax.broadcasted_iota(jnp.int32, (hv, hk), 0) // dvh
               == lax.broadcasted_iota(jnp.int32, (hv, hk), 1) // dkh)
    causal = ((lax.broadcasted_iota(jnp.int32, (B_CHUNK, hk), 1) % B_CHUNK)
              <= lax.broadcasted_iota(jnp.int32, (B_CHUNK, hk), 0))

    st = state_ref[...]
    o_parts = []
    for c in range(tile // B_CHUNK):
        sl = slice(c * B_CHUNK, (c + 1) * B_CHUNK)
        ke_bd = jnp.where(mask_kk, jnp.concatenate([ke[sl]] * B_HEADS, axis=0), 0.0).astype(BF16)
        s_cat = lax.dot_general(qe[sl], ke_bd, NT_DIMS, preferred_element_type=F32)
        s_cat = jnp.where(causal, s_cat, 0.0).astype(BF16)
        v_bd = jnp.where(mask_vbd, jnp.concatenate([vv[sl]] * B_HEADS, axis=0), 0.0).astype(BF16)
        o_c = _dot(s_cat, v_bd) + lax.dot_general(qb[sl], st.astype(BF16), NT_DIMS,
                                                   preferred_element_type=F32)
        kv_t = lax.dot_general(vvb[sl], kd[sl], TN_DIMS, preferred_element_type=F32)
        st = dec[c * B_CHUNK:c * B_CHUNK + 1] * st + jnp.where(mask_st, kv_t, 0.0)
        o_parts.append(o_c)
    state_ref[...] = st
    o = jnp.concatenate(o_parts, axis=0)

    yb_parts = []
    for h in range(B_HEADS):
        oh = o[:, h * dvh:(h + 1) * dvh]
        msq = jnp.mean(oh * oh, axis=-1, keepdims=True)
        yb_parts.append(oh * lax.rsqrt(msq + LN_EPS))
    yb = jnp.concatenate(yb_parts, axis=1) * ng_ref[...] * (r * _sigmoid(r))

    y_cat = jnp.concatenate([ya, yb], axis=1).astype(BF16)
    m = _dot(y_cat, w_out_ref[...])
    o_ref[...] = _layer_norm(DEEPNORM_ALPHA * x + m, ln_g_ref[...], ln_b_ref[...])


def _gla_cumsum_matrices(tile):
    i = jnp.arange(tile)[:, None]
    j = jnp.arange(tile)[None, :]
    same = (i // B_CHUNK) == (j // B_CHUNK)
    m_cum = same & (j <= i)
    m_mid = same & (j <= (i // B_CHUNK) * B_CHUNK + B_CHUNK // 2 - 1)
    m_last = same
    f = lambda t: t.astype(F32)
    return jnp.concatenate([f(m_cum), f(m_cum) - f(m_mid), f(m_last) - f(m_cum)], axis=0).astype(BF16)


def _even_mixer(x, w_in, a_ln_g, a_ln_b, a_ws, a_bs, w_gate_up, b_gate, norm_g, w_out, ln_g, ln_b,
                *, tile=1024):
    bn, s, d = x.shape
    aw = a_ln_g.shape[0]
    hk = w_gate_up.shape[1]
    dkh = hk // B_HEADS
    dvh = norm_g.shape[0]
    hv = B_HEADS * dvh
    rank = w_gate_up.shape[0]
    main = 2 * aw + 2 * hk + 2 * hv
    gd = aw // A_GROUPS
    w_in_p = jnp.concatenate([w_in[:, :main], jnp.pad(w_in[:, main:], ((0, 0), (0, LANES - rank)))],
                             axis=1).astype(BF16)
    w_up_p = jnp.pad(w_gate_up, ((0, LANES - rank), (0, 0))).astype(BF16)
    wcat = jnp.transpose(a_ws, (1, 0, 2)).reshape(A_CHUNK, A_GROUPS * A_CHUNK)
    abias = jnp.repeat(a_bs.T, gd, axis=1)
    mstack = _gla_cumsum_matrices(CS_ROWS)
    ng = jnp.tile(norm_g, B_HEADS)[None, :]
    row = lambda t: t[None, :]
    kern = functools.partial(_even_mixer_kernel, tile=tile, aw=aw, dkh=dkh, dvh=dvh)
    tile_spec = pl.BlockSpec((None, tile, d), lambda b, i: (b, i, 0))
    return pl.pallas_call(
        kern,
        grid=(bn, s // tile),
        in_specs=[tile_spec, _const_spec(w_in_p.shape), _const_spec((1, aw)), _const_spec((1, aw)),
                  _const_spec(wcat.shape), _const_spec(abias.shape), _const_spec(mstack.shape),
                  _const_spec(w_up_p.shape), _const_spec((1, hk)), _const_spec((1, hv)),
                  _const_spec(w_out.shape), _const_spec((1, d)), _const_spec((1, d))],
        out_specs=tile_spec,
        out_shape=jax.ShapeDtypeStruct(x.shape, F32),
        scratch_shapes=[pltpu.VMEM((hv, hk), F32)],
        compiler_params=_params(2),
        name="even_mixer",
    )(x, w_in_p, row(a_ln_g), row(a_ln_b), wcat, abias, mstack, w_up_p, row(b_gate), ng,
      w_out.astype(BF16), row(ln_g), row(ln_b))


def _ple(y, p, wpg_ref, bpg_ref, wpp_ref):
    gate = _sigmoid(_dot(y.astype(BF16), wpg_ref[...]) + bpg_ref[...])
    return y + gate * _dot(p.astype(BF16), wpp_ref[...])


def _ffn_ple_kernel(x_ref, p_ref, wg_ref, wu_ref, wd_ref, ln_g_ref, ln_b_ref, wpg_ref, bpg_ref,
                    wpp_ref, o_ref, *, chunks):
    x = x_ref[...]
    xb = x.astype(BF16)
    acc = None
    for lo, hi in chunks:
        g = _dot(xb, wg_ref[:, lo:hi])
        u = _dot(xb, wu_ref[:, lo:hi])
        h = (g * _sigmoid(g) * u).astype(BF16)
        part = _dot(h, wd_ref[lo:hi, :])
        acc = part if acc is None else acc + part
    y = _layer_norm(DEEPNORM_ALPHA * x + acc, ln_g_ref[...], ln_b_ref[...])
    o_ref[...] = _ple(y, p_ref[...], wpg_ref, bpg_ref, wpp_ref)


def _ffn_ple(x2d, p_all, layer, wg, wu, wd, ln_g, ln_b, wpg, bpg, wpp, *, tile=512, fchunk=1024):
    n, d = x2d.shape
    f = wg.shape[1]
    pd = p_all.shape[2]
    chunks = tuple((lo, min(lo + fchunk, f)) for lo in range(0, f, fchunk))
    row = lambda t: t[None, :]
    return pl.pallas_call(
        functools.partial(_ffn_ple_kernel, chunks=chunks),
        grid=(n // tile,),
        in_specs=[pl.BlockSpec((tile, d), lambda i: (i, 0)),
                  pl.BlockSpec((None, tile, pd), lambda i: (layer, i, 0)),
                  _const_spec((d, f)), _const_spec((d, f)), _const_spec((f, d)),
                  _const_spec((1, d)), _const_spec((1, d)),
                  _const_spec((d, d)), _const_spec((1, d)), _const_spec((pd, d))],
        out_specs=pl.BlockSpec((tile, d), lambda i: (i, 0)),
        out_shape=jax.ShapeDtypeStruct((n, d), F32),
        compiler_params=_params(1),
        name="ffn_ple",
    )(x2d, p_all, wg.astype(BF16), wu.astype(BF16), wd.astype(BF16), row(ln_g), row(ln_b),
      wpg.astype(BF16), row(bpg), wpp.astype(BF16))


def _qkv_kernel(x_ref, w_ref, q_ref, kt_ref, v_ref, *, hd, scale):
    z = _dot(x_ref[...].astype(BF16), w_ref[...])
    q_ref[...] = (z[:, 0:hd] * scale).astype(BF16)
    k_t = z[:, hd:2 * hd].T.astype(BF16)
    heads, dk2, tile = kt_ref.shape
    ones = jnp.ones((tile, dk2), BF16)
    for h in range(heads):
        kt_ref[h] = k_t[h * dk2:(h + 1) * dk2, :]
        v_ref[:, 2 * h * dk2:(2 * h + 1) * dk2] = z[:, 2 * hd + h * dk2:2 * hd + (h + 1) * dk2].astype(BF16)
        v_ref[:, (2 * h + 1) * dk2:(2 * h + 2) * dk2] = ones


def _qkv(x, w_qkv, *, dh, tile):
    bn, s, d = x.shape
    hd = w_qkv.shape[1] // 3
    heads = hd // (2 * dh)
    return pl.pallas_call(
        functools.partial(_qkv_kernel, hd=hd, scale=dh ** -0.5 * LOG2_E),
        grid=(bn, s // tile),
        in_specs=[pl.BlockSpec((None, tile, d), lambda b, i: (b, i, 0)), _const_spec(w_qkv.shape)],
        out_specs=[pl.BlockSpec((None, tile, hd), lambda b, i: (b, i, 0)),
                   pl.BlockSpec((None, heads, None, 2 * dh, tile), lambda b, i: (b, 0, i, 0, 0)),
                   pl.BlockSpec((None, tile, 2 * hd), lambda b, i: (b, i, 0))],
        out_shape=[jax.ShapeDtypeStruct((bn, s, hd), BF16),
                   jax.ShapeDtypeStruct((bn, heads, s // tile, 2 * dh, tile), BF16),
                   jax.ShapeDtypeStruct((bn, s, 2 * hd), BF16)],
        compiler_params=_params(2),
        name="qkv_proj",
    )(x, w_qkv.astype(BF16))


def _diff_attn_kernel(q_ref, kt_ref, v_ref, lam_ref, g_ref, o_ref, m_ref, acc_ref,
                      qm_ref, sa_ref, sb_ref, *, blk, nb, dh, lambda_init):
    pairs = [(qi, ki) for qi in range(nb) for ki in range(qi + 1)]
    bufs = (sa_ref, sb_ref)
    half = blk // 2

    def load_queries(qi):
        q = q_ref[qi * blk:(qi + 1) * blk, :]
        lane = lax.broadcasted_iota(jnp.int32, q.shape, 1)
        zero = jnp.zeros_like(q)
        qm_ref[0] = jnp.where(lane < dh, q, zero)
        qm_ref[1] = jnp.where(lane >= dh, q, zero)

    def scores(ki, s_ref, diagonal):
        kt_blk = kt_ref[ki]
        for c in range(2):
            if diagonal:
                s_ref[c, 0:half, 0:half] = _dot(qm_ref[c, 0:half], kt_blk[:, 0:half])
                s_ref[c, half:blk, :] = _dot(qm_ref[c, half:blk], kt_blk)
            else:
                s_ref[c] = _dot(qm_ref[c], kt_blk)

    def update(c, rows, s, v_part):
        m_prev = m_ref[c, rows]
        m_new = jnp.maximum(m_prev, jnp.max(s, axis=-1, keepdims=True))
        corr = jnp.exp2(m_prev - m_new)
        p = jnp.exp2(s - jnp.concatenate([m_new] * (s.shape[1] // LANES), axis=1))
        acc_ref[c, rows] = (jnp.concatenate([corr, corr], axis=1) * acc_ref[c, rows]
                            + _dot(p.astype(BF16), v_part))
        m_ref[c, rows] = m_new

    def softmax_pv(s_ref, ki, masked):
        v_blk = v_ref[ki * blk:(ki + 1) * blk, :]
        for c in range(2):
            if not masked:
                update(c, slice(0, blk), s_ref[c], v_blk)
                continue
            s_top = s_ref[c, 0:half, 0:half]
            row = lax.broadcasted_iota(jnp.int32, s_top.shape, 0)
            col = lax.broadcasted_iota(jnp.int32, s_top.shape, 1)
            update(c, slice(0, half), jnp.where(col <= row, s_top, -1e30), v_blk[0:half])
            s_bot = s_ref[c, half:blk, :]
            row = lax.broadcasted_iota(jnp.int32, s_bot.shape, 0) + half
            col = lax.broadcasted_iota(jnp.int32, s_bot.shape, 1)
            update(c, slice(half, blk), jnp.where(col <= row, s_bot, -1e30), v_blk)

    def finalize(qi):
        a1 = acc_ref[0, :, 0:2 * dh] * (1.0 / acc_ref[0, :, 2 * dh:4 * dh])
        a2 = acc_ref[1, :, 0:2 * dh] * (1.0 / acc_ref[1, :, 2 * dh:4 * dh])
        lv = lam_ref[...]
        lam = (jnp.exp(jnp.sum(lv[0:1] * lv[1:2], axis=-1, keepdims=True))
               - jnp.exp(jnp.sum(lv[2:3] * lv[3:4], axis=-1, keepdims=True)) + lambda_init)
        o = a1 - lam * a2
        msq = jnp.mean(o * o, axis=-1, keepdims=True)
        o_ref[qi * blk:(qi + 1) * blk, :] = (o * lax.rsqrt(msq + LN_EPS) * g_ref[...]
                                              * (1.0 - lambda_init)).astype(BF16)

    load_queries(0)
    scores(0, bufs[0], True)
    for t, (qi, ki) in enumerate(pairs):
        if t + 1 < len(pairs):
            nqi, nki = pairs[t + 1]
            if nki == 0:
                load_queries(nqi)
            scores(nki, bufs[(t + 1) % 2], nki == nqi)
        if ki == 0:
            m_ref[...] = jnp.full_like(m_ref, -1e30)
            acc_ref[...] = jnp.zeros_like(acc_ref)
        softmax_pv(bufs[t % 2], ki, masked=(ki == qi))
        if ki == qi:
            finalize(qi)


def _diff_attn(q, kt, v_ext, lam_q1, lam_k1, lam_q2, lam_k2, subln_g, lambda_init, *, blk):
    bn, s, hd = q.shape
    dh = lam_q1.shape[0]
    heads = hd // (2 * dh)
    nb = s // blk
    lamv = jnp.zeros((8, LANES), F32).at[0:4, 0:dh].set(jnp.stack([lam_q1, lam_k1, lam_q2, lam_k2]))
    return pl.pallas_call(
        functools.partial(_diff_attn_kernel, blk=blk, nb=nb, dh=dh, lambda_init=lambda_init),
        grid=(bn, heads),
        in_specs=[
            pl.BlockSpec((None, s, 2 * dh), lambda b, h: (b, 0, h)),
            pl.BlockSpec((None, None, nb, 2 * dh, blk), lambda b, h: (b, h, 0, 0, 0)),
            pl.BlockSpec((None, s, 4 * dh), lambda b, h: (b, 0, h)),
            pl.BlockSpec((8, LANES), lambda b, h: (0, 0)),
            pl.BlockSpec((1, 2 * dh), lambda b, h: (0, 0)),
        ],
        out_specs=pl.BlockSpec((None, s, 2 * dh), lambda b, h: (b, 0, h)),
        out_shape=jax.ShapeDtypeStruct((bn, s, hd), BF16),
        scratch_shapes=[pltpu.VMEM((2, blk, LANES), F32),
                        pltpu.VMEM((2, blk, 4 * dh), F32), pltpu.VMEM((2, blk, 2 * dh), BF16),
                        pltpu.VMEM((2, blk, blk), F32), pltpu.VMEM((2, blk, blk), F32)],
        compiler_params=_params(2),
        name="diff_attn",
    )(q, kt, v_ext, lamv, subln_g[None, :])


def _attn_out_kernel(o_ref, x_ref, w_ref, ln_g_ref, ln_b_ref, wr_ref, lstrict_ref,
                     y_ref, route_ref, cnt_ref, base_ref):
    @pl.when(pl.program_id(0) == 0)
    def _():
        base_ref[...] = jnp.zeros_like(base_ref)

    m = _dot(o_ref[...], w_ref[...])
    y = _layer_norm(DEEPNORM_ALPHA * x_ref[...] + m, ln_g_ref[...], ln_b_ref[...])
    y_ref[...] = y
    y_hi = y.astype(BF16)
    y_lo = (y - y_hi.astype(F32)).astype(BF16)
    hi_terms = _dot(y_hi, wr_ref[...])
    logits = hi_terms[:, 0:LANES] + hi_terms[:, LANES:2 * LANES] + _dot(y_lo, wr_ref[:, 0:LANES])

    lane = lax.broadcasted_iota(jnp.int32, logits.shape, 1).astype(F32)
    neg = -jnp.inf
    lg = jnp.where(lane < N_EXPERTS, logits, neg)
    v0 = jnp.max(lg, axis=-1, keepdims=True)
    i0 = jnp.min(jnp.where(lg == v0, lane, float(LANES)), axis=-1, keepdims=True)
    lg2 = jnp.where(lane == i0, neg, lg)
    v1 = jnp.max(lg2, axis=-1, keepdims=True)
    i1 = jnp.min(jnp.where(lg2 == v1, lane, float(LANES)), axis=-1, keepdims=True)
    e = jnp.exp(v1 - v0)
    g0 = 1.0 / (1.0 + e)
    g1 = e / (1.0 + e)
    oh0 = lane == i0
    oh1 = lane == i1
    c = jnp.where(oh0, 1.0, 0.0) + jnp.where(oh1, 1.0, 0.0)
    pre = _dot(lstrict_ref[...], c.astype(BF16)) + base_ref[...]
    r0 = jnp.sum(jnp.where(oh0, pre, 0.0), axis=-1, keepdims=True)
    r1 = jnp.sum(jnp.where(oh1, pre, 0.0), axis=-1, keepdims=True)
    base_ref[...] = base_ref[...] + jnp.sum(c, axis=0, keepdims=True)
    cnt_ref[...] = jnp.broadcast_to(base_ref[...], cnt_ref.shape)
    fields = (i0, i1, r0, r1, g0, g1)
    route = jnp.zeros_like(logits)
    for idx, val in enumerate(fields):
        route = jnp.where(lane == idx, val, route)
    route_ref[...] = route


def _attn_out(o2d, x2d, w_out, ln_g, ln_b, w_router, *, tile=1024):
    n, d = x2d.shape
    wr = jnp.pad(w_router, ((0, 0), (0, LANES - w_router.shape[1])))
    wr_hi = wr.astype(BF16)
    wr_lo = (wr - wr_hi.astype(F32)).astype(BF16)
    wr_split = jnp.concatenate([wr_hi, wr_lo], axis=1)
    lstrict = (jnp.arange(tile)[:, None] > jnp.arange(tile)[None, :]).astype(BF16)
    row = lambda t: t[None, :]
    tspec = pl.BlockSpec((tile, d), lambda i: (i, 0))
    return pl.pallas_call(
        _attn_out_kernel,
        grid=(n // tile,),
        in_specs=[tspec, tspec, _const_spec((d, d)), _const_spec((1, d)), _const_spec((1, d)),
                  _const_spec((d, 2 * LANES)), _const_spec((tile, tile))],
        out_specs=[tspec, pl.BlockSpec((tile, LANES), lambda i: (i, 0)),
                   pl.BlockSpec((8, LANES), lambda i: (0, 0))],
        out_shape=[jax.ShapeDtypeStruct((n, d), F32), jax.ShapeDtypeStruct((n, LANES), F32),
                   jax.ShapeDtypeStruct((8, LANES), F32)],
        scratch_shapes=[pltpu.VMEM((1, LANES), F32)],
        compiler_params=_params(1),
        name="attn_out_router",
    )(o2d, x2d, w_out.astype(BF16), row(ln_g), row(ln_b), wr_split, lstrict)


def _route_tables(route, cnt, *, tm):
    n = route.shape[0]
    counts = cnt[0, :N_EXPERTS].astype(jnp.int32)
    padded = (counts + tm - 1) // tm * tm
    pend = jnp.cumsum(padded)
    pstart = pend - padded

    def dest_of(k):
        e = route[:, k].astype(jnp.int32)
        start = jnp.sum(jnp.where(e[:, None] == jnp.arange(N_EXPERTS)[None, :], pstart[None, :], 0), axis=-1)
        return start + route[:, TOP_K + k].astype(jnp.int32)

    dest = (dest_of(0), dest_of(1))
    n_rows = n * TOP_K + N_EXPERTS * tm
    n_blocks = n_rows // tm
    n_used = pend[-1] // tm
    blk_idx = jnp.arange(n_blocks, dtype=jnp.int32)
    blk_start = jnp.minimum(blk_idx, n_used - 1) * tm
    blk_e = jnp.minimum(jnp.sum(blk_start[:, None] >= pend[None, :], axis=1), N_EXPERTS - 1)
    valid = jnp.clip(pstart[blk_e] + counts[blk_e] - blk_idx * tm, 0, tm)
    return (dest, blk_e.astype(jnp.int32), n_used.reshape(1).astype(jnp.int32),
            valid.astype(jnp.int32), n_rows)


SC_CORES = 2
SC_SUBCORES = 16
SC_WORKERS = SC_CORES * SC_SUBCORES
SC_ROWS = 32


def _sc_mesh():
    return plsc.VectorSubcoreMesh(core_axis_name="c", subcore_axis_name="s",
                                  num_cores=SC_CORES, num_subcores=SC_SUBCORES)


def _sc_worker_id():
    return lax.axis_index("s") * SC_CORES + lax.axis_index("c")


def _sc_dispatch(x2d, dest, n_rows):
    n, d = x2d.shape
    per_w = n // SC_WORKERS
    n_chunks = per_w // SC_ROWS
    assert n == SC_WORKERS * n_chunks * SC_ROWS and n_chunks % 2 == 0
    d0 = dest[0].reshape(SC_WORKERS, n_chunks, SC_ROWS)
    d1 = dest[1].reshape(SC_WORKERS, n_chunks, SC_ROWS)

    def body(x_hbm, d0_hbm, d1_hbm, xs_hbm, d0_v, d1_v, rows_v, rsem, s0sem, s1sem):
        wid = _sc_worker_id()
        base = wid * per_w
        pltpu.sync_copy(d0_hbm.at[wid], d0_v)
        pltpu.sync_copy(d1_hbm.at[wid], d1_v)

        def read(c, slot):
            return pltpu.make_async_copy(x_hbm.at[pl.ds(base + c * SC_ROWS, SC_ROWS)],
                                         rows_v.at[slot], rsem.at[slot])

        def scat(idx_v, sem, c, slot):
            return pltpu.make_async_copy(rows_v.at[slot], xs_hbm.at[idx_v.at[c]], sem.at[slot])

        def start_scatters(c, slot):
            scat(d0_v, s0sem, c, slot).start()
            scat(d1_v, s1sem, c, slot).start()

        def wait_scatters(c, slot):
            scat(d0_v, s0sem, c, slot).wait()
            scat(d1_v, s1sem, c, slot).wait()

        read(0, 0).start()

        def pair(j, carry):
            c0 = 2 * j
            read(c0, 0).wait()

            @pl.when(j > 0)
            def _():
                wait_scatters(c0 - 1, 1)

            read(c0 + 1, 1).start()
            start_scatters(c0, 0)
            read(c0 + 1, 1).wait()
            wait_scatters(c0, 0)

            @pl.when(c0 + 2 < n_chunks)
            def _():
                read(c0 + 2, 0).start()

            start_scatters(c0 + 1, 1)
            return carry

        lax.fori_loop(0, n_chunks // 2, pair, 0)
        wait_scatters(n_chunks - 1, 1)

    return pl.kernel(
        body,
        out_type=jax.ShapeDtypeStruct((n_rows, d), x2d.dtype),
        mesh=_sc_mesh(),
        scratch_types=[pltpu.VMEM((n_chunks, SC_ROWS), jnp.int32), pltpu.VMEM((n_chunks, SC_ROWS), jnp.int32),
                       pltpu.VMEM((2, SC_ROWS, d), x2d.dtype), pltpu.SemaphoreType.DMA((2,)),
                       pltpu.SemaphoreType.DMA((2,)), pltpu.SemaphoreType.DMA((2,))],
        name="sc_dispatch",
    )(x2d, d0, d1)


def _sc_gather(table, idx):
    _, d = table.shape
    b = idx.shape[0]
    per_w = b // SC_WORKERS
    n_chunks = per_w // SC_ROWS
    assert b == SC_WORKERS * n_chunks * SC_ROWS and n_chunks % 2 == 0
    idx3 = idx.reshape(SC_WORKERS, n_chunks, SC_ROWS)

    def body(table_hbm, idx_hbm, out_hbm, idx_v, rows_v, gsem, wsem):
        wid = _sc_worker_id()
        base = wid * per_w
        pltpu.sync_copy(idx_hbm.at[wid], idx_v)

        def gather(c, slot):
            return pltpu.make_async_copy(table_hbm.at[idx_v.at[c]], rows_v.at[slot], gsem.at[slot])

        def write(c, slot):
            return pltpu.make_async_copy(rows_v.at[slot],
                                         out_hbm.at[pl.ds(base + c * SC_ROWS, SC_ROWS)], wsem.at[slot])

        gather(0, 0).start()

        def pair(j, carry):
            c0 = 2 * j
            gather(c0, 0).wait()

            @pl.when(j > 0)
            def _():
                write(c0 - 1, 1).wait()

            gather(c0 + 1, 1).start()
            write(c0, 0).start()
            gather(c0 + 1, 1).wait()
            write(c0, 0).wait()

            @pl.when(c0 + 2 < n_chunks)
            def _():
                gather(c0 + 2, 0).start()

            write(c0 + 1, 1).start()
            return carry

        lax.fori_loop(0, n_chunks // 2, pair, 0)
        write(n_chunks - 1, 1).wait()

    return pl.kernel(
        body,
        out_type=jax.ShapeDtypeStruct((b, d), table.dtype),
        mesh=_sc_mesh(),
        scratch_types=[pltpu.VMEM((n_chunks, SC_ROWS), jnp.int32), pltpu.VMEM((2, SC_ROWS, d), table.dtype),
                       pltpu.SemaphoreType.DMA((2,)), pltpu.SemaphoreType.DMA((2,))],
        name="sc_combine_gather",
    )(table, idx3)


def _moe_kernel(blk_e, n_used, valid, xs_ref, wg_ref, wu_ref, wd_ref, y_ref, *, chunks):
    m = pl.program_id(0)
    f = pl.program_id(1)

    @pl.when(f == 0)
    def _():
        y_ref[...] = jnp.zeros_like(y_ref)

    def expert_rows(n_rows):
        row = lax.broadcasted_iota(jnp.int32, (n_rows, xs_ref.shape[1]), 0)
        x = jnp.where(row < valid[m], xs_ref[0:n_rows, :], 0.0).astype(BF16)
        for lo, hi in chunks:
            g = _dot(x, wg_ref[:, lo:hi])
            u = _dot(x, wu_ref[:, lo:hi])
            h = (g * _sigmoid(g) * u).astype(BF16)
            y_ref[0:n_rows, :] += _dot(h, wd_ref[lo:hi, :])

    half = xs_ref.shape[0] // 2

    @pl.when((m < n_used[0]) & (valid[m] > half))
    def _():
        expert_rows(xs_ref.shape[0])

    @pl.when((m < n_used[0]) & (valid[m] <= half))
    def _():
        expert_rows(half)


def _moe_grouped(xs, blk_e, n_used, valid, wg, wu, wd, *, tm, tf, fchunk=1024):
    n_rows, d = xs.shape
    fdim = wg.shape[2]
    nf = fdim // tf
    assert nf * tf == fdim
    n_blocks = n_rows // tm
    chunks = tuple((lo, min(lo + fchunk, tf)) for lo in range(0, tf, fchunk))

    def f_eff(m, f, nu):
        mm = jnp.minimum(m, nu[0] - 1)
        snake = jnp.where(mm % 2 == 0, f, nf - 1 - f)
        return jnp.where(m < nu[0], snake, jnp.where(mm % 2 == 0, nf - 1, 0))

    grid_spec = pltpu.PrefetchScalarGridSpec(
        num_scalar_prefetch=3,
        grid=(n_blocks, nf),
        in_specs=[
            pl.BlockSpec((tm, d), lambda m, f, be, nu, va: (jnp.minimum(m, nu[0] - 1), 0)),
            pl.BlockSpec((None, d, tf), lambda m, f, be, nu, va: (be[m], 0, f_eff(m, f, nu))),
            pl.BlockSpec((None, d, tf), lambda m, f, be, nu, va: (be[m], 0, f_eff(m, f, nu))),
            pl.BlockSpec((None, tf, d), lambda m, f, be, nu, va: (be[m], f_eff(m, f, nu), 0)),
        ],
        out_specs=pl.BlockSpec((tm, d), lambda m, f, be, nu, va: (m, 0)),
    )
    return pl.pallas_call(
        functools.partial(_moe_kernel, chunks=chunks),
        grid_spec=grid_spec,
        out_shape=jax.ShapeDtypeStruct((n_rows, d), F32),
        compiler_params=pltpu.CompilerParams(dimension_semantics=("arbitrary", "arbitrary"),
                                             vmem_limit_bytes=MOE_VMEM_LIMIT),
        name="moe_grouped",
    )(blk_e, n_used, valid, xs, wg, wu, wd)


def _final_kernel(x_ref, y0_ref, y1_ref, route_ref, p_ref, ln_g_ref, ln_b_ref, wpg_ref, bpg_ref,
                  wpp_ref, *rest):
    o_ref = rest[-1]
    route = route_ref[...]
    f = route[:, 4:5] * y0_ref[...] + route[:, 5:6] * y1_ref[...]
    y = _layer_norm(DEEPNORM_ALPHA * x_ref[...] + f, ln_g_ref[...], ln_b_ref[...])
    o_ref[...] = _ple(y, p_ref[...], wpg_ref, bpg_ref, wpp_ref)


def _final_part(x2d, y01, route, p_all, layer, ln_g, ln_b, wpg, bpg, wpp, *, part, n_parts, prev, tile):
    n, d = x2d.shape
    pd = p_all.shape[2]
    nblk = n // n_parts // tile
    off = part * nblk
    row = lambda t: t[None, :]
    tok = lambda width: pl.BlockSpec((tile, width), lambda i: (i + off, 0))
    in_specs = [tok(d), pl.BlockSpec((tile, d), lambda i: (i, 0)),
                pl.BlockSpec((tile, d), lambda i: (i + nblk, 0)), tok(LANES),
                pl.BlockSpec((None, tile, pd), lambda i: (layer, i + off, 0)),
                _const_spec((1, d)), _const_spec((1, d)),
                _const_spec((d, d)), _const_spec((1, d)), _const_spec((pd, d))]
    args = [x2d, y01, y01, route, p_all, row(ln_g), row(ln_b), wpg.astype(BF16), row(bpg), wpp.astype(BF16)]
    aliases = {}
    if prev is not None:
        in_specs.append(pl.BlockSpec(memory_space=pl.ANY))
        args.append(prev)
        aliases = {len(args) - 1: 0}
    return pl.pallas_call(
        _final_kernel,
        grid=(nblk,),
        in_specs=in_specs,
        out_specs=tok(d),
        out_shape=jax.ShapeDtypeStruct((n, d), F32),
        input_output_aliases=aliases,
        compiler_params=_params(1),
        name="moe_combine_ple",
    )(*args)


def _combine(x2d, y, dest, route, p_all, layer, ln_g, ln_b, wpg, bpg, wpp, *, n_parts=4, tile=1024):
    n = x2d.shape[0]
    step = n // n_parts
    tile = min(tile, step)
    out = None
    for part in range(n_parts):
        rows = slice(part * step, (part + 1) * step)
        y01 = _sc_gather(y, jnp.concatenate([dest[0][rows], dest[1][rows]]))
        out = _final_part(x2d, y01, route, p_all, layer, ln_g, ln_b, wpg, bpg, wpp,
                          part=part, n_parts=n_parts, prev=out, tile=tile)
    return out


def _even_layer(x, p_all, layer_idx, w_in, a_ln_g, a_ln_b, a_ws, a_bs, w_gate_up, b_gate, norm_g, w_out,
                ln1_g, ln1_b, wg, wu, wd, ln2_g, ln2_b, wpp, wpg, bpg):
    bn, s, d = x.shape
    x1 = _even_mixer(x, w_in, a_ln_g, a_ln_b, a_ws, a_bs, w_gate_up, b_gate, norm_g, w_out, ln1_g, ln1_b)
    x2 = _ffn_ple(x1.reshape(bn * s, d), p_all, layer_idx, wg, wu, wd, ln2_g, ln2_b, wpg, bpg, wpp)
    return x2.reshape(bn, s, d)


def _odd_layer(x, p_all, layer_idx, w_qkv, lam_q1, lam_k1, lam_q2, lam_k2, subln_g, w_out, ln1_g, ln1_b,
               w_router, ewg, ewu, ewd, ln2_g, ln2_b, wpp, wpg, bpg, *, attn_blk=1024, moe_tm=512,
               moe_tf=3584):
    bn, s, d = x.shape
    n = bn * s
    lambda_init = 0.8 - 0.6 * math.exp(-0.3 * layer_idx)
    q, kt, v_ext = _qkv(x, w_qkv, dh=lam_q1.shape[0], tile=attn_blk)
    o = _diff_attn(q, kt, v_ext, lam_q1, lam_k1, lam_q2, lam_k2, subln_g, lambda_init, blk=attn_blk)
    x3, route, cnt = _attn_out(o.reshape(n, d), x.reshape(n, d), w_out, ln1_g, ln1_b, w_router)
    dest, blk_e, n_used, valid, n_rows = _route_tables(route, cnt, tm=moe_tm)
    xs = _sc_dispatch(x3, dest, n_rows)
    y = _moe_grouped(xs, blk_e, n_used, valid, ewg.astype(BF16), ewu.astype(BF16), ewd.astype(BF16),
                     tm=moe_tm, tf=moe_tf)
    out = _combine(x3, y, dest, route, p_all, layer_idx, ln2_g, ln2_b, wpg, bpg, wpp)
    return out.reshape(bn, s, d)


def kernel(x, p, e_w_in, e_a_ln_g, e_a_ln_b, e_a_ws, e_a_bs, e_b_w_gate_up, e_b_b_gate, e_b_norm_g, e_w_out, e_ln1_g, e_ln1_b, e_ffn_wg, e_ffn_wu, e_ffn_wd, e_ln2_g, e_ln2_b, o_w_qkv, o_lam_q1, o_lam_k1, o_lam_q2, o_lam_k2, o_subln_g, o_w_out, o_ln1_g, o_ln1_b, o_router, o_exp_wg, o_exp_wu, o_exp_wd, o_ln2_g, o_ln2_b, ple_w_proj, ple_w_gate, ple_b_gate):
    p_all = p.reshape(p.shape[0], -1, p.shape[-1])
    for i in range(DEPTH):
        j = i // 2
        if i % 2 == 0:
            x = _even_layer(x, p_all, i, e_w_in[j], e_a_ln_g[j], e_a_ln_b[j], e_a_ws[j], e_a_bs[j],
                            e_b_w_gate_up[j], e_b_b_gate[j], e_b_norm_g[j], e_w_out[j],
                            e_ln1_g[j], e_ln1_b[j], e_ffn_wg[j], e_ffn_wu[j], e_ffn_wd[j],
                            e_ln2_g[j], e_ln2_b[j], ple_w_proj[i], ple_w_gate[i], ple_b_gate[i])
        else:
            x = _odd_layer(x, p_all, i, o_w_qkv[j], o_lam_q1[j], o_lam_k1[j], o_lam_q2[j], o_lam_k2[j],
                           o_subln_g[j], o_w_out[j], o_ln1_g[j], o_ln1_b[j], o_router[j],
                           o_exp_wg[j], o_exp_wu[j], o_exp_wd[j], o_ln2_g[j], o_ln2_b[j],
                           ple_w_proj[i], ple_w_gate[i], ple_b_gate[i])
    return x
```

```python
import functools
import math

import jax
import jax.numpy as jnp
from jax import lax
from jax.experimental import pallas as pl
from jax.experimental.pallas import tpu as pltpu
from jax.experimental.pallas import tpu_sc as plsc

F32 = jnp.float32
BF16 = jnp.bfloat16

DEPTH = 2
DEEPNORM_ALPHA = (2.0 * DEPTH) ** 0.25
LN_EPS = 1e-5
A_CHUNK = 128
A_GROUPS = 8
B_HEADS = 4
B_CHUNK = 64
B_TAU = 16.0
CS_ROWS = 256
N_EXPERTS = 8
TOP_K = 2
LANES = 128
LOG2_E = 1.4426950408889634
VMEM_LIMIT = 56 * 1024 * 1024
MOE_VMEM_LIMIT = 60 * 1024 * 1024

NT_DIMS = (((1,), (1,)), ((), ()))
TN_DIMS = (((0,), (0,)), ((), ()))


def _dot(a, b):
    return jnp.dot(a, b, preferred_element_type=F32)


def _layer_norm(x, g, b):
    mu = jnp.mean(x, axis=-1, keepdims=True)
    xc = x - mu
    var = jnp.mean(xc * xc, axis=-1, keepdims=True)
    return xc * lax.rsqrt(var + LN_EPS) * g + b


def _sigmoid(x):
    return 1.0 / (1.0 + jnp.exp(-x))


def _split3(a):
    hi = a.astype(BF16)
    r1 = a - hi.astype(F32)
    mid = r1.astype(BF16)
    lo = (r1 - mid.astype(F32)).astype(BF16)
    return hi, mid, lo


def _const_spec(shape):
    zeros = (0,) * len(shape)
    return pl.BlockSpec(shape, lambda *_: zeros, pipeline_mode=pl.Buffered(1))


def _params(n_axes):
    return pltpu.CompilerParams(dimension_semantics=("arbitrary",) * n_axes,
                                vmem_limit_bytes=VMEM_LIMIT)


def _even_mixer_kernel(x_ref, w_in_ref, a_g_ref, a_b_ref, wcat_ref, abias_ref, mstack_ref,
                       w_up_ref, b_gate_ref, ng_ref, w_out_ref, ln_g_ref, ln_b_ref,
                       o_ref, state_ref, *, tile, aw, dkh, dvh):
    @pl.when(pl.program_id(1) == 0)
    def _():
        state_ref[...] = jnp.zeros_like(state_ref)

    hk = B_HEADS * dkh
    hv = B_HEADS * dvh
    x = x_ref[...]
    z = _dot(x.astype(BF16), w_in_ref[...])

    u = jax.nn.gelu(z[:, 0:aw])
    v = _layer_norm(jax.nn.gelu(z[:, aw:2 * aw]), a_g_ref[...], a_b_ref[...])
    gd = aw // A_GROUPS
    rows = lax.broadcasted_iota(jnp.int32, (A_CHUNK, A_GROUPS * A_CHUNK), 0)
    cols = lax.broadcasted_iota(jnp.int32, (A_CHUNK, A_GROUPS * A_CHUNK), 1)
    wcat = jnp.where((cols % A_CHUNK) <= rows, wcat_ref[...], 0.0).astype(BF16)
    r_bd = lax.broadcasted_iota(jnp.int32, (A_GROUPS * A_CHUNK, aw), 0) // A_CHUNK
    c_bd = lax.broadcasted_iota(jnp.int32, (A_GROUPS * A_CHUNK, aw), 1) // gd
    mask_bd = r_bd == c_bd
    ya_parts = []
    for c in range(tile // A_CHUNK):
        sl = slice(c * A_CHUNK, (c + 1) * A_CHUNK)
        v_rep = jnp.concatenate([v[sl]] * A_GROUPS, axis=0)
        v_bd = jnp.where(mask_bd, v_rep, 0.0).astype(BF16)
        sg = _dot(wcat, v_bd) + abias_ref[...]
        ya_parts.append(u[sl] * sg)
    ya = jnp.concatenate(ya_parts, axis=0)

    o0 = 2 * aw
    q = z[:, o0:o0 + hk] * (dkh ** -0.5)
    k = z[:, o0 + hk:o0 + 2 * hk]
    vv = z[:, o0 + 2 * hk:o0 + 2 * hk + hv]
    r = z[:, o0 + 2 * hk + hv:o0 + 2 * hk + 2 * hv]
    g_low = z[:, o0 + 2 * hk + 2 * hv:]
    pre = _dot(g_low.astype(BF16), w_up_ref[...]) + b_gate_ref[...]
    log_a = (jnp.minimum(pre, 0.0) - jnp.log1p(jnp.exp(-jnp.abs(pre)))) * (1.0 / B_TAU)
    la_hi, la_mid, la_lo = _split3(log_a)
    ms = mstack_ref[...]
    b_parts, mid_parts, last_parts = [], [], []
    for t in range(tile // CS_ROWS):
        rs = slice(t * CS_ROWS, (t + 1) * CS_ROWS)
        cs = _dot(ms, la_hi[rs]) + _dot(ms, la_mid[rs]) + _dot(ms, la_lo[rs])
        b_parts.append(cs[0:CS_ROWS])
        mid_parts.append(cs[CS_ROWS:2 * CS_ROWS])
        last_parts.append(cs[2 * CS_ROWS:3 * CS_ROWS])
    b_cum = jnp.concatenate(b_parts, axis=0)
    d_mid = jnp.concatenate(mid_parts, axis=0)
    d_last = jnp.concatenate(last_parts, axis=0)
    qe = (q * jnp.exp(d_mid)).astype(BF16)
    ke = k * jnp.exp(-d_mid)
    kd = (k * jnp.exp(d_last)).astype(BF16)
    qb = (q * jnp.exp(b_cum)).astype(BF16)
    dec = jnp.exp(b_cum + d_last)
    vvb = vv.astype(BF16)

    mask_kk = (lax.broadcasted_iota(jnp.int32, (hk, hk), 0) // dkh
               == lax.broadcasted_iota(jnp.int32, (hk, hk), 1) // dkh)
    mask_vbd = (lax.broadcasted_iota(jnp.int32, (hk, hv), 0) // dkh
                == lax.broadcasted_iota(jnp.int32, (hk, hv), 1) // dvh)
    mask_st = (lax.broadcasted_iota(jnp.int32, (hv, hk), 0) // dvh
               == lax.broadcasted_iota(jnp.int32, (hv, hk), 1) // dkh)
    causal = ((lax.broadcasted_iota(jnp.int32, (B_CHUNK, hk), 1) % B_CHUNK)
              <= lax.broadcasted_iota(jnp.int32, (B_CHUNK, hk), 0))

    st = state_ref[...]
    o_parts = []
    for c in range(tile // B_CHUNK):
        sl = slice(c * B_CHUNK, (c + 1) * B_CHUNK)
        ke_bd = jnp.where(mask_kk, jnp.concatenate([ke[sl]] * B_HEADS, axis=0), 0.0).astype(BF16)
        s_cat = lax.dot_general(qe[sl], ke_bd, NT_DIMS, preferred_element_type=F32)
        s_cat = jnp.where(causal, s_cat, 0.0).astype(BF16)
        v_bd = jnp.where(mask_vbd, jnp.concatenate([vv[sl]] * B_HEADS, axis=0), 0.0).astype(BF16)
        o_c = _dot(s_cat, v_bd) + lax.dot_general(qb[sl], st.astype(BF16), NT_DIMS,
                                                   preferred_element_type=F32)
        kv_t = lax.dot_general(vvb[sl], kd[sl], TN_DIMS, preferred_element_type=F32)
        st = dec[c * B_CHUNK:c * B_CHUNK + 1] * st + jnp.where(mask_st, kv_t, 0.0)
        o_parts.append(o_c)
    state_ref[...] = st
    o = jnp.concatenate(o_parts, axis=0)

    yb_parts = []
    for h in range(B_HEADS):
        oh = o[:, h * dvh:(h + 1) * dvh]
        msq = jnp.mean(oh * oh, axis=-1, keepdims=True)
        yb_parts.append(oh * lax.rsqrt(msq + LN_EPS))
    yb = jnp.concatenate(yb_parts, axis=1) * ng_ref[...] * (r * _sigmoid(r))

    y_cat = jnp.concatenate([ya, yb], axis=1).astype(BF16)
    m = _dot(y_cat, w_out_ref[...])
    o_ref[...] = _layer_norm(DEEPNORM_ALPHA * x + m, ln_g_ref[...], ln_b_ref[...])


def _gla_cumsum_matrices(tile):
    i = jnp.arange(tile)[:, None]
    j = jnp.arange(tile)[None, :]
    same = (i // B_CHUNK) == (j // B_CHUNK)
    m_cum = same & (j <= i)
    m_mid = same & (j <= (i // B_CHUNK) * B_CHUNK + B_CHUNK // 2 - 1)
    m_last = same
    f = lambda t: t.astype(F32)
    return jnp.concatenate([f(m_cum), f(m_cum) - f(m_mid), f(m_last) - f(m_cum)], axis=0).astype(BF16)


def _even_mixer(x, w_in, a_ln_g, a_ln_b, a_ws, a_bs, w_gate_up, b_gate, norm_g, w_out, ln_g, ln_b,
                *, tile=1024):
    bn, s, d = x.shape
    aw = a_ln_g.shape[0]
    hk = w_gate_up.shape[1]
    dkh = hk // B_HEADS
    dvh = norm_g.shape[0]
    hv = B_HEADS * dvh
    rank = w_gate_up.shape[0]
    main = 2 * aw + 2 * hk + 2 * hv
    gd = aw // A_GROUPS
    w_in_p = jnp.concatenate([w_in[:, :main], jnp.pad(w_in[:, main:], ((0, 0), (0, LANES - rank)))],
                             axis=1).astype(BF16)
    w_up_p = jnp.pad(w_gate_up, ((0, LANES - rank), (0, 0))).astype(BF16)
    wcat = jnp.transpose(a_ws, (1, 0, 2)).reshape(A_CHUNK, A_GROUPS * A_CHUNK)
    abias = jnp.repeat(a_bs.T, gd, axis=1)
    mstack = _gla_cumsum_matrices(CS_ROWS)
    ng = jnp.tile(norm_g, B_HEADS)[None, :]
    row = lambda t: t[None, :]
    kern = functools.partial(_even_mixer_kernel, tile=tile, aw=aw, dkh=dkh, dvh=dvh)
    tile_spec = pl.BlockSpec((None, tile, d), lambda b, i: (b, i, 0))
    return pl.pallas_call(
        kern,
        grid=(bn, s // tile),
        in_specs=[tile_spec, _const_spec(w_in_p.shape), _const_spec((1, aw)), _const_spec((1, aw)),
                  _const_spec(wcat.shape), _const_spec(abias.shape), _const_spec(mstack.shape),
                  _const_spec(w_up_p.shape), _const_spec((1, hk)), _const_spec((1, hv)),
                  _const_spec(w_out.shape), _const_spec((1, d)), _const_spec((1, d))],
        out_specs=tile_spec,
        out_shape=jax.ShapeDtypeStruct(x.shape, F32),
        scratch_shapes=[pltpu.VMEM((hv, hk), F32)],
        compiler_params=_params(2),
        name="even_mixer",
    )(x, w_in_p, row(a_ln_g), row(a_ln_b), wcat, abias, mstack, w_up_p, row(b_gate), ng,
      w_out.astype(BF16), row(ln_g), row(ln_b))


def _ple(y, p, wpg_ref, bpg_ref, wpp_ref):
    gate = _sigmoid(_dot(y.astype(BF16), wpg_ref[...]) + bpg_ref[...])
    return y + gate * _dot(p.astype(BF16), wpp_ref[...])


def _ffn_ple_kernel(x_ref, p_ref, wg_ref, wu_ref, wd_ref, ln_g_ref, ln_b_ref, wpg_ref, bpg_ref,
                    wpp_ref, o_ref, *, chunks):
    x = x_ref[...]
    xb = x.astype(BF16)
    acc = None
    for lo, hi in chunks:
        g = _dot(xb, wg_ref[:, lo:hi])
        u = _dot(xb, wu_ref[:, lo:hi])
        h = (g * _sigmoid(g) * u).astype(BF16)
        part = _dot(h, wd_ref[lo:hi, :])
        acc = part if acc is None else acc + part
    y = _layer_norm(DEEPNORM_ALPHA * x + acc, ln_g_ref[...], ln_b_ref[...])
    o_ref[...] = _ple(y, p_ref[...], wpg_ref, bpg_ref, wpp_ref)


def _ffn_ple(x2d, p_all, layer, wg, wu, wd, ln_g, ln_b, wpg, bpg, wpp, *, tile=512, fchunk=1024):
    n, d = x2d.shape
    f = wg.shape[1]
    pd = p_all.shape[2]
    chunks = tuple((lo, min(lo + fchunk, f)) for lo in range(0, f, fchunk))
    row = lambda t: t[None, :]
    return pl.pallas_call(
        functools.partial(_ffn_ple_kernel, chunks=chunks),
        grid=(n // tile,),
        in_specs=[pl.BlockSpec((tile, d), lambda i: (i, 0)),
                  pl.BlockSpec((None, tile, pd), lambda i: (layer, i, 0)),
                  _const_spec((d, f)), _const_spec((d, f)), _const_spec((f, d)),
                  _const_spec((1, d)), _const_spec((1, d)),
                  _const_spec((d, d)), _const_spec((1, d)), _const_spec((pd, d))],
        out_specs=pl.BlockSpec((tile, d), lambda i: (i, 0)),
        out_shape=jax.ShapeDtypeStruct((n, d), F32),
        compiler_params=_params(1),
        name="ffn_ple",
    )(x2d, p_all, wg.astype(BF16), wu.astype(BF16), wd.astype(BF16), row(ln_g), row(ln_b),
      wpg.astype(BF16), row(bpg), wpp.astype(BF16))


def _qkv_kernel(x_ref, w_ref, q_ref, kt_ref, v_ref, *, hd, scale):
    z = _dot(x_ref[...].astype(BF16), w_ref[...])
    q_ref[...] = (z[:, 0:hd] * scale).astype(BF16)
    k_t = z[:, hd:2 * hd].T.astype(BF16)
    heads, dk2, tile = kt_ref.shape
    ones = jnp.ones((tile, dk2), BF16)
    for h in range(heads):
        kt_ref[h] = k_t[h * dk2:(h + 1) * dk2, :]
        v_ref[:, 2 * h * dk2:(2 * h + 1) * dk2] = z[:, 2 * hd + h * dk2:2 * hd + (h + 1) * dk2].astype(BF16)
        v_ref[:, (2 * h + 1) * dk2:(2 * h + 2) * dk2] = ones


def _qkv(x, w_qkv, *, dh, tile):
    bn, s, d = x.shape
    hd = w_qkv.shape[1] // 3
    heads = hd // (2 * dh)
    return pl.pallas_call(
        functools.partial(_qkv_kernel, hd=hd, scale=dh ** -0.5 * LOG2_E),
        grid=(bn, s // tile),
        in_specs=[pl.BlockSpec((None, tile, d), lambda b, i: (b, i, 0)), _const_spec(w_qkv.shape)],
        out_specs=[pl.BlockSpec((None, tile, hd), lambda b, i: (b, i, 0)),
                   pl.BlockSpec((None, heads, None, 2 * dh, tile), lambda b, i: (b, 0, i, 0, 0)),
                   pl.BlockSpec((None, tile, 2 * hd), lambda b, i: (b, i, 0))],
        out_shape=[jax.ShapeDtypeStruct((bn, s, hd), BF16),
                   jax.ShapeDtypeStruct((bn, heads, s // tile, 2 * dh, tile), BF16),
                   jax.ShapeDtypeStruct((bn, s, 2 * hd), BF16)],
        compiler_params=_params(2),
        name="qkv_proj",
    )(x, w_qkv.astype(BF16))


def _diff_attn_kernel(q_ref, kt_ref, v_ref, lam_ref, g_ref, o_ref, m_ref, acc_ref,
                      qm_ref, sa_ref, sb_ref, *, blk, nb, dh, lambda_init):
    pairs = [(qi, ki) for qi in range(nb) for ki in range(qi + 1)]
    bufs = (sa_ref, sb_ref)
    half = blk // 2

    def load_queries(qi):
        q = q_ref[qi * blk:(qi + 1) * blk, :]
        lane = lax.broadcasted_iota(jnp.int32, q.shape, 1)
        zero = jnp.zeros_like(q)
        qm_ref[0] = jnp.where(lane < dh, q, zero)
        qm_ref[1] = jnp.where(lane >= dh, q, zero)

    def scores(ki, s_ref, diagonal):
        kt_blk = kt_ref[ki]
        for c in range(2):
            if diagonal:
                s_ref[c, 0:half, 0:half] = _dot(qm_ref[c, 0:half], kt_blk[:, 0:half])
                s_ref[c, half:blk, :] = _dot(qm_ref[c, half:blk], kt_blk)
            else:
                s_ref[c] = _dot(qm_ref[c], kt_blk)

    def update(c, rows, s, v_part):
        m_prev = m_ref[c, rows]
        m_new = jnp.maximum(m_prev, jnp.max(s, axis=-1, keepdims=True))
        corr = jnp.exp2(m_prev - m_new)
        p = jnp.exp2(s - jnp.concatenate([m_new] * (s.shape[1] // LANES), axis=1))
        acc_ref[c, rows] = (jnp.concatenate([corr, corr], axis=1) * acc_ref[c, rows]
                            + _dot(p.astype(BF16), v_part))
        m_ref[c, rows] = m_new

    def softmax_pv(s_ref, ki, masked):
        v_blk = v_ref[ki * blk:(ki + 1) * blk, :]
        for c in range(2):
            if not masked:
                update(c, slice(0, blk), s_ref[c], v_blk)
                continue
            s_top = s_ref[c, 0:half, 0:half]
            row = lax.broadcasted_iota(jnp.int32, s_top.shape, 0)
            col = lax.broadcasted_iota(jnp.int32, s_top.shape, 1)
            update(c, slice(0, half), jnp.where(col <= row, s_top, -1e30), v_blk[0:half])
            s_bot = s_ref[c, half:blk, :]
            row = lax.broadcasted_iota(jnp.int32, s_bot.shape, 0) + half
            col = lax.broadcasted_iota(jnp.int32, s_bot.shape, 1)
            update(c, slice(half, blk), jnp.where(col <= row, s_bot, -1e30), v_blk)

    def finalize(qi):
        a1 = acc_ref[0, :, 0:2 * dh] * (1.0 / acc_ref[0, :, 2 * dh:4 * dh])
        a2 = acc_ref[1, :, 0:2 * dh] * (1.0 / acc_ref[1, :, 2 * dh:4 * dh])
        lv = lam_ref[...]
        lam = (jnp.exp(jnp.sum(lv[0:1] * lv[1:2], axis=-1, keepdims=True))
               - jnp.exp(jnp.sum(lv[2:3] * lv[3:4], axis=-1, keepdims=True)) + lambda_init)
        o = a1 - lam * a2
        msq = jnp.mean(o * o, axis=-1, keepdims=True)
        o_ref[qi * blk:(qi + 1) * blk, :] = (o * lax.rsqrt(msq + LN_EPS) * g_ref[...]
                                              * (1.0 - lambda_init)).astype(BF16)

    load_queries(0)
    scores(0, bufs[0], True)
    for t, (qi, ki) in enumerate(pairs):
        if t + 1 < len(pairs):
            nqi, nki = pairs[t + 1]
            if nki == 0:
                load_queries(nqi)
            scores(nki, bufs[(t + 1) % 2], nki == nqi)
        if ki == 0:
            m_ref[...] = jnp.full_like(m_ref, -1e30)
            acc_ref[...] = jnp.zeros_like(acc_ref)
        softmax_pv(bufs[t % 2], ki, masked=(ki == qi))
        if ki == qi:
            finalize(qi)


def _diff_attn(q, kt, v_ext, lam_q1, lam_k1, lam_q2, lam_k2, subln_g, lambda_init, *, blk):
    bn, s, hd = q.shape
    dh = lam_q1.shape[0]
    heads = hd // (2 * dh)
    nb = s // blk
    lamv = jnp.zeros((8, LANES), F32).at[0:4, 0:dh].set(jnp.stack([lam_q1, lam_k1, lam_q2, lam_k2]))
    return pl.pallas_call(
        functools.partial(_diff_attn_kernel, blk=blk, nb=nb, dh=dh, lambda_init=lambda_init),
        grid=(bn, heads),
        in_specs=[
            pl.BlockSpec((None, s, 2 * dh), lambda b, h: (b, 0, h)),
            pl.BlockSpec((None, None, nb, 2 * dh, blk), lambda b, h: (b, h, 0, 0, 0)),
            pl.BlockSpec((None, s, 4 * dh), lambda b, h: (b, 0, h)),
            pl.BlockSpec((8, LANES), lambda b, h: (0, 0)),
            pl.BlockSpec((1, 2 * dh), lambda b, h: (0, 0)),
        ],
        out_specs=pl.BlockSpec((None, s, 2 * dh), lambda b, h: (b, 0, h)),
        out_shape=jax.ShapeDtypeStruct((bn, s, hd), BF16),
        scratch_shapes=[pltpu.VMEM((2, blk, LANES), F32),
                        pltpu.VMEM((2, blk, 4 * dh), F32), pltpu.VMEM((2, blk, 2 * dh), BF16),
                        pltpu.VMEM((2, blk, blk), F32), pltpu.VMEM((2, blk, blk), F32)],
        compiler_params=_params(2),
        name="diff_attn",
    )(q, kt, v_ext, lamv, subln_g[None, :])


def _attn_out_kernel(o_ref, x_ref, w_ref, ln_g_ref, ln_b_ref, wr_ref, lstrict_ref,
                     y_ref, route_ref, cnt_ref, base_ref):
    @pl.when(pl.program_id(0) == 0)
    def _():
        base_ref[...] = jnp.zeros_like(base_ref)

    m = _dot(o_ref[...], w_ref[...])
    y = _layer_norm(DEEPNORM_ALPHA * x_ref[...] + m, ln_g_ref[...], ln_b_ref[...])
    y_ref[...] = y
    y_hi = y.astype(BF16)
    y_lo = (y - y_hi.astype(F32)).astype(BF16)
    hi_terms = _dot(y_hi, wr_ref[...])
    logits = hi_terms[:, 0:LANES] + hi_terms[:, LANES:2 * LANES] + _dot(y_lo, wr_ref[:, 0:LANES])

    lane = lax.broadcasted_iota(jnp.int32, logits.shape, 1).astype(F32)
    neg = -jnp.inf
    lg = jnp.where(lane < N_EXPERTS, logits, neg)
    v0 = jnp.max(lg, axis=-1, keepdims=True)
    i0 = jnp.min(jnp.where(lg == v0, lane, float(LANES)), axis=-1, keepdims=True)
    lg2 = jnp.where(lane == i0, neg, lg)
    v1 = jnp.max(lg2, axis=-1, keepdims=True)
    i1 = jnp.min(jnp.where(lg2 == v1, lane, float(LANES)), axis=-1, keepdims=True)
    e = jnp.exp(v1 - v0)
    g0 = 1.0 / (1.0 + e)
    g1 = e / (1.0 + e)
    oh0 = lane == i0
    oh1 = lane == i1
    c = jnp.where(oh0, 1.0, 0.0) + jnp.where(oh1, 1.0, 0.0)
    pre = _dot(lstrict_ref[...], c.astype(BF16)) + base_ref[...]
    r0 = jnp.sum(jnp.where(oh0, pre, 0.0), axis=-1, keepdims=True)
    r1 = jnp.sum(jnp.where(oh1, pre, 0.0), axis=-1, keepdims=True)
    base_ref[...] = base_ref[...] + jnp.sum(c, axis=0, keepdims=True)
    cnt_ref[...] = jnp.broadcast_to(base_ref[...], cnt_ref.shape)
    fields = (i0, i1, r0, r1, g0, g1)
    route = jnp.zeros_like(logits)
    for idx, val in enumerate(fields):
        route = jnp.where(lane == idx, val, route)
    route_ref[...] = route


def _attn_out(o2d, x2d, w_out, ln_g, ln_b, w_router, *, tile=1024):
    n, d = x2d.shape
    wr = jnp.pad(w_router, ((0, 0), (0, LANES - w_router.shape[1])))
    wr_hi = wr.astype(BF16)
    wr_lo = (wr - wr_hi.astype(F32)).astype(BF16)
    wr_split = jnp.concatenate([wr_hi, wr_lo], axis=1)
    lstrict = (jnp.arange(tile)[:, None] > jnp.arange(tile)[None, :]).astype(BF16)
    row = lambda t: t[None, :]
    tspec = pl.BlockSpec((tile, d), lambda i: (i, 0))
    return pl.pallas_call(
        _attn_out_kernel,
        grid=(n // tile,),
        in_specs=[tspec, tspec, _const_spec((d, d)), _const_spec((1, d)), _const_spec((1, d)),
                  _const_spec((d, 2 * LANES)), _const_spec((tile, tile))],
        out_specs=[tspec, pl.BlockSpec((tile, LANES), lambda i: (i, 0)),
                   pl.BlockSpec((8, LANES), lambda i: (0, 0))],
        out_shape=[jax.ShapeDtypeStruct((n, d), F32), jax.ShapeDtypeStruct((n, LANES), F32),
                   jax.ShapeDtypeStruct((8, LANES), F32)],
        scratch_shapes=[pltpu.VMEM((1, LANES), F32)],
        compiler_params=_params(1),
        name="attn_out_router",
    )(o2d, x2d, w_out.astype(BF16), row(ln_g), row(ln_b), wr_split, lstrict)


def _route_tables(route, cnt, *, tm):
    n = route.shape[0]
    counts = cnt[0, :N_EXPERTS].astype(jnp.int32)
    padded = (counts + tm - 1) // tm * tm
    pend = jnp.cumsum(padded)
    pstart = pend - padded

    def dest_of(k):
        e = route[:, k].astype(jnp.int32)
        start = jnp.sum(jnp.where(e[:, None] == jnp.arange(N_EXPERTS)[None, :], pstart[None, :], 0), axis=-1)
        return start + route[:, TOP_K + k].astype(jnp.int32)

    dest = (dest_of(0), dest_of(1))
    n_rows = n * TOP_K + N_EXPERTS * tm
    n_blocks = n_rows // tm
    n_used = pend[-1] // tm
    blk_idx = jnp.arange(n_blocks, dtype=jnp.int32)
    blk_start = jnp.minimum(blk_idx, n_used - 1) * tm
    blk_e = jnp.minimum(jnp.sum(blk_start[:, None] >= pend[None, :], axis=1), N_EXPERTS - 1)
    valid = jnp.clip(pstart[blk_e] + counts[blk_e] - blk_idx * tm, 0, tm)
    return (dest, blk_e.astype(jnp.int32), n_used.reshape(1).astype(jnp.int32),
            valid.astype(jnp.int32), n_rows)


SC_CORES = 2
SC_SUBCORES = 16
SC_WORKERS = SC_CORES * SC_SUBCORES
SC_ROWS = 32


def _sc_mesh():
    return plsc.VectorSubcoreMesh(core_axis_name="c", subcore_axis_name="s",
                                  num_cores=SC_CORES, num_subcores=SC_SUBCORES)


def _sc_worker_id():
    return lax.axis_index("s") * SC_CORES + lax.axis_index("c")


def _sc_dispatch(x2d, dest, n_rows):
    n, d = x2d.shape
    per_w = n // SC_WORKERS
    n_chunks = per_w // SC_ROWS
    assert n == SC_WORKERS * n_chunks * SC_ROWS and n_chunks % 2 == 0
    d0 = dest[0].reshape(SC_WORKERS, n_chunks, SC_ROWS)
    d1 = dest[1].reshape(SC_WORKERS, n_chunks, SC_ROWS)

    def body(x_hbm, d0_hbm, d1_hbm, xs_hbm, d0_v, d1_v, rows_v, rsem, s0sem, s1sem):
        wid = _sc_worker_id()
        base = wid * per_w
        pltpu.sync_copy(d0_hbm.at[wid], d0_v)
        pltpu.sync_copy(d1_hbm.at[wid], d1_v)

        def read(c, slot):
            return pltpu.make_async_copy(x_hbm.at[pl.ds(base + c * SC_ROWS, SC_ROWS)],
                                         rows_v.at[slot], rsem.at[slot])

        def scat(idx_v, sem, c, slot):
            return pltpu.make_async_copy(rows_v.at[slot], xs_hbm.at[idx_v.at[c]], sem.at[slot])

        def start_scatters(c, slot):
            scat(d0_v, s0sem, c, slot).start()
            scat(d1_v, s1sem, c, slot).start()

        def wait_scatters(c, slot):
            scat(d0_v, s0sem, c, slot).wait()
            scat(d1_v, s1sem, c, slot).wait()

        read(0, 0).start()

        def pair(j, carry):
            c0 = 2 * j
            read(c0, 0).wait()

            @pl.when(j > 0)
            def _():
                wait_scatters(c0 - 1, 1)

            read(c0 + 1, 1).start()
            start_scatters(c0, 0)
            read(c0 + 1, 1).wait()
            wait_scatters(c0, 0)

            @pl.when(c0 + 2 < n_chunks)
            def _():
                read(c0 + 2, 0).start()

            start_scatters(c0 + 1, 1)
            return carry

        lax.fori_loop(0, n_chunks // 2, pair, 0)
        wait_scatters(n_chunks - 1, 1)

    return pl.kernel(
        body,
        out_type=jax.ShapeDtypeStruct((n_rows, d), x2d.dtype),
        mesh=_sc_mesh(),
        scratch_types=[pltpu.VMEM((n_chunks, SC_ROWS), jnp.int32), pltpu.VMEM((n_chunks, SC_ROWS), jnp.int32),
                       pltpu.VMEM((2, SC_ROWS, d), x2d.dtype), pltpu.SemaphoreType.DMA((2,)),
                       pltpu.SemaphoreType.DMA((2,)), pltpu.SemaphoreType.DMA((2,))],
        name="sc_dispatch",
    )(x2d, d0, d1)


def _sc_gather(table, idx):
    _, d = table.shape
    b = idx.shape[0]
    per_w = b // SC_WORKERS
    n_chunks = per_w // SC_ROWS
    assert b == SC_WORKERS * n_chunks * SC_ROWS and n_chunks % 2 == 0
    idx3 = idx.reshape(SC_WORKERS, n_chunks, SC_ROWS)

    def body(table_hbm, idx_hbm, out_hbm, idx_v, rows_v, gsem, wsem):
        wid = _sc_worker_id()
        base = wid * per_w
        pltpu.sync_copy(idx_hbm.at[wid], idx_v)

        def gather(c, slot):
            return pltpu.make_async_copy(table_hbm.at[idx_v.at[c]], rows_v.at[slot], gsem.at[slot])

        def write(c, slot):
            return pltpu.make_async_copy(rows_v.at[slot],
                                         out_hbm.at[pl.ds(base + c * SC_ROWS, SC_ROWS)], wsem.at[slot])

        gather(0, 0).start()

        def pair(j, carry):
            c0 = 2 * j
            gather(c0, 0).wait()

            @pl.when(j > 0)
            def _():
                write(c0 - 1, 1).wait()

            gather(c0 + 1, 1).start()
            write(c0, 0).start()
            gather(c0 + 1, 1).wait()
            write(c0, 0).wait()

            @pl.when(c0 + 2 < n_chunks)
            def _():
                gather(c0 + 2, 0).start()

            write(c0 + 1, 1).start()
            return carry

        lax.fori_loop(0, n_chunks // 2, pair, 0)
        write(n_chunks - 1, 1).wait()

    return pl.kernel(
        body,
        out_type=jax.ShapeDtypeStruct((b, d), table.dtype),
        mesh=_sc_mesh(),
        scratch_types=[pltpu.VMEM((n_chunks, SC_ROWS), jnp.int32), pltpu.VMEM((2, SC_ROWS, d), table.dtype),
                       pltpu.SemaphoreType.DMA((2,)), pltpu.SemaphoreType.DMA((2,))],
        name="sc_combine_gather",
    )(table, idx3)


def _moe_kernel(blk_e, n_used, valid, xs_ref, wg_ref, wu_ref, wd_ref, y_ref, *, chunks):
    m = pl.program_id(0)
    tm = xs_ref.shape[0]

    def expert_rows(n_rows):
        row = lax.broadcasted_iota(jnp.int32, (n_rows, xs_ref.shape[1]), 0)
        x = jnp.where(row < valid[m], xs_ref[0:n_rows, :], 0.0).astype(BF16)
        for j, (lo, hi) in enumerate(chunks):
            g = _dot(x, wg_ref[:, lo:hi])
            u = _dot(x, wu_ref[:, lo:hi])
            h = (g * _sigmoid(g) * u).astype(BF16)
            part = _dot(h, wd_ref[lo:hi, :])
            if j == 0:
                y_ref[0:n_rows, :] = part
            else:
                y_ref[0:n_rows, :] += part
        if n_rows < tm:
            y_ref[n_rows:tm, :] = jnp.zeros((tm - n_rows, y_ref.shape[1]), y_ref.dtype)

    half = tm // 2

    @pl.when((m < n_used[0]) & (valid[m] > half))
    def _():
        expert_rows(tm)

    @pl.when((m < n_used[0]) & (valid[m] <= half))
    def _():
        expert_rows(half)

    @pl.when(m >= n_used[0])
    def _():
        y_ref[...] = jnp.zeros_like(y_ref)


def _moe_grouped(xs, blk_e, n_used, valid, wg, wu, wd, *, tm, fchunk=1024):
    n_rows, d = xs.shape
    fdim = wg.shape[2]
    n_blocks = n_rows // tm
    chunks = tuple((lo, min(lo + fchunk, fdim)) for lo in range(0, fdim, fchunk))
    grid_spec = pltpu.PrefetchScalarGridSpec(
        num_scalar_prefetch=3,
        grid=(n_blocks,),
        in_specs=[
            pl.BlockSpec((tm, d), lambda m, be, nu, va: (jnp.minimum(m, nu[0] - 1), 0)),
            pl.BlockSpec((None, d, fdim), lambda m, be, nu, va: (be[m], 0, 0)),
            pl.BlockSpec((None, d, fdim), lambda m, be, nu, va: (be[m], 0, 0)),
            pl.BlockSpec((None, fdim, d), lambda m, be, nu, va: (be[m], 0, 0)),
        ],
        out_specs=pl.BlockSpec((tm, d), lambda m, be, nu, va: (m, 0)),
    )
    return pl.pallas_call(
        functools.partial(_moe_kernel, chunks=chunks),
        grid_spec=grid_spec,
        out_shape=jax.ShapeDtypeStruct((n_rows, d), F32),
        compiler_params=pltpu.CompilerParams(dimension_semantics=("arbitrary",),
                                             vmem_limit_bytes=MOE_VMEM_LIMIT),
        name="moe_grouped",
    )(blk_e, n_used, valid, xs, wg, wu, wd)


def _final_kernel(x_ref, y0_ref, y1_ref, route_ref, p_ref, ln_g_ref, ln_b_ref, wpg_ref, bpg_ref,
                  wpp_ref, *rest):
    o_ref = rest[-1]
    route = route_ref[...]
    f = route[:, 4:5] * y0_ref[...] + route[:, 5:6] * y1_ref[...]
    y = _layer_norm(DEEPNORM_ALPHA * x_ref[...] + f, ln_g_ref[...], ln_b_ref[...])
    o_ref[...] = _ple(y, p_ref[...], wpg_ref, bpg_ref, wpp_ref)


def _final_part(x2d, y01, route, p_all, layer, ln_g, ln_b, wpg, bpg, wpp, *, part, n_parts, prev, tile):
    n, d = x2d.shape
    pd = p_all.shape[2]
    nblk = n // n_parts // tile
    off = part * nblk
    row = lambda t: t[None, :]
    tok = lambda width: pl.BlockSpec((tile, width), lambda i: (i + off, 0))
    in_specs = [tok(d), pl.BlockSpec((tile, d), lambda i: (i, 0)),
                pl.BlockSpec((tile, d), lambda i: (i + nblk, 0)), tok(LANES),
                pl.BlockSpec((None, tile, pd), lambda i: (layer, i + off, 0)),
                _const_spec((1, d)), _const_spec((1, d)),
                _const_spec((d, d)), _const_spec((1, d)), _const_spec((pd, d))]
    args = [x2d, y01, y01, route, p_all, row(ln_g), row(ln_b), wpg.astype(BF16), row(bpg), wpp.astype(BF16)]
    aliases = {}
    if prev is not None:
        in_specs.append(pl.BlockSpec(memory_space=pl.ANY))
        args.append(prev)
        aliases = {len(args) - 1: 0}
    return pl.pallas_call(
        _final_kernel,
        grid=(nblk,),
        in_specs=in_specs,
        out_specs=tok(d),
        out_shape=jax.ShapeDtypeStruct((n, d), F32),
        input_output_aliases=aliases,
        compiler_params=_params(1),
        name="moe_combine_ple",
    )(*args)


def _combine(x2d, y, dest, route, p_all, layer, ln_g, ln_b, wpg, bpg, wpp, *, n_parts=4, tile=1024):
    n = x2d.shape[0]
    step = n // n_parts
    tile = min(tile, step)
    out = None
    for part in range(n_parts):
        rows = slice(part * step, (part + 1) * step)
        y01 = _sc_gather(y, jnp.concatenate([dest[0][rows], dest[1][rows]]))
        out = _final_part(x2d, y01, route, p_all, layer, ln_g, ln_b, wpg, bpg, wpp,
                          part=part, n_parts=n_parts, prev=out, tile=tile)
    return out


def _even_layer(x, p_all, layer_idx, w_in, a_ln_g, a_ln_b, a_ws, a_bs, w_gate_up, b_gate, norm_g, w_out,
                ln1_g, ln1_b, wg, wu, wd, ln2_g, ln2_b, wpp, wpg, bpg):
    bn, s, d = x.shape
    x1 = _even_mixer(x, w_in, a_ln_g, a_ln_b, a_ws, a_bs, w_gate_up, b_gate, norm_g, w_out, ln1_g, ln1_b)
    x2 = _ffn_ple(x1.reshape(bn * s, d), p_all, layer_idx, wg, wu, wd, ln2_g, ln2_b, wpg, bpg, wpp)
    return x2.reshape(bn, s, d)


def _odd_layer(x, p_all, layer_idx, w_qkv, lam_q1, lam_k1, lam_q2, lam_k2, subln_g, w_out, ln1_g, ln1_b,
               w_router, ewg, ewu, ewd, ln2_g, ln2_b, wpp, wpg, bpg, *, attn_blk=1024, moe_tm=512):
    bn, s, d = x.shape
    n = bn * s
    lambda_init = 0.8 - 0.6 * math.exp(-0.3 * layer_idx)
    q, kt, v_ext = _qkv(x, w_qkv, dh=lam_q1.shape[0], tile=attn_blk)
    o = _diff_attn(q, kt, v_ext, lam_q1, lam_k1, lam_q2, lam_k2, subln_g, lambda_init, blk=attn_blk)
    x3, route, cnt = _attn_out(o.reshape(n, d), x.reshape(n, d), w_out, ln1_g, ln1_b, w_router)
    dest, blk_e, n_used, valid, n_rows = _route_tables(route, cnt, tm=moe_tm)
    xs = _sc_dispatch(x3, dest, n_rows)
    y = _moe_grouped(xs, blk_e, n_used, valid, ewg.astype(BF16), ewu.astype(BF16), ewd.astype(BF16),
                     tm=moe_tm)
    out = _combine(x3, y, dest, route, p_all, layer_idx, ln2_g, ln2_b, wpg, bpg, wpp)
    return out.reshape(bn, s, d)


def kernel(x, p, e_w_in, e_a_ln_g, e_a_ln_b, e_a_ws, e_a_bs, e_b_w_gate_up, e_b_b_gate, e_b_norm_g, e_w_out, e_ln1_g, e_ln1_b, e_ffn_wg, e_ffn_wu, e_ffn_wd, e_ln2_g, e_ln2_b, o_w_qkv, o_lam_q1, o_lam_k1, o_lam_q2, o_lam_k2, o_subln_g, o_w_out, o_ln1_g, o_ln1_b, o_router, o_exp_wg, o_exp_wu, o_exp_wd, o_ln2_g, o_ln2_b, ple_w_proj, ple_w_gate, ple_b_gate):
    p_all = p.reshape(p.shape[0], -1, p.shape[-1])
    for i in range(DEPTH):
        j = i // 2
        if i % 2 == 0:
            x = _even_layer(x, p_all, i, e_w_in[j], e_a_ln_g[j], e_a_ln_b[j], e_a_ws[j], e_a_bs[j],
                            e_b_w_gate_up[j], e_b_b_gate[j], e_b_norm_g[j], e_w_out[j],
                            e_ln1_g[j], e_ln1_b[j], e_ffn_wg[j], e_ffn_wu[j], e_ffn_wd[j],
                            e_ln2_g[j], e_ln2_b[j], ple_w_proj[i], ple_w_gate[i], ple_b_gate[i])
        else:
            x = _odd_layer(x, p_all, i, o_w_qkv[j], o_lam_q1[j], o_lam_k1[j], o_lam_q2[j], o_lam_k2[j],
                           o_subln_g[j], o_w_out[j], o_ln1_g[j], o_ln1_b[j], o_router[j],
                           o_exp_wg[j], o_exp_wu[j], o_exp_wd[j], o_ln2_g[j], o_ln2_b[j],
                           ple_w_proj[i], ple_w_gate[i], ple_b_gate[i])
    return x
```

```python
import functools
import math

import jax
import jax.numpy as jnp
from jax import lax
from jax.experimental import pallas as pl
from jax.experimental.pallas import tpu as pltpu
from jax.experimental.pallas import tpu_sc as plsc

F32 = jnp.float32
BF16 = jnp.bfloat16

DEPTH = 2
DEEPNORM_ALPHA = (2.0 * DEPTH) ** 0.25
LN_EPS = 1e-5
A_CHUNK = 128
A_GROUPS = 8
B_HEADS = 4
B_CHUNK = 64
B_TAU = 16.0
CS_ROWS = 256
N_EXPERTS = 8
TOP_K = 2
LANES = 128
LOG2_E = 1.4426950408889634
VMEM_LIMIT = 56 * 1024 * 1024
MOE_VMEM_LIMIT = 60 * 1024 * 1024

NT_DIMS = (((1,), (1,)), ((), ()))
TN_DIMS = (((0,), (0,)), ((), ()))


def _dot(a, b):
    return jnp.dot(a, b, preferred_element_type=F32)


def _layer_norm(x, g, b):
    mu = jnp.mean(x, axis=-1, keepdims=True)
    xc = x - mu
    var = jnp.mean(xc * xc, axis=-1, keepdims=True)
    return xc * lax.rsqrt(var + LN_EPS) * g + b


def _sigmoid(x):
    return 1.0 / (1.0 + jnp.exp(-x))


def _split3(a):
    hi = a.astype(BF16)
    r1 = a - hi.astype(F32)
    mid = r1.astype(BF16)
    lo = (r1 - mid.astype(F32)).astype(BF16)
    return hi, mid, lo


def _const_spec(shape):
    zeros = (0,) * len(shape)
    return pl.BlockSpec(shape, lambda *_: zeros, pipeline_mode=pl.Buffered(1))


def _params(n_axes):
    return pltpu.CompilerParams(dimension_semantics=("arbitrary",) * n_axes,
                                vmem_limit_bytes=VMEM_LIMIT)


def _even_mixer_kernel(x_ref, w_in_ref, a_g_ref, a_b_ref, wcat_ref, abias_ref, mstack_ref,
                       w_up_ref, b_gate_ref, ng_ref, w_out_ref, ln_g_ref, ln_b_ref,
                       o_ref, state_ref, *, tile, aw, dkh, dvh):
    @pl.when(pl.program_id(1) == 0)
    def _():
        state_ref[...] = jnp.zeros_like(state_ref)

    hk = B_HEADS * dkh
    hv = B_HEADS * dvh
    x = x_ref[...]
    z = _dot(x.astype(BF16), w_in_ref[...])

    u = jax.nn.gelu(z[:, 0:aw])
    v = _layer_norm(jax.nn.gelu(z[:, aw:2 * aw]), a_g_ref[...], a_b_ref[...])
    gd = aw // A_GROUPS
    rows = lax.broadcasted_iota(jnp.int32, (A_CHUNK, A_GROUPS * A_CHUNK), 0)
    cols = lax.broadcasted_iota(jnp.int32, (A_CHUNK, A_GROUPS * A_CHUNK), 1)
    wcat = jnp.where((cols % A_CHUNK) <= rows, wcat_ref[...], 0.0).astype(BF16)
    r_bd = lax.broadcasted_iota(jnp.int32, (A_GROUPS * A_CHUNK, aw), 0) // A_CHUNK
    c_bd = lax.broadcasted_iota(jnp.int32, (A_GROUPS * A_CHUNK, aw), 1) // gd
    mask_bd = r_bd == c_bd
    ya_parts = []
    for c in range(tile // A_CHUNK):
        sl = slice(c * A_CHUNK, (c + 1) * A_CHUNK)
        v_rep = jnp.concatenate([v[sl]] * A_GROUPS, axis=0)
        v_bd = jnp.where(mask_bd, v_rep, 0.0).astype(BF16)
        sg = _dot(wcat, v_bd) + abias_ref[...]
        ya_parts.append(u[sl] * sg)
    ya = jnp.concatenate(ya_parts, axis=0)

    o0 = 2 * aw
    q = z[:, o0:o0 + hk] * (dkh ** -0.5)
    k = z[:, o0 + hk:o0 + 2 * hk]
    vv = z[:, o0 + 2 * hk:o0 + 2 * hk + hv]
    r = z[:, o0 + 2 * hk + hv:o0 + 2 * hk + 2 * hv]
    g_low = z[:, o0 + 2 * hk + 2 * hv:]
    pre = _dot(g_low.astype(BF16), w_up_ref[...]) + b_gate_ref[...]
    log_a = (jnp.minimum(pre, 0.0) - jnp.log1p(jnp.exp(-jnp.abs(pre)))) * (1.0 / B_TAU)
    la_hi, la_mid, la_lo = _split3(log_a)
    ms = mstack_ref[...]
    b_parts, mid_parts, last_parts = [], [], []
    for t in range(tile // CS_ROWS):
        rs = slice(t * CS_ROWS, (t + 1) * CS_ROWS)
        cs = _dot(ms, la_hi[rs]) + _dot(ms, la_mid[rs]) + _dot(ms, la_lo[rs])
        b_parts.append(cs[0:CS_ROWS])
        mid_parts.append(cs[CS_ROWS:2 * CS_ROWS])
        last_parts.append(cs[2 * CS_ROWS:3 * CS_ROWS])
    b_cum = jnp.concatenate(b_parts, axis=0)
    d_mid = jnp.concatenate(mid_parts, axis=0)
    d_last = jnp.concatenate(last_parts, axis=0)
    qe = (q * jnp.exp(d_mid)).astype(BF16)
    ke = k * jnp.exp(-d_mid)
    kd = (k * jnp.exp(d_last)).astype(BF16)
    qb = (q * jnp.exp(b_cum)).astype(BF16)
    dec = jnp.exp(b_cum + d_last)
    vvb = vv.astype(BF16)

    mask_kk = (lax.broadcasted_iota(jnp.int32, (hk, hk), 0) // dkh
               == lax.broadcasted_iota(jnp.int32, (hk, hk), 1) // dkh)
    mask_vbd = (lax.broadcasted_iota(jnp.int32, (hk, hv), 0) // dkh
                == lax.broadcasted_iota(jnp.int32, (hk, hv), 1) // dvh)
    mask_st = (lax.broadcasted_iota(jnp.int32, (hv, hk), 0) // dvh
               == lax.broadcasted_iota(jnp.int32, (hv, hk), 1) // dkh)
    causal = ((lax.broadcasted_iota(jnp.int32, (B_CHUNK, hk), 1) % B_CHUNK)
              <= lax.broadcasted_iota(jnp.int32, (B_CHUNK, hk), 0))

    st = state_ref[...]
    o_parts = []
    for c in range(tile // B_CHUNK):
        sl = slice(c * B_CHUNK, (c + 1) * B_CHUNK)
        ke_bd = jnp.where(mask_kk, jnp.concatenate([ke[sl]] * B_HEADS, axis=0), 0.0).astype(BF16)
        s_cat = lax.dot_general(qe[sl], ke_bd, NT_DIMS, preferred_element_type=F32)
        s_cat = jnp.where(causal, s_cat, 0.0).astype(BF16)
        v_bd = jnp.where(mask_vbd, jnp.concatenate([vv[sl]] * B_HEADS, axis=0), 0.0).astype(BF16)
        o_c = _dot(s_cat, v_bd) + lax.dot_general(qb[sl], st.astype(BF16), NT_DIMS,
                                                   preferred_element_type=F32)
        kv_t = lax.dot_general(vvb[sl], kd[sl], TN_DIMS, preferred_element_type=F32)
        st = dec[c * B_CHUNK:c * B_CHUNK + 1] * st + jnp.where(mask_st, kv_t, 0.0)
        o_parts.append(o_c)
    state_ref[...] = st
    o = jnp.concatenate(o_parts, axis=0)

    yb_parts = []
    for h in range(B_HEADS):
        oh = o[:, h * dvh:(h + 1) * dvh]
        msq = jnp.mean(oh * oh, axis=-1, keepdims=True)
        yb_parts.append(oh * lax.rsqrt(msq + LN_EPS))
    yb = jnp.concatenate(yb_parts, axis=1) * ng_ref[...] * (r * _sigmoid(r))

    y_cat = jnp.concatenate([ya, yb], axis=1).astype(BF16)
    m = _dot(y_cat, w_out_ref[...])
    o_ref[...] = _layer_norm(DEEPNORM_ALPHA * x + m, ln_g_ref[...], ln_b_ref[...])


def _gla_cumsum_matrices(tile):
    i = jnp.arange(tile)[:, None]
    j = jnp.arange(tile)[None, :]
    same = (i // B_CHUNK) == (j // B_CHUNK)
    m_cum = same & (j <= i)
    m_mid = same & (j <= (i // B_CHUNK) * B_CHUNK + B_CHUNK // 2 - 1)
    m_last = same
    f = lambda t: t.astype(F32)
    return jnp.concatenate([f(m_cum), f(m_cum) - f(m_mid), f(m_last) - f(m_cum)], axis=0).astype(BF16)


def _even_mixer(x, w_in, a_ln_g, a_ln_b, a_ws, a_bs, w_gate_up, b_gate, norm_g, w_out, ln_g, ln_b,
                *, tile=1024):
    bn, s, d = x.shape
    aw = a_ln_g.shape[0]
    hk = w_gate_up.shape[1]
    dkh = hk // B_HEADS
    dvh = norm_g.shape[0]
    hv = B_HEADS * dvh
    rank = w_gate_up.shape[0]
    main = 2 * aw + 2 * hk + 2 * hv
    gd = aw // A_GROUPS
    w_in_p = jnp.concatenate([w_in[:, :main], jnp.pad(w_in[:, main:], ((0, 0), (0, LANES - rank)))],
                             axis=1).astype(BF16)
    w_up_p = jnp.pad(w_gate_up, ((0, LANES - rank), (0, 0))).astype(BF16)
    wcat = jnp.transpose(a_ws, (1, 0, 2)).reshape(A_CHUNK, A_GROUPS * A_CHUNK)
    abias = jnp.repeat(a_bs.T, gd, axis=1)
    mstack = _gla_cumsum_matrices(CS_ROWS)
    ng = jnp.tile(norm_g, B_HEADS)[None, :]
    row = lambda t: t[None, :]
    kern = functools.partial(_even_mixer_kernel, tile=tile, aw=aw, dkh=dkh, dvh=dvh)
    tile_spec = pl.BlockSpec((None, tile, d), lambda b, i: (b, i, 0))
    return pl.pallas_call(
        kern,
        grid=(bn, s // tile),
        in_specs=[tile_spec, _const_spec(w_in_p.shape), _const_spec((1, aw)), _const_spec((1, aw)),
                  _const_spec(wcat.shape), _const_spec(abias.shape), _const_spec(mstack.shape),
                  _const_spec(w_up_p.shape), _const_spec((1, hk)), _const_spec((1, hv)),
                  _const_spec(w_out.shape), _const_spec((1, d)), _const_spec((1, d))],
        out_specs=tile_spec,
        out_shape=jax.ShapeDtypeStruct(x.shape, F32),
        scratch_shapes=[pltpu.VMEM((hv, hk), F32)],
        compiler_params=_params(2),
        name="even_mixer",
    )(x, w_in_p, row(a_ln_g), row(a_ln_b), wcat, abias, mstack, w_up_p, row(b_gate), ng,
      w_out.astype(BF16), row(ln_g), row(ln_b))


def _ple(y, p, wpg_ref, bpg_ref, wpp_ref):
    gate = _sigmoid(_dot(y.astype(BF16), wpg_ref[...]) + bpg_ref[...])
    return y + gate * _dot(p.astype(BF16), wpp_ref[...])


def _ffn_ple_kernel(x_ref, p_ref, wg_ref, wu_ref, wd_ref, ln_g_ref, ln_b_ref, wpg_ref, bpg_ref,
                    wpp_ref, o_ref, *, chunks):
    x = x_ref[...]
    xb = x.astype(BF16)
    acc = None
    for lo, hi in chunks:
        g = _dot(xb, wg_ref[:, lo:hi])
        u = _dot(xb, wu_ref[:, lo:hi])
        h = (g * _sigmoid(g) * u).astype(BF16)
        part = _dot(h, wd_ref[lo:hi, :])
        acc = part if acc is None else acc + part
    y = _layer_norm(DEEPNORM_ALPHA * x + acc, ln_g_ref[...], ln_b_ref[...])
    o_ref[...] = _ple(y, p_ref[...], wpg_ref, bpg_ref, wpp_ref)


def _ffn_ple(x2d, p_all, layer, wg, wu, wd, ln_g, ln_b, wpg, bpg, wpp, *, tile=512, fchunk=1024):
    n, d = x2d.shape
    f = wg.shape[1]
    pd = p_all.shape[2]
    chunks = tuple((lo, min(lo + fchunk, f)) for lo in range(0, f, fchunk))
    row = lambda t: t[None, :]
    return pl.pallas_call(
        functools.partial(_ffn_ple_kernel, chunks=chunks),
        grid=(n // tile,),
        in_specs=[pl.BlockSpec((tile, d), lambda i: (i, 0)),
                  pl.BlockSpec((None, tile, pd), lambda i: (layer, i, 0)),
                  _const_spec((d, f)), _const_spec((d, f)), _const_spec((f, d)),
                  _const_spec((1, d)), _const_spec((1, d)),
                  _const_spec((d, d)), _const_spec((1, d)), _const_spec((pd, d))],
        out_specs=pl.BlockSpec((tile, d), lambda i: (i, 0)),
        out_shape=jax.ShapeDtypeStruct((n, d), F32),
        compiler_params=_params(1),
        name="ffn_ple",
    )(x2d, p_all, wg.astype(BF16), wu.astype(BF16), wd.astype(BF16), row(ln_g), row(ln_b),
      wpg.astype(BF16), row(bpg), wpp.astype(BF16))


def _qkv_kernel(x_ref, w_ref, q_ref, kt_ref, v_ref, *, hd, scale):
    z = _dot(x_ref[...].astype(BF16), w_ref[...])
    q_ref[...] = (z[:, 0:hd] * scale).astype(BF16)
    k_t = z[:, hd:2 * hd].T.astype(BF16)
    heads, dk2, tile = kt_ref.shape
    ones = jnp.ones((tile, dk2), BF16)
    for h in range(heads):
        kt_ref[h] = k_t[h * dk2:(h + 1) * dk2, :]
        v_ref[:, 2 * h * dk2:(2 * h + 1) * dk2] = z[:, 2 * hd + h * dk2:2 * hd + (h + 1) * dk2].astype(BF16)
        v_ref[:, (2 * h + 1) * dk2:(2 * h + 2) * dk2] = ones


def _qkv(x, w_qkv, *, dh, tile):
    bn, s, d = x.shape
    hd = w_qkv.shape[1] // 3
    heads = hd // (2 * dh)
    return pl.pallas_call(
        functools.partial(_qkv_kernel, hd=hd, scale=dh ** -0.5 * LOG2_E),
        grid=(bn, s // tile),
        in_specs=[pl.BlockSpec((None, tile, d), lambda b, i: (b, i, 0)), _const_spec(w_qkv.shape)],
        out_specs=[pl.BlockSpec((None, tile, hd), lambda b, i: (b, i, 0)),
                   pl.BlockSpec((None, heads, None, 2 * dh, tile), lambda b, i: (b, 0, i, 0, 0)),
                   pl.BlockSpec((None, tile, 2 * hd), lambda b, i: (b, i, 0))],
        out_shape=[jax.ShapeDtypeStruct((bn, s, hd), BF16),
                   jax.ShapeDtypeStruct((bn, heads, s // tile, 2 * dh, tile), BF16),
                   jax.ShapeDtypeStruct((bn, s, 2 * hd), BF16)],
        compiler_params=_params(2),
        name="qkv_proj",
    )(x, w_qkv.astype(BF16))


def _diff_attn_kernel(q_ref, kt_ref, v_ref, lam_ref, g_ref, o_ref, m_ref, acc_ref,
                      qm_ref, sa_ref, sb_ref, *, blk, nb, dh, lambda_init):
    pairs = [(qi, ki) for qi in range(nb) for ki in range(qi + 1)]
    bufs = (sa_ref, sb_ref)
    half = blk // 2

    def load_queries(qi):
        q = q_ref[qi * blk:(qi + 1) * blk, :]
        lane = lax.broadcasted_iota(jnp.int32, q.shape, 1)
        zero = jnp.zeros_like(q)
        qm_ref[0] = jnp.where(lane < dh, q, zero)
        qm_ref[1] = jnp.where(lane >= dh, q, zero)

    def scores(ki, s_ref, diagonal):
        kt_blk = kt_ref[ki]
        for c in range(2):
            if diagonal:
                s_ref[c, 0:half, 0:half] = _dot(qm_ref[c, 0:half], kt_blk[:, 0:half])
                s_ref[c, half:blk, :] = _dot(qm_ref[c, half:blk], kt_blk)
            else:
                s_ref[c] = _dot(qm_ref[c], kt_blk)

    def update(c, rows, s, v_part):
        m_prev = m_ref[c, rows]
        m_new = jnp.maximum(m_prev, jnp.max(s, axis=-1, keepdims=True))
        corr = jnp.exp2(m_prev - m_new)
        p = jnp.exp2(s - jnp.concatenate([m_new] * (s.shape[1] // LANES), axis=1))
        acc_ref[c, rows] = (jnp.concatenate([corr, corr], axis=1) * acc_ref[c, rows]
                            + _dot(p.astype(BF16), v_part))
        m_ref[c, rows] = m_new

    def softmax_pv(s_ref, ki, masked):
        v_blk = v_ref[ki * blk:(ki + 1) * blk, :]
        for c in range(2):
            if not masked:
                update(c, slice(0, half), s_ref[c, 0:half, :], v_blk)
                update(c, slice(half, blk), s_ref[c, half:blk, :], v_blk)
                continue
            s_top = s_ref[c, 0:half, 0:half]
            row = lax.broadcasted_iota(jnp.int32, s_top.shape, 0)
            col = lax.broadcasted_iota(jnp.int32, s_top.shape, 1)
            update(c, slice(0, half), jnp.where(col <= row, s_top, -1e30), v_blk[0:half])
            s_bot = s_ref[c, half:blk, :]
            row = lax.broadcasted_iota(jnp.int32, s_bot.shape, 0) + half
            col = lax.broadcasted_iota(jnp.int32, s_bot.shape, 1)
            update(c, slice(half, blk), jnp.where(col <= row, s_bot, -1e30), v_blk)

    def finalize(qi):
        a1 = acc_ref[0, :, 0:2 * dh] * (1.0 / acc_ref[0, :, 2 * dh:4 * dh])
        a2 = acc_ref[1, :, 0:2 * dh] * (1.0 / acc_ref[1, :, 2 * dh:4 * dh])
        lv = lam_ref[...]
        lam = (jnp.exp(jnp.sum(lv[0:1] * lv[1:2], axis=-1, keepdims=True))
               - jnp.exp(jnp.sum(lv[2:3] * lv[3:4], axis=-1, keepdims=True)) + lambda_init)
        o = a1 - lam * a2
        msq = jnp.mean(o * o, axis=-1, keepdims=True)
        o_ref[qi * blk:(qi + 1) * blk, :] = (o * lax.rsqrt(msq + LN_EPS) * g_ref[...]
                                              * (1.0 - lambda_init)).astype(BF16)

    load_queries(0)
    scores(0, bufs[0], True)
    for t, (qi, ki) in enumerate(pairs):
        if t + 1 < len(pairs):
            nqi, nki = pairs[t + 1]
            if nki == 0:
                load_queries(nqi)
            scores(nki, bufs[(t + 1) % 2], nki == nqi)
        if ki == 0:
            m_ref[...] = jnp.full_like(m_ref, -1e30)
            acc_ref[...] = jnp.zeros_like(acc_ref)
        softmax_pv(bufs[t % 2], ki, masked=(ki == qi))
        if ki == qi:
            finalize(qi)


def _diff_attn(q, kt, v_ext, lam_q1, lam_k1, lam_q2, lam_k2, subln_g, lambda_init, *, blk):
    bn, s, hd = q.shape
    dh = lam_q1.shape[0]
    heads = hd // (2 * dh)
    nb = s // blk
    lamv = jnp.zeros((8, LANES), F32).at[0:4, 0:dh].set(jnp.stack([lam_q1, lam_k1, lam_q2, lam_k2]))
    return pl.pallas_call(
        functools.partial(_diff_attn_kernel, blk=blk, nb=nb, dh=dh, lambda_init=lambda_init),
        grid=(bn, heads),
        in_specs=[
            pl.BlockSpec((None, s, 2 * dh), lambda b, h: (b, 0, h)),
            pl.BlockSpec((None, None, nb, 2 * dh, blk), lambda b, h: (b, h, 0, 0, 0)),
            pl.BlockSpec((None, s, 4 * dh), lambda b, h: (b, 0, h)),
            pl.BlockSpec((8, LANES), lambda b, h: (0, 0)),
            pl.BlockSpec((1, 2 * dh), lambda b, h: (0, 0)),
        ],
        out_specs=pl.BlockSpec((None, s, 2 * dh), lambda b, h: (b, 0, h)),
        out_shape=jax.ShapeDtypeStruct((bn, s, hd), BF16),
        scratch_shapes=[pltpu.VMEM((2, blk, LANES), F32),
                        pltpu.VMEM((2, blk, 4 * dh), F32), pltpu.VMEM((2, blk, 2 * dh), BF16),
                        pltpu.VMEM((2, blk, blk), F32), pltpu.VMEM((2, blk, blk), F32)],
        compiler_params=_params(2),
        name="diff_attn",
    )(q, kt, v_ext, lamv, subln_g[None, :])


def _attn_out_kernel(o_ref, x_ref, w_ref, ln_g_ref, ln_b_ref, wr_ref, lstrict_ref,
                     y_ref, route_ref, cnt_ref, base_ref):
    @pl.when(pl.program_id(0) == 0)
    def _():
        base_ref[...] = jnp.zeros_like(base_ref)

    m = _dot(o_ref[...], w_ref[...])
    y = _layer_norm(DEEPNORM_ALPHA * x_ref[...] + m, ln_g_ref[...], ln_b_ref[...])
    y_ref[...] = y
    y_hi = y.astype(BF16)
    y_lo = (y - y_hi.astype(F32)).astype(BF16)
    hi_terms = _dot(y_hi, wr_ref[...])
    logits = hi_terms[:, 0:LANES] + hi_terms[:, LANES:2 * LANES] + _dot(y_lo, wr_ref[:, 0:LANES])

    lane = lax.broadcasted_iota(jnp.int32, logits.shape, 1).astype(F32)
    neg = -jnp.inf
    lg = jnp.where(lane < N_EXPERTS, logits, neg)
    v0 = jnp.max(lg, axis=-1, keepdims=True)
    i0 = jnp.min(jnp.where(lg == v0, lane, float(LANES)), axis=-1, keepdims=True)
    lg2 = jnp.where(lane == i0, neg, lg)
    v1 = jnp.max(lg2, axis=-1, keepdims=True)
    i1 = jnp.min(jnp.where(lg2 == v1, lane, float(LANES)), axis=-1, keepdims=True)
    e = jnp.exp(v1 - v0)
    g0 = 1.0 / (1.0 + e)
    g1 = e / (1.0 + e)
    oh0 = lane == i0
    oh1 = lane == i1
    c = jnp.where(oh0, 1.0, 0.0) + jnp.where(oh1, 1.0, 0.0)
    pre = _dot(lstrict_ref[...], c.astype(BF16)) + base_ref[...]
    r0 = jnp.sum(jnp.where(oh0, pre, 0.0), axis=-1, keepdims=True)
    r1 = jnp.sum(jnp.where(oh1, pre, 0.0), axis=-1, keepdims=True)
    base_ref[...] = base_ref[...] + jnp.sum(c, axis=0, keepdims=True)
    cnt_ref[...] = jnp.broadcast_to(base_ref[...], cnt_ref.shape)
    fields = (i0, i1, r0, r1, g0, g1)
    route = jnp.zeros_like(logits)
    for idx, val in enumerate(fields):
        route = jnp.where(lane == idx, val, route)
    route_ref[...] = route


def _attn_out(o2d, x2d, w_out, ln_g, ln_b, w_router, *, tile=1024):
    n, d = x2d.shape
    wr = jnp.pad(w_router, ((0, 0), (0, LANES - w_router.shape[1])))
    wr_hi = wr.astype(BF16)
    wr_lo = (wr - wr_hi.astype(F32)).astype(BF16)
    wr_split = jnp.concatenate([wr_hi, wr_lo], axis=1)
    lstrict = (jnp.arange(tile)[:, None] > jnp.arange(tile)[None, :]).astype(BF16)
    row = lambda t: t[None, :]
    tspec = pl.BlockSpec((tile, d), lambda i: (i, 0))
    return pl.pallas_call(
        _attn_out_kernel,
        grid=(n // tile,),
        in_specs=[tspec, tspec, _const_spec((d, d)), _const_spec((1, d)), _const_spec((1, d)),
                  _const_spec((d, 2 * LANES)), _const_spec((tile, tile))],
        out_specs=[tspec, pl.BlockSpec((tile, LANES), lambda i: (i, 0)),
                   pl.BlockSpec((8, LANES), lambda i: (0, 0))],
        out_shape=[jax.ShapeDtypeStruct((n, d), F32), jax.ShapeDtypeStruct((n, LANES), F32),
                   jax.ShapeDtypeStruct((8, LANES), F32)],
        scratch_shapes=[pltpu.VMEM((1, LANES), F32)],
        compiler_params=_params(1),
        name="attn_out_router",
    )(o2d, x2d, w_out.astype(BF16), row(ln_g), row(ln_b), wr_split, lstrict)


def _route_tables(route, cnt, *, tm):
    n = route.shape[0]
    counts = cnt[0, :N_EXPERTS].astype(jnp.int32)
    padded = (counts + tm - 1) // tm * tm
    pend = jnp.cumsum(padded)
    pstart = pend - padded

    def dest_of(k):
        e = route[:, k].astype(jnp.int32)
        start = jnp.sum(jnp.where(e[:, None] == jnp.arange(N_EXPERTS)[None, :], pstart[None, :], 0), axis=-1)
        return start + route[:, TOP_K + k].astype(jnp.int32)

    dest = (dest_of(0), dest_of(1))
    n_rows = n * TOP_K + N_EXPERTS * tm
    n_blocks = n_rows // tm
    n_used = pend[-1] // tm
    blk_idx = jnp.arange(n_blocks, dtype=jnp.int32)
    blk_start = jnp.minimum(blk_idx, n_used - 1) * tm
    blk_e = jnp.minimum(jnp.sum(blk_start[:, None] >= pend[None, :], axis=1), N_EXPERTS - 1)
    valid = jnp.clip(pstart[blk_e] + counts[blk_e] - blk_idx * tm, 0, tm)
    return (dest, blk_e.astype(jnp.int32), n_used.reshape(1).astype(jnp.int32),
            valid.astype(jnp.int32), n_rows)


SC_CORES = 2
SC_SUBCORES = 16
SC_WORKERS = SC_CORES * SC_SUBCORES
SC_ROWS = 32


def _sc_mesh():
    return plsc.VectorSubcoreMesh(core_axis_name="c", subcore_axis_name="s",
                                  num_cores=SC_CORES, num_subcores=SC_SUBCORES)


def _sc_worker_id():
    return lax.axis_index("s") * SC_CORES + lax.axis_index("c")


def _sc_dispatch(x2d, dest, n_rows):
    n, d = x2d.shape
    per_w = n // SC_WORKERS
    n_chunks = per_w // SC_ROWS
    assert n == SC_WORKERS * n_chunks * SC_ROWS and n_chunks % 2 == 0
    d0 = dest[0].reshape(SC_WORKERS, n_chunks, SC_ROWS)
    d1 = dest[1].reshape(SC_WORKERS, n_chunks, SC_ROWS)

    def body(x_hbm, d0_hbm, d1_hbm, xs_hbm, d0_v, d1_v, rows_v, rsem, s0sem, s1sem):
        wid = _sc_worker_id()
        base = wid * per_w
        pltpu.sync_copy(d0_hbm.at[wid], d0_v)
        pltpu.sync_copy(d1_hbm.at[wid], d1_v)

        def read(c, slot):
            return pltpu.make_async_copy(x_hbm.at[pl.ds(base + c * SC_ROWS, SC_ROWS)],
                                         rows_v.at[slot], rsem.at[slot])

        def scat(idx_v, sem, c, slot):
            return pltpu.make_async_copy(rows_v.at[slot], xs_hbm.at[idx_v.at[c]], sem.at[slot])

        def start_scatters(c, slot):
            scat(d0_v, s0sem, c, slot).start()
            scat(d1_v, s1sem, c, slot).start()

        def wait_scatters(c, slot):
            scat(d0_v, s0sem, c, slot).wait()
            scat(d1_v, s1sem, c, slot).wait()

        read(0, 0).start()

        def pair(j, carry):
            c0 = 2 * j
            read(c0, 0).wait()

            @pl.when(j > 0)
            def _():
                wait_scatters(c0 - 1, 1)

            read(c0 + 1, 1).start()
            start_scatters(c0, 0)
            read(c0 + 1, 1).wait()
            wait_scatters(c0, 0)

            @pl.when(c0 + 2 < n_chunks)
            def _():
                read(c0 + 2, 0).start()

            start_scatters(c0 + 1, 1)
            return carry

        lax.fori_loop(0, n_chunks // 2, pair, 0)
        wait_scatters(n_chunks - 1, 1)

    return pl.kernel(
        body,
        out_type=jax.ShapeDtypeStruct((n_rows, d), x2d.dtype),
        mesh=_sc_mesh(),
        scratch_types=[pltpu.VMEM((n_chunks, SC_ROWS), jnp.int32), pltpu.VMEM((n_chunks, SC_ROWS), jnp.int32),
                       pltpu.VMEM((2, SC_ROWS, d), x2d.dtype), pltpu.SemaphoreType.DMA((2,)),
                       pltpu.SemaphoreType.DMA((2,)), pltpu.SemaphoreType.DMA((2,))],
        name="sc_dispatch",
    )(x2d, d0, d1)


def _sc_gather(table, idx):
    _, d = table.shape
    b = idx.shape[0]
    per_w = b // SC_WORKERS
    n_chunks = per_w // SC_ROWS
    assert b == SC_WORKERS * n_chunks * SC_ROWS and n_chunks % 2 == 0
    idx3 = idx.reshape(SC_WORKERS, n_chunks, SC_ROWS)

    def body(table_hbm, idx_hbm, out_hbm, idx_v, rows_v, gsem, wsem):
        wid = _sc_worker_id()
        base = wid * per_w
        pltpu.sync_copy(idx_hbm.at[wid], idx_v)

        def gather(c, slot):
            return pltpu.make_async_copy(table_hbm.at[idx_v.at[c]], rows_v.at[slot], gsem.at[slot])

        def write(c, slot):
            return pltpu.make_async_copy(rows_v.at[slot],
                                         out_hbm.at[pl.ds(base + c * SC_ROWS, SC_ROWS)], wsem.at[slot])

        gather(0, 0).start()

        def pair(j, carry):
            c0 = 2 * j
            gather(c0, 0).wait()

            @pl.when(j > 0)
            def _():
                write(c0 - 1, 1).wait()

            gather(c0 + 1, 1).start()
            write(c0, 0).start()
            gather(c0 + 1, 1).wait()
            write(c0, 0).wait()

            @pl.when(c0 + 2 < n_chunks)
            def _():
                gather(c0 + 2, 0).start()

            write(c0 + 1, 1).start()
            return carry

        lax.fori_loop(0, n_chunks // 2, pair, 0)
        write(n_chunks - 1, 1).wait()

    return pl.kernel(
        body,
        out_type=jax.ShapeDtypeStruct((b, d), table.dtype),
        mesh=_sc_mesh(),
        scratch_types=[pltpu.VMEM((n_chunks, SC_ROWS), jnp.int32), pltpu.VMEM((2, SC_ROWS, d), table.dtype),
                       pltpu.SemaphoreType.DMA((2,)), pltpu.SemaphoreType.DMA((2,))],
        name="sc_combine_gather",
    )(table, idx3)


def _moe_kernel(blk_e, n_used, valid, xs_ref, wg_ref, wu_ref, wd_ref, y_ref, *, chunks):
    m = pl.program_id(0)
    tm = xs_ref.shape[0]

    def expert_rows(n_rows):
        row = lax.broadcasted_iota(jnp.int32, (n_rows, xs_ref.shape[1]), 0)
        x = jnp.where(row < valid[m], xs_ref[0:n_rows, :], 0.0).astype(BF16)
        for j, (lo, hi) in enumerate(chunks):
            g = _dot(x, wg_ref[:, lo:hi])
            u = _dot(x, wu_ref[:, lo:hi])
            h = (g * _sigmoid(g) * u).astype(BF16)
            part = _dot(h, wd_ref[lo:hi, :])
            if j == 0:
                y_ref[0:n_rows, :] = part
            else:
                y_ref[0:n_rows, :] += part
        if n_rows < tm:
            y_ref[n_rows:tm, :] = jnp.zeros((tm - n_rows, y_ref.shape[1]), y_ref.dtype)

    half = tm // 2

    @pl.when((m < n_used[0]) & (valid[m] > half))
    def _():
        expert_rows(tm)

    @pl.when((m < n_used[0]) & (valid[m] <= half))
    def _():
        expert_rows(half)

    @pl.when(m >= n_used[0])
    def _():
        y_ref[...] = jnp.zeros_like(y_ref)


def _moe_grouped(xs, blk_e, n_used, valid, wg, wu, wd, *, tm, fchunk=1024):
    n_rows, d = xs.shape
    fdim = wg.shape[2]
    n_blocks = n_rows // tm
    chunks = tuple((lo, min(lo + fchunk, fdim)) for lo in range(0, fdim, fchunk))
    grid_spec = pltpu.PrefetchScalarGridSpec(
        num_scalar_prefetch=3,
        grid=(n_blocks,),
        in_specs=[
            pl.BlockSpec((tm, d), lambda m, be, nu, va: (jnp.minimum(m, nu[0] - 1), 0)),
            pl.BlockSpec((None, d, fdim), lambda m, be, nu, va: (be[m], 0, 0)),
            pl.BlockSpec((None, d, fdim), lambda m, be, nu, va: (be[m], 0, 0)),
            pl.BlockSpec((None, fdim, d), lambda m, be, nu, va: (be[m], 0, 0)),
        ],
        out_specs=pl.BlockSpec((tm, d), lambda m, be, nu, va: (m, 0)),
    )
    return pl.pallas_call(
        functools.partial(_moe_kernel, chunks=chunks),
        grid_spec=grid_spec,
        out_shape=jax.ShapeDtypeStruct((n_rows, d), F32),
        compiler_params=pltpu.CompilerParams(dimension_semantics=("arbitrary",),
                                             vmem_limit_bytes=MOE_VMEM_LIMIT),
        name="moe_grouped",
    )(blk_e, n_used, valid, xs, wg, wu, wd)


def _final_kernel(x_ref, y0_ref, y1_ref, route_ref, p_ref, ln_g_ref, ln_b_ref, wpg_ref, bpg_ref,
                  wpp_ref, *rest):
    o_ref = rest[-1]
    route = route_ref[...]
    f = route[:, 4:5] * y0_ref[...] + route[:, 5:6] * y1_ref[...]
    y = _layer_norm(DEEPNORM_ALPHA * x_ref[...] + f, ln_g_ref[...], ln_b_ref[...])
    o_ref[...] = _ple(y, p_ref[...], wpg_ref, bpg_ref, wpp_ref)


def _final_part(x2d, y01, route, p_all, layer, ln_g, ln_b, wpg, bpg, wpp, *, part, n_parts, prev, tile):
    n, d = x2d.shape
    pd = p_all.shape[2]
    nblk = n // n_parts // tile
    off = part * nblk
    row = lambda t: t[None, :]
    tok = lambda width: pl.BlockSpec((tile, width), lambda i: (i + off, 0))
    in_specs = [tok(d), pl.BlockSpec((tile, d), lambda i: (i, 0)),
                pl.BlockSpec((tile, d), lambda i: (i + nblk, 0)), tok(LANES),
                pl.BlockSpec((None, tile, pd), lambda i: (layer, i + off, 0)),
                _const_spec((1, d)), _const_spec((1, d)),
                _const_spec((d, d)), _const_spec((1, d)), _const_spec((pd, d))]
    args = [x2d, y01, y01, route, p_all, row(ln_g), row(ln_b), wpg.astype(BF16), row(bpg), wpp.astype(BF16)]
    aliases = {}
    if prev is not None:
        in_specs.append(pl.BlockSpec(memory_space=pl.ANY))
        args.append(prev)
        aliases = {len(args) - 1: 0}
    return pl.pallas_call(
        _final_kernel,
        grid=(nblk,),
        in_specs=in_specs,
        out_specs=tok(d),
        out_shape=jax.ShapeDtypeStruct((n, d), F32),
        input_output_aliases=aliases,
        compiler_params=_params(1),
        name="moe_combine_ple",
    )(*args)


def _combine(x2d, y, dest, route, p_all, layer, ln_g, ln_b, wpg, bpg, wpp, *, n_parts=4, tile=1024):
    n = x2d.shape[0]
    step = n // n_parts
    tile = min(tile, step)
    out = None
    for part in range(n_parts):
        rows = slice(part * step, (part + 1) * step)
        y01 = _sc_gather(y, jnp.concatenate([dest[0][rows], dest[1][rows]]))
        out = _final_part(x2d, y01, route, p_all, layer, ln_g, ln_b, wpg, bpg, wpp,
                          part=part, n_parts=n_parts, prev=out, tile=tile)
    return out


def _even_layer(x, p_all, layer_idx, w_in, a_ln_g, a_ln_b, a_ws, a_bs, w_gate_up, b_gate, norm_g, w_out,
                ln1_g, ln1_b, wg, wu, wd, ln2_g, ln2_b, wpp, wpg, bpg):
    bn, s, d = x.shape
    x1 = _even_mixer(x, w_in, a_ln_g, a_ln_b, a_ws, a_bs, w_gate_up, b_gate, norm_g, w_out, ln1_g, ln1_b)
    x2 = _ffn_ple(x1.reshape(bn * s, d), p_all, layer_idx, wg, wu, wd, ln2_g, ln2_b, wpg, bpg, wpp)
    return x2.reshape(bn, s, d)


def _odd_layer(x, p_all, layer_idx, w_qkv, lam_q1, lam_k1, lam_q2, lam_k2, subln_g, w_out, ln1_g, ln1_b,
               w_router, ewg, ewu, ewd, ln2_g, ln2_b, wpp, wpg, bpg, *, attn_blk=1024, moe_tm=512):
    bn, s, d = x.shape
    n = bn * s
    lambda_init = 0.8 - 0.6 * math.exp(-0.3 * layer_idx)
    q, kt, v_ext = _qkv(x, w_qkv, dh=lam_q1.shape[0], tile=attn_blk)
    o = _diff_attn(q, kt, v_ext, lam_q1, lam_k1, lam_q2, lam_k2, subln_g, lambda_init, blk=attn_blk)
    x3, route, cnt = _attn_out(o.reshape(n, d), x.reshape(n, d), w_out, ln1_g, ln1_b, w_router)
    dest, blk_e, n_used, valid, n_rows = _route_tables(route, cnt, tm=moe_tm)
    xs = _sc_dispatch(x3, dest, n_rows)
    y = _moe_grouped(xs, blk_e, n_used, valid, ewg.astype(BF16), ewu.astype(BF16), ewd.astype(BF16),
                     tm=moe_tm)
    out = _combine(x3, y, dest, route, p_all, layer_idx, ln2_g, ln2_b, wpg, bpg, wpp)
    return out.reshape(bn, s, d)


def kernel(x, p, e_w_in, e_a_ln_g, e_a_ln_b, e_a_ws, e_a_bs, e_b_w_gate_up, e_b_b_gate, e_b_norm_g, e_w_out, e_ln1_g, e_ln1_b, e_ffn_wg, e_ffn_wu, e_ffn_wd, e_ln2_g, e_ln2_b, o_w_qkv, o_lam_q1, o_lam_k1, o_lam_q2, o_lam_k2, o_subln_g, o_w_out, o_ln1_g, o_ln1_b, o_router, o_exp_wg, o_exp_wu, o_exp_wd, o_ln2_g, o_ln2_b, ple_w_proj, ple_w_gate, ple_b_gate):
    p_all = p.reshape(p.shape[0], -1, p.shape[-1])
    for i in range(DEPTH):
        j = i // 2
        if i % 2 == 0:
            x = _even_layer(x, p_all, i, e_w_in[j], e_a_ln_g[j], e_a_ln_b[j], e_a_ws[j], e_a_bs[j],
                            e_b_w_gate_up[j], e_b_b_gate[j], e_b_norm_g[j], e_w_out[j],
                            e_ln1_g[j], e_ln1_b[j], e_ffn_wg[j], e_ffn_wu[j], e_ffn_wd[j],
                            e_ln2_g[j], e_ln2_b[j], ple_w_proj[i], ple_w_gate[i], ple_b_gate[i])
        else:
            x = _odd_layer(x, p_all, i, o_w_qkv[j], o_lam_q1[j], o_lam_k1[j], o_lam_q2[j], o_lam_k2[j],
                           o_subln_g[j], o_w_out[j], o_ln1_g[j], o_ln1_b[j], o_router[j],
                           o_exp_wg[j], o_exp_wu[j], o_exp_wd[j], o_ln2_g[j], o_ln2_b[j],
                           ple_w_proj[i], ple_w_gate[i], ple_b_gate[i])
    return x
```

```python
import functools
import math

import jax
import jax.numpy as jnp
from jax import lax
from jax.experimental import pallas as pl
from jax.experimental.pallas import tpu as pltpu
from jax.experimental.pallas import tpu_sc as plsc

F32 = jnp.float32
BF16 = jnp.bfloat16

DEPTH = 2
DEEPNORM_ALPHA = (2.0 * DEPTH) ** 0.25
LN_EPS = 1e-5
A_CHUNK = 128
A_GROUPS = 8
B_HEADS = 4
B_CHUNK = 64
B_TAU = 16.0
CS_ROWS = 256
N_EXPERTS = 8
TOP_K = 2
LANES = 128
LOG2_E = 1.4426950408889634
VMEM_LIMIT = 56 * 1024 * 1024
MOE_VMEM_LIMIT = 60 * 1024 * 1024

NT_DIMS = (((1,), (1,)), ((), ()))
TN_DIMS = (((0,), (0,)), ((), ()))


def _dot(a, b):
    return jnp.dot(a, b, preferred_element_type=F32)


def _layer_norm(x, g, b):
    mu = jnp.mean(x, axis=-1, keepdims=True)
    xc = x - mu
    var = jnp.mean(xc * xc, axis=-1, keepdims=True)
    return xc * lax.rsqrt(var + LN_EPS) * g + b


def _sigmoid(x):
    return 1.0 / (1.0 + jnp.exp(-x))


def _split3(a):
    hi = a.astype(BF16)
    r1 = a - hi.astype(F32)
    mid = r1.astype(BF16)
    lo = (r1 - mid.astype(F32)).astype(BF16)
    return hi, mid, lo


def _const_spec(shape):
    zeros = (0,) * len(shape)
    return pl.BlockSpec(shape, lambda *_: zeros, pipeline_mode=pl.Buffered(1))


def _params(n_axes):
    return pltpu.CompilerParams(dimension_semantics=("arbitrary",) * n_axes,
                                vmem_limit_bytes=VMEM_LIMIT)


def _even_mixer_kernel(x_ref, w_in_ref, a_g_ref, a_b_ref, wcat_ref, abias_ref, mstack_ref,
                       w_up_ref, b_gate_ref, ng_ref, w_out_ref, ln_g_ref, ln_b_ref,
                       o_ref, state_ref, *, tile, aw, dkh, dvh):
    @pl.when(pl.program_id(1) == 0)
    def _():
        state_ref[...] = jnp.zeros_like(state_ref)

    hk = B_HEADS * dkh
    hv = B_HEADS * dvh
    x = x_ref[...]
    z = _dot(x.astype(BF16), w_in_ref[...])

    u = jax.nn.gelu(z[:, 0:aw])
    v = _layer_norm(jax.nn.gelu(z[:, aw:2 * aw]), a_g_ref[...], a_b_ref[...])
    gd = aw // A_GROUPS
    rows = lax.broadcasted_iota(jnp.int32, (A_CHUNK, A_GROUPS * A_CHUNK), 0)
    cols = lax.broadcasted_iota(jnp.int32, (A_CHUNK, A_GROUPS * A_CHUNK), 1)
    wcat = jnp.where((cols % A_CHUNK) <= rows, wcat_ref[...], 0.0).astype(BF16)
    r_bd = lax.broadcasted_iota(jnp.int32, (A_GROUPS * A_CHUNK, aw), 0) // A_CHUNK
    c_bd = lax.broadcasted_iota(jnp.int32, (A_GROUPS * A_CHUNK, aw), 1) // gd
    mask_bd = r_bd == c_bd
    ya_parts = []
    for c in range(tile // A_CHUNK):
        sl = slice(c * A_CHUNK, (c + 1) * A_CHUNK)
        v_rep = jnp.concatenate([v[sl]] * A_GROUPS, axis=0)
        v_bd = jnp.where(mask_bd, v_rep, 0.0).astype(BF16)
        sg = _dot(wcat, v_bd) + abias_ref[...]
        ya_parts.append(u[sl] * sg)
    ya = jnp.concatenate(ya_parts, axis=0)

    o0 = 2 * aw
    q = z[:, o0:o0 + hk] * (dkh ** -0.5)
    k = z[:, o0 + hk:o0 + 2 * hk]
    vv = z[:, o0 + 2 * hk:o0 + 2 * hk + hv]
    r = z[:, o0 + 2 * hk + hv:o0 + 2 * hk + 2 * hv]
    g_low = z[:, o0 + 2 * hk + 2 * hv:]
    pre = _dot(g_low.astype(BF16), w_up_ref[...]) + b_gate_ref[...]
    log_a = (jnp.minimum(pre, 0.0) - jnp.log1p(jnp.exp(-jnp.abs(pre)))) * (1.0 / B_TAU)
    la_hi, la_mid, la_lo = _split3(log_a)
    ms = mstack_ref[...]
    b_parts, mid_parts, last_parts = [], [], []
    for t in range(tile // CS_ROWS):
        rs = slice(t * CS_ROWS, (t + 1) * CS_ROWS)
        cs = _dot(ms, la_hi[rs]) + _dot(ms, la_mid[rs]) + _dot(ms, la_lo[rs])
        b_parts.append(cs[0:CS_ROWS])
        mid_parts.append(cs[CS_ROWS:2 * CS_ROWS])
        last_parts.append(cs[2 * CS_ROWS:3 * CS_ROWS])
    b_cum = jnp.concatenate(b_parts, axis=0)
    d_mid = jnp.concatenate(mid_parts, axis=0)
    d_last = jnp.concatenate(last_parts, axis=0)
    qe = (q * jnp.exp(d_mid)).astype(BF16)
    ke = k * jnp.exp(-d_mid)
    kd = (k * jnp.exp(d_last)).astype(BF16)
    qb = (q * jnp.exp(b_cum)).astype(BF16)
    dec = jnp.exp(b_cum + d_last)
    vvb = vv.astype(BF16)

    mask_kk = (lax.broadcasted_iota(jnp.int32, (hk, hk), 0) // dkh
               == lax.broadcasted_iota(jnp.int32, (hk, hk), 1) // dkh)
    mask_vbd = (lax.broadcasted_iota(jnp.int32, (hk, hv), 0) // dkh
                == lax.broadcasted_iota(jnp.int32, (hk, hv), 1) // dvh)
    mask_st = (lax.broadcasted_iota(jnp.int32, (hv, hk), 0) // dvh
               == lax.broadcasted_iota(jnp.int32, (hv, hk), 1) // dkh)
    causal = ((lax.broadcasted_iota(jnp.int32, (B_CHUNK, hk), 1) % B_CHUNK)
              <= lax.broadcasted_iota(jnp.int32, (B_CHUNK, hk), 0))

    st = state_ref[...]
    o_parts = []
    for c in range(tile // B_CHUNK):
        sl = slice(c * B_CHUNK, (c + 1) * B_CHUNK)
        ke_bd = jnp.where(mask_kk, jnp.concatenate([ke[sl]] * B_HEADS, axis=0), 0.0).astype(BF16)
        s_cat = lax.dot_general(qe[sl], ke_bd, NT_DIMS, preferred_element_type=F32)
        s_cat = jnp.where(causal, s_cat, 0.0).astype(BF16)
        v_bd = jnp.where(mask_vbd, jnp.concatenate([vv[sl]] * B_HEADS, axis=0), 0.0).astype(BF16)
        o_c = _dot(s_cat, v_bd) + lax.dot_general(qb[sl], st.astype(BF16), NT_DIMS,
                                                   preferred_element_type=F32)
        kv_t = lax.dot_general(vvb[sl], kd[sl], TN_DIMS, preferred_element_type=F32)
        st = dec[c * B_CHUNK:c * B_CHUNK + 1] * st + jnp.where(mask_st, kv_t, 0.0)
        o_parts.append(o_c)
    state_ref[...] = st
    o = jnp.concatenate(o_parts, axis=0)

    yb_parts = []
    for h in range(B_HEADS):
        oh = o[:, h * dvh:(h + 1) * dvh]
        msq = jnp.mean(oh * oh, axis=-1, keepdims=True)
        yb_parts.append(oh * lax.rsqrt(msq + LN_EPS))
    yb = jnp.concatenate(yb_parts, axis=1) * ng_ref[...] * (r * _sigmoid(r))

    y_cat = jnp.concatenate([ya, yb], axis=1).astype(BF16)
    m = _dot(y_cat, w_out_ref[...])
    o_ref[...] = _layer_norm(DEEPNORM_ALPHA * x + m, ln_g_ref[...], ln_b_ref[...])


def _gla_cumsum_matrices(tile):
    i = jnp.arange(tile)[:, None]
    j = jnp.arange(tile)[None, :]
    same = (i // B_CHUNK) == (j // B_CHUNK)
    m_cum = same & (j <= i)
    m_mid = same & (j <= (i // B_CHUNK) * B_CHUNK + B_CHUNK // 2 - 1)
    m_last = same
    f = lambda t: t.astype(F32)
    return jnp.concatenate([f(m_cum), f(m_cum) - f(m_mid), f(m_last) - f(m_cum)], axis=0).astype(BF16)


def _even_mixer(x, w_in, a_ln_g, a_ln_b, a_ws, a_bs, w_gate_up, b_gate, norm_g, w_out, ln_g, ln_b,
                *, tile=1024):
    bn, s, d = x.shape
    aw = a_ln_g.shape[0]
    hk = w_gate_up.shape[1]
    dkh = hk // B_HEADS
    dvh = norm_g.shape[0]
    hv = B_HEADS * dvh
    rank = w_gate_up.shape[0]
    main = 2 * aw + 2 * hk + 2 * hv
    gd = aw // A_GROUPS
    w_in_p = jnp.concatenate([w_in[:, :main], jnp.pad(w_in[:, main:], ((0, 0), (0, LANES - rank)))],
                             axis=1).astype(BF16)
    w_up_p = jnp.pad(w_gate_up, ((0, LANES - rank), (0, 0))).astype(BF16)
    wcat = jnp.transpose(a_ws, (1, 0, 2)).reshape(A_CHUNK, A_GROUPS * A_CHUNK)
    abias = jnp.repeat(a_bs.T, gd, axis=1)
    mstack = _gla_cumsum_matrices(CS_ROWS)
    ng = jnp.tile(norm_g, B_HEADS)[None, :]
    row = lambda t: t[None, :]
    kern = functools.partial(_even_mixer_kernel, tile=tile, aw=aw, dkh=dkh, dvh=dvh)
    tile_spec = pl.BlockSpec((None, tile, d), lambda b, i: (b, i, 0))
    return pl.pallas_call(
        kern,
        grid=(bn, s // tile),
        in_specs=[tile_spec, _const_spec(w_in_p.shape), _const_spec((1, aw)), _const_spec((1, aw)),
                  _const_spec(wcat.shape), _const_spec(abias.shape), _const_spec(mstack.shape),
                  _const_spec(w_up_p.shape), _const_spec((1, hk)), _const_spec((1, hv)),
                  _const_spec(w_out.shape), _const_spec((1, d)), _const_spec((1, d))],
        out_specs=tile_spec,
        out_shape=jax.ShapeDtypeStruct(x.shape, F32),
        scratch_shapes=[pltpu.VMEM((hv, hk), F32)],
        compiler_params=_params(2),
        name="even_mixer",
    )(x, w_in_p, row(a_ln_g), row(a_ln_b), wcat, abias, mstack, w_up_p, row(b_gate), ng,
      w_out.astype(BF16), row(ln_g), row(ln_b))


def _ple(y, p, wpg_ref, bpg_ref, wpp_ref):
    gate = _sigmoid(_dot(y.astype(BF16), wpg_ref[...]) + bpg_ref[...])
    return y + gate * _dot(p.astype(BF16), wpp_ref[...])


def _ffn_ple_kernel(x_ref, p_ref, wg_ref, wu_ref, wd_ref, ln_g_ref, ln_b_ref, wpg_ref, bpg_ref,
                    wpp_ref, o_ref, *, chunks):
    x = x_ref[...]
    xb = x.astype(BF16)
    acc = None
    for lo, hi in chunks:
        g = _dot(xb, wg_ref[:, lo:hi])
        u = _dot(xb, wu_ref[:, lo:hi])
        h = (g * _sigmoid(g) * u).astype(BF16)
        part = _dot(h, wd_ref[lo:hi, :])
        acc = part if acc is None else acc + part
    y = _layer_norm(DEEPNORM_ALPHA * x + acc, ln_g_ref[...], ln_b_ref[...])
    o_ref[...] = _ple(y, p_ref[...], wpg_ref, bpg_ref, wpp_ref)


def _ffn_ple(x2d, p_all, layer, wg, wu, wd, ln_g, ln_b, wpg, bpg, wpp, *, tile=512, fchunk=1024):
    n, d = x2d.shape
    f = wg.shape[1]
    pd = p_all.shape[2]
    chunks = tuple((lo, min(lo + fchunk, f)) for lo in range(0, f, fchunk))
    row = lambda t: t[None, :]
    return pl.pallas_call(
        functools.partial(_ffn_ple_kernel, chunks=chunks),
        grid=(n // tile,),
        in_specs=[pl.BlockSpec((tile, d), lambda i: (i, 0)),
                  pl.BlockSpec((None, tile, pd), lambda i: (layer, i, 0)),
                  _const_spec((d, f)), _const_spec((d, f)), _const_spec((f, d)),
                  _const_spec((1, d)), _const_spec((1, d)),
                  _const_spec((d, d)), _const_spec((1, d)), _const_spec((pd, d))],
        out_specs=pl.BlockSpec((tile, d), lambda i: (i, 0)),
        out_shape=jax.ShapeDtypeStruct((n, d), F32),
        compiler_params=_params(1),
        name="ffn_ple",
    )(x2d, p_all, wg.astype(BF16), wu.astype(BF16), wd.astype(BF16), row(ln_g), row(ln_b),
      wpg.astype(BF16), row(bpg), wpp.astype(BF16))


def _qkv_kernel(x_ref, w_ref, q_ref, kt_ref, v_ref, *, hd, scale):
    z = _dot(x_ref[...].astype(BF16), w_ref[...])
    q_ref[...] = (z[:, 0:hd] * scale).astype(BF16)
    k_t = z[:, hd:2 * hd].T.astype(BF16)
    heads, dk2, tile = kt_ref.shape
    ones = jnp.ones((tile, dk2), BF16)
    for h in range(heads):
        kt_ref[h] = k_t[h * dk2:(h + 1) * dk2, :]
        v_ref[:, 2 * h * dk2:(2 * h + 1) * dk2] = z[:, 2 * hd + h * dk2:2 * hd + (h + 1) * dk2].astype(BF16)
        v_ref[:, (2 * h + 1) * dk2:(2 * h + 2) * dk2] = ones


def _qkv(x, w_qkv, *, dh, tile):
    bn, s, d = x.shape
    hd = w_qkv.shape[1] // 3
    heads = hd // (2 * dh)
    return pl.pallas_call(
        functools.partial(_qkv_kernel, hd=hd, scale=dh ** -0.5 * LOG2_E),
        grid=(bn, s // tile),
        in_specs=[pl.BlockSpec((None, tile, d), lambda b, i: (b, i, 0)), _const_spec(w_qkv.shape)],
        out_specs=[pl.BlockSpec((None, tile, hd), lambda b, i: (b, i, 0)),
                   pl.BlockSpec((None, heads, None, 2 * dh, tile), lambda b, i: (b, 0, i, 0, 0)),
                   pl.BlockSpec((None, tile, 2 * hd), lambda b, i: (b, i, 0))],
        out_shape=[jax.ShapeDtypeStruct((bn, s, hd), BF16),
                   jax.ShapeDtypeStruct((bn, heads, s // tile, 2 * dh, tile), BF16),
                   jax.ShapeDtypeStruct((bn, s, 2 * hd), BF16)],
        compiler_params=_params(2),
        name="qkv_proj",
    )(x, w_qkv.astype(BF16))


def _diff_attn_kernel(q_ref, kt_ref, v_ref, lam_ref, g_ref, o_ref, m_ref, acc_ref,
                      qm_ref, sa_ref, sb_ref, *, blk, nb, dh, lambda_init):
    pairs = [(qi, ki) for qi in range(nb) for ki in range(qi + 1)]
    bufs = (sa_ref, sb_ref)
    half = blk // 2

    def load_queries(qi):
        q = q_ref[qi * blk:(qi + 1) * blk, :]
        lane = lax.broadcasted_iota(jnp.int32, q.shape, 1)
        zero = jnp.zeros_like(q)
        qm_ref[0] = jnp.where(lane < dh, q, zero)
        qm_ref[1] = jnp.where(lane >= dh, q, zero)

    def scores(ki, s_ref, diagonal):
        kt_blk = kt_ref[ki]
        for c in range(2):
            if diagonal:
                s_ref[c, 0:half, 0:half] = _dot(qm_ref[c, 0:half], kt_blk[:, 0:half])
                s_ref[c, half:blk, :] = _dot(qm_ref[c, half:blk], kt_blk)
            else:
                s_ref[c] = _dot(qm_ref[c], kt_blk)

    def update(c, rows, s, v_part):
        m_prev = m_ref[c, rows]
        m_new = jnp.maximum(m_prev, jnp.max(s, axis=-1, keepdims=True))
        corr = jnp.exp2(m_prev - m_new)
        p = jnp.exp2(s - jnp.concatenate([m_new] * (s.shape[1] // LANES), axis=1))
        acc_ref[c, rows] = (jnp.concatenate([corr, corr], axis=1) * acc_ref[c, rows]
                            + _dot(p.astype(BF16), v_part))
        m_ref[c, rows] = m_new

    def softmax_pv(s_ref, ki, masked):
        v_blk = v_ref[ki * blk:(ki + 1) * blk, :]
        for c in range(2):
            if not masked:
                update(c, slice(0, blk), s_ref[c], v_blk)
                continue
            s_top = s_ref[c, 0:half, 0:half]
            row = lax.broadcasted_iota(jnp.int32, s_top.shape, 0)
            col = lax.broadcasted_iota(jnp.int32, s_top.shape, 1)
            update(c, slice(0, half), jnp.where(col <= row, s_top, -1e30), v_blk[0:half])
            s_bot = s_ref[c, half:blk, :]
            row = lax.broadcasted_iota(jnp.int32, s_bot.shape, 0) + half
            col = lax.broadcasted_iota(jnp.int32, s_bot.shape, 1)
            update(c, slice(half, blk), jnp.where(col <= row, s_bot, -1e30), v_blk)

    def finalize(qi):
        a1 = acc_ref[0, :, 0:2 * dh] * (1.0 / acc_ref[0, :, 2 * dh:4 * dh])
        a2 = acc_ref[1, :, 0:2 * dh] * (1.0 / acc_ref[1, :, 2 * dh:4 * dh])
        lv = lam_ref[...]
        lam = (jnp.exp(jnp.sum(lv[0:1] * lv[1:2], axis=-1, keepdims=True))
               - jnp.exp(jnp.sum(lv[2:3] * lv[3:4], axis=-1, keepdims=True)) + lambda_init)
        o = a1 - lam * a2
        msq = jnp.mean(o * o, axis=-1, keepdims=True)
        o_ref[qi * blk:(qi + 1) * blk, :] = (o * lax.rsqrt(msq + LN_EPS) * g_ref[...]
                                              * (1.0 - lambda_init)).astype(BF16)

    load_queries(0)
    scores(0, bufs[0], True)
    for t, (qi, ki) in enumerate(pairs):
        if t + 1 < len(pairs):
            nqi, nki = pairs[t + 1]
            if nki == 0:
                load_queries(nqi)
            scores(nki, bufs[(t + 1) % 2], nki == nqi)
        if ki == 0:
            m_ref[...] = jnp.full_like(m_ref, -1e30)
            acc_ref[...] = jnp.zeros_like(acc_ref)
        softmax_pv(bufs[t % 2], ki, masked=(ki == qi))
        if ki == qi:
            finalize(qi)


def _diff_attn(q, kt, v_ext, lam_q1, lam_k1, lam_q2, lam_k2, subln_g, lambda_init, *, blk):
    bn, s, hd = q.shape
    dh = lam_q1.shape[0]
    heads = hd // (2 * dh)
    nb = s // blk
    lamv = jnp.zeros((8, LANES), F32).at[0:4, 0:dh].set(jnp.stack([lam_q1, lam_k1, lam_q2, lam_k2]))
    return pl.pallas_call(
        functools.partial(_diff_attn_kernel, blk=blk, nb=nb, dh=dh, lambda_init=lambda_init),
        grid=(bn, heads),
        in_specs=[
            pl.BlockSpec((None, s, 2 * dh), lambda b, h: (b, 0, h)),
            pl.BlockSpec((None, None, nb, 2 * dh, blk), lambda b, h: (b, h, 0, 0, 0)),
            pl.BlockSpec((None, s, 4 * dh), lambda b, h: (b, 0, h)),
            pl.BlockSpec((8, LANES), lambda b, h: (0, 0)),
            pl.BlockSpec((1, 2 * dh), lambda b, h: (0, 0)),
        ],
        out_specs=pl.BlockSpec((None, s, 2 * dh), lambda b, h: (b, 0, h)),
        out_shape=jax.ShapeDtypeStruct((bn, s, hd), BF16),
        scratch_shapes=[pltpu.VMEM((2, blk, LANES), F32),
                        pltpu.VMEM((2, blk, 4 * dh), F32), pltpu.VMEM((2, blk, 2 * dh), BF16),
                        pltpu.VMEM((2, blk, blk), F32), pltpu.VMEM((2, blk, blk), F32)],
        compiler_params=_params(2),
        name="diff_attn",
    )(q, kt, v_ext, lamv, subln_g[None, :])


def _attn_out_kernel(o_ref, x_ref, w_ref, ln_g_ref, ln_b_ref, wr_ref, lstrict_ref,
                     y_ref, route_ref, cnt_ref, base_ref):
    @pl.when(pl.program_id(0) == 0)
    def _():
        base_ref[...] = jnp.zeros_like(base_ref)

    m = _dot(o_ref[...], w_ref[...])
    y = _layer_norm(DEEPNORM_ALPHA * x_ref[...] + m, ln_g_ref[...], ln_b_ref[...])
    y_ref[...] = y
    y_hi = y.astype(BF16)
    y_lo = (y - y_hi.astype(F32)).astype(BF16)
    hi_terms = _dot(y_hi, wr_ref[...])
    logits = hi_terms[:, 0:LANES] + hi_terms[:, LANES:2 * LANES] + _dot(y_lo, wr_ref[:, 0:LANES])

    lane = lax.broadcasted_iota(jnp.int32, logits.shape, 1).astype(F32)
    neg = -jnp.inf
    lg = jnp.where(lane < N_EXPERTS, logits, neg)
    v0 = jnp.max(lg, axis=-1, keepdims=True)
    i0 = jnp.min(jnp.where(lg == v0, lane, float(LANES)), axis=-1, keepdims=True)
    lg2 = jnp.where(lane == i0, neg, lg)
    v1 = jnp.max(lg2, axis=-1, keepdims=True)
    i1 = jnp.min(jnp.where(lg2 == v1, lane, float(LANES)), axis=-1, keepdims=True)
    e = jnp.exp(v1 - v0)
    g0 = 1.0 / (1.0 + e)
    g1 = e / (1.0 + e)
    oh0 = lane == i0
    oh1 = lane == i1
    c = jnp.where(oh0, 1.0, 0.0) + jnp.where(oh1, 1.0, 0.0)
    pre = _dot(lstrict_ref[...], c.astype(BF16)) + base_ref[...]
    r0 = jnp.sum(jnp.where(oh0, pre, 0.0), axis=-1, keepdims=True)
    r1 = jnp.sum(jnp.where(oh1, pre, 0.0), axis=-1, keepdims=True)
    base_ref[...] = base_ref[...] + jnp.sum(c, axis=0, keepdims=True)
    cnt_ref[...] = jnp.broadcast_to(base_ref[...], cnt_ref.shape)
    fields = (i0, i1, r0, r1, g0, g1)
    route = jnp.zeros_like(logits)
    for idx, val in enumerate(fields):
        route = jnp.where(lane == idx, val, route)
    route_ref[...] = route


def _attn_out(o2d, x2d, w_out, ln_g, ln_b, w_router, *, tile=1024):
    n, d = x2d.shape
    wr = jnp.pad(w_router, ((0, 0), (0, LANES - w_router.shape[1])))
    wr_hi = wr.astype(BF16)
    wr_lo = (wr - wr_hi.astype(F32)).astype(BF16)
    wr_split = jnp.concatenate([wr_hi, wr_lo], axis=1)
    lstrict = (jnp.arange(tile)[:, None] > jnp.arange(tile)[None, :]).astype(BF16)
    row = lambda t: t[None, :]
    tspec = pl.BlockSpec((tile, d), lambda i: (i, 0))
    return pl.pallas_call(
        _attn_out_kernel,
        grid=(n // tile,),
        in_specs=[tspec, tspec, _const_spec((d, d)), _const_spec((1, d)), _const_spec((1, d)),
                  _const_spec((d, 2 * LANES)), _const_spec((tile, tile))],
        out_specs=[tspec, pl.BlockSpec((tile, LANES), lambda i: (i, 0)),
                   pl.BlockSpec((8, LANES), lambda i: (0, 0))],
        out_shape=[jax.ShapeDtypeStruct((n, d), F32), jax.ShapeDtypeStruct((n, LANES), F32),
                   jax.ShapeDtypeStruct((8, LANES), F32)],
        scratch_shapes=[pltpu.VMEM((1, LANES), F32)],
        compiler_params=_params(1),
        name="attn_out_router",
    )(o2d, x2d, w_out.astype(BF16), row(ln_g), row(ln_b), wr_split, lstrict)


def _route_tables(route, cnt, *, tm):
    n = route.shape[0]
    counts = cnt[0, :N_EXPERTS].astype(jnp.int32)
    padded = (counts + tm - 1) // tm * tm
    pend = jnp.cumsum(padded)
    pstart = pend - padded

    def dest_of(k):
        e = route[:, k].astype(jnp.int32)
        start = jnp.sum(jnp.where(e[:, None] == jnp.arange(N_EXPERTS)[None, :], pstart[None, :], 0), axis=-1)
        return start + route[:, TOP_K + k].astype(jnp.int32)

    dest = (dest_of(0), dest_of(1))
    n_rows = n * TOP_K + N_EXPERTS * tm
    n_blocks = n_rows // tm
    n_used = pend[-1] // tm
    blk_idx = jnp.arange(n_blocks, dtype=jnp.int32)
    blk_start = jnp.minimum(blk_idx, n_used - 1) * tm
    blk_e = jnp.minimum(jnp.sum(blk_start[:, None] >= pend[None, :], axis=1), N_EXPERTS - 1)
    valid = jnp.clip(pstart[blk_e] + counts[blk_e] - blk_idx * tm, 0, tm)
    return (dest, blk_e.astype(jnp.int32), n_used.reshape(1).astype(jnp.int32),
            valid.astype(jnp.int32), n_rows)


SC_CORES = 2
SC_SUBCORES = 16
SC_WORKERS = SC_CORES * SC_SUBCORES
SC_ROWS = 32


def _sc_mesh():
    return plsc.VectorSubcoreMesh(core_axis_name="c", subcore_axis_name="s",
                                  num_cores=SC_CORES, num_subcores=SC_SUBCORES)


def _sc_worker_id():
    return lax.axis_index("s") * SC_CORES + lax.axis_index("c")


def _sc_dispatch(x2d, dest, n_rows):
    n, d = x2d.shape
    per_w = n // SC_WORKERS
    n_chunks = per_w // SC_ROWS
    assert n == SC_WORKERS * n_chunks * SC_ROWS and n_chunks % 2 == 0
    d0 = dest[0].reshape(SC_WORKERS, n_chunks, SC_ROWS)
    d1 = dest[1].reshape(SC_WORKERS, n_chunks, SC_ROWS)

    def body(x_hbm, d0_hbm, d1_hbm, xs_hbm, d0_v, d1_v, rows_v, rsem, s0sem, s1sem):
        wid = _sc_worker_id()
        base = wid * per_w
        pltpu.sync_copy(d0_hbm.at[wid], d0_v)
        pltpu.sync_copy(d1_hbm.at[wid], d1_v)

        def read(c, slot):
            return pltpu.make_async_copy(x_hbm.at[pl.ds(base + c * SC_ROWS, SC_ROWS)],
                                         rows_v.at[slot], rsem.at[slot])

        def scat(idx_v, sem, c, slot):
            return pltpu.make_async_copy(rows_v.at[slot], xs_hbm.at[idx_v.at[c]], sem.at[slot])

        def start_scatters(c, slot):
            scat(d0_v, s0sem, c, slot).start()
            scat(d1_v, s1sem, c, slot).start()

        def wait_scatters(c, slot):
            scat(d0_v, s0sem, c, slot).wait()
            scat(d1_v, s1sem, c, slot).wait()

        read(0, 0).start()

        def pair(j, carry):
            c0 = 2 * j
            read(c0, 0).wait()

            @pl.when(j > 0)
            def _():
                wait_scatters(c0 - 1, 1)

            read(c0 + 1, 1).start()
            start_scatters(c0, 0)
            read(c0 + 1, 1).wait()
            wait_scatters(c0, 0)

            @pl.when(c0 + 2 < n_chunks)
            def _():
                read(c0 + 2, 0).start()

            start_scatters(c0 + 1, 1)
            return carry

        lax.fori_loop(0, n_chunks // 2, pair, 0)
        wait_scatters(n_chunks - 1, 1)

    return pl.kernel(
        body,
        out_type=jax.ShapeDtypeStruct((n_rows, d), x2d.dtype),
        mesh=_sc_mesh(),
        scratch_types=[pltpu.VMEM((n_chunks, SC_ROWS), jnp.int32), pltpu.VMEM((n_chunks, SC_ROWS), jnp.int32),
                       pltpu.VMEM((2, SC_ROWS, d), x2d.dtype), pltpu.SemaphoreType.DMA((2,)),
                       pltpu.SemaphoreType.DMA((2,)), pltpu.SemaphoreType.DMA((2,))],
        name="sc_dispatch",
    )(x2d, d0, d1)


def _sc_gather(table, idx):
    _, d = table.shape
    b = idx.shape[0]
    per_w = b // SC_WORKERS
    n_chunks = per_w // SC_ROWS
    assert b == SC_WORKERS * n_chunks * SC_ROWS and n_chunks % 2 == 0
    idx3 = idx.reshape(SC_WORKERS, n_chunks, SC_ROWS)

    def body(table_hbm, idx_hbm, out_hbm, idx_v, rows_v, gsem, wsem):
        wid = _sc_worker_id()
        base = wid * per_w
        pltpu.sync_copy(idx_hbm.at[wid], idx_v)

        def gather(c, slot):
            return pltpu.make_async_copy(table_hbm.at[idx_v.at[c]], rows_v.at[slot], gsem.at[slot])

        def write(c, slot):
            return pltpu.make_async_copy(rows_v.at[slot],
                                         out_hbm.at[pl.ds(base + c * SC_ROWS, SC_ROWS)], wsem.at[slot])

        gather(0, 0).start()

        def pair(j, carry):
            c0 = 2 * j
            gather(c0, 0).wait()

            @pl.when(j > 0)
            def _():
                write(c0 - 1, 1).wait()

            gather(c0 + 1, 1).start()
            write(c0, 0).start()
            gather(c0 + 1, 1).wait()
            write(c0, 0).wait()

            @pl.when(c0 + 2 < n_chunks)
            def _():
                gather(c0 + 2, 0).start()

            write(c0 + 1, 1).start()
            return carry

        lax.fori_loop(0, n_chunks // 2, pair, 0)
        write(n_chunks - 1, 1).wait()

    return pl.kernel(
        body,
        out_type=jax.ShapeDtypeStruct((b, d), table.dtype),
        mesh=_sc_mesh(),
        scratch_types=[pltpu.VMEM((n_chunks, SC_ROWS), jnp.int32), pltpu.VMEM((2, SC_ROWS, d), table.dtype),
                       pltpu.SemaphoreType.DMA((2,)), pltpu.SemaphoreType.DMA((2,))],
        name="sc_combine_gather",
    )(table, idx3)


def _moe_kernel(blk_e, n_used, valid, xs_ref, wg_ref, wu_ref, wd_ref, y_ref, *, chunks):
    m = pl.program_id(0)
    tm = xs_ref.shape[0]

    def expert_rows(n_rows):
        row = lax.broadcasted_iota(jnp.int32, (n_rows, xs_ref.shape[1]), 0)
        x = jnp.where(row < valid[m], xs_ref[0:n_rows, :], 0.0).astype(BF16)
        for j, (lo, hi) in enumerate(chunks):
            g = _dot(x, wg_ref[:, lo:hi])
            u = _dot(x, wu_ref[:, lo:hi])
            h = (g * _sigmoid(g) * u).astype(BF16)
            part = _dot(h, wd_ref[lo:hi, :])
            if j == 0:
                y_ref[0:n_rows, :] = part
            else:
                y_ref[0:n_rows, :] += part
        if n_rows < tm:
            y_ref[n_rows:tm, :] = jnp.zeros((tm - n_rows, y_ref.shape[1]), y_ref.dtype)

    half = tm // 2

    @pl.when((m < n_used[0]) & (valid[m] > half))
    def _():
        expert_rows(tm)

    @pl.when((m < n_used[0]) & (valid[m] <= half))
    def _():
        expert_rows(half)

    @pl.when(m >= n_used[0])
    def _():
        y_ref[...] = jnp.zeros_like(y_ref)


def _moe_grouped(xs, blk_e, n_used, valid, wg, wu, wd, *, tm, fchunk=512):
    n_rows, d = xs.shape
    fdim = wg.shape[2]
    n_blocks = n_rows // tm
    chunks = tuple((lo, min(lo + fchunk, fdim)) for lo in range(0, fdim, fchunk))
    grid_spec = pltpu.PrefetchScalarGridSpec(
        num_scalar_prefetch=3,
        grid=(n_blocks,),
        in_specs=[
            pl.BlockSpec((tm, d), lambda m, be, nu, va: (jnp.minimum(m, nu[0] - 1), 0)),
            pl.BlockSpec((None, d, fdim), lambda m, be, nu, va: (be[m], 0, 0)),
            pl.BlockSpec((None, d, fdim), lambda m, be, nu, va: (be[m], 0, 0)),
            pl.BlockSpec((None, fdim, d), lambda m, be, nu, va: (be[m], 0, 0)),
        ],
        out_specs=pl.BlockSpec((tm, d), lambda m, be, nu, va: (m, 0)),
    )
    return pl.pallas_call(
        functools.partial(_moe_kernel, chunks=chunks),
        grid_spec=grid_spec,
        out_shape=jax.ShapeDtypeStruct((n_rows, d), F32),
        compiler_params=pltpu.CompilerParams(dimension_semantics=("arbitrary",),
                                             vmem_limit_bytes=MOE_VMEM_LIMIT),
        name="moe_grouped",
    )(blk_e, n_used, valid, xs, wg, wu, wd)


def _final_kernel(x_ref, y0_ref, y1_ref, route_ref, p_ref, ln_g_ref, ln_b_ref, wpg_ref, bpg_ref,
                  wpp_ref, *rest):
    o_ref = rest[-1]
    route = route_ref[...]
    f = route[:, 4:5] * y0_ref[...] + route[:, 5:6] * y1_ref[...]
    y = _layer_norm(DEEPNORM_ALPHA * x_ref[...] + f, ln_g_ref[...], ln_b_ref[...])
    o_ref[...] = _ple(y, p_ref[...], wpg_ref, bpg_ref, wpp_ref)


def _final_part(x2d, y01, route, p_all, layer, ln_g, ln_b, wpg, bpg, wpp, *, part, n_parts, prev, tile):
    n, d = x2d.shape
    pd = p_all.shape[2]
    nblk = n // n_parts // tile
    off = part * nblk
    row = lambda t: t[None, :]
    tok = lambda width: pl.BlockSpec((tile, width), lambda i: (i + off, 0))
    in_specs = [tok(d), pl.BlockSpec((tile, d), lambda i: (i, 0)),
                pl.BlockSpec((tile, d), lambda i: (i + nblk, 0)), tok(LANES),
                pl.BlockSpec((None, tile, pd), lambda i: (layer, i + off, 0)),
                _const_spec((1, d)), _const_spec((1, d)),
                _const_spec((d, d)), _const_spec((1, d)), _const_spec((pd, d))]
    args = [x2d, y01, y01, route, p_all, row(ln_g), row(ln_b), wpg.astype(BF16), row(bpg), wpp.astype(BF16)]
    aliases = {}
    if prev is not None:
        in_specs.append(pl.BlockSpec(memory_space=pl.ANY))
        args.append(prev)
        aliases = {len(args) - 1: 0}
    return pl.pallas_call(
        _final_kernel,
        grid=(nblk,),
        in_specs=in_specs,
        out_specs=tok(d),
        out_shape=jax.ShapeDtypeStruct((n, d), F32),
        input_output_aliases=aliases,
        compiler_params=_params(1),
        name="moe_combine_ple",
    )(*args)


def _combine(x2d, y, dest, route, p_all, layer, ln_g, ln_b, wpg, bpg, wpp, *, n_parts=4, tile=1024):
    n = x2d.shape[0]
    step = n // n_parts
    tile = min(tile, step)
    out = None
    for part in range(n_parts):
        rows = slice(part * step, (part + 1) * step)
        y01 = _sc_gather(y, jnp.concatenate([dest[0][rows], dest[1][rows]]))
        out = _final_part(x2d, y01, route, p_all, layer, ln_g, ln_b, wpg, bpg, wpp,
                          part=part, n_parts=n_parts, prev=out, tile=tile)
    return out


def _even_layer(x, p_all, layer_idx, w_in, a_ln_g, a_ln_b, a_ws, a_bs, w_gate_up, b_gate, norm_g, w_out,
                ln1_g, ln1_b, wg, wu, wd, ln2_g, ln2_b, wpp, wpg, bpg):
    bn, s, d = x.shape
    x1 = _even_mixer(x, w_in, a_ln_g, a_ln_b, a_ws, a_bs, w_gate_up, b_gate, norm_g, w_out, ln1_g, ln1_b)
    x2 = _ffn_ple(x1.reshape(bn * s, d), p_all, layer_idx, wg, wu, wd, ln2_g, ln2_b, wpg, bpg, wpp)
    return x2.reshape(bn, s, d)


def _odd_layer(x, p_all, layer_idx, w_qkv, lam_q1, lam_k1, lam_q2, lam_k2, subln_g, w_out, ln1_g, ln1_b,
               w_router, ewg, ewu, ewd, ln2_g, ln2_b, wpp, wpg, bpg, *, attn_blk=1024, moe_tm=512):
    bn, s, d = x.shape
    n = bn * s
    lambda_init = 0.8 - 0.6 * math.exp(-0.3 * layer_idx)
    q, kt, v_ext = _qkv(x, w_qkv, dh=lam_q1.shape[0], tile=attn_blk)
    o = _diff_attn(q, kt, v_ext, lam_q1, lam_k1, lam_q2, lam_k2, subln_g, lambda_init, blk=attn_blk)
    x3, route, cnt = _attn_out(o.reshape(n, d), x.reshape(n, d), w_out, ln1_g, ln1_b, w_router)
    dest, blk_e, n_used, valid, n_rows = _route_tables(route, cnt, tm=moe_tm)
    xs = _sc_dispatch(x3, dest, n_rows)
    y = _moe_grouped(xs, blk_e, n_used, valid, ewg.astype(BF16), ewu.astype(BF16), ewd.astype(BF16),
                     tm=moe_tm)
    out = _combine(x3, y, dest, route, p_all, layer_idx, ln2_g, ln2_b, wpg, bpg, wpp)
    return out.reshape(bn, s, d)


def kernel(x, p, e_w_in, e_a_ln_g, e_a_ln_b, e_a_ws, e_a_bs, e_b_w_gate_up, e_b_b_gate, e_b_norm_g, e_w_out, e_ln1_g, e_ln1_b, e_ffn_wg, e_ffn_wu, e_ffn_wd, e_ln2_g, e_ln2_b, o_w_qkv, o_lam_q1, o_lam_k1, o_lam_q2, o_lam_k2, o_subln_g, o_w_out, o_ln1_g, o_ln1_b, o_router, o_exp_wg, o_exp_wu, o_exp_wd, o_ln2_g, o_ln2_b, ple_w_proj, ple_w_gate, ple_b_gate):
    p_all = p.reshape(p.shape[0], -1, p.shape[-1])
    for i in range(DEPTH):
        j = i // 2
        if i % 2 == 0:
            x = _even_layer(x, p_all, i, e_w_in[j], e_a_ln_g[j], e_a_ln_b[j], e_a_ws[j], e_a_bs[j],
                            e_b_w_gate_up[j], e_b_b_gate[j], e_b_norm_g[j], e_w_out[j],
                            e_ln1_g[j], e_ln1_b[j], e_ffn_wg[j], e_ffn_wu[j], e_ffn_wd[j],
                            e_ln2_g[j], e_ln2_b[j], ple_w_proj[i], ple_w_gate[i], ple_b_gate[i])
        else:
            x = _odd_layer(x, p_all, i, o_w_qkv[j], o_lam_q1[j], o_lam_k1[j], o_lam_q2[j], o_lam_k2[j],
                           o_subln_g[j], o_w_out[j], o_ln1_g[j], o_ln1_b[j], o_router[j],
                           o_exp_wg[j], o_exp_wu[j], o_exp_wd[j], o_ln2_g[j], o_ln2_b[j],
                           ple_w_proj[i], ple_w_gate[i], ple_b_gate[i])
    return x
```

```python
import functools
import math

import jax
import jax.numpy as jnp
from jax import lax
from jax.experimental import pallas as pl
from jax.experimental.pallas import tpu as pltpu
from jax.experimental.pallas import tpu_sc as plsc

F32 = jnp.float32
BF16 = jnp.bfloat16

DEPTH = 2
DEEPNORM_ALPHA = (2.0 * DEPTH) ** 0.25
LN_EPS = 1e-5
A_CHUNK = 128
A_GROUPS = 8
B_HEADS = 4
B_CHUNK = 64
B_TAU = 16.0
CS_ROWS = 256
N_EXPERTS = 8
TOP_K = 2
LANES = 128
LOG2_E = 1.4426950408889634
VMEM_LIMIT = 56 * 1024 * 1024
MOE_VMEM_LIMIT = 60 * 1024 * 1024

NT_DIMS = (((1,), (1,)), ((), ()))
TN_DIMS = (((0,), (0,)), ((), ()))


def _dot(a, b):
    return jnp.dot(a, b, preferred_element_type=F32)


def _layer_norm(x, g, b):
    mu = jnp.mean(x, axis=-1, keepdims=True)
    xc = x - mu
    var = jnp.mean(xc * xc, axis=-1, keepdims=True)
    return xc * lax.rsqrt(var + LN_EPS) * g + b


def _sigmoid(x):
    return 1.0 / (1.0 + jnp.exp(-x))


def _split3(a):
    hi = a.astype(BF16)
    r1 = a - hi.astype(F32)
    mid = r1.astype(BF16)
    lo = (r1 - mid.astype(F32)).astype(BF16)
    return hi, mid, lo


def _const_spec(shape):
    zeros = (0,) * len(shape)
    return pl.BlockSpec(shape, lambda *_: zeros, pipeline_mode=pl.Buffered(1))


def _params(n_axes):
    return pltpu.CompilerParams(dimension_semantics=("arbitrary",) * n_axes,
                                vmem_limit_bytes=VMEM_LIMIT)


def _even_mixer_kernel(x_ref, w_in_ref, a_g_ref, a_b_ref, wcat_ref, abias_ref, mstack_ref,
                       w_up_ref, b_gate_ref, ng_ref, w_out_ref, ln_g_ref, ln_b_ref,
                       o_ref, state_ref, *, tile, aw, dkh, dvh):
    @pl.when(pl.program_id(1) == 0)
    def _():
        state_ref[...] = jnp.zeros_like(state_ref)

    hk = B_HEADS * dkh
    hv = B_HEADS * dvh
    x = x_ref[...]
    z = _dot(x.astype(BF16), w_in_ref[...])

    u = jax.nn.gelu(z[:, 0:aw])
    v = _layer_norm(jax.nn.gelu(z[:, aw:2 * aw]), a_g_ref[...], a_b_ref[...])
    gd = aw // A_GROUPS
    rows = lax.broadcasted_iota(jnp.int32, (A_CHUNK, A_GROUPS * A_CHUNK), 0)
    cols = lax.broadcasted_iota(jnp.int32, (A_CHUNK, A_GROUPS * A_CHUNK), 1)
    wcat = jnp.where((cols % A_CHUNK) <= rows, wcat_ref[...], 0.0).astype(BF16)
    r_bd = lax.broadcasted_iota(jnp.int32, (A_GROUPS * A_CHUNK, aw), 0) // A_CHUNK
    c_bd = lax.broadcasted_iota(jnp.int32, (A_GROUPS * A_CHUNK, aw), 1) // gd
    mask_bd = r_bd == c_bd
    ya_parts = []
    for c in range(tile // A_CHUNK):
        sl = slice(c * A_CHUNK, (c + 1) * A_CHUNK)
        v_rep = jnp.concatenate([v[sl]] * A_GROUPS, axis=0)
        v_bd = jnp.where(mask_bd, v_rep, 0.0).astype(BF16)
        sg = _dot(wcat, v_bd) + abias_ref[...]
        ya_parts.append(u[sl] * sg)
    ya = jnp.concatenate(ya_parts, axis=0)

    o0 = 2 * aw
    q = z[:, o0:o0 + hk] * (dkh ** -0.5)
    k = z[:, o0 + hk:o0 + 2 * hk]
    vv = z[:, o0 + 2 * hk:o0 + 2 * hk + hv]
    r = z[:, o0 + 2 * hk + hv:o0 + 2 * hk + 2 * hv]
    g_low = z[:, o0 + 2 * hk + 2 * hv:]
    pre = _dot(g_low.astype(BF16), w_up_ref[...]) + b_gate_ref[...]
    log_a = (jnp.minimum(pre, 0.0) - jnp.log1p(jnp.exp(-jnp.abs(pre)))) * (1.0 / B_TAU)
    la_hi, la_mid, la_lo = _split3(log_a)
    ms = mstack_ref[...]
    b_parts, mid_parts, last_parts = [], [], []
    for t in range(tile // CS_ROWS):
        rs = slice(t * CS_ROWS, (t + 1) * CS_ROWS)
        cs = _dot(ms, la_hi[rs]) + _dot(ms, la_mid[rs]) + _dot(ms, la_lo[rs])
        b_parts.append(cs[0:CS_ROWS])
        mid_parts.append(cs[CS_ROWS:2 * CS_ROWS])
        last_parts.append(cs[2 * CS_ROWS:3 * CS_ROWS])
    b_cum = jnp.concatenate(b_parts, axis=0)
    d_mid = jnp.concatenate(mid_parts, axis=0)
    d_last = jnp.concatenate(last_parts, axis=0)
    qe = (q * jnp.exp(d_mid)).astype(BF16)
    ke = k * jnp.exp(-d_mid)
    kd = (k * jnp.exp(d_last)).astype(BF16)
    qb = (q * jnp.exp(b_cum)).astype(BF16)
    dec = jnp.exp(b_cum + d_last)
    vvb = vv.astype(BF16)

    mask_kk = (lax.broadcasted_iota(jnp.int32, (hk, hk), 0) // dkh
               == lax.broadcasted_iota(jnp.int32, (hk, hk), 1) // dkh)
    mask_vbd = (lax.broadcasted_iota(jnp.int32, (hk, hv), 0) // dkh
                == lax.broadcasted_iota(jnp.int32, (hk, hv), 1) // dvh)
    mask_st = (lax.broadcasted_iota(jnp.int32, (hv, hk), 0) // dvh
               == lax.broadcasted_iota(jnp.int32, (hv, hk), 1) // dkh)
    causal = ((lax.broadcasted_iota(jnp.int32, (B_CHUNK, hk), 1) % B_CHUNK)
              <= lax.broadcasted_iota(jnp.int32, (B_CHUNK, hk), 0))

    st = state_ref[...]
    o_parts = []
    for c in range(tile // B_CHUNK):
        sl = slice(c * B_CHUNK, (c + 1) * B_CHUNK)
        ke_bd = jnp.where(mask_kk, jnp.concatenate([ke[sl]] * B_HEADS, axis=0), 0.0).astype(BF16)
        s_cat = lax.dot_general(qe[sl], ke_bd, NT_DIMS, preferred_element_type=F32)
        s_cat = jnp.where(causal, s_cat, 0.0).astype(BF16)
        v_bd = jnp.where(mask_vbd, jnp.concatenate([vv[sl]] * B_HEADS, axis=0), 0.0).astype(BF16)
        o_c = _dot(s_cat, v_bd) + lax.dot_general(qb[sl], st.astype(BF16), NT_DIMS,
                                                   preferred_element_type=F32)
        kv_t = lax.dot_general(vvb[sl], kd[sl], TN_DIMS, preferred_element_type=F32)
        st = dec[c * B_CHUNK:c * B_CHUNK + 1] * st + jnp.where(mask_st, kv_t, 0.0)
        o_parts.append(o_c)
    state_ref[...] = st
    o = jnp.concatenate(o_parts, axis=0)

    yb_parts = []
    for h in range(B_HEADS):
        oh = o[:, h * dvh:(h + 1) * dvh]
        msq = jnp.mean(oh * oh, axis=-1, keepdims=True)
        yb_parts.append(oh * lax.rsqrt(msq + LN_EPS))
    yb = jnp.concatenate(yb_parts, axis=1) * ng_ref[...] * (r * _sigmoid(r))

    y_cat = jnp.concatenate([ya, yb], axis=1).astype(BF16)
    m = _dot(y_cat, w_out_ref[...])
    o_ref[...] = _layer_norm(DEEPNORM_ALPHA * x + m, ln_g_ref[...], ln_b_ref[...])


def _gla_cumsum_matrices(tile):
    i = jnp.arange(tile)[:, None]
    j = jnp.arange(tile)[None, :]
    same = (i // B_CHUNK) == (j // B_CHUNK)
    m_cum = same & (j <= i)
    m_mid = same & (j <= (i // B_CHUNK) * B_CHUNK + B_CHUNK // 2 - 1)
    m_last = same
    f = lambda t: t.astype(F32)
    return jnp.concatenate([f(m_cum), f(m_cum) - f(m_mid), f(m_last) - f(m_cum)], axis=0).astype(BF16)


def _even_mixer(x, w_in, a_ln_g, a_ln_b, a_ws, a_bs, w_gate_up, b_gate, norm_g, w_out, ln_g, ln_b,
                *, tile=1024):
    bn, s, d = x.shape
    aw = a_ln_g.shape[0]
    hk = w_gate_up.shape[1]
    dkh = hk // B_HEADS
    dvh = norm_g.shape[0]
    hv = B_HEADS * dvh
    rank = w_gate_up.shape[0]
    main = 2 * aw + 2 * hk + 2 * hv
    gd = aw // A_GROUPS
    w_in_p = jnp.concatenate([w_in[:, :main], jnp.pad(w_in[:, main:], ((0, 0), (0, LANES - rank)))],
                             axis=1).astype(BF16)
    w_up_p = jnp.pad(w_gate_up, ((0, LANES - rank), (0, 0))).astype(BF16)
    wcat = jnp.transpose(a_ws, (1, 0, 2)).reshape(A_CHUNK, A_GROUPS * A_CHUNK)
    abias = jnp.repeat(a_bs.T, gd, axis=1)
    mstack = _gla_cumsum_matrices(CS_ROWS)
    ng = jnp.tile(norm_g, B_HEADS)[None, :]
    row = lambda t: t[None, :]
    kern = functools.partial(_even_mixer_kernel, tile=tile, aw=aw, dkh=dkh, dvh=dvh)
    tile_spec = pl.BlockSpec((None, tile, d), lambda b, i: (b, i, 0))
    return pl.pallas_call(
        kern,
        grid=(bn, s // tile),
        in_specs=[tile_spec, _const_spec(w_in_p.shape), _const_spec((1, aw)), _const_spec((1, aw)),
                  _const_spec(wcat.shape), _const_spec(abias.shape), _const_spec(mstack.shape),
                  _const_spec(w_up_p.shape), _const_spec((1, hk)), _const_spec((1, hv)),
                  _const_spec(w_out.shape), _const_spec((1, d)), _const_spec((1, d))],
        out_specs=tile_spec,
        out_shape=jax.ShapeDtypeStruct(x.shape, F32),
        scratch_shapes=[pltpu.VMEM((hv, hk), F32)],
        compiler_params=_params(2),
        name="even_mixer",
    )(x, w_in_p, row(a_ln_g), row(a_ln_b), wcat, abias, mstack, w_up_p, row(b_gate), ng,
      w_out.astype(BF16), row(ln_g), row(ln_b))


def _ple(y, p, wpg_ref, bpg_ref, wpp_ref):
    gate = _sigmoid(_dot(y.astype(BF16), wpg_ref[...]) + bpg_ref[...])
    return y + gate * _dot(p.astype(BF16), wpp_ref[...])


def _ffn_ple_kernel(x_ref, p_ref, wg_ref, wu_ref, wd_ref, ln_g_ref, ln_b_ref, wpg_ref, bpg_ref,
                    wpp_ref, o_ref, *, chunks):
    x = x_ref[...]
    xb = x.astype(BF16)
    acc = None
    for lo, hi in chunks:
        g = _dot(xb, wg_ref[:, lo:hi])
        u = _dot(xb, wu_ref[:, lo:hi])
        h = (g * _sigmoid(g) * u).astype(BF16)
        part = _dot(h, wd_ref[lo:hi, :])
        acc = part if acc is None else acc + part
    y = _layer_norm(DEEPNORM_ALPHA * x + acc, ln_g_ref[...], ln_b_ref[...])
    o_ref[...] = _ple(y, p_ref[...], wpg_ref, bpg_ref, wpp_ref)


def _ffn_ple(x2d, p_all, layer, wg, wu, wd, ln_g, ln_b, wpg, bpg, wpp, *, tile=512, fchunk=512):
    n, d = x2d.shape
    f = wg.shape[1]
    pd = p_all.shape[2]
    chunks = tuple((lo, min(lo + fchunk, f)) for lo in range(0, f, fchunk))
    row = lambda t: t[None, :]
    return pl.pallas_call(
        functools.partial(_ffn_ple_kernel, chunks=chunks),
        grid=(n // tile,),
        in_specs=[pl.BlockSpec((tile, d), lambda i: (i, 0)),
                  pl.BlockSpec((None, tile, pd), lambda i: (layer, i, 0)),
                  _const_spec((d, f)), _const_spec((d, f)), _const_spec((f, d)),
                  _const_spec((1, d)), _const_spec((1, d)),
                  _const_spec((d, d)), _const_spec((1, d)), _const_spec((pd, d))],
        out_specs=pl.BlockSpec((tile, d), lambda i: (i, 0)),
        out_shape=jax.ShapeDtypeStruct((n, d), F32),
        compiler_params=_params(1),
        name="ffn_ple",
    )(x2d, p_all, wg.astype(BF16), wu.astype(BF16), wd.astype(BF16), row(ln_g), row(ln_b),
      wpg.astype(BF16), row(bpg), wpp.astype(BF16))


def _qkv_kernel(x_ref, w_ref, q_ref, kt_ref, v_ref, *, hd, scale):
    z = _dot(x_ref[...].astype(BF16), w_ref[...])
    q_ref[...] = (z[:, 0:hd] * scale).astype(BF16)
    k_t = z[:, hd:2 * hd].T.astype(BF16)
    heads, dk2, tile = kt_ref.shape
    ones = jnp.ones((tile, dk2), BF16)
    for h in range(heads):
        kt_ref[h] = k_t[h * dk2:(h + 1) * dk2, :]
        v_ref[:, 2 * h * dk2:(2 * h + 1) * dk2] = z[:, 2 * hd + h * dk2:2 * hd + (h + 1) * dk2].astype(BF16)
        v_ref[:, (2 * h + 1) * dk2:(2 * h + 2) * dk2] = ones


def _qkv(x, w_qkv, *, dh, tile):
    bn, s, d = x.shape
    hd = w_qkv.shape[1] // 3
    heads = hd // (2 * dh)
    return pl.pallas_call(
        functools.partial(_qkv_kernel, hd=hd, scale=dh ** -0.5 * LOG2_E),
        grid=(bn, s // tile),
        in_specs=[pl.BlockSpec((None, tile, d), lambda b, i: (b, i, 0)), _const_spec(w_qkv.shape)],
        out_specs=[pl.BlockSpec((None, tile, hd), lambda b, i: (b, i, 0)),
                   pl.BlockSpec((None, heads, None, 2 * dh, tile), lambda b, i: (b, 0, i, 0, 0)),
                   pl.BlockSpec((None, tile, 2 * hd), lambda b, i: (b, i, 0))],
        out_shape=[jax.ShapeDtypeStruct((bn, s, hd), BF16),
                   jax.ShapeDtypeStruct((bn, heads, s // tile, 2 * dh, tile), BF16),
                   jax.ShapeDtypeStruct((bn, s, 2 * hd), BF16)],
        compiler_params=_params(2),
        name="qkv_proj",
    )(x, w_qkv.astype(BF16))


def _diff_attn_kernel(q_ref, kt_ref, v_ref, lam_ref, g_ref, o_ref, m_ref, acc_ref,
                      qm_ref, sa_ref, sb_ref, *, blk, nb, dh, lambda_init):
    pairs = [(qi, ki) for qi in range(nb) for ki in range(qi + 1)]
    bufs = (sa_ref, sb_ref)
    half = blk // 2

    def load_queries(qi):
        q = q_ref[qi * blk:(qi + 1) * blk, :]
        lane = lax.broadcasted_iota(jnp.int32, q.shape, 1)
        zero = jnp.zeros_like(q)
        qm_ref[0] = jnp.where(lane < dh, q, zero)
        qm_ref[1] = jnp.where(lane >= dh, q, zero)

    def scores(ki, s_ref, diagonal):
        kt_blk = kt_ref[ki]
        for c in range(2):
            if diagonal:
                s_ref[c, 0:half, 0:half] = _dot(qm_ref[c, 0:half], kt_blk[:, 0:half])
                s_ref[c, half:blk, :] = _dot(qm_ref[c, half:blk], kt_blk)
            else:
                s_ref[c] = _dot(qm_ref[c], kt_blk)

    def update(c, rows, s, v_part):
        m_prev = m_ref[c, rows]
        m_new = jnp.maximum(m_prev, jnp.max(s, axis=-1, keepdims=True))
        corr = jnp.exp2(m_prev - m_new)
        p = jnp.exp2(s - jnp.concatenate([m_new] * (s.shape[1] // LANES), axis=1))
        acc_ref[c, rows] = (jnp.concatenate([corr, corr], axis=1) * acc_ref[c, rows]
                            + _dot(p.astype(BF16), v_part))
        m_ref[c, rows] = m_new

    def softmax_pv(s_ref, ki, masked):
        v_blk = v_ref[ki * blk:(ki + 1) * blk, :]
        for c in range(2):
            if not masked:
                update(c, slice(0, blk), s_ref[c], v_blk)
                continue
            s_top = s_ref[c, 0:half, 0:half]
            row = lax.broadcasted_iota(jnp.int32, s_top.shape, 0)
            col = lax.broadcasted_iota(jnp.int32, s_top.shape, 1)
            update(c, slice(0, half), jnp.where(col <= row, s_top, -1e30), v_blk[0:half])
            s_bot = s_ref[c, half:blk, :]
            row = lax.broadcasted_iota(jnp.int32, s_bot.shape, 0) + half
            col = lax.broadcasted_iota(jnp.int32, s_bot.shape, 1)
            update(c, slice(half, blk), jnp.where(col <= row, s_bot, -1e30), v_blk)

    def finalize(qi):
        a1 = acc_ref[0, :, 0:2 * dh] * (1.0 / acc_ref[0, :, 2 * dh:4 * dh])
        a2 = acc_ref[1, :, 0:2 * dh] * (1.0 / acc_ref[1, :, 2 * dh:4 * dh])
        lv = lam_ref[...]
        lam = (jnp.exp(jnp.sum(lv[0:1] * lv[1:2], axis=-1, keepdims=True))
               - jnp.exp(jnp.sum(lv[2:3] * lv[3:4], axis=-1, keepdims=True)) + lambda_init)
        o = a1 - lam * a2
        msq = jnp.mean(o * o, axis=-1, keepdims=True)
        o_ref[qi * blk:(qi + 1) * blk, :] = (o * lax.rsqrt(msq + LN_EPS) * g_ref[...]
                                              * (1.0 - lambda_init)).astype(BF16)

    load_queries(0)
    scores(0, bufs[0], True)
    for t, (qi, ki) in enumerate(pairs):
        if t + 1 < len(pairs):
            nqi, nki = pairs[t + 1]
            if nki == 0:
                load_queries(nqi)
            scores(nki, bufs[(t + 1) % 2], nki == nqi)
        if ki == 0:
            m_ref[...] = jnp.full_like(m_ref, -1e30)
            acc_ref[...] = jnp.zeros_like(acc_ref)
        softmax_pv(bufs[t % 2], ki, masked=(ki == qi))
        if ki == qi:
            finalize(qi)


def _diff_attn(q, kt, v_ext, lam_q1, lam_k1, lam_q2, lam_k2, subln_g, lambda_init, *, blk):
    bn, s, hd = q.shape
    dh = lam_q1.shape[0]
    heads = hd // (2 * dh)
    nb = s // blk
    lamv = jnp.zeros((8, LANES), F32).at[0:4, 0:dh].set(jnp.stack([lam_q1, lam_k1, lam_q2, lam_k2]))
    return pl.pallas_call(
        functools.partial(_diff_attn_kernel, blk=blk, nb=nb, dh=dh, lambda_init=lambda_init),
        grid=(bn, heads),
        in_specs=[
            pl.BlockSpec((None, s, 2 * dh), lambda b, h: (b, 0, h)),
            pl.BlockSpec((None, None, nb, 2 * dh, blk), lambda b, h: (b, h, 0, 0, 0)),
            pl.BlockSpec((None, s, 4 * dh), lambda b, h: (b, 0, h)),
            pl.BlockSpec((8, LANES), lambda b, h: (0, 0)),
            pl.BlockSpec((1, 2 * dh), lambda b, h: (0, 0)),
        ],
        out_specs=pl.BlockSpec((None, s, 2 * dh), lambda b, h: (b, 0, h)),
        out_shape=jax.ShapeDtypeStruct((bn, s, hd), BF16),
        scratch_shapes=[pltpu.VMEM((2, blk, LANES), F32),
                        pltpu.VMEM((2, blk, 4 * dh), F32), pltpu.VMEM((2, blk, 2 * dh), BF16),
                        pltpu.VMEM((2, blk, blk), F32), pltpu.VMEM((2, blk, blk), F32)],
        compiler_params=_params(2),
        name="diff_attn",
    )(q, kt, v_ext, lamv, subln_g[None, :])


def _attn_out_kernel(o_ref, x_ref, w_ref, ln_g_ref, ln_b_ref, wr_ref, lstrict_ref,
                     y_ref, route_ref, cnt_ref, base_ref):
    @pl.when(pl.program_id(0) == 0)
    def _():
        base_ref[...] = jnp.zeros_like(base_ref)

    m = _dot(o_ref[...], w_ref[...])
    y = _layer_norm(DEEPNORM_ALPHA * x_ref[...] + m, ln_g_ref[...], ln_b_ref[...])
    y_ref[...] = y
    y_hi = y.astype(BF16)
    y_lo = (y - y_hi.astype(F32)).astype(BF16)
    hi_terms = _dot(y_hi, wr_ref[...])
    logits = hi_terms[:, 0:LANES] + hi_terms[:, LANES:2 * LANES] + _dot(y_lo, wr_ref[:, 0:LANES])

    lane = lax.broadcasted_iota(jnp.int32, logits.shape, 1).astype(F32)
    neg = -jnp.inf
    lg = jnp.where(lane < N_EXPERTS, logits, neg)
    v0 = jnp.max(lg, axis=-1, keepdims=True)
    i0 = jnp.min(jnp.where(lg == v0, lane, float(LANES)), axis=-1, keepdims=True)
    lg2 = jnp.where(lane == i0, neg, lg)
    v1 = jnp.max(lg2, axis=-1, keepdims=True)
    i1 = jnp.min(jnp.where(lg2 == v1, lane, float(LANES)), axis=-1, keepdims=True)
    e = jnp.exp(v1 - v0)
    g0 = 1.0 / (1.0 + e)
    g1 = e / (1.0 + e)
    oh0 = lane == i0
    oh1 = lane == i1
    c = jnp.where(oh0, 1.0, 0.0) + jnp.where(oh1, 1.0, 0.0)
    pre = _dot(lstrict_ref[...], c.astype(BF16)) + base_ref[...]
    r0 = jnp.sum(jnp.where(oh0, pre, 0.0), axis=-1, keepdims=True)
    r1 = jnp.sum(jnp.where(oh1, pre, 0.0), axis=-1, keepdims=True)
    base_ref[...] = base_ref[...] + jnp.sum(c, axis=0, keepdims=True)
    cnt_ref[...] = jnp.broadcast_to(base_ref[...], cnt_ref.shape)
    fields = (i0, i1, r0, r1, g0, g1)
    route = jnp.zeros_like(logits)
    for idx, val in enumerate(fields):
        route = jnp.where(lane == idx, val, route)
    route_ref[...] = route


def _attn_out(o2d, x2d, w_out, ln_g, ln_b, w_router, *, tile=1024):
    n, d = x2d.shape
    wr = jnp.pad(w_router, ((0, 0), (0, LANES - w_router.shape[1])))
    wr_hi = wr.astype(BF16)
    wr_lo = (wr - wr_hi.astype(F32)).astype(BF16)
    wr_split = jnp.concatenate([wr_hi, wr_lo], axis=1)
    lstrict = (jnp.arange(tile)[:, None] > jnp.arange(tile)[None, :]).astype(BF16)
    row = lambda t: t[None, :]
    tspec = pl.BlockSpec((tile, d), lambda i: (i, 0))
    return pl.pallas_call(
        _attn_out_kernel,
        grid=(n // tile,),
        in_specs=[tspec, tspec, _const_spec((d, d)), _const_spec((1, d)), _const_spec((1, d)),
                  _const_spec((d, 2 * LANES)), _const_spec((tile, tile))],
        out_specs=[tspec, pl.BlockSpec((tile, LANES), lambda i: (i, 0)),
                   pl.BlockSpec((8, LANES), lambda i: (0, 0))],
        out_shape=[jax.ShapeDtypeStruct((n, d), F32), jax.ShapeDtypeStruct((n, LANES), F32),
                   jax.ShapeDtypeStruct((8, LANES), F32)],
        scratch_shapes=[pltpu.VMEM((1, LANES), F32)],
        compiler_params=_params(1),
        name="attn_out_router",
    )(o2d, x2d, w_out.astype(BF16), row(ln_g), row(ln_b), wr_split, lstrict)


def _route_tables(route, cnt, *, tm):
    n = route.shape[0]
    counts = cnt[0, :N_EXPERTS].astype(jnp.int32)
    padded = (counts + tm - 1) // tm * tm
    pend = jnp.cumsum(padded)
    pstart = pend - padded

    def dest_of(k):
        e = route[:, k].astype(jnp.int32)
        start = jnp.sum(jnp.where(e[:, None] == jnp.arange(N_EXPERTS)[None, :], pstart[None, :], 0), axis=-1)
        return start + route[:, TOP_K + k].astype(jnp.int32)

    dest = (dest_of(0), dest_of(1))
    n_rows = n * TOP_K + N_EXPERTS * tm
    n_blocks = n_rows // tm
    n_used = pend[-1] // tm
    blk_idx = jnp.arange(n_blocks, dtype=jnp.int32)
    blk_start = jnp.minimum(blk_idx, n_used - 1) * tm
    blk_e = jnp.minimum(jnp.sum(blk_start[:, None] >= pend[None, :], axis=1), N_EXPERTS - 1)
    valid = jnp.clip(pstart[blk_e] + counts[blk_e] - blk_idx * tm, 0, tm)
    return (dest, blk_e.astype(jnp.int32), n_used.reshape(1).astype(jnp.int32),
            valid.astype(jnp.int32), n_rows)


SC_CORES = 2
SC_SUBCORES = 16
SC_WORKERS = SC_CORES * SC_SUBCORES
SC_ROWS = 32


def _sc_mesh():
    return plsc.VectorSubcoreMesh(core_axis_name="c", subcore_axis_name="s",
                                  num_cores=SC_CORES, num_subcores=SC_SUBCORES)


def _sc_worker_id():
    return lax.axis_index("s") * SC_CORES + lax.axis_index("c")


def _sc_dispatch(x2d, dest, n_rows):
    n, d = x2d.shape
    per_w = n // SC_WORKERS
    n_chunks = per_w // SC_ROWS
    assert n == SC_WORKERS * n_chunks * SC_ROWS and n_chunks % 2 == 0
    d0 = dest[0].reshape(SC_WORKERS, n_chunks, SC_ROWS)
    d1 = dest[1].reshape(SC_WORKERS, n_chunks, SC_ROWS)

    def body(x_hbm, d0_hbm, d1_hbm, xs_hbm, d0_v, d1_v, rows_v, rsem, s0sem, s1sem):
        wid = _sc_worker_id()
        base = wid * per_w
        pltpu.sync_copy(d0_hbm.at[wid], d0_v)
        pltpu.sync_copy(d1_hbm.at[wid], d1_v)

        def read(c, slot):
            return pltpu.make_async_copy(x_hbm.at[pl.ds(base + c * SC_ROWS, SC_ROWS)],
                                         rows_v.at[slot], rsem.at[slot])

        def scat(idx_v, sem, c, slot):
            return pltpu.make_async_copy(rows_v.at[slot], xs_hbm.at[idx_v.at[c]], sem.at[slot])

        def start_scatters(c, slot):
            scat(d0_v, s0sem, c, slot).start()
            scat(d1_v, s1sem, c, slot).start()

        def wait_scatters(c, slot):
            scat(d0_v, s0sem, c, slot).wait()
            scat(d1_v, s1sem, c, slot).wait()

        read(0, 0).start()

        def pair(j, carry):
            c0 = 2 * j
            read(c0, 0).wait()

            @pl.when(j > 0)
            def _():
                wait_scatters(c0 - 1, 1)

            read(c0 + 1, 1).start()
            start_scatters(c0, 0)
            read(c0 + 1, 1).wait()
            wait_scatters(c0, 0)

            @pl.when(c0 + 2 < n_chunks)
            def _():
                read(c0 + 2, 0).start()

            start_scatters(c0 + 1, 1)
            return carry

        lax.fori_loop(0, n_chunks // 2, pair, 0)
        wait_scatters(n_chunks - 1, 1)

    return pl.kernel(
        body,
        out_type=jax.ShapeDtypeStruct((n_rows, d), x2d.dtype),
        mesh=_sc_mesh(),
        scratch_types=[pltpu.VMEM((n_chunks, SC_ROWS), jnp.int32), pltpu.VMEM((n_chunks, SC_ROWS), jnp.int32),
                       pltpu.VMEM((2, SC_ROWS, d), x2d.dtype), pltpu.SemaphoreType.DMA((2,)),
                       pltpu.SemaphoreType.DMA((2,)), pltpu.SemaphoreType.DMA((2,))],
        name="sc_dispatch",
    )(x2d, d0, d1)


def _sc_gather(table, idx):
    _, d = table.shape
    b = idx.shape[0]
    per_w = b // SC_WORKERS
    n_chunks = per_w // SC_ROWS
    assert b == SC_WORKERS * n_chunks * SC_ROWS and n_chunks % 2 == 0
    idx3 = idx.reshape(SC_WORKERS, n_chunks, SC_ROWS)

    def body(table_hbm, idx_hbm, out_hbm, idx_v, rows_v, gsem, wsem):
        wid = _sc_worker_id()
        base = wid * per_w
        pltpu.sync_copy(idx_hbm.at[wid], idx_v)

        def gather(c, slot):
            return pltpu.make_async_copy(table_hbm.at[idx_v.at[c]], rows_v.at[slot], gsem.at[slot])

        def write(c, slot):
            return pltpu.make_async_copy(rows_v.at[slot],
                                         out_hbm.at[pl.ds(base + c * SC_ROWS, SC_ROWS)], wsem.at[slot])

        gather(0, 0).start()

        def pair(j, carry):
            c0 = 2 * j
            gather(c0, 0).wait()

            @pl.when(j > 0)
            def _():
                write(c0 - 1, 1).wait()

            gather(c0 + 1, 1).start()
            write(c0, 0).start()
            gather(c0 + 1, 1).wait()
            write(c0, 0).wait()

            @pl.when(c0 + 2 < n_chunks)
            def _():
                gather(c0 + 2, 0).start()

            write(c0 + 1, 1).start()
            return carry

        lax.fori_loop(0, n_chunks // 2, pair, 0)
        write(n_chunks - 1, 1).wait()

    return pl.kernel(
        body,
        out_type=jax.ShapeDtypeStruct((b, d), table.dtype),
        mesh=_sc_mesh(),
        scratch_types=[pltpu.VMEM((n_chunks, SC_ROWS), jnp.int32), pltpu.VMEM((2, SC_ROWS, d), table.dtype),
                       pltpu.SemaphoreType.DMA((2,)), pltpu.SemaphoreType.DMA((2,))],
        name="sc_combine_gather",
    )(table, idx3)


def _moe_kernel(blk_e, n_used, valid, xs_ref, wg_ref, wu_ref, wd_ref, y_ref, *, chunks):
    m = pl.program_id(0)
    tm = xs_ref.shape[0]

    def expert_rows(n_rows):
        row = lax.broadcasted_iota(jnp.int32, (n_rows, xs_ref.shape[1]), 0)
        x = jnp.where(row < valid[m], xs_ref[0:n_rows, :], 0.0).astype(BF16)
        for j, (lo, hi) in enumerate(chunks):
            g = _dot(x, wg_ref[:, lo:hi])
            u = _dot(x, wu_ref[:, lo:hi])
            h = (g * _sigmoid(g) * u).astype(BF16)
            part = _dot(h, wd_ref[lo:hi, :])
            if j == 0:
                y_ref[0:n_rows, :] = part
            else:
                y_ref[0:n_rows, :] += part
        if n_rows < tm:
            y_ref[n_rows:tm, :] = jnp.zeros((tm - n_rows, y_ref.shape[1]), y_ref.dtype)

    half = tm // 2

    @pl.when((m < n_used[0]) & (valid[m] > half))
    def _():
        expert_rows(tm)

    @pl.when((m < n_used[0]) & (valid[m] <= half))
    def _():
        expert_rows(half)

    @pl.when(m >= n_used[0])
    def _():
        y_ref[...] = jnp.zeros_like(y_ref)


def _moe_grouped(xs, blk_e, n_used, valid, wg, wu, wd, *, tm, fchunk=512):
    n_rows, d = xs.shape
    fdim = wg.shape[2]
    n_blocks = n_rows // tm
    chunks = tuple((lo, min(lo + fchunk, fdim)) for lo in range(0, fdim, fchunk))
    grid_spec = pltpu.PrefetchScalarGridSpec(
        num_scalar_prefetch=3,
        grid=(n_blocks,),
        in_specs=[
            pl.BlockSpec((tm, d), lambda m, be, nu, va: (jnp.minimum(m, nu[0] - 1), 0)),
            pl.BlockSpec((None, d, fdim), lambda m, be, nu, va: (be[m], 0, 0)),
            pl.BlockSpec((None, d, fdim), lambda m, be, nu, va: (be[m], 0, 0)),
            pl.BlockSpec((None, fdim, d), lambda m, be, nu, va: (be[m], 0, 0)),
        ],
        out_specs=pl.BlockSpec((tm, d), lambda m, be, nu, va: (m, 0)),
    )
    return pl.pallas_call(
        functools.partial(_moe_kernel, chunks=chunks),
        grid_spec=grid_spec,
        out_shape=jax.ShapeDtypeStruct((n_rows, d), F32),
        compiler_params=pltpu.CompilerParams(dimension_semantics=("arbitrary",),
                                             vmem_limit_bytes=MOE_VMEM_LIMIT),
        name="moe_grouped",
    )(blk_e, n_used, valid, xs, wg, wu, wd)


def _final_kernel(x_ref, y0_ref, y1_ref, route_ref, p_ref, ln_g_ref, ln_b_ref, wpg_ref, bpg_ref,
                  wpp_ref, *rest):
    o_ref = rest[-1]
    route = route_ref[...]
    f = route[:, 4:5] * y0_ref[...] + route[:, 5:6] * y1_ref[...]
    y = _layer_norm(DEEPNORM_ALPHA * x_ref[...] + f, ln_g_ref[...], ln_b_ref[...])
    o_ref[...] = _ple(y, p_ref[...], wpg_ref, bpg_ref, wpp_ref)


def _final_part(x2d, y01, route, p_all, layer, ln_g, ln_b, wpg, bpg, wpp, *, part, n_parts, prev, tile):
    n, d = x2d.shape
    pd = p_all.shape[2]
    nblk = n // n_parts // tile
    off = part * nblk
    row = lambda t: t[None, :]
    tok = lambda width: pl.BlockSpec((tile, width), lambda i: (i + off, 0))
    in_specs = [tok(d), pl.BlockSpec((tile, d), lambda i: (i, 0)),
                pl.BlockSpec((tile, d), lambda i: (i + nblk, 0)), tok(LANES),
                pl.BlockSpec((None, tile, pd), lambda i: (layer, i + off, 0)),
                _const_spec((1, d)), _const_spec((1, d)),
                _const_spec((d, d)), _const_spec((1, d)), _const_spec((pd, d))]
    args = [x2d, y01, y01, route, p_all, row(ln_g), row(ln_b), wpg.astype(BF16), row(bpg), wpp.astype(BF16)]
    aliases = {}
    if prev is not None:
        in_specs.append(pl.BlockSpec(memory_space=pl.ANY))
        args.append(prev)
        aliases = {len(args) - 1: 0}
    return pl.pallas_call(
        _final_kernel,
        grid=(nblk,),
        in_specs=in_specs,
        out_specs=tok(d),
        out_shape=jax.ShapeDtypeStruct((n, d), F32),
        input_output_aliases=aliases,
        compiler_params=_params(1),
        name="moe_combine_ple",
    )(*args)


def _combine(x2d, y, dest, route, p_all, layer, ln_g, ln_b, wpg, bpg, wpp, *, n_parts=4, tile=1024):
    n = x2d.shape[0]
    step = n // n_parts
    tile = min(tile, step)
    out = None
    for part in range(n_parts):
        rows = slice(part * step, (part + 1) * step)
        y01 = _sc_gather(y, jnp.concatenate([dest[0][rows], dest[1][rows]]))
        out = _final_part(x2d, y01, route, p_all, layer, ln_g, ln_b, wpg, bpg, wpp,
                          part=part, n_parts=n_parts, prev=out, tile=tile)
    return out


def _even_layer(x, p_all, layer_idx, w_in, a_ln_g, a_ln_b, a_ws, a_bs, w_gate_up, b_gate, norm_g, w_out,
                ln1_g, ln1_b, wg, wu, wd, ln2_g, ln2_b, wpp, wpg, bpg):
    bn, s, d = x.shape
    x1 = _even_mixer(x, w_in, a_ln_g, a_ln_b, a_ws, a_bs, w_gate_up, b_gate, norm_g, w_out, ln1_g, ln1_b)
    x2 = _ffn_ple(x1.reshape(bn * s, d), p_all, layer_idx, wg, wu, wd, ln2_g, ln2_b, wpg, bpg, wpp)
    return x2.reshape(bn, s, d)


def _odd_layer(x, p_all, layer_idx, w_qkv, lam_q1, lam_k1, lam_q2, lam_k2, subln_g, w_out, ln1_g, ln1_b,
               w_router, ewg, ewu, ewd, ln2_g, ln2_b, wpp, wpg, bpg, *, attn_blk=1024, moe_tm=512):
    bn, s, d = x.shape
    n = bn * s
    lambda_init = 0.8 - 0.6 * math.exp(-0.3 * layer_idx)
    q, kt, v_ext = _qkv(x, w_qkv, dh=lam_q1.shape[0], tile=attn_blk)
    o = _diff_attn(q, kt, v_ext, lam_q1, lam_k1, lam_q2, lam_k2, subln_g, lambda_init, blk=attn_blk)
    x3, route, cnt = _attn_out(o.reshape(n, d), x.reshape(n, d), w_out, ln1_g, ln1_b, w_router)
    dest, blk_e, n_used, valid, n_rows = _route_tables(route, cnt, tm=moe_tm)
    xs = _sc_dispatch(x3, dest, n_rows)
    y = _moe_grouped(xs, blk_e, n_used, valid, ewg.astype(BF16), ewu.astype(BF16), ewd.astype(BF16),
                     tm=moe_tm)
    out = _combine(x3, y, dest, route, p_all, layer_idx, ln2_g, ln2_b, wpg, bpg, wpp)
    return out.reshape(bn, s, d)


def kernel(x, p, e_w_in, e_a_ln_g, e_a_ln_b, e_a_ws, e_a_bs, e_b_w_gate_up, e_b_b_gate, e_b_norm_g, e_w_out, e_ln1_g, e_ln1_b, e_ffn_wg, e_ffn_wu, e_ffn_wd, e_ln2_g, e_ln2_b, o_w_qkv, o_lam_q1, o_lam_k1, o_lam_q2, o_lam_k2, o_subln_g, o_w_out, o_ln1_g, o_ln1_b, o_router, o_exp_wg, o_exp_wu, o_exp_wd, o_ln2_g, o_ln2_b, ple_w_proj, ple_w_gate, ple_b_gate):
    p_all = p.reshape(p.shape[0], -1, p.shape[-1])
    for i in range(DEPTH):
        j = i // 2
        if i % 2 == 0:
            x = _even_layer(x, p_all, i, e_w_in[j], e_a_ln_g[j], e_a_ln_b[j], e_a_ws[j], e_a_bs[j],
                            e_b_w_gate_up[j], e_b_b_gate[j], e_b_norm_g[j], e_w_out[j],
                            e_ln1_g[j], e_ln1_b[j], e_ffn_wg[j], e_ffn_wu[j], e_ffn_wd[j],
                            e_ln2_g[j], e_ln2_b[j], ple_w_proj[i], ple_w_gate[i], ple_b_gate[i])
        else:
            x = _odd_layer(x, p_all, i, o_w_qkv[j], o_lam_q1[j], o_lam_k1[j], o_lam_q2[j], o_lam_k2[j],
                           o_subln_g[j], o_w_out[j], o_ln1_g[j], o_ln1_b[j], o_router[j],
                           o_exp_wg[j], o_exp_wu[j], o_exp_wd[j], o_ln2_g[j], o_ln2_b[j],
                           ple_w_proj[i], ple_w_gate[i], ple_b_gate[i])
    return x
```
